```python
import jax, jax.numpy as jnp
from jax import lax
import numpy as np

D_MODEL = 1024
BATCH = 16
SEQ = 256
DEPTH = 2
DEC_BATCH = 2
DEC_SEQ = 2048
PAST_LEN = 512

GRID_W = 64
ROPE_THETA = 10000.0
NORM_EPS = 1e-6
Q_BLOCK = 128
WINDOW = 128
NEG_INF = -1e30

HEAD_DIM = 64
N_HEADS_A = 8
N_KV_A = 2
N_HEADS_B = 8
Q_LORA = 384
KV_LORA = 256
QK_NOPE = 64
QK_ROPE = 32
V_DIM_B = 64
N_HEADS_C = 16
N_KV_C = 2
D_FF = 2816
CONV_W = 3

N_EVEN = (DEPTH + 1) // 2
N_ODD = DEPTH // 2
IN_E = N_HEADS_A * HEAD_DIM + 2 * N_KV_A * HEAD_DIM + Q_LORA + KV_LORA + QK_ROPE
SPLIT_E = [N_HEADS_A * HEAD_DIM,
           N_HEADS_A * HEAD_DIM + N_KV_A * HEAD_DIM,
           N_HEADS_A * HEAD_DIM + 2 * N_KV_A * HEAD_DIM,
           N_HEADS_A * HEAD_DIM + 2 * N_KV_A * HEAD_DIM + Q_LORA,
           N_HEADS_A * HEAD_DIM + 2 * N_KV_A * HEAD_DIM + Q_LORA + KV_LORA]
MIX_E = N_HEADS_A * HEAD_DIM + N_HEADS_B * V_DIM_B
IN_O = N_HEADS_C * HEAD_DIM + 2 * N_KV_C * HEAD_DIM
SPLIT_O = [N_HEADS_C * HEAD_DIM, N_HEADS_C * HEAD_DIM + N_KV_C * HEAD_DIM]
MIX_O = N_HEADS_C * HEAD_DIM

kernel_name = "hybrid_diffusion_prefix_ctx_step"


def rmsnorm(x, g):
    xf = x.astype(jnp.float32)
    y = xf * lax.rsqrt(jnp.mean(xf * xf, axis=-1, keepdims=True) + NORM_EPS)
    return (y * g.astype(jnp.float32)).astype(x.dtype)


def axial_rope_tables(n_tokens, rot_dim):
    rows = n_tokens // GRID_W
    row = jnp.repeat(jnp.arange(rows), GRID_W).astype(jnp.float32)
    col = jnp.tile(jnp.arange(GRID_W), rows).astype(jnp.float32)
    d_axis = rot_dim // 2
    freqs = ROPE_THETA ** (-jnp.arange(0, d_axis, 2, dtype=jnp.float32) / d_axis)
    ang = jnp.concatenate([row[:, None] * freqs, col[:, None] * freqs], axis=-1)
    return jnp.cos(ang), jnp.sin(ang)


def apply_rope(x, cos, sin):
    x1 = x[..., 0::2].astype(jnp.float32)
    x2 = x[..., 1::2].astype(jnp.float32)
    c = cos[None, :, None, :]
    s = sin[None, :, None, :]
    out = jnp.stack([x1 * c - x2 * s, x1 * s + x2 * c], axis=-1).reshape(x.shape)
    return out.astype(x.dtype)


def modulation(cond, w_mod, b_mod):
    m = jax.nn.silu(cond) @ w_mod + b_mod
    return jnp.split(m[..., None, :], 6, axis=-1)


def softmax_attend(q, k, v, scale, bias=None, sink=None):
    s = jnp.einsum('bqhgd,bthd->bhgqt', q, k, preferred_element_type=jnp.float32) * scale
    if bias is not None:
        s = s + bias
    if sink is not None:
        sk = jnp.broadcast_to(sink.astype(jnp.float32)[None, :, :, None, None], s.shape[:-1] + (1,))
        p = jax.nn.softmax(jnp.concatenate([s, sk], axis=-1), axis=-1)[..., :-1]
    else:
        p = jax.nn.softmax(s, axis=-1)
    return jnp.einsum('bhgqt,bthd->bqhgd', p.astype(v.dtype), v)


def dense_attention_blocked(q, k, v, scale, sink=None):
    B, S, H, dk = q.shape
    hkv = k.shape[2]
    g = H // hkv
    nb = S // Q_BLOCK
    qb = q.reshape(B, nb, Q_BLOCK, hkv, g, dk).transpose(1, 0, 2, 3, 4, 5)
    ob = lax.map(lambda qq: softmax_attend(qq, k, v, scale, sink=sink), qb)
    return ob.transpose(1, 0, 2, 3, 4, 5).reshape(B, S, H, v.shape[-1])


def window_attention_with_context(q, k, v, k_ctx, v_ctx, scale, sink):
    B, S, H, dk = q.shape
    hkv = k.shape[2]
    g = H // hkv
    nb = S // Q_BLOCK
    band = Q_BLOCK + 2 * WINDOW
    kp = jnp.pad(k, ((0, 0), (WINDOW, WINDOW), (0, 0), (0, 0)))
    vp = jnp.pad(v, ((0, 0), (WINDOW, WINDOW), (0, 0), (0, 0)))
    qb = q.reshape(B, nb, Q_BLOCK, hkv, g, dk).transpose(1, 0, 2, 3, 4, 5)
    qi = jnp.arange(Q_BLOCK)[:, None]
    kj = jnp.arange(band)[None, :]
    rel = kj - WINDOW - qi
    ctx_bias = jnp.zeros((Q_BLOCK, k_ctx.shape[1]), jnp.float32)

    def block(args):
        b, qq = args
        start = b * Q_BLOCK
        kb = lax.dynamic_slice_in_dim(kp, start, band, axis=1)
        vb = lax.dynamic_slice_in_dim(vp, start, band, axis=1)
        kpos = start - WINDOW + kj
        ok = (jnp.abs(rel) <= WINDOW) & (kpos >= 0) & (kpos < S)
        bias = jnp.concatenate([jnp.where(ok, 0.0, NEG_INF).astype(jnp.float32), ctx_bias], axis=-1)
        return softmax_attend(qq, jnp.concatenate([kb, k_ctx], axis=1),
                              jnp.concatenate([vb, v_ctx], axis=1), scale, bias=bias, sink=sink)

    ob = lax.map(block, (jnp.arange(nb), qb))
    return ob.transpose(1, 0, 2, 3, 4, 5).reshape(B, S, H, v.shape[-1])


def even_mixer(h, w_in, g_qn, g_kn, g_cq, w_uq, g_ckv, w_ukv, w_out, rope, ctx):
    B, T, _ = h.shape
    q_a, k_a, v_a, cq, ckv, kpe = jnp.split(h @ w_in, SPLIT_E, axis=-1)
    q_a = rmsnorm(q_a.reshape(B, T, N_HEADS_A, HEAD_DIM), g_qn)
    k_a = rmsnorm(k_a.reshape(B, T, N_KV_A, HEAD_DIM), g_kn)
    v_a = v_a.reshape(B, T, N_KV_A, HEAD_DIM)
    q_b = (rmsnorm(cq, g_cq) @ w_uq).reshape(B, T, N_HEADS_B, QK_NOPE + QK_ROPE)
    q_b_nope, q_b_pe = q_b[..., :QK_NOPE], q_b[..., QK_NOPE:]
    ckv = rmsnorm(ckv, g_ckv)
    if rope is not None:
        cos_a, sin_a, cos_b, sin_b = rope
        q_a = apply_rope(q_a, cos_a, sin_a)
        k_a = apply_rope(k_a, cos_a, sin_a)
        q_b_pe = apply_rope(q_b_pe, cos_b, sin_b)
        kpe = apply_rope(kpe[:, :, None, :], cos_b, sin_b)[:, :, 0, :]
    own = (k_a, v_a, ckv, kpe)
    if ctx is None:
        k_all, v_all, ckv_all, kpe_all = own
    else:
        k_all = jnp.concatenate([k_a, ctx[0]], axis=1)
        v_all = jnp.concatenate([v_a, ctx[1]], axis=1)
        ckv_all = jnp.concatenate([ckv, ctx[2]], axis=1)
        kpe_all = jnp.concatenate([kpe, ctx[3]], axis=1)
    o_a = dense_attention_blocked(q_a, k_all, v_all, HEAD_DIM ** -0.5)
    tk = ckv_all.shape[1]
    kv = (ckv_all @ w_ukv).reshape(B, tk, N_HEADS_B, QK_NOPE + V_DIM_B)
    k_b = jnp.concatenate([kv[..., :QK_NOPE],
                           jnp.broadcast_to(kpe_all[:, :, None, :], (B, tk, N_HEADS_B, QK_ROPE))], axis=-1)
    v_b = kv[..., QK_NOPE:]
    o_b = dense_attention_blocked(jnp.concatenate([q_b_nope, q_b_pe], axis=-1), k_b, v_b,
                                  (QK_NOPE + QK_ROPE) ** -0.5)
    o = jnp.concatenate([o_a.reshape(B, T, -1), o_b.reshape(B, T, -1)], axis=-1) @ w_out
    return o, own


def odd_mixer(h, w_in, sink, w_out, rope, ctx):
    B, T, _ = h.shape
    q, k, v = jnp.split(h @ w_in, SPLIT_O, axis=-1)
    q = q.reshape(B, T, N_HEADS_C, HEAD_DIM)
    k = k.reshape(B, T, N_KV_C, HEAD_DIM)
    v = v.reshape(B, T, N_KV_C, HEAD_DIM)
    if rope is not None:
        cos_c, sin_c = rope
        q = apply_rope(q, cos_c, sin_c)
        k = apply_rope(k, cos_c, sin_c)
    sk = sink.reshape(N_KV_C, N_HEADS_C // N_KV_C)
    if ctx is None:
        o = dense_attention_blocked(q, k, v, HEAD_DIM ** -0.5, sink=sk)
    else:
        o = window_attention_with_context(q, k, v, ctx[0], ctx[1], HEAD_DIM ** -0.5, sk)
    return o.reshape(B, T, -1) @ w_out, (k, v)


def conv_ffn(h, w_up, conv_w, conv_b, w_down):
    gate, val = jnp.split(h @ w_up, 2, axis=-1)
    gp = jnp.pad(gate, ((0, 0), (1, 1), (0, 0)))
    gate = gp[:, :-2] * conv_w[0] + gp[:, 1:-1] * conv_w[1] + gp[:, 2:] * conv_w[2] + conv_b
    return (jax.nn.silu(gate) * val) @ w_down


def trunk(x, cond, W, rope_e, rope_o, cache):
    states = ([], [], [], [], [], [])
    for l in range(DEPTH):
        sh1, sc1, g1, sh2, sc2, g2 = modulation(cond, W['w_mod'][l], W['b_mod'][l])
        h = rmsnorm(x, W['g_mix_norm'][l]) * (1 + sc1) + sh1
        if l % 2 == 0:
            e = l // 2
            ctx = None if cache is None else (cache[0][:, e], cache[1][:, e], cache[2][:, e], cache[3][:, e])
            o, own = even_mixer(h, W['w_in_e'][e], W['g_qnorm_a'][e], W['g_knorm_a'][e], W['g_cq_b'][e],
                                W['w_uq_b'][e], W['g_ckv_b'][e], W['w_ukv_b'][e], W['w_out_e'][e], rope_e, ctx)
            if cache is None:
                for s, t in zip(states[:4], own):
                    s.append(t)
        else:
            od = l // 2
            ctx = None if cache is None else (cache[4][:, od], cache[5][:, od])
            o, own = odd_mixer(h, W['w_in_o'][od], W['sink_c'][od], W['w_out_o'][od], rope_o, ctx)
            if cache is None:
                for s, t in zip(states[4:], own):
                    s.append(t)
        x = x + g1 * o
        h = rmsnorm(x, W['g_ffn_norm'][l]) * (1 + sc2) + sh2
        x = x + g2 * conv_ffn(h, W['w_up'][l], W['conv_w'][l], W['conv_b'][l], W['w_down'][l])
    return rmsnorm(x, W['g_final']), states


def setup_inputs(seed: int = 0) -> dict:
    key = jax.random.key(seed)
    ks = jax.random.split(key, 32)

    def nrm(k, shape, scale=1.0):
        return jax.random.normal(k, shape, jnp.float32) * scale

    def gain(k, shape):
        return 1.0 + 0.1 * jax.random.normal(k, shape, jnp.float32)

    return {
        'x_prompt': nrm(ks[0], (BATCH, SEQ, D_MODEL)),
        'x_sample': nrm(ks[1], (DEC_BATCH, DEC_SEQ, D_MODEL)),
        'cache_a_k': nrm(ks[2], (DEC_BATCH, N_EVEN, PAST_LEN, N_KV_A, HEAD_DIM)),
        'cache_a_v': nrm(ks[3], (DEC_BATCH, N_EVEN, PAST_LEN, N_KV_A, HEAD_DIM)),
        'cache_b_ckv': nrm(ks[4], (DEC_BATCH, N_EVEN, PAST_LEN, KV_LORA)),
        'cache_b_kpe': nrm(ks[5], (DEC_BATCH, N_EVEN, PAST_LEN, QK_ROPE)),
        'cache_c_k': nrm(ks[6], (DEC_BATCH, N_ODD, PAST_LEN, N_KV_C, HEAD_DIM)),
        'cache_c_v': nrm(ks[7], (DEC_BATCH, N_ODD, PAST_LEN, N_KV_C, HEAD_DIM)),
        'c': nrm(ks[8], (DEC_BATCH, D_MODEL)),
        'c_ctx': nrm(ks[9], (D_MODEL,)),
        'w_mod': nrm(ks[10], (DEPTH, D_MODEL, 6 * D_MODEL), 0.5 * D_MODEL ** -0.5),
        'b_mod': nrm(ks[11], (DEPTH, 6 * D_MODEL), 0.01),
        'g_mix_norm': gain(ks[12], (DEPTH, D_MODEL)),
        'g_ffn_norm': gain(ks[13], (DEPTH, D_MODEL)),
        'w_in_e': nrm(ks[14], (N_EVEN, D_MODEL, IN_E), D_MODEL ** -0.5),
        'g_qnorm_a': gain(ks[15], (N_EVEN, HEAD_DIM)),
        'g_knorm_a': gain(ks[16], (N_EVEN, HEAD_DIM)),
        'g_cq_b': gain(ks[17], (N_EVEN, Q_LORA)),
        'w_uq_b': nrm(ks[18], (N_EVEN, Q_LORA, N_HEADS_B * (QK_NOPE + QK_ROPE)), Q_LORA ** -0.5),
        'g_ckv_b': gain(ks[19], (N_EVEN, KV_LORA)),
        'w_ukv_b': nrm(ks[20], (N_EVEN, KV_LORA, N_HEADS_B * (QK_NOPE + V_DIM_B)), KV_LORA ** -0.5),
        'w_out_e': nrm(ks[21], (N_EVEN, MIX_E, D_MODEL), MIX_E ** -0.5),
        'w_in_o': nrm(ks[22], (N_ODD, D_MODEL, IN_O), D_MODEL ** -0.5),
        'sink_c': nrm(ks[23], (N_ODD, N_HEADS_C)),
        'w_out_o': nrm(ks[24], (N_ODD, MIX_O, D_MODEL), MIX_O ** -0.5),
        'w_up': nrm(ks[25], (DEPTH, D_MODEL, 2 * D_FF), D_MODEL ** -0.5),
        'conv_w': nrm(ks[26], (DEPTH, CONV_W, D_FF), CONV_W ** -0.5),
        'conv_b': nrm(ks[27], (DEPTH, D_FF), 0.01),
        'w_down': nrm(ks[28], (DEPTH, D_FF, D_MODEL), D_FF ** -0.5),
        'g_final': gain(ks[29], (D_MODEL,)),
    }


def reference(x_prompt, x_sample, cache_a_k, cache_a_v, cache_b_ckv, cache_b_kpe, cache_c_k, cache_c_v,
              c, c_ctx, w_mod, b_mod, g_mix_norm, g_ffn_norm, w_in_e, g_qnorm_a, g_knorm_a, g_cq_b,
              w_uq_b, g_ckv_b, w_ukv_b, w_out_e, w_in_o, sink_c, w_out_o, w_up, conv_w, conv_b,
              w_down, g_final):
    W = {'w_mod': w_mod, 'b_mod': b_mod, 'g_mix_norm': g_mix_norm, 'g_ffn_norm': g_ffn_norm,
         'w_in_e': w_in_e, 'g_qnorm_a': g_qnorm_a, 'g_knorm_a': g_knorm_a, 'g_cq_b': g_cq_b,
         'w_uq_b': w_uq_b, 'g_ckv_b': g_ckv_b, 'w_ukv_b': w_ukv_b, 'w_out_e': w_out_e,
         'w_in_o': w_in_o, 'sink_c': sink_c, 'w_out_o': w_out_o, 'w_up': w_up, 'conv_w': conv_w,
         'conv_b': conv_b, 'w_down': w_down, 'g_final': g_final}

    y_prompt, st = trunk(x_prompt, c_ctx, W, None, None, None)
    new_a_k = jnp.stack(st[0], axis=1)
    new_a_v = jnp.stack(st[1], axis=1)
    new_b_ckv = jnp.stack(st[2], axis=1)
    new_b_kpe = jnp.stack(st[3], axis=1)
    new_c_k = jnp.stack(st[4], axis=1)
    new_c_v = jnp.stack(st[5], axis=1)

    n_lat = x_sample.shape[1]
    cos_a, sin_a = axial_rope_tables(n_lat, HEAD_DIM)
    cos_b, sin_b = axial_rope_tables(n_lat, QK_ROPE)
    cache = (cache_a_k, cache_a_v, cache_b_ckv, cache_b_kpe, cache_c_k, cache_c_v)
    y_sample, _ = trunk(x_sample, c, W, (cos_a, sin_a, cos_b, sin_b), (cos_a, sin_a), cache)

    return (y_prompt, y_sample, new_a_k, new_a_v, new_b_ckv, new_b_kpe, new_c_k, new_c_v)
```

```python
import functools

import jax
import jax.numpy as jnp
from jax import lax
from jax.experimental import pallas as pl
from jax.experimental.pallas import tpu as pltpu

F32 = jnp.float32
BF16 = jnp.bfloat16

D_MODEL = 1024
N_CTX_SEQ = 16
CTX_LEN = 256
N_LAT_SEQ = 2
LAT_LEN = 2048
PAST_LEN = 512
GRID_W = 64
ROPE_THETA = 10000.0
NORM_EPS = 1e-6
WINDOW = 128
NEG_INF = -1e30
HEAD_DIM = 64
N_HEADS_A, N_KV_A = 8, 2
N_HEADS_B = 8
Q_LORA, KV_LORA = 384, 256
QK_NOPE, QK_ROPE, V_DIM_B = 64, 32, 64
N_HEADS_C, N_KV_C = 16, 2
D_FF = 2816

N_CTX_TOK = N_CTX_SEQ * CTX_LEN
N_TOK = N_CTX_TOK + N_LAT_SEQ * LAT_LEN
TM = 256
N_TILES = N_TOK // TM
N_CTX_TILES = N_CTX_TOK // TM
LAT_TILES = LAT_LEN // TM
HALO = 16
FF_CHUNK = 256
N_FF_CHUNKS = D_FF // FF_CHUNK
KB_PAD = 128
VMEM_LIMIT = 56 * 1024 * 1024


def _dot(a, b):
    return jnp.dot(a, b, preferred_element_type=F32)


def _dot_nt(a, b):
    return lax.dot_general(a, b, (((1,), (1,)), ((), ())), preferred_element_type=F32)


def _rms(x, g):
    return x * lax.rsqrt(jnp.mean(x * x, axis=-1, keepdims=True) + NORM_EPS) * g


def _head_rms(x, g, ones_ref):
    w = x.shape[1]
    sq = x * x
    hi = sq.astype(BF16)
    lo = (sq - hi.astype(F32)).astype(BF16)
    parts = []
    for c in range(0, w, 256):
        cw = min(256, w - c)
        ones = ones_ref[0:cw, 0:cw]
        parts.append(_dot(hi[:, c:c + cw], ones) + _dot(lo[:, c:c + cw], ones))
    ssum = parts[0] if len(parts) == 1 else jnp.concatenate(parts, axis=1)
    return x * lax.rsqrt(ssum * (1.0 / HEAD_DIM) + NORM_EPS) * g


def _swap_pairs(x):
    w = x.shape[1]
    up = pltpu.roll(x, w - 1, axis=1)
    dn = pltpu.roll(x, 1, axis=1)
    lane = lax.broadcasted_iota(jnp.int32, x.shape, 1)
    return jnp.where((lane & 1) == 0, up, dn)


def _rope(x, cos, sin_signed, reps):
    if reps > 1:
        cos = jnp.concatenate([cos] * reps, axis=1)
        sin_signed = jnp.concatenate([sin_signed] * reps, axis=1)
    return x * cos + _swap_pairs(x) * sin_signed


def _cond_row(i):
    return jnp.where(i < N_CTX_TILES, 0, 1 + (i - N_CTX_TILES) // LAT_TILES)


def _rope_block(i):
    return jnp.where(i < N_CTX_TILES, LAT_TILES, (i - N_CTX_TILES) % LAT_TILES)


def _const_spec(shape):
    zeros = (0,) * len(shape)
    return pl.BlockSpec(shape, lambda *_: zeros)


def _mod_kernel(cond_ref, w_ref, b_ref, o_ref):
    c = cond_ref[...]
    s = c * jax.nn.sigmoid(c)
    o_ref[0] = jnp.dot(s, w_ref[0], preferred_element_type=F32,
                       precision=lax.Precision.HIGHEST) + b_ref[0]


def _modulation(cond8, w_mod, b_mod):
    depth, _, n = w_mod.shape
    tn = 1536
    return pl.pallas_call(
        _mod_kernel,
        grid=(depth, n // tn),
        in_specs=[
            pl.BlockSpec((8, D_MODEL), lambda l, j: (0, 0)),
            pl.BlockSpec((1, D_MODEL, tn), lambda l, j: (l, 0, j)),
            pl.BlockSpec((1, 1, tn), lambda l, j: (l, 0, j)),
        ],
        out_specs=pl.BlockSpec((1, 8, tn), lambda l, j: (l, 0, j)),
        out_shape=jax.ShapeDtypeStruct((depth, 8, n), F32),
        compiler_params=pltpu.CompilerParams(
            dimension_semantics=("arbitrary", "arbitrary"), vmem_limit_bytes=VMEM_LIMIT),
        name="modulation",
    )(cond8, w_mod, b_mod.reshape(depth, 1, n))


def _pre0_kernel(x_ref, mod_ref, gmix_ref, win_ref, gq_ref, gk_ref, gcq_ref, gckv_ref,
                 wuq_ref, wukk_ref, wukv_ref, cosa_ref, sina_ref, cosb_ref, sinb_ref, ones_ref,
                 qa_ref, ka_ref, va_ref, qb_ref, kb_ref, vb_ref,
                 nak_ref, nav_ref, nckv_ref, nkpe_ref):
    i = pl.program_id(0)
    x = x_ref[...]
    sh1 = mod_ref[0:1, :]
    sc1 = mod_ref[1:2, :]
    h = _rms(x, gmix_ref[...]) * (1.0 + sc1) + sh1
    proj = _dot(h.astype(BF16), win_ref[...])
    qa = _head_rms(proj[:, 0:512], gq_ref[...], ones_ref)
    ka = _head_rms(proj[:, 512:640], gk_ref[...], ones_ref)
    va = proj[:, 640:768]
    cq = _rms(proj[:, 768:1152], gcq_ref[...])
    ckv = _rms(proj[:, 1152:1408], gckv_ref[...])
    kpe = proj[:, 1408:1536]
    qb = _dot(cq.astype(BF16), wuq_ref[...])

    cosa, sina = cosa_ref[...], sina_ref[...]
    cosb, sinb = cosb_ref[...], sinb_ref[...]
    qa = _rope(qa, cosa, sina, 4)
    ka = _rope(ka, cosa, sina, 1)
    qb = _rope(qb, cosb, sinb, 8)
    kpe = _rope(kpe, cosb, sinb, 1)

    ckv_b = ckv.astype(BF16)
    kbn = _dot(ckv_b, wukk_ref[...])
    vb = _dot(ckv_b, wukv_ref[...])

    qa_s = qa * (HEAD_DIM ** -0.5)
    qb_s = qb * ((QK_NOPE + QK_ROPE) ** -0.5)
    for hh in range(N_HEADS_A):
        qa_ref[hh] = qa_s[:, hh * 64:(hh + 1) * 64].astype(BF16)
    for hh in range(N_KV_A):
        ka_ref[hh] = ka[:, hh * 64:(hh + 1) * 64].astype(BF16)
        va_ref[hh] = va[:, hh * 64:(hh + 1) * 64].astype(BF16)
    for hh in range(N_HEADS_B):
        qb_ref[hh] = qb_s[:, hh * KB_PAD:(hh + 1) * KB_PAD].astype(BF16)
        kb_ref[hh] = (kbn[:, hh * KB_PAD:(hh + 1) * KB_PAD] + kpe).astype(BF16)
        vb_ref[hh] = vb[:, hh * 64:(hh + 1) * 64].astype(BF16)

    @pl.when(i < N_CTX_TILES)
    def _():
        nak_ref[...] = ka
        nav_ref[...] = va
        nckv_ref[...] = ckv
        nkpe_ref[...] = kpe[:, QK_NOPE:QK_NOPE + QK_ROPE]


def _pre0(x, mod, layer, gmix, win, gq, gk, gcq, gckv, wuq, wukk, wukv, cosa, sina, cosb, sinb, ones):
    tile = lambda i: (i, 0)
    head_tile = lambda i: (0, i, 0)
    ctx_tile = lambda i: (jnp.minimum(i, N_CTX_TILES - 1), 0)
    rope_tile = lambda i: (_rope_block(i), 0)
    in_specs = [
        pl.BlockSpec((TM, D_MODEL), tile),
        pl.BlockSpec((None, 6, D_MODEL), lambda i: (layer * 8 + _cond_row(i), 0, 0)),
        _const_spec((1, D_MODEL)),
        _const_spec(win.shape),
        _const_spec(gq.shape), _const_spec(gk.shape), _const_spec(gcq.shape), _const_spec(gckv.shape),
        _const_spec(wuq.shape), _const_spec(wukk.shape), _const_spec(wukv.shape),
        pl.BlockSpec((TM, 128), rope_tile), pl.BlockSpec((TM, 128), rope_tile),
        pl.BlockSpec((TM, 128), rope_tile), pl.BlockSpec((TM, 128), rope_tile),
        _const_spec(ones.shape),
    ]
    out_shape = [
        jax.ShapeDtypeStruct((N_HEADS_A, N_TOK, 64), BF16),
        jax.ShapeDtypeStruct((N_KV_A, N_TOK, 64), BF16),
        jax.ShapeDtypeStruct((N_KV_A, N_TOK, 64), BF16),
        jax.ShapeDtypeStruct((N_HEADS_B, N_TOK, KB_PAD), BF16),
        jax.ShapeDtypeStruct((N_HEADS_B, N_TOK, KB_PAD), BF16),
        jax.ShapeDtypeStruct((N_HEADS_B, N_TOK, 64), BF16),
        jax.ShapeDtypeStruct((N_CTX_TOK, 128), F32),
        jax.ShapeDtypeStruct((N_CTX_TOK, 128), F32),
        jax.ShapeDtypeStruct((N_CTX_TOK, KV_LORA), F32),
        jax.ShapeDtypeStruct((N_CTX_TOK, QK_ROPE), F32),
    ]
    out_specs = [
        pl.BlockSpec((N_HEADS_A, TM, 64), head_tile),
        pl.BlockSpec((N_KV_A, TM, 64), head_tile),
        pl.BlockSpec((N_KV_A, TM, 64), head_tile),
        pl.BlockSpec((N_HEADS_B, TM, KB_PAD), head_tile),
        pl.BlockSpec((N_HEADS_B, TM, KB_PAD), head_tile),
        pl.BlockSpec((N_HEADS_B, TM, 64), head_tile),
        pl.BlockSpec((TM, 128), ctx_tile),
        pl.BlockSpec((TM, 128), ctx_tile),
        pl.BlockSpec((TM, KV_LORA), ctx_tile),
        pl.BlockSpec((TM, QK_ROPE), ctx_tile),
    ]
    return pl.pallas_call(
        _pre0_kernel, grid=(N_TILES,), in_specs=in_specs, out_specs=out_specs, out_shape=out_shape,
        compiler_params=pltpu.CompilerParams(
            dimension_semantics=("arbitrary",), vmem_limit_bytes=VMEM_LIMIT),
        name="pre0",
    )(x, mod, gmix, win, gq, gk, gcq, gckv, wuq, wukk, wukv, cosa, sina, cosb, sinb, ones)


def _pre1_kernel(x_ref, mod_ref, gmix_ref, win_ref, cosa_ref, sina_ref,
                 q_ref, k_ref, v_ref, nk_ref, nv_ref):
    i = pl.program_id(0)
    x = x_ref[...]
    sh1 = mod_ref[0:1, :]
    sc1 = mod_ref[1:2, :]
    h = _rms(x, gmix_ref[...]) * (1.0 + sc1) + sh1
    proj = _dot(h.astype(BF16), win_ref[...])
    cosa, sina = cosa_ref[...], sina_ref[...]
    q = _rope(proj[:, 0:1024], cosa, sina, 8) * (HEAD_DIM ** -0.5)
    k = _rope(proj[:, 1024:1152], cosa, sina, 1)
    v = proj[:, 1152:1280]
    for hh in range(N_HEADS_C):
        q_ref[hh] = q[:, hh * 64:(hh + 1) * 64].astype(BF16)
    for hh in range(N_KV_C):
        k_ref[hh] = k[:, hh * 64:(hh + 1) * 64].astype(BF16)
        v_ref[hh] = v[:, hh * 64:(hh + 1) * 64].astype(BF16)

    @pl.when(i < N_CTX_TILES)
    def _():
        nk_ref[...] = k
        nv_ref[...] = v


def _pre1(x, mod, layer, gmix, win, cosa, sina):
    tile = lambda i: (i, 0)
    head_tile = lambda i: (0, i, 0)
    ctx_tile = lambda i: (jnp.minimum(i, N_CTX_TILES - 1), 0)
    rope_tile = lambda i: (_rope_block(i), 0)
    in_specs = [
        pl.BlockSpec((TM, D_MODEL), tile),
        pl.BlockSpec((None, 6, D_MODEL), lambda i: (layer * 8 + _cond_row(i), 0, 0)),
        _const_spec((1, D_MODEL)),
        _const_spec(win.shape),
        pl.BlockSpec((TM, 128), rope_tile), pl.BlockSpec((TM, 128), rope_tile),
    ]
    out_shape = [
        jax.ShapeDtypeStruct((N_HEADS_C, N_TOK, 64), BF16),
        jax.ShapeDtypeStruct((N_KV_C, N_TOK, 64), BF16),
        jax.ShapeDtypeStruct((N_KV_C, N_TOK, 64), BF16),
        jax.ShapeDtypeStruct((N_CTX_TOK, 128), F32),
        jax.ShapeDtypeStruct((N_CTX_TOK, 128), F32),
    ]
    out_specs = [
        pl.BlockSpec((N_HEADS_C, TM, 64), head_tile),
        pl.BlockSpec((N_KV_C, TM, 64), head_tile),
        pl.BlockSpec((N_KV_C, TM, 64), head_tile),
        pl.BlockSpec((TM, 128), ctx_tile),
        pl.BlockSpec((TM, 128), ctx_tile),
    ]
    return pl.pallas_call(
        _pre1_kernel, grid=(N_TILES,), in_specs=in_specs, out_specs=out_specs, out_shape=out_shape,
        compiler_params=pltpu.CompilerParams(
            dimension_semantics=("arbitrary",), vmem_limit_bytes=VMEM_LIMIT),
        name="pre1",
    )(x, mod, gmix, win, cosa, sina)


def _ctx_kernel(ak_ref, av_ref, ckv_ref, kpe_ref, ck_ref, cv_ref, wukk_ref, wukv_ref,
                ka_ref, va_ref, kb_ref, vb_ref, kc_ref, vc_ref):
    ak, av, ck, cv = ak_ref[...], av_ref[...], ck_ref[...], cv_ref[...]
    for hh in range(2):
        sl = slice(hh * 64, (hh + 1) * 64)
        ka_ref[hh] = ak[:, sl].astype(BF16)
        va_ref[hh] = av[:, sl].astype(BF16)
        kc_ref[hh] = ck[:, sl].astype(BF16)
        vc_ref[hh] = cv[:, sl].astype(BF16)
    ckv_b = ckv_ref[...].astype(BF16)
    kbn = _dot(ckv_b, wukk_ref[...])
    vb = _dot(ckv_b, wukv_ref[...])
    kpe = kpe_ref[...]
    for hh in range(N_HEADS_B):
        kb_ref[hh] = (kbn[:, hh * KB_PAD:(hh + 1) * KB_PAD] + kpe).astype(BF16)
        vb_ref[hh] = vb[:, hh * 64:(hh + 1) * 64].astype(BF16)


def _ctx_prep(ak, av, ckv, kpe_pad, ck, cv, wukk, wukv):
    n = N_LAT_SEQ * PAST_LEN
    row = lambda b: (b, 0)
    head_row = lambda b: (0, b, 0)
    in_specs = [
        pl.BlockSpec((PAST_LEN, 128), row), pl.BlockSpec((PAST_LEN, 128), row),
        pl.BlockSpec((PAST_LEN, KV_LORA), row), pl.BlockSpec((PAST_LEN, KB_PAD), row),
        pl.BlockSpec((PAST_LEN, 128), row), pl.BlockSpec((PAST_LEN, 128), row),
        _const_spec(wukk.shape), _const_spec(wukv.shape),
    ]
    out_shape = [
        jax.ShapeDtypeStruct((2, n, 64), BF16), jax.ShapeDtypeStruct((2, n, 64), BF16),
        jax.ShapeDtypeStruct((N_HEADS_B, n, KB_PAD), BF16), jax.ShapeDtypeStruct((N_HEADS_B, n, 64), BF16),
        jax.ShapeDtypeStruct((2, n, 64), BF16), jax.ShapeDtypeStruct((2, n, 64), BF16),
    ]
    out_specs = [
        pl.BlockSpec((2, PAST_LEN, 64), head_row), pl.BlockSpec((2, PAST_LEN, 64), head_row),
        pl.BlockSpec((N_HEADS_B, PAST_LEN, KB_PAD), head_row), pl.BlockSpec((N_HEADS_B, PAST_LEN, 64), head_row),
        pl.BlockSpec((2, PAST_LEN, 64), head_row), pl.BlockSpec((2, PAST_LEN, 64), head_row),
    ]
    return pl.pallas_call(
        _ctx_kernel, grid=(N_LAT_SEQ,), in_specs=in_specs, out_specs=out_specs, out_shape=out_shape,
        compiler_params=pltpu.CompilerParams(
            dimension_semantics=("arbitrary",), vmem_limit_bytes=VMEM_LIMIT),
        name="ctx_prep",
    )(ak, av, ckv, kpe_pad, ck, cv, wukk, wukv)


def _attn_kernel(*refs, n_kv, group, tq, has_ctx, has_sink, window, seq_len):
    q_ref, k_ref, v_ref = refs[0:3]
    pos = 3
    if has_ctx:
        kc_ref, vc_ref = refs[3:5]
        pos = 5
    if has_sink:
        sink_ref = refs[pos]
        pos += 1
    o_ref = refs[pos]
    j = pl.program_id(1)
    rows = group * tq
    dk = q_ref.shape[-1]
    if window:
        band = tq + 2 * window
        start = pl.multiple_of(jnp.clip(j * tq - window, 0, seq_len - band), 128)
        row = lax.broadcasted_iota(jnp.int32, (rows, band), 0) & (tq - 1)
        col = lax.broadcasted_iota(jnp.int32, (rows, band), 1)
        rel = (start + col) - (j * tq + row)
        in_window = jnp.abs(rel) <= window
    outs = []
    for hk in range(n_kv):
        q = q_ref[hk * group:(hk + 1) * group].reshape(rows, dk)
        if window:
            k = k_ref[hk, pl.ds(start, band), :]
            v = v_ref[hk, pl.ds(start, band), :]
        else:
            k = k_ref[hk]
            v = v_ref[hk]
        s = _dot_nt(q, k)
        if window:
            s = jnp.where(in_window, s, NEG_INF)
        m = jnp.max(s, axis=-1, keepdims=True)
        if has_ctx:
            sc = _dot_nt(q, kc_ref[hk])
            m = jnp.maximum(m, jnp.max(sc, axis=-1, keepdims=True))
        if has_sink:
            sk = jnp.concatenate(
                [jnp.full((tq, 1), sink_ref[hk * group + g], F32) for g in range(group)], axis=0)
            m = jnp.maximum(m, sk)
        p = jnp.exp(s - m)
        l = jnp.sum(p, axis=-1, keepdims=True)
        o = _dot(p.astype(BF16), v)
        if has_ctx:
            pc = jnp.exp(sc - m)
            l = l + jnp.sum(pc, axis=-1, keepdims=True)
            o = o + _dot(pc.astype(BF16), vc_ref[hk])
        if has_sink:
            l = l + jnp.exp(sk - m)
        o = o / l
        for g in range(group):
            outs.append(o[g * tq:(g + 1) * tq])
    o_ref[...] = jnp.concatenate(outs, axis=1).astype(BF16)


def _attention(q, k, v, kc, vc, sink, *, n_seq, seq_len, tok_base, tq, window, name):
    n_q, _, dk = q.shape
    n_kv, _, dv = v.shape
    group = n_q // n_kv
    n_qt = seq_len // tq
    q_base = tok_base // tq
    s_base = tok_base // seq_len
    has_ctx = kc is not None
    has_sink = sink is not None
    in_specs = [
        pl.BlockSpec((n_q, tq, dk), lambda b, j: (0, q_base + b * n_qt + j, 0)),
        pl.BlockSpec((n_kv, seq_len, dk), lambda b, j: (0, s_base + b, 0)),
        pl.BlockSpec((n_kv, seq_len, dv), lambda b, j: (0, s_base + b, 0)),
    ]
    args = [q, k, v]
    if has_ctx:
        in_specs += [
            pl.BlockSpec((n_kv, PAST_LEN, dk), lambda b, j: (0, b, 0)),
            pl.BlockSpec((n_kv, PAST_LEN, dv), lambda b, j: (0, b, 0)),
        ]
        args += [kc, vc]
    if has_sink:
        in_specs.append(pl.BlockSpec(memory_space=pltpu.SMEM))
        args.append(sink)
    kern = functools.partial(_attn_kernel, n_kv=n_kv, group=group, tq=tq, has_ctx=has_ctx,
                             has_sink=has_sink, window=window, seq_len=seq_len)
    return pl.pallas_call(
        kern, grid=(n_seq, n_qt), in_specs=in_specs,
        out_specs=pl.BlockSpec((tq, n_q * dv), lambda b, j: (b * n_qt + j, 0)),
        out_shape=jax.ShapeDtypeStruct((n_seq * seq_len, n_q * dv), BF16),
        compiler_params=pltpu.CompilerParams(
            dimension_semantics=("arbitrary", "arbitrary"), vmem_limit_bytes=VMEM_LIMIT),
        name=name,
    )(*args)


def _post_kernel(x_ref, xp_ref, xn_ref, o_ref, op_ref, on_ref, mod_ref, gffn_ref, wout_ref,
                 wup_ref, cw_ref, cb_ref, wdown_ref, gfin_ref, y_ref, *, final):
    i = pl.program_id(0)
    lat = i - N_CTX_TILES
    has_prev = jnp.logical_and(lat >= 0, lat % LAT_TILES != 0)
    has_next = jnp.logical_and(lat >= 0, lat % LAT_TILES != LAT_TILES - 1)
    g1 = mod_ref[2:3, :]
    sh2 = mod_ref[3:4, :]
    sc2 = mod_ref[4:5, :]
    g2 = mod_ref[5:6, :]

    xe = jnp.concatenate([xp_ref[...], x_ref[...], xn_ref[...]], axis=0)
    oe = jnp.concatenate([op_ref[...], o_ref[...], on_ref[...]], axis=0)
    x1e = xe + g1 * _dot(oe, wout_ref[...])
    h2e = (_rms(x1e, gffn_ref[...]) * (1.0 + sc2) + sh2).astype(BF16)
    h2 = h2e[HALO:HALO + TM]
    x1 = x1e[HALO:HALO + TM]

    te = TM + 2 * HALO
    erow = lax.broadcasted_iota(jnp.int32, (te, 1), 0)
    keep = jnp.logical_and(jnp.logical_or(erow >= HALO, has_prev),
                           jnp.logical_or(erow < HALO + TM, has_next))
    keep = keep.astype(F32)
    cw = cw_ref[...]
    cb = cb_ref[...]
    acc = jnp.zeros((TM, D_MODEL), F32)
    for c in range(N_FF_CHUNKS):
        lo = c * FF_CHUNK
        ge = _dot(h2e, wup_ref[:, lo:lo + FF_CHUNK]) * keep
        val = _dot(h2, wup_ref[:, D_FF + lo:D_FF + lo + FF_CHUNK])
        g_prev = pltpu.roll(ge, 1, axis=0)[HALO:HALO + TM]
        g_next = pltpu.roll(ge, te - 1, axis=0)[HALO:HALO + TM]
        gate = (g_prev * cw[0:1, lo:lo + FF_CHUNK] + ge[HALO:HALO + TM] * cw[1:2, lo:lo + FF_CHUNK]
                + g_next * cw[2:3, lo:lo + FF_CHUNK] + cb[:, lo:lo + FF_CHUNK])
        act = (gate * jax.nn.sigmoid(gate) * val).astype(BF16)
        acc = acc + _dot(act, wdown_ref[lo:lo + FF_CHUNK, :])
    x2 = x1 + g2 * acc
    if final:
        x2 = _rms(x2, gfin_ref[...])
    y_ref[...] = x2


def _post(x, o, mod, layer, gffn, wout, wup, cw, cb, wdown, gfin, *, final):
    nh = TM // HALO
    nblk = N_TOK // HALO
    tile = lambda i: (i, 0)
    prev = lambda i: (jnp.maximum(i * nh - 1, 0), 0)
    nxt = lambda i: (jnp.minimum((i + 1) * nh, nblk - 1), 0)
    in_specs = [
        pl.BlockSpec((TM, D_MODEL), tile), pl.BlockSpec((HALO, D_MODEL), prev), pl.BlockSpec((HALO, D_MODEL), nxt),
        pl.BlockSpec((TM, D_MODEL), tile), pl.BlockSpec((HALO, D_MODEL), prev), pl.BlockSpec((HALO, D_MODEL), nxt),
        pl.BlockSpec((None, 6, D_MODEL), lambda i: (layer * 8 + _cond_row(i), 0, 0)),
        _const_spec((1, D_MODEL)),
        _const_spec(wout.shape), _const_spec(wup.shape), _const_spec(cw.shape), _const_spec(cb.shape),
        _const_spec(wdown.shape), _const_spec((1, D_MODEL)),
    ]
    return pl.pallas_call(
        functools.partial(_post_kernel, final=final),
        grid=(N_TILES,), in_specs=in_specs,
        out_specs=pl.BlockSpec((TM, D_MODEL), tile),
        out_shape=jax.ShapeDtypeStruct((N_TOK, D_MODEL), F32),
        compiler_params=pltpu.CompilerParams(
            dimension_semantics=("arbitrary",), vmem_limit_bytes=VMEM_LIMIT),
        name="post_final" if final else "post",
    )(x, x, x, o, o, o, mod, gffn, wout, wup, cw, cb, wdown, gfin)


def _rope_tables(rot_dim):
    t = jnp.arange(LAT_LEN)
    row = (t // GRID_W).astype(F32)
    col = (t % GRID_W).astype(F32)
    d_axis = rot_dim // 2
    freqs = ROPE_THETA ** (-jnp.arange(0, d_axis, 2, dtype=F32) / d_axis)
    ang = jnp.concatenate([row[:, None] * freqs, col[:, None] * freqs], axis=-1)
    cos = jnp.repeat(jnp.cos(ang), 2, axis=-1)
    sin = jnp.repeat(jnp.sin(ang), 2, axis=-1) * jnp.tile(jnp.array([-1.0, 1.0], F32), rot_dim // 2)
    if rot_dim == HEAD_DIM:
        cos = jnp.tile(cos, (1, 2))
        sin = jnp.tile(sin, (1, 2))
    else:
        cos = jnp.concatenate([jnp.ones((LAT_LEN, QK_NOPE), F32), cos,
                               jnp.ones((LAT_LEN, 128 - QK_NOPE - rot_dim), F32)], axis=-1)
        sin = jnp.concatenate([jnp.zeros((LAT_LEN, QK_NOPE), F32), sin,
                               jnp.zeros((LAT_LEN, 128 - QK_NOPE - rot_dim), F32)], axis=-1)
    cos = jnp.concatenate([cos, jnp.ones((TM, 128), F32)], axis=0)
    sin = jnp.concatenate([sin, jnp.zeros((TM, 128), F32)], axis=0)
    return cos, sin


def kernel(x_prompt, x_sample, cache_a_k, cache_a_v, cache_b_ckv, cache_b_kpe, cache_c_k, cache_c_v, c, c_ctx, w_mod, b_mod, g_mix_norm, g_ffn_norm, w_in_e, g_qnorm_a, g_knorm_a, g_cq_b, w_uq_b, g_ckv_b, w_ukv_b, w_out_e, w_in_o, sink_c, w_out_o, w_up, conv_w, conv_b, w_down, g_final):
    x0 = jnp.concatenate([x_prompt.reshape(N_CTX_TOK, D_MODEL),
                          x_sample.reshape(N_LAT_SEQ * LAT_LEN, D_MODEL)], axis=0)
    cond8 = jnp.concatenate([c_ctx[None, :], c, jnp.zeros((5, D_MODEL), F32)], axis=0)

    w_e = w_in_e[0]
    kpe_cols = jnp.pad(w_e[:, 1408:1440], ((0, 0), (QK_NOPE, KB_PAD - QK_NOPE - QK_ROPE)))
    win_e = jnp.concatenate([w_e[:, :1408], kpe_cols], axis=1).astype(BF16)
    wuq = jnp.pad(w_uq_b[0].reshape(Q_LORA, N_HEADS_B, QK_NOPE + QK_ROPE),
                  ((0, 0), (0, 0), (0, KB_PAD - QK_NOPE - QK_ROPE))).reshape(Q_LORA, N_HEADS_B * KB_PAD).astype(BF16)
    wukv3 = w_ukv_b[0].reshape(KV_LORA, N_HEADS_B, QK_NOPE + V_DIM_B)
    wukk = jnp.pad(wukv3[:, :, :QK_NOPE], ((0, 0), (0, 0), (0, KB_PAD - QK_NOPE))
                   ).reshape(KV_LORA, N_HEADS_B * KB_PAD).astype(BF16)
    wukv = wukv3[:, :, QK_NOPE:].reshape(KV_LORA, N_HEADS_B * V_DIM_B).astype(BF16)
    gq = jnp.tile(g_qnorm_a[0], N_HEADS_A)[None, :]
    gk = jnp.tile(g_knorm_a[0], N_KV_A)[None, :]
    seg = jnp.arange(256) // HEAD_DIM
    ones = (seg[:, None] == seg[None, :]).astype(BF16)
    cosa, sina = _rope_tables(HEAD_DIM)
    cosb, sinb = _rope_tables(QK_ROPE)

    mod = _modulation(cond8, w_mod, b_mod).reshape(2 * 8, 6, D_MODEL)

    n_past = N_LAT_SEQ * PAST_LEN
    kpe_pad = jnp.pad(cache_b_kpe.reshape(n_past, QK_ROPE), ((0, 0), (QK_NOPE, KB_PAD - QK_NOPE - QK_ROPE)))
    ka_c, va_c, kb_c, vb_c, kc_c, vc_c = _ctx_prep(
        cache_a_k.reshape(n_past, 128), cache_a_v.reshape(n_past, 128),
        cache_b_ckv.reshape(n_past, KV_LORA), kpe_pad,
        cache_c_k.reshape(n_past, 128), cache_c_v.reshape(n_past, 128), wukk, wukv)

    qa, ka, va, qb, kb, vb, nak, nav, nckv, nkpe = _pre0(
        x0, mod, 0, g_mix_norm[0][None, :], win_e, gq, gk, g_cq_b[0][None, :], g_ckv_b[0][None, :],
        wuq, wukk, wukv, cosa, sina, cosb, sinb, ones)
    ctx_kw = dict(n_seq=N_CTX_SEQ, seq_len=CTX_LEN, tok_base=0, tq=CTX_LEN, window=0)
    lat_kw = dict(n_seq=N_LAT_SEQ, seq_len=LAT_LEN, tok_base=N_CTX_TOK, window=0)
    oa = jnp.concatenate([
        _attention(qa, ka, va, None, None, None, name="attn_a_ctx", **ctx_kw),
        _attention(qa, ka, va, ka_c, va_c, None, tq=128, name="attn_a_lat", **lat_kw)], axis=0)
    ob = jnp.concatenate([
        _attention(qb, kb, vb, None, None, None, name="attn_b_ctx", **ctx_kw),
        _attention(qb, kb, vb, kb_c, vb_c, None, tq=256, name="attn_b_lat", **lat_kw)], axis=0)
    o0 = jnp.concatenate([oa, ob], axis=1)
    x1 = _post(x0, o0, mod, 0, g_ffn_norm[0][None, :], w_out_e[0].astype(BF16), w_up[0].astype(BF16),
               conv_w[0], conv_b[0][None, :], w_down[0].astype(BF16), g_final[None, :], final=False)

    qc, kc, vc, nck, ncv = _pre1(x1, mod, 1, g_mix_norm[1][None, :], w_in_o[0].astype(BF16), cosa, sina)
    sink = sink_c[0]
    oc = jnp.concatenate([
        _attention(qc, kc, vc, None, None, sink, name="attn_c_ctx", **ctx_kw),
        _attention(qc, kc, vc, kc_c, vc_c, sink, n_seq=N_LAT_SEQ, seq_len=LAT_LEN, tok_base=N_CTX_TOK,
                   tq=128, window=WINDOW, name="attn_c_lat")], axis=0)
    y = _post(x1, oc, mod, 1, g_ffn_norm[1][None, :], w_out_o[0].astype(BF16), w_up[1].astype(BF16),
              conv_w[1], conv_b[1][None, :], w_down[1].astype(BF16), g_final[None, :], final=True)

    y_prompt = y[:N_CTX_TOK].reshape(N_CTX_SEQ, CTX_LEN, D_MODEL)
    y_sample = y[N_CTX_TOK:].reshape(N_LAT_SEQ, LAT_LEN, D_MODEL)
    return (y_prompt, y_sample,
            nak.reshape(N_CTX_SEQ, 1, CTX_LEN, N_KV_A, HEAD_DIM),
            nav.reshape(N_CTX_SEQ, 1, CTX_LEN, N_KV_A, HEAD_DIM),
            nckv.reshape(N_CTX_SEQ, 1, CTX_LEN, KV_LORA),
            nkpe.reshape(N_CTX_SEQ, 1, CTX_LEN, QK_ROPE),
            nck.reshape(N_CTX_SEQ, 1, CTX_LEN, N_KV_C, HEAD_DIM),
            ncv.reshape(N_CTX_SEQ, 1, CTX_LEN, N_KV_C, HEAD_DIM))
```

```python
import functools

import jax
import jax.numpy as jnp
from jax import lax
from jax.experimental import pallas as pl
from jax.experimental.pallas import tpu as pltpu

F32 = jnp.float32
BF16 = jnp.bfloat16

D_MODEL = 1024
N_CTX_SEQ = 16
CTX_LEN = 256
N_LAT_SEQ = 2
LAT_LEN = 2048
PAST_LEN = 512
GRID_W = 64
ROPE_THETA = 10000.0
NORM_EPS = 1e-6
WINDOW = 128
NEG_INF = -1e30
HEAD_DIM = 64
N_HEADS_A, N_KV_A = 8, 2
N_HEADS_B = 8
Q_LORA, KV_LORA = 384, 256
QK_NOPE, QK_ROPE, V_DIM_B = 64, 32, 64
N_HEADS_C, N_KV_C = 16, 2
D_FF = 2816

N_CTX_TOK = N_CTX_SEQ * CTX_LEN
N_TOK = N_CTX_TOK + N_LAT_SEQ * LAT_LEN
TM = 256
N_TILES = N_TOK // TM
N_CTX_TILES = N_CTX_TOK // TM
LAT_TILES = LAT_LEN // TM
HALO = 16
FF_CHUNK = 256
N_FF_CHUNKS = D_FF // FF_CHUNK
KB_PAD = 128
VT_ROWS = 80
Q_UNIT = 256
KEY_CHUNK = 512
SCORE_LOOKAHEAD = 2
VMEM_LIMIT = 56 * 1024 * 1024


def _dot(a, b):
    return jnp.dot(a, b, preferred_element_type=F32)


def _dot_nt(a, b):
    return lax.dot_general(a, b, (((1,), (1,)), ((), ())), preferred_element_type=F32)


def _rms(x, g):
    return x * lax.rsqrt(jnp.mean(x * x, axis=-1, keepdims=True) + NORM_EPS) * g


def _head_rms(x, g, ones_ref):
    w = x.shape[1]
    sq = x * x
    hi = sq.astype(BF16)
    lo = (sq - hi.astype(F32)).astype(BF16)
    parts = []
    for c in range(0, w, 256):
        cw = min(256, w - c)
        ones = ones_ref[0:cw, 0:cw]
        parts.append(_dot(hi[:, c:c + cw], ones) + _dot(lo[:, c:c + cw], ones))
    ssum = parts[0] if len(parts) == 1 else jnp.concatenate(parts, axis=1)
    return x * lax.rsqrt(ssum * (1.0 / HEAD_DIM) + NORM_EPS) * g


def _swap_pairs(x):
    w = x.shape[1]
    up = pltpu.roll(x, w - 1, axis=1)
    dn = pltpu.roll(x, 1, axis=1)
    lane = lax.broadcasted_iota(jnp.int32, x.shape, 1)
    return jnp.where((lane & 1) == 0, up, dn)


def _rope(x, cos, sin_signed, reps):
    if reps > 1:
        cos = jnp.concatenate([cos] * reps, axis=1)
        sin_signed = jnp.concatenate([sin_signed] * reps, axis=1)
    return x * cos + _swap_pairs(x) * sin_signed


def _store_vt(vt_ref, v, n_heads):
    t = v.shape[0]
    vt = v.T.astype(BF16)
    ones = jnp.ones((VT_ROWS - 64, t), BF16)
    for hh in range(n_heads):
        vt_ref[hh, 0:64, :] = vt[hh * 64:(hh + 1) * 64]
        vt_ref[hh, 64:VT_ROWS, :] = ones


def _cond_row(i):
    return jnp.where(i < N_CTX_TILES, 0, 1 + (i - N_CTX_TILES) // LAT_TILES)


def _rope_block(i):
    return jnp.where(i < N_CTX_TILES, LAT_TILES, (i - N_CTX_TILES) % LAT_TILES)


def _const_spec(shape):
    zeros = (0,) * len(shape)
    return pl.BlockSpec(shape, lambda *_: zeros)


def _mod_kernel(cond_ref, w_ref, b_ref, o_ref):
    c = cond_ref[...]
    s = c * jax.nn.sigmoid(c)
    o_ref[0] = jnp.dot(s, w_ref[0], preferred_element_type=F32,
                       precision=lax.Precision.HIGHEST) + b_ref[0]


def _modulation(cond8, w_mod, b_mod):
    depth, _, n = w_mod.shape
    tn = 1536
    return pl.pallas_call(
        _mod_kernel,
        grid=(depth, n // tn),
        in_specs=[
            pl.BlockSpec((8, D_MODEL), lambda l, j: (0, 0)),
            pl.BlockSpec((1, D_MODEL, tn), lambda l, j: (l, 0, j)),
            pl.BlockSpec((1, 1, tn), lambda l, j: (l, 0, j)),
        ],
        out_specs=pl.BlockSpec((1, 8, tn), lambda l, j: (l, 0, j)),
        out_shape=jax.ShapeDtypeStruct((depth, 8, n), F32),
        compiler_params=pltpu.CompilerParams(
            dimension_semantics=("arbitrary", "arbitrary"), vmem_limit_bytes=VMEM_LIMIT),
        name="modulation",
    )(cond8, w_mod, b_mod.reshape(depth, 1, n))


def _pre0_kernel(x_ref, mod_ref, gmix_ref, win_ref, gq_ref, gk_ref, gcq_ref, gckv_ref,
                 wuq_ref, wukk_ref, wukv_ref, cosa_ref, sina_ref, cosb_ref, sinb_ref, ones_ref,
                 qa_ref, ka_ref, va_ref, qb_ref, kb_ref, vb_ref,
                 nak_ref, nav_ref, nckv_ref, nkpe_ref):
    i = pl.program_id(0)
    x = x_ref[...]
    sh1 = mod_ref[0:1, :]
    sc1 = mod_ref[1:2, :]
    h = _rms(x, gmix_ref[...]) * (1.0 + sc1) + sh1
    proj = _dot(h.astype(BF16), win_ref[...])
    qa = _head_rms(proj[:, 0:512], gq_ref[...], ones_ref)
    ka = _head_rms(proj[:, 512:640], gk_ref[...], ones_ref)
    va = proj[:, 640:768]
    cq = _rms(proj[:, 768:1152], gcq_ref[...])
    ckv = _rms(proj[:, 1152:1408], gckv_ref[...])
    kpe = proj[:, 1408:1536]
    qb = _dot(cq.astype(BF16), wuq_ref[...])

    cosa, sina = cosa_ref[...], sina_ref[...]
    cosb, sinb = cosb_ref[...], sinb_ref[...]
    qa = _rope(qa, cosa, sina, 4)
    ka = _rope(ka, cosa, sina, 1)
    qb = _rope(qb, cosb, sinb, 8)
    kpe = _rope(kpe, cosb, sinb, 1)

    ckv_b = ckv.astype(BF16)
    kbn = _dot(ckv_b, wukk_ref[...])
    vb = _dot(ckv_b, wukv_ref[...])

    qa_s = qa * (HEAD_DIM ** -0.5)
    qb_s = qb * ((QK_NOPE + QK_ROPE) ** -0.5)
    for hh in range(N_HEADS_A):
        qa_ref[hh] = qa_s[:, hh * 64:(hh + 1) * 64].astype(BF16)
    for hh in range(N_KV_A):
        ka_ref[hh] = ka[:, hh * 64:(hh + 1) * 64].astype(BF16)
    _store_vt(va_ref, va, N_KV_A)
    for hh in range(N_HEADS_B):
        qb_ref[hh] = qb_s[:, hh * KB_PAD:(hh + 1) * KB_PAD].astype(BF16)
        kb_ref[hh] = (kbn[:, hh * KB_PAD:(hh + 1) * KB_PAD] + kpe).astype(BF16)
    _store_vt(vb_ref, vb, N_HEADS_B)

    @pl.when(i < N_CTX_TILES)
    def _():
        nak_ref[...] = ka
        nav_ref[...] = va
        nckv_ref[...] = ckv
        nkpe_ref[...] = kpe[:, QK_NOPE:QK_NOPE + QK_ROPE]


def _pre0(x, mod, layer, gmix, win, gq, gk, gcq, gckv, wuq, wukk, wukv, cosa, sina, cosb, sinb, ones):
    tile = lambda i: (i, 0)
    head_tile = lambda i: (0, i, 0)
    vt_tile = lambda i: (0, 0, i)
    ctx_tile = lambda i: (jnp.minimum(i, N_CTX_TILES - 1), 0)
    rope_tile = lambda i: (_rope_block(i), 0)
    in_specs = [
        pl.BlockSpec((TM, D_MODEL), tile),
        pl.BlockSpec((None, 6, D_MODEL), lambda i: (layer * 8 + _cond_row(i), 0, 0)),
        _const_spec((1, D_MODEL)),
        _const_spec(win.shape),
        _const_spec(gq.shape), _const_spec(gk.shape), _const_spec(gcq.shape), _const_spec(gckv.shape),
        _const_spec(wuq.shape), _const_spec(wukk.shape), _const_spec(wukv.shape),
        pl.BlockSpec((TM, 128), rope_tile), pl.BlockSpec((TM, 128), rope_tile),
        pl.BlockSpec((TM, 128), rope_tile), pl.BlockSpec((TM, 128), rope_tile),
        _const_spec(ones.shape),
    ]
    out_shape = [
        jax.ShapeDtypeStruct((N_HEADS_A, N_TOK, 64), BF16),
        jax.ShapeDtypeStruct((N_KV_A, N_TOK, 64), BF16),
        jax.ShapeDtypeStruct((N_KV_A, VT_ROWS, N_TOK), BF16),
        jax.ShapeDtypeStruct((N_HEADS_B, N_TOK, KB_PAD), BF16),
        jax.ShapeDtypeStruct((N_HEADS_B, N_TOK, KB_PAD), BF16),
        jax.ShapeDtypeStruct((N_HEADS_B, VT_ROWS, N_TOK), BF16),
        jax.ShapeDtypeStruct((N_CTX_TOK, 128), F32),
        jax.ShapeDtypeStruct((N_CTX_TOK, 128), F32),
        jax.ShapeDtypeStruct((N_CTX_TOK, KV_LORA), F32),
        jax.ShapeDtypeStruct((N_CTX_TOK, QK_ROPE), F32),
    ]
    out_specs = [
        pl.BlockSpec((N_HEADS_A, TM, 64), head_tile),
        pl.BlockSpec((N_KV_A, TM, 64), head_tile),
        pl.BlockSpec((N_KV_A, VT_ROWS, TM), vt_tile),
        pl.BlockSpec((N_HEADS_B, TM, KB_PAD), head_tile),
        pl.BlockSpec((N_HEADS_B, TM, KB_PAD), head_tile),
        pl.BlockSpec((N_HEADS_B, VT_ROWS, TM), vt_tile),
        pl.BlockSpec((TM, 128), ctx_tile),
        pl.BlockSpec((TM, 128), ctx_tile),
        pl.BlockSpec((TM, KV_LORA), ctx_tile),
        pl.BlockSpec((TM, QK_ROPE), ctx_tile),
    ]
    return pl.pallas_call(
        _pre0_kernel, grid=(N_TILES,), in_specs=in_specs, out_specs=out_specs, out_shape=out_shape,
        compiler_params=pltpu.CompilerParams(
            dimension_semantics=("arbitrary",), vmem_limit_bytes=VMEM_LIMIT),
        name="pre0",
    )(x, mod, gmix, win, gq, gk, gcq, gckv, wuq, wukk, wukv, cosa, sina, cosb, sinb, ones)


def _pre1_kernel(x_ref, mod_ref, gmix_ref, win_ref, cosa_ref, sina_ref,
                 q_ref, k_ref, v_ref, nk_ref, nv_ref):
    i = pl.program_id(0)
    x = x_ref[...]
    sh1 = mod_ref[0:1, :]
    sc1 = mod_ref[1:2, :]
    h = _rms(x, gmix_ref[...]) * (1.0 + sc1) + sh1
    proj = _dot(h.astype(BF16), win_ref[...])
    cosa, sina = cosa_ref[...], sina_ref[...]
    q = _rope(proj[:, 0:1024], cosa, sina, 8) * (HEAD_DIM ** -0.5)
    k = _rope(proj[:, 1024:1152], cosa, sina, 1)
    v = proj[:, 1152:1280]
    for hh in range(N_HEADS_C):
        q_ref[hh] = q[:, hh * 64:(hh + 1) * 64].astype(BF16)
    for hh in range(N_KV_C):
        k_ref[hh] = k[:, hh * 64:(hh + 1) * 64].astype(BF16)
    _store_vt(v_ref, v, N_KV_C)

    @pl.when(i < N_CTX_TILES)
    def _():
        nk_ref[...] = k
        nv_ref[...] = v


def _pre1(x, mod, layer, gmix, win, cosa, sina):
    tile = lambda i: (i, 0)
    head_tile = lambda i: (0, i, 0)
    vt_tile = lambda i: (0, 0, i)
    ctx_tile = lambda i: (jnp.minimum(i, N_CTX_TILES - 1), 0)
    rope_tile = lambda i: (_rope_block(i), 0)
    in_specs = [
        pl.BlockSpec((TM, D_MODEL), tile),
        pl.BlockSpec((None, 6, D_MODEL), lambda i: (layer * 8 + _cond_row(i), 0, 0)),
        _const_spec((1, D_MODEL)),
        _const_spec(win.shape),
        pl.BlockSpec((TM, 128), rope_tile), pl.BlockSpec((TM, 128), rope_tile),
    ]
    out_shape = [
        jax.ShapeDtypeStruct((N_HEADS_C, N_TOK, 64), BF16),
        jax.ShapeDtypeStruct((N_KV_C, N_TOK, 64), BF16),
        jax.ShapeDtypeStruct((N_KV_C, VT_ROWS, N_TOK), BF16),
        jax.ShapeDtypeStruct((N_CTX_TOK, 128), F32),
        jax.ShapeDtypeStruct((N_CTX_TOK, 128), F32),
    ]
    out_specs = [
        pl.BlockSpec((N_HEADS_C, TM, 64), head_tile),
        pl.BlockSpec((N_KV_C, TM, 64), head_tile),
        pl.BlockSpec((N_KV_C, VT_ROWS, TM), vt_tile),
        pl.BlockSpec((TM, 128), ctx_tile),
        pl.BlockSpec((TM, 128), ctx_tile),
    ]
    return pl.pallas_call(
        _pre1_kernel, grid=(N_TILES,), in_specs=in_specs, out_specs=out_specs, out_shape=out_shape,
        compiler_params=pltpu.CompilerParams(
            dimension_semantics=("arbitrary",), vmem_limit_bytes=VMEM_LIMIT),
        name="pre1",
    )(x, mod, gmix, win, cosa, sina)


def _ctx_kernel(ak_ref, av_ref, ckv_ref, kpe_ref, ck_ref, cv_ref, wukk_ref, wukv_ref,
                ka_ref, va_ref, kb_ref, vb_ref, kc_ref, vc_ref):
    ak, av, ck, cv = ak_ref[...], av_ref[...], ck_ref[...], cv_ref[...]
    for hh in range(2):
        sl = slice(hh * 64, (hh + 1) * 64)
        ka_ref[hh] = ak[:, sl].astype(BF16)
        kc_ref[hh] = ck[:, sl].astype(BF16)
    _store_vt(va_ref, av, 2)
    _store_vt(vc_ref, cv, 2)
    ckv_b = ckv_ref[...].astype(BF16)
    kbn = _dot(ckv_b, wukk_ref[...])
    vb = _dot(ckv_b, wukv_ref[...])
    kpe = kpe_ref[...]
    for hh in range(N_HEADS_B):
        kb_ref[hh] = (kbn[:, hh * KB_PAD:(hh + 1) * KB_PAD] + kpe).astype(BF16)
    _store_vt(vb_ref, vb, N_HEADS_B)


def _ctx_prep(ak, av, ckv, kpe_pad, ck, cv, wukk, wukv):
    n = N_LAT_SEQ * PAST_LEN
    row = lambda b: (b, 0)
    head_row = lambda b: (0, b, 0)
    vt_row = lambda b: (0, 0, b)
    in_specs = [
        pl.BlockSpec((PAST_LEN, 128), row), pl.BlockSpec((PAST_LEN, 128), row),
        pl.BlockSpec((PAST_LEN, KV_LORA), row), pl.BlockSpec((PAST_LEN, KB_PAD), row),
        pl.BlockSpec((PAST_LEN, 128), row), pl.BlockSpec((PAST_LEN, 128), row),
        _const_spec(wukk.shape), _const_spec(wukv.shape),
    ]
    out_shape = [
        jax.ShapeDtypeStruct((2, n, 64), BF16), jax.ShapeDtypeStruct((2, VT_ROWS, n), BF16),
        jax.ShapeDtypeStruct((N_HEADS_B, n, KB_PAD), BF16), jax.ShapeDtypeStruct((N_HEADS_B, VT_ROWS, n), BF16),
        jax.ShapeDtypeStruct((2, n, 64), BF16), jax.ShapeDtypeStruct((2, VT_ROWS, n), BF16),
    ]
    out_specs = [
        pl.BlockSpec((2, PAST_LEN, 64), head_row), pl.BlockSpec((2, VT_ROWS, PAST_LEN), vt_row),
        pl.BlockSpec((N_HEADS_B, PAST_LEN, KB_PAD), head_row), pl.BlockSpec((N_HEADS_B, VT_ROWS, PAST_LEN), vt_row),
        pl.BlockSpec((2, PAST_LEN, 64), head_row), pl.BlockSpec((2, VT_ROWS, PAST_LEN), vt_row),
    ]
    return pl.pallas_call(
        _ctx_kernel, grid=(N_LAT_SEQ,), in_specs=in_specs, out_specs=out_specs, out_shape=out_shape,
        compiler_params=pltpu.CompilerParams(
            dimension_semantics=("arbitrary",), vmem_limit_bytes=VMEM_LIMIT),
        name="ctx_prep",
    )(ak, av, ckv, kpe_pad, ck, cv, wukk, wukv)


def _softmax_units(units, lookahead):
    tasks = [(u, c) for u, unit in enumerate(units) for c in range(len(unit["chunks"]))]
    scores = {}

    def emit_scores(t):
        u, c = tasks[t]
        k, _, mask = units[u]["chunks"][c]
        s = _dot_nt(k, units[u]["q"])
        scores[t] = s if mask is None else jnp.where(mask, s, NEG_INF)

    for t in range(min(lookahead, len(tasks))):
        emit_scores(t)
    for t, (u, c) in enumerate(tasks):
        if t + lookahead < len(tasks):
            emit_scores(t + lookahead)
        unit = units[u]
        s = scores.pop(t)
        m, acc = unit["m"], unit["acc"]
        cmax = jnp.max(s, axis=0, keepdims=True)
        m_new = cmax if m is None else jnp.maximum(m, cmax)
        pv = _dot(unit["chunks"][c][1], jnp.exp(s - m_new).astype(BF16))
        unit["acc"] = pv if acc is None else acc * jnp.exp(m - m_new) + pv
        unit["m"] = m_new
    return [unit["acc"][0:64] * (1.0 / unit["acc"][64:65]) for unit in units]


def _attn_kernel(*refs, n_kv, group, tq, seq_len, has_ctx, has_sink, window):
    refs = list(refs)
    q_ref = refs.pop(0)
    if window:
        kp_ref, kc_ref, kn_ref, vp_ref, vc_ref, vn_ref = refs[:6]
        refs = refs[6:]
    else:
        k_ref, vt_ref = refs[:2]
        refs = refs[2:]
    if has_ctx:
        kx_ref, vx_ref = refs[:2]
        refs = refs[2:]
    if has_sink:
        sink_ref = refs.pop(0)
    o_ref = refs.pop(0)

    j = pl.program_id(1)
    dk = q_ref.shape[-1]
    heads_per_unit = Q_UNIT // tq
    lane = lax.broadcasted_iota(jnp.int32, (1, Q_UNIT), 1)
    if window:
        n_band = tq + 2 * window
        krow = lax.broadcasted_iota(jnp.int32, (n_band, Q_UNIT), 0)
        qcol = lax.broadcasted_iota(jnp.int32, (n_band, Q_UNIT), 1) & (tq - 1)
        rel = (krow - window) - qcol
        band_mask = ((jnp.abs(rel) <= window)
                     & ((krow >= window) | (j > 0))
                     & ((krow < window + tq) | (j < seq_len // tq - 1)))
    if has_sink:
        acc0 = jnp.where(lax.broadcasted_iota(jnp.int32, (VT_ROWS, Q_UNIT), 0) >= 64, 1.0, 0.0)

    units = []
    for hk in range(n_kv):
        chunks = []
        if window:
            chunks.append((jnp.concatenate([kp_ref[hk], kc_ref[hk], kn_ref[hk]], axis=0),
                           jnp.concatenate([vp_ref[hk], vc_ref[hk], vn_ref[hk]], axis=1), band_mask))
        else:
            for c in range(0, seq_len, KEY_CHUNK):
                n = min(KEY_CHUNK, seq_len - c)
                chunks.append((k_ref[hk, c:c + n, :], vt_ref[hk, :, c:c + n], None))
        if has_ctx:
            chunks.append((kx_ref[hk], vx_ref[hk], None))
        for u in range(group // heads_per_unit):
            h0 = hk * group + u * heads_per_unit
            unit = dict(q=q_ref[h0:h0 + heads_per_unit].reshape(Q_UNIT, dk), chunks=chunks, m=None, acc=None)
            if has_sink:
                m0 = jnp.full((1, Q_UNIT), sink_ref[h0], F32)
                for e in range(1, heads_per_unit):
                    m0 = jnp.where(lane >= e * tq, sink_ref[h0 + e], m0)
                unit.update(m=m0, acc=acc0)
            units.append(unit)
    outs = []
    for o in _softmax_units(units, SCORE_LOOKAHEAD):
        for e in range(heads_per_unit):
            outs.append(o[:, e * tq:(e + 1) * tq])
    o_ref[...] = jnp.concatenate(outs, axis=0).T.astype(BF16)


def _attention(q, k, vt, kx, vx, sink, *, n_seq, seq_len, tok_base, tq, window, name):
    n_q, _, dk = q.shape
    n_kv = k.shape[0]
    group = n_q // n_kv
    n_qt = seq_len // tq
    q_base = tok_base // tq
    s_base = tok_base // seq_len
    has_ctx = kx is not None
    has_sink = sink is not None
    in_specs = [pl.BlockSpec((n_q, tq, dk), lambda b, j: (0, q_base + b * n_qt + j, 0))]
    args = [q]
    if window:
        assert window == tq
        prev = lambda b, j: q_base + b * n_qt + jnp.maximum(j - 1, 0)
        cur = lambda b, j: q_base + b * n_qt + j
        nxt = lambda b, j: q_base + b * n_qt + jnp.minimum(j + 1, n_qt - 1)
        for blk in (prev, cur, nxt):
            in_specs.append(pl.BlockSpec((n_kv, tq, dk), lambda b, j, blk=blk: (0, blk(b, j), 0)))
        for blk in (prev, cur, nxt):
            in_specs.append(pl.BlockSpec((n_kv, VT_ROWS, tq), lambda b, j, blk=blk: (0, 0, blk(b, j))))
        args += [k, k, k, vt, vt, vt]
    else:
        in_specs += [
            pl.BlockSpec((n_kv, seq_len, dk), lambda b, j: (0, s_base + b, 0)),
            pl.BlockSpec((n_kv, VT_ROWS, seq_len), lambda b, j: (0, 0, s_base + b)),
        ]
        args += [k, vt]
    if has_ctx:
        in_specs += [
            pl.BlockSpec((n_kv, PAST_LEN, dk), lambda b, j: (0, b, 0)),
            pl.BlockSpec((n_kv, VT_ROWS, PAST_LEN), lambda b, j: (0, 0, b)),
        ]
        args += [kx, vx]
    if has_sink:
        in_specs.append(pl.BlockSpec(memory_space=pltpu.SMEM))
        args.append(sink)
    kern = functools.partial(_attn_kernel, n_kv=n_kv, group=group, tq=tq, seq_len=seq_len,
                             has_ctx=has_ctx, has_sink=has_sink, window=window)
    dv = 64
    return pl.pallas_call(
        kern, grid=(n_seq, n_qt), in_specs=in_specs,
        out_specs=pl.BlockSpec((tq, n_q * dv), lambda b, j: (b * n_qt + j, 0)),
        out_shape=jax.ShapeDtypeStruct((n_seq * seq_len, n_q * dv), BF16),
        compiler_params=pltpu.CompilerParams(
            dimension_semantics=("arbitrary", "arbitrary"), vmem_limit_bytes=VMEM_LIMIT),
        name=name,
    )(*args)


def _post_kernel(x_ref, xp_ref, xn_ref, o_ref, op_ref, on_ref, mod_ref, gffn_ref, wout_ref,
                 wup_ref, cw_ref, cb_ref, wdown_ref, gfin_ref, y_ref, *, final):
    i = pl.program_id(0)
    lat = i - N_CTX_TILES
    has_prev = jnp.logical_and(lat >= 0, lat % LAT_TILES != 0)
    has_next = jnp.logical_and(lat >= 0, lat % LAT_TILES != LAT_TILES - 1)
    g1 = mod_ref[2:3, :]
    sh2 = mod_ref[3:4, :]
    sc2 = mod_ref[4:5, :]
    g2 = mod_ref[5:6, :]

    xe = jnp.concatenate([xp_ref[...], x_ref[...], xn_ref[...]], axis=0)
    oe = jnp.concatenate([op_ref[...], o_ref[...], on_ref[...]], axis=0)
    x1e = xe + g1 * _dot(oe, wout_ref[...])
    h2e = (_rms(x1e, gffn_ref[...]) * (1.0 + sc2) + sh2).astype(BF16)
    h2 = h2e[HALO:HALO + TM]
    x1 = x1e[HALO:HALO + TM]

    te = TM + 2 * HALO
    erow = lax.broadcasted_iota(jnp.int32, (te, 1), 0)
    keep = jnp.logical_and(jnp.logical_or(erow >= HALO, has_prev),
                           jnp.logical_or(erow < HALO + TM, has_next))
    keep = keep.astype(F32)
    cw = cw_ref[...]
    cb = cb_ref[...]
    acc = jnp.zeros((TM, D_MODEL), F32)
    for c in range(N_FF_CHUNKS):
        lo = c * FF_CHUNK
        ge = _dot(h2e, wup_ref[:, lo:lo + FF_CHUNK]) * keep
        val = _dot(h2, wup_ref[:, D_FF + lo:D_FF + lo + FF_CHUNK])
        g_prev = pltpu.roll(ge, 1, axis=0)[HALO:HALO + TM]
        g_next = pltpu.roll(ge, te - 1, axis=0)[HALO:HALO + TM]
        gate = (g_prev * cw[0:1, lo:lo + FF_CHUNK] + ge[HALO:HALO + TM] * cw[1:2, lo:lo + FF_CHUNK]
                + g_next * cw[2:3, lo:lo + FF_CHUNK] + cb[:, lo:lo + FF_CHUNK])
        act = (gate * jax.nn.sigmoid(gate) * val).astype(BF16)
        acc = acc + _dot(act, wdown_ref[lo:lo + FF_CHUNK, :])
    x2 = x1 + g2 * acc
    if final:
        x2 = _rms(x2, gfin_ref[...])
    y_ref[...] = x2


def _post(x, o, mod, layer, gffn, wout, wup, cw, cb, wdown, gfin, *, final):
    nh = TM // HALO
    nblk = N_TOK // HALO
    tile = lambda i: (i, 0)
    prev = lambda i: (jnp.maximum(i * nh - 1, 0), 0)
    nxt = lambda i: (jnp.minimum((i + 1) * nh, nblk - 1), 0)
    in_specs = [
        pl.BlockSpec((TM, D_MODEL), tile), pl.BlockSpec((HALO, D_MODEL), prev), pl.BlockSpec((HALO, D_MODEL), nxt),
        pl.BlockSpec((TM, D_MODEL), tile), pl.BlockSpec((HALO, D_MODEL), prev), pl.BlockSpec((HALO, D_MODEL), nxt),
        pl.BlockSpec((None, 6, D_MODEL), lambda i: (layer * 8 + _cond_row(i), 0, 0)),
        _const_spec((1, D_MODEL)),
        _const_spec(wout.shape), _const_spec(wup.shape), _const_spec(cw.shape), _const_spec(cb.shape),
        _const_spec(wdown.shape), _const_spec((1, D_MODEL)),
    ]
    return pl.pallas_call(
        functools.partial(_post_kernel, final=final),
        grid=(N_TILES,), in_specs=in_specs,
        out_specs=pl.BlockSpec((TM, D_MODEL), tile),
        out_shape=jax.ShapeDtypeStruct((N_TOK, D_MODEL), F32),
        compiler_params=pltpu.CompilerParams(
            dimension_semantics=("arbitrary",), vmem_limit_bytes=VMEM_LIMIT),
        name="post_final" if final else "post",
    )(x, x, x, o, o, o, mod, gffn, wout, wup, cw, cb, wdown, gfin)


def _rope_tables(rot_dim):
    t = jnp.arange(LAT_LEN)
    row = (t // GRID_W).astype(F32)
    col = (t % GRID_W).astype(F32)
    d_axis = rot_dim // 2
    freqs = ROPE_THETA ** (-jnp.arange(0, d_axis, 2, dtype=F32) / d_axis)
    ang = jnp.concatenate([row[:, None] * freqs, col[:, None] * freqs], axis=-1)
    cos = jnp.repeat(jnp.cos(ang), 2, axis=-1)
    sin = jnp.repeat(jnp.sin(ang), 2, axis=-1) * jnp.tile(jnp.array([-1.0, 1.0], F32), rot_dim // 2)
    if rot_dim == HEAD_DIM:
        cos = jnp.tile(cos, (1, 2))
        sin = jnp.tile(sin, (1, 2))
    else:
        cos = jnp.concatenate([jnp.ones((LAT_LEN, QK_NOPE), F32), cos,
                               jnp.ones((LAT_LEN, 128 - QK_NOPE - rot_dim), F32)], axis=-1)
        sin = jnp.concatenate([jnp.zeros((LAT_LEN, QK_NOPE), F32), sin,
                               jnp.zeros((LAT_LEN, 128 - QK_NOPE - rot_dim), F32)], axis=-1)
    cos = jnp.concatenate([cos, jnp.ones((TM, 128), F32)], axis=0)
    sin = jnp.concatenate([sin, jnp.zeros((TM, 128), F32)], axis=0)
    return cos, sin


def kernel(x_prompt, x_sample, cache_a_k, cache_a_v, cache_b_ckv, cache_b_kpe, cache_c_k, cache_c_v, c, c_ctx, w_mod, b_mod, g_mix_norm, g_ffn_norm, w_in_e, g_qnorm_a, g_knorm_a, g_cq_b, w_uq_b, g_ckv_b, w_ukv_b, w_out_e, w_in_o, sink_c, w_out_o, w_up, conv_w, conv_b, w_down, g_final):
    x0 = jnp.concatenate([x_prompt.reshape(N_CTX_TOK, D_MODEL),
                          x_sample.reshape(N_LAT_SEQ * LAT_LEN, D_MODEL)], axis=0)
    cond8 = jnp.concatenate([c_ctx[None, :], c, jnp.zeros((5, D_MODEL), F32)], axis=0)

    w_e = w_in_e[0]
    kpe_cols = jnp.pad(w_e[:, 1408:1440], ((0, 0), (QK_NOPE, KB_PAD - QK_NOPE - QK_ROPE)))
    win_e = jnp.concatenate([w_e[:, :1408], kpe_cols], axis=1).astype(BF16)
    wuq = jnp.pad(w_uq_b[0].reshape(Q_LORA, N_HEADS_B, QK_NOPE + QK_ROPE),
                  ((0, 0), (0, 0), (0, KB_PAD - QK_NOPE - QK_ROPE))).reshape(Q_LORA, N_HEADS_B * KB_PAD).astype(BF16)
    wukv3 = w_ukv_b[0].reshape(KV_LORA, N_HEADS_B, QK_NOPE + V_DIM_B)
    wukk = jnp.pad(wukv3[:, :, :QK_NOPE], ((0, 0), (0, 0), (0, KB_PAD - QK_NOPE))
                   ).reshape(KV_LORA, N_HEADS_B * KB_PAD).astype(BF16)
    wukv = wukv3[:, :, QK_NOPE:].reshape(KV_LORA, N_HEADS_B * V_DIM_B).astype(BF16)
    gq = jnp.tile(g_qnorm_a[0], N_HEADS_A)[None, :]
    gk = jnp.tile(g_knorm_a[0], N_KV_A)[None, :]
    seg = jnp.arange(256) // HEAD_DIM
    ones = (seg[:, None] == seg[None, :]).astype(BF16)
    cosa, sina = _rope_tables(HEAD_DIM)
    cosb, sinb = _rope_tables(QK_ROPE)

    mod = _modulation(cond8, w_mod, b_mod).reshape(2 * 8, 6, D_MODEL)

    n_past = N_LAT_SEQ * PAST_LEN
    kpe_pad = jnp.pad(cache_b_kpe.reshape(n_past, QK_ROPE), ((0, 0), (QK_NOPE, KB_PAD - QK_NOPE - QK_ROPE)))
    ka_c, va_c, kb_c, vb_c, kc_c, vc_c = _ctx_prep(
        cache_a_k.reshape(n_past, 128), cache_a_v.reshape(n_past, 128),
        cache_b_ckv.reshape(n_past, KV_LORA), kpe_pad,
        cache_c_k.reshape(n_past, 128), cache_c_v.reshape(n_past, 128), wukk, wukv)

    qa, ka, va, qb, kb, vb, nak, nav, nckv, nkpe = _pre0(
        x0, mod, 0, g_mix_norm[0][None, :], win_e, gq, gk, g_cq_b[0][None, :], g_ckv_b[0][None, :],
        wuq, wukk, wukv, cosa, sina, cosb, sinb, ones)
    ctx_kw = dict(n_seq=N_CTX_SEQ, seq_len=CTX_LEN, tok_base=0, tq=CTX_LEN, window=0)
    lat_kw = dict(n_seq=N_LAT_SEQ, seq_len=LAT_LEN, tok_base=N_CTX_TOK, window=0)
    oa = jnp.concatenate([
        _attention(qa, ka, va, None, None, None, name="attn_a_ctx", **ctx_kw),
        _attention(qa, ka, va, ka_c, va_c, None, tq=256, name="attn_a_lat", **lat_kw)], axis=0)
    ob = jnp.concatenate([
        _attention(qb, kb, vb, None, None, None, name="attn_b_ctx", **ctx_kw),
        _attention(qb, kb, vb, kb_c, vb_c, None, tq=256, name="attn_b_lat", **lat_kw)], axis=0)
    o0 = jnp.concatenate([oa, ob], axis=1)
    x1 = _post(x0, o0, mod, 0, g_ffn_norm[0][None, :], w_out_e[0].astype(BF16), w_up[0].astype(BF16),
               conv_w[0], conv_b[0][None, :], w_down[0].astype(BF16), g_final[None, :], final=False)

    qc, kc, vc, nck, ncv = _pre1(x1, mod, 1, g_mix_norm[1][None, :], w_in_o[0].astype(BF16), cosa, sina)
    sink = sink_c[0]
    oc = jnp.concatenate([
        _attention(qc, kc, vc, None, None, sink, name="attn_c_ctx", **ctx_kw),
        _attention(qc, kc, vc, kc_c, vc_c, sink, n_seq=N_LAT_SEQ, seq_len=LAT_LEN, tok_base=N_CTX_TOK,
                   tq=128, window=WINDOW, name="attn_c_lat")], axis=0)
    y = _post(x1, oc, mod, 1, g_ffn_norm[1][None, :], w_out_o[0].astype(BF16), w_up[1].astype(BF16),
              conv_w[1], conv_b[1][None, :], w_down[1].astype(BF16), g_final[None, :], final=True)

    y_prompt = y[:N_CTX_TOK].reshape(N_CTX_SEQ, CTX_LEN, D_MODEL)
    y_sample = y[N_CTX_TOK:].reshape(N_LAT_SEQ, LAT_LEN, D_MODEL)
    return (y_prompt, y_sample,
            nak.reshape(N_CTX_SEQ, 1, CTX_LEN, N_KV_A, HEAD_DIM),
            nav.reshape(N_CTX_SEQ, 1, CTX_LEN, N_KV_A, HEAD_DIM),
            nckv.reshape(N_CTX_SEQ, 1, CTX_LEN, KV_LORA),
            nkpe.reshape(N_CTX_SEQ, 1, CTX_LEN, QK_ROPE),
            nck.reshape(N_CTX_SEQ, 1, CTX_LEN, N_KV_C, HEAD_DIM),
            ncv.reshape(N_CTX_SEQ, 1, CTX_LEN, N_KV_C, HEAD_DIM))
```

```python
import functools

import jax
import jax.numpy as jnp
from jax import lax
from jax.experimental import pallas as pl
from jax.experimental.pallas import tpu as pltpu

F32 = jnp.float32
BF16 = jnp.bfloat16

D_MODEL = 1024
N_CTX_SEQ = 16
CTX_LEN = 256
N_LAT_SEQ = 2
LAT_LEN = 2048
PAST_LEN = 512
GRID_W = 64
ROPE_THETA = 10000.0
NORM_EPS = 1e-6
WINDOW = 128
NEG_INF = -1e30
HEAD_DIM = 64
N_HEADS_A, N_KV_A = 8, 2
N_HEADS_B = 8
Q_LORA, KV_LORA = 384, 256
QK_NOPE, QK_ROPE, V_DIM_B = 64, 32, 64
N_HEADS_C, N_KV_C = 16, 2
D_FF = 2816
IN_E_MAIN = N_HEADS_A * HEAD_DIM + 2 * N_KV_A * HEAD_DIM + Q_LORA + KV_LORA

N_CTX_TOK = N_CTX_SEQ * CTX_LEN
N_TOK = N_CTX_TOK + N_LAT_SEQ * LAT_LEN
TM = 256
N_TILES = N_TOK // TM
N_CTX_TILES = N_CTX_TOK // TM
LAT_TILES = LAT_LEN // TM
HALO = 16
FFN_TM = 1024
FFN_SUB = 256
FFN_CTX_TILES = N_CTX_TOK // FFN_TM
FFN_LAT_TILES = LAT_LEN // FFN_TM
FF_CHUNK = 256
N_FF_CHUNKS = D_FF // FF_CHUNK
KB_PAD = 128
VT_ROWS = 80
Q_UNIT = 256
KEY_CHUNK = 512
SCORE_LOOKAHEAD = 2
VMEM_LIMIT = 56 * 1024 * 1024
FFN_VMEM_LIMIT = 60 * 1024 * 1024


def _dot(a, b):
    return jnp.dot(a, b, preferred_element_type=F32)


def _dot_nt(a, b):
    return lax.dot_general(a, b, (((1,), (1,)), ((), ())), preferred_element_type=F32)


def _rms(x, g):
    return x * lax.rsqrt(jnp.mean(x * x, axis=-1, keepdims=True) + NORM_EPS) * g


def _head_rms(x, g, ones_ref):
    w = x.shape[1]
    sq = x * x
    hi = sq.astype(BF16)
    lo = (sq - hi.astype(F32)).astype(BF16)
    parts = []
    for c in range(0, w, 256):
        cw = min(256, w - c)
        ones = ones_ref[0:cw, 0:cw]
        parts.append(_dot(hi[:, c:c + cw], ones) + _dot(lo[:, c:c + cw], ones))
    ssum = parts[0] if len(parts) == 1 else jnp.concatenate(parts, axis=1)
    return x * lax.rsqrt(ssum * (1.0 / HEAD_DIM) + NORM_EPS) * g


def _swap_pairs(x):
    w = x.shape[1]
    up = pltpu.roll(x, w - 1, axis=1)
    dn = pltpu.roll(x, 1, axis=1)
    lane = lax.broadcasted_iota(jnp.int32, x.shape, 1)
    return jnp.where((lane & 1) == 0, up, dn)


def _rope(x, cos, sin_signed, reps):
    if reps > 1:
        cos = jnp.concatenate([cos] * reps, axis=1)
        sin_signed = jnp.concatenate([sin_signed] * reps, axis=1)
    return x * cos + _swap_pairs(x) * sin_signed


def _store_vt(vt_ref, v, n_heads):
    t = v.shape[0]
    vt = v.T.astype(BF16)
    ones = jnp.ones((VT_ROWS - 64, t), BF16)
    for hh in range(n_heads):
        vt_ref[hh, 0:64, :] = vt[hh * 64:(hh + 1) * 64]
        vt_ref[hh, 64:VT_ROWS, :] = ones


def _cond_row(i):
    return jnp.where(i < N_CTX_TILES, 0, 1 + (i - N_CTX_TILES) // LAT_TILES)


def _rope_block(i):
    return jnp.where(i < N_CTX_TILES, LAT_TILES, (i - N_CTX_TILES) % LAT_TILES)


def _const_spec(shape):
    zeros = (0,) * len(shape)
    return pl.BlockSpec(shape, lambda *_: zeros)


def _layer_spec(shape, layer):
    idx = (layer,) + (0,) * len(shape)
    return pl.BlockSpec((None,) + tuple(shape), lambda *_: idx)


def _mod_kernel(cond_ref, w_ref, b_ref, o_ref):
    c = cond_ref[...]
    s = c * jax.nn.sigmoid(c)
    o_ref[0] = jnp.dot(s, w_ref[0], preferred_element_type=F32,
                       precision=lax.Precision.HIGHEST) + b_ref[0]


def _modulation(cond8, w_mod, b_mod):
    depth, _, n = w_mod.shape
    tn = 1536
    return pl.pallas_call(
        _mod_kernel,
        grid=(depth, n // tn),
        in_specs=[
            pl.BlockSpec((8, D_MODEL), lambda l, j: (0, 0)),
            pl.BlockSpec((1, D_MODEL, tn), lambda l, j: (l, 0, j)),
            pl.BlockSpec((1, 1, tn), lambda l, j: (l, 0, j)),
        ],
        out_specs=pl.BlockSpec((1, 8, tn), lambda l, j: (l, 0, j)),
        out_shape=jax.ShapeDtypeStruct((depth, 8, n), F32),
        compiler_params=pltpu.CompilerParams(
            dimension_semantics=("arbitrary", "arbitrary"), vmem_limit_bytes=VMEM_LIMIT),
        name="modulation",
    )(cond8, w_mod, b_mod.reshape(depth, 1, n))


def _pre0_kernel(xp_ref, xs_ref, mod_ref, gmix_ref, win_ref, wkpe_ref, gq_ref, gk_ref, gcq_ref, gckv_ref,
                 wuq_ref, wukk_ref, wukv_ref, cosa_ref, sina_ref, cosb_ref, sinb_ref, ones_ref,
                 x0_ref, qa_ref, ka_ref, va_ref, qb_ref, kb_ref, vb_ref,
                 nak_ref, nav_ref, nckv_ref, nkpe_ref,
                 win_s, wuq_s, wukk_s, wukv_s):
    i = pl.program_id(0)

    @pl.when(i == 0)
    def _():
        win_s[:, 0:IN_E_MAIN] = win_ref[...].astype(BF16)
        win_s[:, IN_E_MAIN:] = wkpe_ref[...].astype(BF16)
        wuq_s[...] = wuq_ref[...].astype(BF16)
        wukk_s[...] = wukk_ref[...].astype(BF16)
        wukv_s[...] = wukv_ref[...].astype(BF16)

    x = jnp.where(i < N_CTX_TILES, xp_ref[...], xs_ref[...])
    x0_ref[...] = x
    sh1 = mod_ref[0:1, :]
    sc1 = mod_ref[1:2, :]
    h = _rms(x, gmix_ref[...]) * (1.0 + sc1) + sh1
    proj = _dot(h.astype(BF16), win_s[...])
    qa = _head_rms(proj[:, 0:512], gq_ref[...], ones_ref)
    ka = _head_rms(proj[:, 512:640], gk_ref[...], ones_ref)
    va = proj[:, 640:768]
    cq = _rms(proj[:, 768:1152], gcq_ref[...])
    ckv = _rms(proj[:, 1152:1408], gckv_ref[...])
    kpe = proj[:, 1408:1536]
    qb = _dot(cq.astype(BF16), wuq_s[...])

    cosa, sina = cosa_ref[...], sina_ref[...]
    cosb, sinb = cosb_ref[...], sinb_ref[...]
    qa = _rope(qa, cosa, sina, 4)
    ka = _rope(ka, cosa, sina, 1)
    qb = _rope(qb, cosb, sinb, 8)
    kpe = _rope(kpe, cosb, sinb, 1)

    ckv_b = ckv.astype(BF16)
    kbn = _dot(ckv_b, wukk_s[...])
    vb = _dot(ckv_b, wukv_s[...])

    qa_s = qa * (HEAD_DIM ** -0.5)
    qb_s = qb * ((QK_NOPE + QK_ROPE) ** -0.5)
    for hh in range(N_HEADS_A):
        qa_ref[hh] = qa_s[:, hh * 64:(hh + 1) * 64].astype(BF16)
    for hh in range(N_KV_A):
        ka_ref[hh] = ka[:, hh * 64:(hh + 1) * 64].astype(BF16)
    _store_vt(va_ref, va, N_KV_A)
    for hh in range(N_HEADS_B):
        qb_ref[hh] = qb_s[:, hh * KB_PAD:(hh + 1) * KB_PAD].astype(BF16)
        kb_ref[hh] = (kbn[:, hh * KB_PAD:(hh + 1) * KB_PAD] + kpe).astype(BF16)
    _store_vt(vb_ref, vb, N_HEADS_B)

    @pl.when(i < N_CTX_TILES)
    def _():
        nak_ref[...] = ka
        nav_ref[...] = va
        nckv_ref[...] = ckv
        nkpe_ref[...] = kpe[:, QK_NOPE:QK_NOPE + QK_ROPE]


def _pre0(xp, xs, mod, gmix, w_in_e, wkpe, gq, gk, gcq, gckv, wuq, wukk, wukv, cosa, sina, cosb, sinb, ones):
    tile = lambda i: (i, 0)
    head_tile = lambda i: (0, i, 0)
    vt_tile = lambda i: (0, 0, i)
    ctx_tile = lambda i: (jnp.minimum(i, N_CTX_TILES - 1), 0)
    lat_tile = lambda i: (jnp.maximum(i - N_CTX_TILES, 0), 0)
    rope_tile = lambda i: (_rope_block(i), 0)
    in_specs = [
        pl.BlockSpec((TM, D_MODEL), ctx_tile),
        pl.BlockSpec((TM, D_MODEL), lat_tile),
        pl.BlockSpec((None, 6, D_MODEL), lambda i: (_cond_row(i), 0, 0)),
        _layer_spec((1, D_MODEL), 0),
        _layer_spec((D_MODEL, IN_E_MAIN), 0),
        _const_spec(wkpe.shape),
        _const_spec(gq.shape), _const_spec(gk.shape), _layer_spec((1, Q_LORA), 0), _layer_spec((1, KV_LORA), 0),
        _const_spec(wuq.shape), _const_spec(wukk.shape), _const_spec(wukv.shape),
        pl.BlockSpec((TM, 128), rope_tile), pl.BlockSpec((TM, 128), rope_tile),
        pl.BlockSpec((TM, 128), rope_tile), pl.BlockSpec((TM, 128), rope_tile),
        _const_spec(ones.shape),
    ]
    out_shape = [
        jax.ShapeDtypeStruct((N_TOK, D_MODEL), F32),
        jax.ShapeDtypeStruct((N_HEADS_A, N_TOK, 64), BF16),
        jax.ShapeDtypeStruct((N_KV_A, N_TOK, 64), BF16),
        jax.ShapeDtypeStruct((N_KV_A, VT_ROWS, N_TOK), BF16),
        jax.ShapeDtypeStruct((N_HEADS_B, N_TOK, KB_PAD), BF16),
        jax.ShapeDtypeStruct((N_HEADS_B, N_TOK, KB_PAD), BF16),
        jax.ShapeDtypeStruct((N_HEADS_B, VT_ROWS, N_TOK), BF16),
        jax.ShapeDtypeStruct((N_CTX_TOK, 128), F32),
        jax.ShapeDtypeStruct((N_CTX_TOK, 128), F32),
        jax.ShapeDtypeStruct((N_CTX_TOK, KV_LORA), F32),
        jax.ShapeDtypeStruct((N_CTX_TOK, QK_ROPE), F32),
    ]
    out_specs = [
        pl.BlockSpec((TM, D_MODEL), tile),
        pl.BlockSpec((N_HEADS_A, TM, 64), head_tile),
        pl.BlockSpec((N_KV_A, TM, 64), head_tile),
        pl.BlockSpec((N_KV_A, VT_ROWS, TM), vt_tile),
        pl.BlockSpec((N_HEADS_B, TM, KB_PAD), head_tile),
        pl.BlockSpec((N_HEADS_B, TM, KB_PAD), head_tile),
        pl.BlockSpec((N_HEADS_B, VT_ROWS, TM), vt_tile),
        pl.BlockSpec((TM, 128), ctx_tile),
        pl.BlockSpec((TM, 128), ctx_tile),
        pl.BlockSpec((TM, KV_LORA), ctx_tile),
        pl.BlockSpec((TM, QK_ROPE), ctx_tile),
    ]
    scratch = [
        pltpu.VMEM((D_MODEL, IN_E_MAIN + KB_PAD), BF16),
        pltpu.VMEM(wuq.shape, BF16), pltpu.VMEM(wukk.shape, BF16), pltpu.VMEM(wukv.shape, BF16),
    ]
    return pl.pallas_call(
        _pre0_kernel, grid=(N_TILES,), in_specs=in_specs, out_specs=out_specs, out_shape=out_shape,
        scratch_shapes=scratch,
        compiler_params=pltpu.CompilerParams(
            dimension_semantics=("arbitrary",), vmem_limit_bytes=VMEM_LIMIT),
        name="pre0",
    )(xp, xs, mod, gmix, w_in_e, wkpe, gq, gk, gcq, gckv, wuq, wukk, wukv, cosa, sina, cosb, sinb, ones)


def _pre1_kernel(x_ref, mod_ref, gmix_ref, win_ref, cosa_ref, sina_ref,
                 q_ref, k_ref, v_ref, nk_ref, nv_ref, win_s):
    i = pl.program_id(0)

    @pl.when(i == 0)
    def _():
        win_s[...] = win_ref[...].astype(BF16)

    x = x_ref[...]
    sh1 = mod_ref[0:1, :]
    sc1 = mod_ref[1:2, :]
    h = _rms(x, gmix_ref[...]) * (1.0 + sc1) + sh1
    proj = _dot(h.astype(BF16), win_s[...])
    cosa, sina = cosa_ref[...], sina_ref[...]
    q = _rope(proj[:, 0:1024], cosa, sina, 8) * (HEAD_DIM ** -0.5)
    k = _rope(proj[:, 1024:1152], cosa, sina, 1)
    v = proj[:, 1152:1280]
    for hh in range(N_HEADS_C):
        q_ref[hh] = q[:, hh * 64:(hh + 1) * 64].astype(BF16)
    for hh in range(N_KV_C):
        k_ref[hh] = k[:, hh * 64:(hh + 1) * 64].astype(BF16)
    _store_vt(v_ref, v, N_KV_C)

    @pl.when(i < N_CTX_TILES)
    def _():
        nk_ref[...] = k
        nv_ref[...] = v


def _pre1(x, mod, gmix, w_in_o, cosa, sina):
    n_in = w_in_o.shape[-1]
    tile = lambda i: (i, 0)
    head_tile = lambda i: (0, i, 0)
    vt_tile = lambda i: (0, 0, i)
    ctx_tile = lambda i: (jnp.minimum(i, N_CTX_TILES - 1), 0)
    rope_tile = lambda i: (_rope_block(i), 0)
    in_specs = [
        pl.BlockSpec((TM, D_MODEL), tile),
        pl.BlockSpec((None, 6, D_MODEL), lambda i: (8 + _cond_row(i), 0, 0)),
        _layer_spec((1, D_MODEL), 1),
        _layer_spec((D_MODEL, n_in), 0),
        pl.BlockSpec((TM, 128), rope_tile), pl.BlockSpec((TM, 128), rope_tile),
    ]
    out_shape = [
        jax.ShapeDtypeStruct((N_HEADS_C, N_TOK, 64), BF16),
        jax.ShapeDtypeStruct((N_KV_C, N_TOK, 64), BF16),
        jax.ShapeDtypeStruct((N_KV_C, VT_ROWS, N_TOK), BF16),
        jax.ShapeDtypeStruct((N_CTX_TOK, 128), F32),
        jax.ShapeDtypeStruct((N_CTX_TOK, 128), F32),
    ]
    out_specs = [
        pl.BlockSpec((N_HEADS_C, TM, 64), head_tile),
        pl.BlockSpec((N_KV_C, TM, 64), head_tile),
        pl.BlockSpec((N_KV_C, VT_ROWS, TM), vt_tile),
        pl.BlockSpec((TM, 128), ctx_tile),
        pl.BlockSpec((TM, 128), ctx_tile),
    ]
    return pl.pallas_call(
        _pre1_kernel, grid=(N_TILES,), in_specs=in_specs, out_specs=out_specs, out_shape=out_shape,
        scratch_shapes=[pltpu.VMEM((D_MODEL, n_in), BF16)],
        compiler_params=pltpu.CompilerParams(
            dimension_semantics=("arbitrary",), vmem_limit_bytes=VMEM_LIMIT),
        name="pre1",
    )(x, mod, gmix, w_in_o, cosa, sina)


def _ctx_kernel(ak_ref, av_ref, ckv_ref, kpe_ref, ck_ref, cv_ref, wukk_ref, wukv_ref,
                ka_ref, va_ref, kb_ref, vb_ref, kc_ref, vc_ref):
    ak, av, ck, cv = ak_ref[...], av_ref[...], ck_ref[...], cv_ref[...]
    for hh in range(2):
        sl = slice(hh * 64, (hh + 1) * 64)
        ka_ref[hh] = ak[:, sl].astype(BF16)
        kc_ref[hh] = ck[:, sl].astype(BF16)
    _store_vt(va_ref, av, 2)
    _store_vt(vc_ref, cv, 2)
    ckv_b = ckv_ref[...].astype(BF16)
    kbn = _dot(ckv_b, wukk_ref[...].astype(BF16))
    vb = _dot(ckv_b, wukv_ref[...].astype(BF16))
    kpe = kpe_ref[...]
    for hh in range(N_HEADS_B):
        kb_ref[hh] = (kbn[:, hh * KB_PAD:(hh + 1) * KB_PAD] + kpe).astype(BF16)
    _store_vt(vb_ref, vb, N_HEADS_B)


def _ctx_prep(ak, av, ckv, kpe_pad, ck, cv, wukk, wukv):
    n = N_LAT_SEQ * PAST_LEN
    row = lambda b: (b, 0)
    head_row = lambda b: (0, b, 0)
    vt_row = lambda b: (0, 0, b)
    in_specs = [
        pl.BlockSpec((PAST_LEN, 128), row), pl.BlockSpec((PAST_LEN, 128), row),
        pl.BlockSpec((PAST_LEN, KV_LORA), row), pl.BlockSpec((PAST_LEN, KB_PAD), row),
        pl.BlockSpec((PAST_LEN, 128), row), pl.BlockSpec((PAST_LEN, 128), row),
        _const_spec(wukk.shape), _const_spec(wukv.shape),
    ]
    out_shape = [
        jax.ShapeDtypeStruct((2, n, 64), BF16), jax.ShapeDtypeStruct((2, VT_ROWS, n), BF16),
        jax.ShapeDtypeStruct((N_HEADS_B, n, KB_PAD), BF16), jax.ShapeDtypeStruct((N_HEADS_B, VT_ROWS, n), BF16),
        jax.ShapeDtypeStruct((2, n, 64), BF16), jax.ShapeDtypeStruct((2, VT_ROWS, n), BF16),
    ]
    out_specs = [
        pl.BlockSpec((2, PAST_LEN, 64), head_row), pl.BlockSpec((2, VT_ROWS, PAST_LEN), vt_row),
        pl.BlockSpec((N_HEADS_B, PAST_LEN, KB_PAD), head_row), pl.BlockSpec((N_HEADS_B, VT_ROWS, PAST_LEN), vt_row),
        pl.BlockSpec((2, PAST_LEN, 64), head_row), pl.BlockSpec((2, VT_ROWS, PAST_LEN), vt_row),
    ]
    return pl.pallas_call(
        _ctx_kernel, grid=(N_LAT_SEQ,), in_specs=in_specs, out_specs=out_specs, out_shape=out_shape,
        compiler_params=pltpu.CompilerParams(
            dimension_semantics=("arbitrary",), vmem_limit_bytes=VMEM_LIMIT),
        name="ctx_prep",
    )(ak, av, ckv, kpe_pad, ck, cv, wukk, wukv)


def _softmax_units(units, lookahead):
    tasks = [(u, c) for u, unit in enumerate(units) for c in range(len(unit["chunks"]))]
    scores = {}

    def emit_scores(t):
        u, c = tasks[t]
        k, _, mask = units[u]["chunks"][c]
        s = _dot_nt(k, units[u]["q"])
        scores[t] = s if mask is None else jnp.where(mask, s, NEG_INF)

    for t in range(min(lookahead, len(tasks))):
        emit_scores(t)
    for t, (u, c) in enumerate(tasks):
        if t + lookahead < len(tasks):
            emit_scores(t + lookahead)
        unit = units[u]
        s = scores.pop(t)
        m, acc = unit["m"], unit["acc"]
        cmax = jnp.max(s, axis=0, keepdims=True)
        m_new = cmax if m is None else jnp.maximum(m, cmax)
        pv = _dot(unit["chunks"][c][1], jnp.exp(s - m_new).astype(BF16))
        unit["acc"] = pv if acc is None else acc * jnp.exp(m - m_new) + pv
        unit["m"] = m_new
    return [unit["acc"][0:64] * (1.0 / unit["acc"][64:65]) for unit in units]


def _attn_kernel(*refs, n_kv, group, tq, seq_len, has_ctx, has_sink, has_alias, window):
    refs = list(refs)
    q_ref = refs.pop(0)
    if window:
        kp_ref, kc_ref, kn_ref, vp_ref, vc_ref, vn_ref = refs[:6]
        refs = refs[6:]
    else:
        k_ref, vt_ref = refs[:2]
        refs = refs[2:]
    if has_ctx:
        kx_ref, vx_ref = refs[:2]
        refs = refs[2:]
    if has_sink:
        sink_ref = refs.pop(0)
    if has_alias:
        refs.pop(0)
    o_ref = refs.pop(0)

    j = pl.program_id(1)
    dk = q_ref.shape[-1]
    heads_per_unit = Q_UNIT // tq
    lane = lax.broadcasted_iota(jnp.int32, (1, Q_UNIT), 1)
    if window:
        n_band = tq + 2 * window
        krow = lax.broadcasted_iota(jnp.int32, (n_band, Q_UNIT), 0)
        qcol = lax.broadcasted_iota(jnp.int32, (n_band, Q_UNIT), 1) & (tq - 1)
        rel = (krow - window) - qcol
        band_mask = ((jnp.abs(rel) <= window)
                     & ((krow >= window) | (j > 0))
                     & ((krow < window + tq) | (j < seq_len // tq - 1)))
    if has_sink:
        acc0 = jnp.where(lax.broadcasted_iota(jnp.int32, (VT_ROWS, Q_UNIT), 0) >= 64, 1.0, 0.0)

    units = []
    for hk in range(n_kv):
        chunks = []
        if window:
            chunks.append((jnp.concatenate([kp_ref[hk], kc_ref[hk], kn_ref[hk]], axis=0),
                           jnp.concatenate([vp_ref[hk], vc_ref[hk], vn_ref[hk]], axis=1), band_mask))
        else:
            for c in range(0, seq_len, KEY_CHUNK):
                n = min(KEY_CHUNK, seq_len - c)
                chunks.append((k_ref[hk, c:c + n, :], vt_ref[hk, :, c:c + n], None))
        if has_ctx:
            chunks.append((kx_ref[hk], vx_ref[hk], None))
        for u in range(group // heads_per_unit):
            h0 = hk * group + u * heads_per_unit
            unit = dict(q=q_ref[h0:h0 + heads_per_unit].reshape(Q_UNIT, dk), chunks=chunks, m=None, acc=None)
            if has_sink:
                m0 = jnp.full((1, Q_UNIT), sink_ref[h0], F32)
                for e in range(1, heads_per_unit):
                    m0 = jnp.where(lane >= e * tq, sink_ref[h0 + e], m0)
                unit.update(m=m0, acc=acc0)
            units.append(unit)
    outs = []
    for o in _softmax_units(units, SCORE_LOOKAHEAD):
        for e in range(heads_per_unit):
            outs.append(o[:, e * tq:(e + 1) * tq])
    o_ref[...] = jnp.concatenate(outs, axis=0).T.astype(BF16)


def _attention(q, k, vt, kx, vx, sink, other, *, n_seq, seq_len, tok_base, tq, window, name):
    n_q, _, dk = q.shape
    n_kv = k.shape[0]
    group = n_q // n_kv
    n_qt = seq_len // tq
    q_base = tok_base // tq
    s_base = tok_base // seq_len
    has_ctx = kx is not None
    has_sink = sink is not None
    has_alias = other is not None
    q_blk = lambda b, j: q_base + b * n_qt + j
    in_specs = [pl.BlockSpec((n_q, tq, dk), lambda b, j: (0, q_blk(b, j), 0))]
    args = [q]
    if window:
        assert window == tq
        prev = lambda b, j: q_base + b * n_qt + jnp.maximum(j - 1, 0)
        nxt = lambda b, j: q_base + b * n_qt + jnp.minimum(j + 1, n_qt - 1)
        for blk in (prev, q_blk, nxt):
            in_specs.append(pl.BlockSpec((n_kv, tq, dk), lambda b, j, blk=blk: (0, blk(b, j), 0)))
        for blk in (prev, q_blk, nxt):
            in_specs.append(pl.BlockSpec((n_kv, VT_ROWS, tq), lambda b, j, blk=blk: (0, 0, blk(b, j))))
        args += [k, k, k, vt, vt, vt]
    else:
        in_specs += [
            pl.BlockSpec((n_kv, seq_len, dk), lambda b, j: (0, s_base + b, 0)),
            pl.BlockSpec((n_kv, VT_ROWS, seq_len), lambda b, j: (0, 0, s_base + b)),
        ]
        args += [k, vt]
    if has_ctx:
        in_specs += [
            pl.BlockSpec((n_kv, PAST_LEN, dk), lambda b, j: (0, b, 0)),
            pl.BlockSpec((n_kv, VT_ROWS, PAST_LEN), lambda b, j: (0, 0, b)),
        ]
        args += [kx, vx]
    if has_sink:
        in_specs.append(pl.BlockSpec(memory_space=pltpu.SMEM))
        args.append(sink)
    aliases = {}
    if has_alias:
        aliases = {len(args): 0}
        in_specs.append(pl.BlockSpec(memory_space=pl.ANY))
        args.append(other)
    kern = functools.partial(_attn_kernel, n_kv=n_kv, group=group, tq=tq, seq_len=seq_len,
                             has_ctx=has_ctx, has_sink=has_sink, has_alias=has_alias, window=window)
    return pl.pallas_call(
        kern, grid=(n_seq, n_qt), in_specs=in_specs,
        out_specs=pl.BlockSpec((tq, n_q * HEAD_DIM), lambda b, j: (q_blk(b, j), 0)),
        out_shape=jax.ShapeDtypeStruct((N_TOK, n_q * HEAD_DIM), BF16),
        input_output_aliases=aliases,
        compiler_params=pltpu.CompilerParams(
            dimension_semantics=("arbitrary", "arbitrary"), vmem_limit_bytes=VMEM_LIMIT),
        name=name,
    )(*args)


def _ffn_kernel(*refs, n_o, final):
    x_ref, xp_ref, xn_ref = refs[0:3]
    o_refs = [refs[3 + 3 * t:6 + 3 * t] for t in range(n_o)]
    pos = 3 + 3 * n_o
    wo_refs = refs[pos:pos + n_o]
    pos += n_o
    mod_ref, gffn_ref, wg_ref, wv_ref, cw_ref, cb_ref, wd_ref, gfin_ref = refs[pos:pos + 8]
    pos += 8
    out_refs = refs[pos:pos + (2 if final else 1)]
    h2e_ref, acc_ref = refs[pos + len(out_refs):]

    m = pl.program_id(0)
    c = pl.program_id(1)
    is_ctx = m < FFN_CTX_TILES
    part = (m - FFN_CTX_TILES) % FFN_LAT_TILES
    has_prev = jnp.logical_and(jnp.logical_not(is_ctx), part != 0)
    has_next = jnp.logical_and(jnp.logical_not(is_ctx), part != FFN_LAT_TILES - 1)
    g1 = mod_ref[2:3, :]
    sh2 = mod_ref[3:4, :]
    sc2 = mod_ref[4:5, :]
    g2 = mod_ref[5:6, :]

    @pl.when(c == 0)
    def _():
        wos = [w[...].astype(BF16) for w in wo_refs]

        def residual_and_norm(xv, ovs):
            attn = _dot(ovs[0], wos[0])
            for ov, wo in zip(ovs[1:], wos[1:]):
                attn = attn + _dot(ov, wo)
            x1 = xv + g1 * attn
            return x1, (_rms(x1, gffn_ref[...]) * (1.0 + sc2) + sh2).astype(BF16)

        for r in range(0, FFN_TM, FFN_SUB):
            x1, h2 = residual_and_norm(x_ref[r:r + FFN_SUB, :], [o[0][r:r + FFN_SUB, :] for o in o_refs])
            acc_ref[r:r + FFN_SUB, :] = x1
            h2e_ref[HALO + r:HALO + r + FFN_SUB, :] = h2
        _, h2h = residual_and_norm(
            jnp.concatenate([xp_ref[...], xn_ref[...]], axis=0),
            [jnp.concatenate([o[1][...], o[2][...]], axis=0) for o in o_refs])
        h2e_ref[0:HALO, :] = h2h[0:HALO]
        h2e_ref[HALO + FFN_TM:, :] = h2h[HALO:]

    te = FFN_TM + 2 * HALO
    ge = _dot(h2e_ref[...], wg_ref[...].astype(BF16))
    val = _dot(h2e_ref[HALO:HALO + FFN_TM, :], wv_ref[...].astype(BF16))
    row = lax.broadcasted_iota(jnp.int32, (FFN_TM, FF_CHUNK), 0)
    seq_mask = jnp.where(is_ctx, CTX_LEN - 1, FFN_TM - 1)
    prev_ok = jnp.logical_or((row & seq_mask) != 0, has_prev)
    next_ok = jnp.logical_or((row & seq_mask) != seq_mask, has_next)
    g_prev = jnp.where(prev_ok, pltpu.roll(ge, 1, axis=0)[HALO:HALO + FFN_TM], 0.0)
    g_next = jnp.where(next_ok, pltpu.roll(ge, te - 1, axis=0)[HALO:HALO + FFN_TM], 0.0)
    cw = cw_ref[...]
    gate = g_prev * cw[0:1, :] + ge[HALO:HALO + FFN_TM] * cw[1:2, :] + g_next * cw[2:3, :] + cb_ref[...]
    act = (gate * jax.nn.sigmoid(gate) * val).astype(BF16)
    acc_ref[...] += g2 * _dot(act, wd_ref[...].astype(BF16))

    @pl.when(c == N_FF_CHUNKS - 1)
    def _():
        x2 = acc_ref[...]
        if final:
            x2 = _rms(x2, gfin_ref[...])

            @pl.when(is_ctx)
            def _():
                out_refs[0][...] = x2

            @pl.when(jnp.logical_not(is_ctx))
            def _():
                out_refs[1][...] = x2
        else:
            out_refs[0][...] = x2


def _ffn(x, os, mod, layer, g_ffn, w_out, w_up, conv_w, conv_b, w_down, g_final, *, final):
    nh = FFN_TM // HALO
    nblk = N_TOK // HALO
    n_tiles = N_TOK // FFN_TM
    n_o = len(os)
    tile = lambda m, c: (m, 0)
    prev = lambda m, c: (jnp.maximum(m * nh - 1, 0), 0)
    nxt = lambda m, c: (jnp.minimum((m + 1) * nh, nblk - 1), 0)
    cond = lambda m: jnp.where(m < FFN_CTX_TILES, 0, 1 + (m - FFN_CTX_TILES) // FFN_LAT_TILES)
    in_specs = [pl.BlockSpec((FFN_TM, D_MODEL), tile), pl.BlockSpec((HALO, D_MODEL), prev),
                pl.BlockSpec((HALO, D_MODEL), nxt)]
    args = [x, x, x]
    for o in os:
        w = o.shape[1]
        in_specs += [pl.BlockSpec((FFN_TM, w), tile), pl.BlockSpec((HALO, w), prev), pl.BlockSpec((HALO, w), nxt)]
        args += [o, o, o]
    w_rows = D_MODEL // n_o
    for t in range(n_o):
        in_specs.append(pl.BlockSpec((None, w_rows, D_MODEL), lambda m, c, t=t: (0, t, 0)))
        args.append(w_out)
    in_specs += [
        pl.BlockSpec((None, 6, D_MODEL), lambda m, c: (layer * 8 + cond(m), 0, 0)),
        pl.BlockSpec((None, 1, D_MODEL), lambda m, c: (layer, 0, 0)),
        pl.BlockSpec((None, D_MODEL, FF_CHUNK), lambda m, c: (layer, 0, c)),
        pl.BlockSpec((None, D_MODEL, FF_CHUNK), lambda m, c: (layer, 0, N_FF_CHUNKS + c)),
        pl.BlockSpec((None, 3, FF_CHUNK), lambda m, c: (layer, 0, c)),
        pl.BlockSpec((None, 1, FF_CHUNK), lambda m, c: (layer, 0, c)),
        pl.BlockSpec((None, FF_CHUNK, D_MODEL), lambda m, c: (layer, c, 0)),
        pl.BlockSpec((1, D_MODEL), lambda m, c: (0, 0)),
    ]
    args += [mod, g_ffn, w_up, w_up, conv_w, conv_b, w_down, g_final]
    if final:
        n_lat_tiles = n_tiles - FFN_CTX_TILES
        out_shape = [jax.ShapeDtypeStruct((N_CTX_TOK, D_MODEL), F32),
                     jax.ShapeDtypeStruct((N_TOK - N_CTX_TOK, D_MODEL), F32)]
        out_specs = [
            pl.BlockSpec((FFN_TM, D_MODEL), lambda m, c: (jnp.minimum(m, FFN_CTX_TILES - 1), 0)),
            pl.BlockSpec((FFN_TM, D_MODEL), lambda m, c: (jnp.clip(m - FFN_CTX_TILES, 0, n_lat_tiles - 1), 0)),
        ]
    else:
        out_shape = [jax.ShapeDtypeStruct((N_TOK, D_MODEL), F32)]
        out_specs = [pl.BlockSpec((FFN_TM, D_MODEL), tile)]
    return pl.pallas_call(
        functools.partial(_ffn_kernel, n_o=n_o, final=final),
        grid=(n_tiles, N_FF_CHUNKS), in_specs=in_specs, out_specs=out_specs, out_shape=out_shape,
        scratch_shapes=[pltpu.VMEM((FFN_TM + 2 * HALO, D_MODEL), BF16), pltpu.VMEM((FFN_TM, D_MODEL), F32)],
        compiler_params=pltpu.CompilerParams(
            dimension_semantics=("arbitrary", "arbitrary"), vmem_limit_bytes=FFN_VMEM_LIMIT),
        name="ffn_final" if final else "ffn",
    )(*args)


def _rope_tables(rot_dim):
    t = jnp.arange(LAT_LEN)
    row = (t // GRID_W).astype(F32)
    col = (t % GRID_W).astype(F32)
    d_axis = rot_dim // 2
    freqs = ROPE_THETA ** (-jnp.arange(0, d_axis, 2, dtype=F32) / d_axis)
    ang = jnp.concatenate([row[:, None] * freqs, col[:, None] * freqs], axis=-1)
    cos = jnp.repeat(jnp.cos(ang), 2, axis=-1)
    sin = jnp.repeat(jnp.sin(ang), 2, axis=-1) * jnp.tile(jnp.array([-1.0, 1.0], F32), rot_dim // 2)
    if rot_dim == HEAD_DIM:
        cos = jnp.tile(cos, (1, 2))
        sin = jnp.tile(sin, (1, 2))
    else:
        cos = jnp.concatenate([jnp.ones((LAT_LEN, QK_NOPE), F32), cos,
                               jnp.ones((LAT_LEN, 128 - QK_NOPE - rot_dim), F32)], axis=-1)
        sin = jnp.concatenate([jnp.zeros((LAT_LEN, QK_NOPE), F32), sin,
                               jnp.zeros((LAT_LEN, 128 - QK_NOPE - rot_dim), F32)], axis=-1)
    cos = jnp.concatenate([cos, jnp.ones((TM, 128), F32)], axis=0)
    sin = jnp.concatenate([sin, jnp.zeros((TM, 128), F32)], axis=0)
    return cos, sin


def kernel(x_prompt, x_sample, cache_a_k, cache_a_v, cache_b_ckv, cache_b_kpe, cache_c_k, cache_c_v, c, c_ctx, w_mod, b_mod, g_mix_norm, g_ffn_norm, w_in_e, g_qnorm_a, g_knorm_a, g_cq_b, w_uq_b, g_ckv_b, w_ukv_b, w_out_e, w_in_o, sink_c, w_out_o, w_up, conv_w, conv_b, w_down, g_final):
    depth = w_mod.shape[0]
    cond8 = jnp.concatenate([c_ctx[None, :], c, jnp.zeros((5, D_MODEL), F32)], axis=0)
    lane_pad = KB_PAD - QK_NOPE - QK_ROPE
    wkpe = jnp.pad(w_in_e[0][:, IN_E_MAIN:], ((0, 0), (QK_NOPE, lane_pad)))
    wuq = jnp.pad(w_uq_b[0].reshape(Q_LORA, N_HEADS_B, QK_NOPE + QK_ROPE),
                  ((0, 0), (0, 0), (0, lane_pad))).reshape(Q_LORA, N_HEADS_B * KB_PAD)
    wukv3 = w_ukv_b[0].reshape(KV_LORA, N_HEADS_B, QK_NOPE + V_DIM_B)
    wukk = jnp.pad(wukv3[:, :, :QK_NOPE], ((0, 0), (0, 0), (0, KB_PAD - QK_NOPE))
                   ).reshape(KV_LORA, N_HEADS_B * KB_PAD)
    wukv = wukv3[:, :, QK_NOPE:].reshape(KV_LORA, N_HEADS_B * V_DIM_B)
    gq = jnp.tile(g_qnorm_a[0], N_HEADS_A)[None, :]
    gk = jnp.tile(g_knorm_a[0], N_KV_A)[None, :]
    seg = jnp.arange(256) // HEAD_DIM
    ones = (seg[:, None] == seg[None, :]).astype(BF16)
    cosa, sina = _rope_tables(HEAD_DIM)
    cosb, sinb = _rope_tables(QK_ROPE)
    g_mix3 = g_mix_norm.reshape(depth, 1, D_MODEL)
    g_ffn3 = g_ffn_norm.reshape(depth, 1, D_MODEL)
    conv_b3 = conv_b.reshape(depth, 1, D_FF)
    g_fin2 = g_final[None, :]

    mod = _modulation(cond8, w_mod, b_mod).reshape(depth * 8, 6, D_MODEL)

    n_past = N_LAT_SEQ * PAST_LEN
    kpe_pad = jnp.pad(cache_b_kpe.reshape(n_past, QK_ROPE), ((0, 0), (QK_NOPE, lane_pad)))
    ka_c, va_c, kb_c, vb_c, kc_c, vc_c = _ctx_prep(
        cache_a_k.reshape(n_past, 128), cache_a_v.reshape(n_past, 128),
        cache_b_ckv.reshape(n_past, KV_LORA), kpe_pad,
        cache_c_k.reshape(n_past, 128), cache_c_v.reshape(n_past, 128), wukk, wukv)

    x0, qa, ka, va, qb, kb, vb, nak, nav, nckv, nkpe = _pre0(
        x_prompt.reshape(N_CTX_TOK, D_MODEL), x_sample.reshape(N_TOK - N_CTX_TOK, D_MODEL), mod, g_mix3,
        w_in_e, wkpe, gq, gk, g_cq_b.reshape(1, 1, Q_LORA), g_ckv_b.reshape(1, 1, KV_LORA),
        wuq, wukk, wukv, cosa, sina, cosb, sinb, ones)
    ctx_kw = dict(n_seq=N_CTX_SEQ, seq_len=CTX_LEN, tok_base=0, tq=CTX_LEN, window=0)
    lat_kw = dict(n_seq=N_LAT_SEQ, seq_len=LAT_LEN, tok_base=N_CTX_TOK)
    oa = _attention(qa, ka, va, None, None, None, None, name="attn_a_ctx", **ctx_kw)
    oa = _attention(qa, ka, va, ka_c, va_c, None, oa, tq=256, window=0, name="attn_a_lat", **lat_kw)
    ob = _attention(qb, kb, vb, None, None, None, None, name="attn_b_ctx", **ctx_kw)
    ob = _attention(qb, kb, vb, kb_c, vb_c, None, ob, tq=256, window=0, name="attn_b_lat", **lat_kw)
    (x1,) = _ffn(x0, [oa, ob], mod, 0, g_ffn3, w_out_e, w_up, conv_w, conv_b3, w_down, g_fin2, final=False)

    qc, kc, vc, nck, ncv = _pre1(x1, mod, g_mix3, w_in_o, cosa, sina)
    sink = sink_c[0]
    oc = _attention(qc, kc, vc, None, None, sink, None, name="attn_c_ctx", **ctx_kw)
    oc = _attention(qc, kc, vc, kc_c, vc_c, sink, oc, tq=128, window=WINDOW, name="attn_c_lat", **lat_kw)
    y_prompt, y_sample = _ffn(x1, [oc], mod, 1, g_ffn3, w_out_o, w_up, conv_w, conv_b3, w_down, g_fin2,
                              final=True)

    return (y_prompt.reshape(N_CTX_SEQ, CTX_LEN, D_MODEL), y_sample.reshape(N_LAT_SEQ, LAT_LEN, D_MODEL),
            nak.reshape(N_CTX_SEQ, 1, CTX_LEN, N_KV_A, HEAD_DIM),
            nav.reshape(N_CTX_SEQ, 1, CTX_LEN, N_KV_A, HEAD_DIM),
            nckv.reshape(N_CTX_SEQ, 1, CTX_LEN, KV_LORA),
            nkpe.reshape(N_CTX_SEQ, 1, CTX_LEN, QK_ROPE),
            nck.reshape(N_CTX_SEQ, 1, CTX_LEN, N_KV_C, HEAD_DIM),
            ncv.reshape(N_CTX_SEQ, 1, CTX_LEN, N_KV_C, HEAD_DIM))
```

```python
import functools

import jax
import jax.numpy as jnp
from jax import lax
from jax.experimental import pallas as pl
from jax.experimental.pallas import tpu as pltpu

F32 = jnp.float32
BF16 = jnp.bfloat16

D_MODEL = 1024
N_CTX_SEQ = 16
CTX_LEN = 256
N_LAT_SEQ = 2
LAT_LEN = 2048
PAST_LEN = 512
GRID_W = 64
ROPE_THETA = 10000.0
NORM_EPS = 1e-6
WINDOW = 128
NEG_INF = -1e30
HEAD_DIM = 64
N_HEADS_A, N_KV_A = 8, 2
N_HEADS_B = 8
Q_LORA, KV_LORA = 384, 256
QK_NOPE, QK_ROPE, V_DIM_B = 64, 32, 64
N_HEADS_C, N_KV_C = 16, 2
D_FF = 2816
IN_E_MAIN = N_HEADS_A * HEAD_DIM + 2 * N_KV_A * HEAD_DIM + Q_LORA + KV_LORA

N_CTX_TOK = N_CTX_SEQ * CTX_LEN
N_TOK = N_CTX_TOK + N_LAT_SEQ * LAT_LEN
TM = 256
N_TILES = N_TOK // TM
N_CTX_TILES = N_CTX_TOK // TM
LAT_TILES = LAT_LEN // TM
HALO = 16
FFN_TM = 1024
FFN_SUB = CTX_LEN
UP_LOOKAHEAD = 3
FFN_CTX_TILES = N_CTX_TOK // FFN_TM
FFN_LAT_TILES = LAT_LEN // FFN_TM
FF_CHUNK = 256
N_FF_CHUNKS = D_FF // FF_CHUNK
KB_PAD = 128
VT_ROWS = 80
Q_UNIT = 256
KEY_CHUNK = 512
SCORE_LOOKAHEAD = 2
VMEM_LIMIT = 56 * 1024 * 1024
FFN_VMEM_LIMIT = 60 * 1024 * 1024


def _dot(a, b):
    return jnp.dot(a, b, preferred_element_type=F32)


def _dot_nt(a, b):
    return lax.dot_general(a, b, (((1,), (1,)), ((), ())), preferred_element_type=F32)


def _rms(x, g):
    return x * lax.rsqrt(jnp.mean(x * x, axis=-1, keepdims=True) + NORM_EPS) * g


def _head_rms(x, g, ones_ref):
    w = x.shape[1]
    sq = x * x
    hi = sq.astype(BF16)
    lo = (sq - hi.astype(F32)).astype(BF16)
    parts = []
    for c in range(0, w, 256):
        cw = min(256, w - c)
        ones = ones_ref[0:cw, 0:cw]
        parts.append(_dot(hi[:, c:c + cw], ones) + _dot(lo[:, c:c + cw], ones))
    ssum = parts[0] if len(parts) == 1 else jnp.concatenate(parts, axis=1)
    return x * lax.rsqrt(ssum * (1.0 / HEAD_DIM) + NORM_EPS) * g


def _swap_pairs(x):
    w = x.shape[1]
    up = pltpu.roll(x, w - 1, axis=1)
    dn = pltpu.roll(x, 1, axis=1)
    lane = lax.broadcasted_iota(jnp.int32, x.shape, 1)
    return jnp.where((lane & 1) == 0, up, dn)


def _rope(x, cos, sin_signed, reps):
    if reps > 1:
        cos = jnp.concatenate([cos] * reps, axis=1)
        sin_signed = jnp.concatenate([sin_signed] * reps, axis=1)
    return x * cos + _swap_pairs(x) * sin_signed


def _store_vt(vt_ref, v, n_heads):
    t = v.shape[0]
    vt = v.T.astype(BF16)
    ones = jnp.ones((VT_ROWS - 64, t), BF16)
    for hh in range(n_heads):
        vt_ref[hh, 0:64, :] = vt[hh * 64:(hh + 1) * 64]
        vt_ref[hh, 64:VT_ROWS, :] = ones


def _cond_row(i):
    return jnp.where(i < N_CTX_TILES, 0, 1 + (i - N_CTX_TILES) // LAT_TILES)


def _rope_block(i):
    return jnp.where(i < N_CTX_TILES, LAT_TILES, (i - N_CTX_TILES) % LAT_TILES)


def _const_spec(shape):
    zeros = (0,) * len(shape)
    return pl.BlockSpec(shape, lambda *_: zeros)


def _layer_spec(shape, layer):
    idx = (layer,) + (0,) * len(shape)
    return pl.BlockSpec((None,) + tuple(shape), lambda *_: idx)


def _mod_kernel(cond_ref, w_ref, b_ref, o_ref):
    c = cond_ref[...]
    s = c * jax.nn.sigmoid(c)
    o_ref[0] = jnp.dot(s, w_ref[0], preferred_element_type=F32,
                       precision=lax.Precision.HIGHEST) + b_ref[0]


def _modulation(cond8, w_mod, b_mod):
    depth, _, n = w_mod.shape
    tn = 1536
    return pl.pallas_call(
        _mod_kernel,
        grid=(depth, n // tn),
        in_specs=[
            pl.BlockSpec((8, D_MODEL), lambda l, j: (0, 0)),
            pl.BlockSpec((1, D_MODEL, tn), lambda l, j: (l, 0, j)),
            pl.BlockSpec((1, 1, tn), lambda l, j: (l, 0, j)),
        ],
        out_specs=pl.BlockSpec((1, 8, tn), lambda l, j: (l, 0, j)),
        out_shape=jax.ShapeDtypeStruct((depth, 8, n), F32),
        compiler_params=pltpu.CompilerParams(
            dimension_semantics=("arbitrary", "arbitrary"), vmem_limit_bytes=VMEM_LIMIT),
        name="modulation",
    )(cond8, w_mod, b_mod.reshape(depth, 1, n))


def _pre0_kernel(xp_ref, xs_ref, mod_ref, gmix_ref, win_ref, wkpe_ref, gq_ref, gk_ref, gcq_ref, gckv_ref,
                 wuq_ref, wukk_ref, wukv_ref, cosa_ref, sina_ref, cosb_ref, sinb_ref, ones_ref,
                 x0_ref, qa_ref, ka_ref, va_ref, qb_ref, kb_ref, vb_ref,
                 nak_ref, nav_ref, nckv_ref, nkpe_ref,
                 win_s, wuq_s, wukk_s, wukv_s):
    i = pl.program_id(0)

    @pl.when(i == 0)
    def _():
        win_s[:, 0:IN_E_MAIN] = win_ref[...].astype(BF16)
        win_s[:, IN_E_MAIN:] = wkpe_ref[...].astype(BF16)
        wuq_s[...] = wuq_ref[...].astype(BF16)
        wukk_s[...] = wukk_ref[...].astype(BF16)
        wukv_s[...] = wukv_ref[...].astype(BF16)

    x = jnp.where(i < N_CTX_TILES, xp_ref[...], xs_ref[...])
    x0_ref[...] = x
    sh1 = mod_ref[0:1, :]
    sc1 = mod_ref[1:2, :]
    h = _rms(x, gmix_ref[...]) * (1.0 + sc1) + sh1
    proj = _dot(h.astype(BF16), win_s[...])
    qa = _head_rms(proj[:, 0:512], gq_ref[...], ones_ref)
    ka = _head_rms(proj[:, 512:640], gk_ref[...], ones_ref)
    va = proj[:, 640:768]
    cq = _rms(proj[:, 768:1152], gcq_ref[...])
    ckv = _rms(proj[:, 1152:1408], gckv_ref[...])
    kpe = proj[:, 1408:1536]
    qb = _dot(cq.astype(BF16), wuq_s[...])

    cosa, sina = cosa_ref[...], sina_ref[...]
    cosb, sinb = cosb_ref[...], sinb_ref[...]
    qa = _rope(qa, cosa, sina, 4)
    ka = _rope(ka, cosa, sina, 1)
    qb = _rope(qb, cosb, sinb, 8)
    kpe = _rope(kpe, cosb, sinb, 1)

    ckv_b = ckv.astype(BF16)
    kbn = _dot(ckv_b, wukk_s[...])
    vb = _dot(ckv_b, wukv_s[...])

    qa_s = qa * (HEAD_DIM ** -0.5)
    qb_s = qb * ((QK_NOPE + QK_ROPE) ** -0.5)
    for hh in range(N_HEADS_A):
        qa_ref[hh] = qa_s[:, hh * 64:(hh + 1) * 64].astype(BF16)
    for hh in range(N_KV_A):
        ka_ref[hh] = ka[:, hh * 64:(hh + 1) * 64].astype(BF16)
    _store_vt(va_ref, va, N_KV_A)
    for hh in range(N_HEADS_B):
        qb_ref[hh] = qb_s[:, hh * KB_PAD:(hh + 1) * KB_PAD].astype(BF16)
        kb_ref[hh] = (kbn[:, hh * KB_PAD:(hh + 1) * KB_PAD] + kpe).astype(BF16)
    _store_vt(vb_ref, vb, N_HEADS_B)

    @pl.when(i < N_CTX_TILES)
    def _():
        nak_ref[...] = ka
        nav_ref[...] = va
        nckv_ref[...] = ckv
        nkpe_ref[...] = kpe[:, QK_NOPE:QK_NOPE + QK_ROPE]


def _pre0(xp, xs, mod, gmix, w_in_e, wkpe, gq, gk, gcq, gckv, wuq, wukk, wukv, cosa, sina, cosb, sinb, ones):
    tile = lambda i: (i, 0)
    head_tile = lambda i: (0, i, 0)
    vt_tile = lambda i: (0, 0, i)
    ctx_tile = lambda i: (jnp.minimum(i, N_CTX_TILES - 1), 0)
    lat_tile = lambda i: (jnp.maximum(i - N_CTX_TILES, 0), 0)
    rope_tile = lambda i: (_rope_block(i), 0)
    in_specs = [
        pl.BlockSpec((TM, D_MODEL), ctx_tile),
        pl.BlockSpec((TM, D_MODEL), lat_tile),
        pl.BlockSpec((None, 6, D_MODEL), lambda i: (_cond_row(i), 0, 0)),
        _layer_spec((1, D_MODEL), 0),
        _layer_spec((D_MODEL, IN_E_MAIN), 0),
        _const_spec(wkpe.shape),
        _const_spec(gq.shape), _const_spec(gk.shape), _layer_spec((1, Q_LORA), 0), _layer_spec((1, KV_LORA), 0),
        _const_spec(wuq.shape), _const_spec(wukk.shape), _const_spec(wukv.shape),
        pl.BlockSpec((TM, 128), rope_tile), pl.BlockSpec((TM, 128), rope_tile),
        pl.BlockSpec((TM, 128), rope_tile), pl.BlockSpec((TM, 128), rope_tile),
        _const_spec(ones.shape),
    ]
    out_shape = [
        jax.ShapeDtypeStruct((N_TOK, D_MODEL), F32),
        jax.ShapeDtypeStruct((N_HEADS_A, N_TOK, 64), BF16),
        jax.ShapeDtypeStruct((N_KV_A, N_TOK, 64), BF16),
        jax.ShapeDtypeStruct((N_KV_A, VT_ROWS, N_TOK), BF16),
        jax.ShapeDtypeStruct((N_HEADS_B, N_TOK, KB_PAD), BF16),
        jax.ShapeDtypeStruct((N_HEADS_B, N_TOK, KB_PAD), BF16),
        jax.ShapeDtypeStruct((N_HEADS_B, VT_ROWS, N_TOK), BF16),
        jax.ShapeDtypeStruct((N_CTX_TOK, 128), F32),
        jax.ShapeDtypeStruct((N_CTX_TOK, 128), F32),
        jax.ShapeDtypeStruct((N_CTX_TOK, KV_LORA), F32),
        jax.ShapeDtypeStruct((N_CTX_TOK, QK_ROPE), F32),
    ]
    out_specs = [
        pl.BlockSpec((TM, D_MODEL), tile),
        pl.BlockSpec((N_HEADS_A, TM, 64), head_tile),
        pl.BlockSpec((N_KV_A, TM, 64), head_tile),
        pl.BlockSpec((N_KV_A, VT_ROWS, TM), vt_tile),
        pl.BlockSpec((N_HEADS_B, TM, KB_PAD), head_tile),
        pl.BlockSpec((N_HEADS_B, TM, KB_PAD), head_tile),
        pl.BlockSpec((N_HEADS_B, VT_ROWS, TM), vt_tile),
        pl.BlockSpec((TM, 128), ctx_tile),
        pl.BlockSpec((TM, 128), ctx_tile),
        pl.BlockSpec((TM, KV_LORA), ctx_tile),
        pl.BlockSpec((TM, QK_ROPE), ctx_tile),
    ]
    scratch = [
        pltpu.VMEM((D_MODEL, IN_E_MAIN + KB_PAD), BF16),
        pltpu.VMEM(wuq.shape, BF16), pltpu.VMEM(wukk.shape, BF16), pltpu.VMEM(wukv.shape, BF16),
    ]
    return pl.pallas_call(
        _pre0_kernel, grid=(N_TILES,), in_specs=in_specs, out_specs=out_specs, out_shape=out_shape,
        scratch_shapes=scratch,
        compiler_params=pltpu.CompilerParams(
            dimension_semantics=("arbitrary",), vmem_limit_bytes=VMEM_LIMIT),
        name="pre0",
    )(xp, xs, mod, gmix, w_in_e, wkpe, gq, gk, gcq, gckv, wuq, wukk, wukv, cosa, sina, cosb, sinb, ones)


def _pre1_kernel(x_ref, mod_ref, gmix_ref, win_ref, cosa_ref, sina_ref,
                 q_ref, k_ref, v_ref, nk_ref, nv_ref, win_s):
    i = pl.program_id(0)

    @pl.when(i == 0)
    def _():
        win_s[...] = win_ref[...].astype(BF16)

    x = x_ref[...]
    sh1 = mod_ref[0:1, :]
    sc1 = mod_ref[1:2, :]
    h = _rms(x, gmix_ref[...]) * (1.0 + sc1) + sh1
    proj = _dot(h.astype(BF16), win_s[...])
    cosa, sina = cosa_ref[...], sina_ref[...]
    q = _rope(proj[:, 0:1024], cosa, sina, 8) * (HEAD_DIM ** -0.5)
    k = _rope(proj[:, 1024:1152], cosa, sina, 1)
    v = proj[:, 1152:1280]
    for hh in range(N_HEADS_C):
        q_ref[hh] = q[:, hh * 64:(hh + 1) * 64].astype(BF16)
    for hh in range(N_KV_C):
        k_ref[hh] = k[:, hh * 64:(hh + 1) * 64].astype(BF16)
    _store_vt(v_ref, v, N_KV_C)

    @pl.when(i < N_CTX_TILES)
    def _():
        nk_ref[...] = k
        nv_ref[...] = v


def _pre1(x, mod, gmix, w_in_o, cosa, sina):
    n_in = w_in_o.shape[-1]
    tile = lambda i: (i, 0)
    head_tile = lambda i: (0, i, 0)
    vt_tile = lambda i: (0, 0, i)
    ctx_tile = lambda i: (jnp.minimum(i, N_CTX_TILES - 1), 0)
    rope_tile = lambda i: (_rope_block(i), 0)
    in_specs = [
        pl.BlockSpec((TM, D_MODEL), tile),
        pl.BlockSpec((None, 6, D_MODEL), lambda i: (8 + _cond_row(i), 0, 0)),
        _layer_spec((1, D_MODEL), 1),
        _layer_spec((D_MODEL, n_in), 0),
        pl.BlockSpec((TM, 128), rope_tile), pl.BlockSpec((TM, 128), rope_tile),
    ]
    out_shape = [
        jax.ShapeDtypeStruct((N_HEADS_C, N_TOK, 64), BF16),
        jax.ShapeDtypeStruct((N_KV_C, N_TOK, 64), BF16),
        jax.ShapeDtypeStruct((N_KV_C, VT_ROWS, N_TOK), BF16),
        jax.ShapeDtypeStruct((N_CTX_TOK, 128), F32),
        jax.ShapeDtypeStruct((N_CTX_TOK, 128), F32),
    ]
    out_specs = [
        pl.BlockSpec((N_HEADS_C, TM, 64), head_tile),
        pl.BlockSpec((N_KV_C, TM, 64), head_tile),
        pl.BlockSpec((N_KV_C, VT_ROWS, TM), vt_tile),
        pl.BlockSpec((TM, 128), ctx_tile),
        pl.BlockSpec((TM, 128), ctx_tile),
    ]
    return pl.pallas_call(
        _pre1_kernel, grid=(N_TILES,), in_specs=in_specs, out_specs=out_specs, out_shape=out_shape,
        scratch_shapes=[pltpu.VMEM((D_MODEL, n_in), BF16)],
        compiler_params=pltpu.CompilerParams(
            dimension_semantics=("arbitrary",), vmem_limit_bytes=VMEM_LIMIT),
        name="pre1",
    )(x, mod, gmix, w_in_o, cosa, sina)


def _ctx_kernel(ak_ref, av_ref, ckv_ref, kpe_ref, ck_ref, cv_ref, wukk_ref, wukv_ref,
                ka_ref, va_ref, kb_ref, vb_ref, kc_ref, vc_ref):
    ak, av, ck, cv = ak_ref[...], av_ref[...], ck_ref[...], cv_ref[...]
    for hh in range(2):
        sl = slice(hh * 64, (hh + 1) * 64)
        ka_ref[hh] = ak[:, sl].astype(BF16)
        kc_ref[hh] = ck[:, sl].astype(BF16)
    _store_vt(va_ref, av, 2)
    _store_vt(vc_ref, cv, 2)
    ckv_b = ckv_ref[...].astype(BF16)
    kbn = _dot(ckv_b, wukk_ref[...].astype(BF16))
    vb = _dot(ckv_b, wukv_ref[...].astype(BF16))
    kpe = kpe_ref[...]
    for hh in range(N_HEADS_B):
        kb_ref[hh] = (kbn[:, hh * KB_PAD:(hh + 1) * KB_PAD] + kpe).astype(BF16)
    _store_vt(vb_ref, vb, N_HEADS_B)


def _ctx_prep(ak, av, ckv, kpe_pad, ck, cv, wukk, wukv):
    n = N_LAT_SEQ * PAST_LEN
    row = lambda b: (b, 0)
    head_row = lambda b: (0, b, 0)
    vt_row = lambda b: (0, 0, b)
    in_specs = [
        pl.BlockSpec((PAST_LEN, 128), row), pl.BlockSpec((PAST_LEN, 128), row),
        pl.BlockSpec((PAST_LEN, KV_LORA), row), pl.BlockSpec((PAST_LEN, KB_PAD), row),
        pl.BlockSpec((PAST_LEN, 128), row), pl.BlockSpec((PAST_LEN, 128), row),
        _const_spec(wukk.shape), _const_spec(wukv.shape),
    ]
    out_shape = [
        jax.ShapeDtypeStruct((2, n, 64), BF16), jax.ShapeDtypeStruct((2, VT_ROWS, n), BF16),
        jax.ShapeDtypeStruct((N_HEADS_B, n, KB_PAD), BF16), jax.ShapeDtypeStruct((N_HEADS_B, VT_ROWS, n), BF16),
        jax.ShapeDtypeStruct((2, n, 64), BF16), jax.ShapeDtypeStruct((2, VT_ROWS, n), BF16),
    ]
    out_specs = [
        pl.BlockSpec((2, PAST_LEN, 64), head_row), pl.BlockSpec((2, VT_ROWS, PAST_LEN), vt_row),
        pl.BlockSpec((N_HEADS_B, PAST_LEN, KB_PAD), head_row), pl.BlockSpec((N_HEADS_B, VT_ROWS, PAST_LEN), vt_row),
        pl.BlockSpec((2, PAST_LEN, 64), head_row), pl.BlockSpec((2, VT_ROWS, PAST_LEN), vt_row),
    ]
    return pl.pallas_call(
        _ctx_kernel, grid=(N_LAT_SEQ,), in_specs=in_specs, out_specs=out_specs, out_shape=out_shape,
        compiler_params=pltpu.CompilerParams(
            dimension_semantics=("arbitrary",), vmem_limit_bytes=VMEM_LIMIT),
        name="ctx_prep",
    )(ak, av, ckv, kpe_pad, ck, cv, wukk, wukv)


def _softmax_units(units, lookahead):
    tasks = [(u, c) for u, unit in enumerate(units) for c in range(len(unit["chunks"]))]
    scores = {}

    def emit_scores(t):
        u, c = tasks[t]
        k, _, mask = units[u]["chunks"][c]
        s = _dot_nt(k, units[u]["q"])
        scores[t] = s if mask is None else jnp.where(mask, s, NEG_INF)

    for t in range(min(lookahead, len(tasks))):
        emit_scores(t)
    for t, (u, c) in enumerate(tasks):
        if t + lookahead < len(tasks):
            emit_scores(t + lookahead)
        unit = units[u]
        s = scores.pop(t)
        m, acc = unit["m"], unit["acc"]
        cmax = jnp.max(s, axis=0, keepdims=True)
        m_new = cmax if m is None else jnp.maximum(m, cmax)
        pv = _dot(unit["chunks"][c][1], jnp.exp(s - m_new).astype(BF16))
        unit["acc"] = pv if acc is None else acc * jnp.exp(m - m_new) + pv
        unit["m"] = m_new
    return [unit["acc"][0:64] * (1.0 / unit["acc"][64:65]) for unit in units]


def _attn_kernel(*refs, n_kv, group, tq, seq_len, has_ctx, has_sink, has_alias, window):
    refs = list(refs)
    q_ref = refs.pop(0)
    if window:
        kp_ref, kc_ref, kn_ref, vp_ref, vc_ref, vn_ref = refs[:6]
        refs = refs[6:]
    else:
        k_ref, vt_ref = refs[:2]
        refs = refs[2:]
    if has_ctx:
        kx_ref, vx_ref = refs[:2]
        refs = refs[2:]
    if has_sink:
        sink_ref = refs.pop(0)
    if has_alias:
        refs.pop(0)
    o_ref = refs.pop(0)

    j = pl.program_id(1)
    dk = q_ref.shape[-1]
    heads_per_unit = Q_UNIT // tq
    lane = lax.broadcasted_iota(jnp.int32, (1, Q_UNIT), 1)
    if window:
        n_band = tq + 2 * window
        krow = lax.broadcasted_iota(jnp.int32, (n_band, Q_UNIT), 0)
        qcol = lax.broadcasted_iota(jnp.int32, (n_band, Q_UNIT), 1) & (tq - 1)
        rel = (krow - window) - qcol
        band_mask = ((jnp.abs(rel) <= window)
                     & ((krow >= window) | (j > 0))
                     & ((krow < window + tq) | (j < seq_len // tq - 1)))
    if has_sink:
        acc0 = jnp.where(lax.broadcasted_iota(jnp.int32, (VT_ROWS, Q_UNIT), 0) >= 64, 1.0, 0.0)

    units = []
    for hk in range(n_kv):
        chunks = []
        if window:
            chunks.append((jnp.concatenate([kp_ref[hk], kc_ref[hk], kn_ref[hk]], axis=0),
                           jnp.concatenate([vp_ref[hk], vc_ref[hk], vn_ref[hk]], axis=1), band_mask))
        else:
            for c in range(0, seq_len, KEY_CHUNK):
                n = min(KEY_CHUNK, seq_len - c)
                chunks.append((k_ref[hk, c:c + n, :], vt_ref[hk, :, c:c + n], None))
        if has_ctx:
            chunks.append((kx_ref[hk], vx_ref[hk], None))
        for u in range(group // heads_per_unit):
            h0 = hk * group + u * heads_per_unit
            unit = dict(q=q_ref[h0:h0 + heads_per_unit].reshape(Q_UNIT, dk), chunks=chunks, m=None, acc=None)
            if has_sink:
                m0 = jnp.full((1, Q_UNIT), sink_ref[h0], F32)
                for e in range(1, heads_per_unit):
                    m0 = jnp.where(lane >= e * tq, sink_ref[h0 + e], m0)
                unit.update(m=m0, acc=acc0)
            units.append(unit)
    outs = []
    for o in _softmax_units(units, SCORE_LOOKAHEAD):
        for e in range(heads_per_unit):
            outs.append(o[:, e * tq:(e + 1) * tq])
    o_ref[...] = jnp.concatenate(outs, axis=0).T.astype(BF16)


def _attention(q, k, vt, kx, vx, sink, other, *, n_seq, seq_len, tok_base, tq, window, name):
    n_q, _, dk = q.shape
    n_kv = k.shape[0]
    group = n_q // n_kv
    n_qt = seq_len // tq
    q_base = tok_base // tq
    s_base = tok_base // seq_len
    has_ctx = kx is not None
    has_sink = sink is not None
    has_alias = other is not None
    q_blk = lambda b, j: q_base + b * n_qt + j
    in_specs = [pl.BlockSpec((n_q, tq, dk), lambda b, j: (0, q_blk(b, j), 0))]
    args = [q]
    if window:
        assert window == tq
        prev = lambda b, j: q_base + b * n_qt + jnp.maximum(j - 1, 0)
        nxt = lambda b, j: q_base + b * n_qt + jnp.minimum(j + 1, n_qt - 1)
        for blk in (prev, q_blk, nxt):
            in_specs.append(pl.BlockSpec((n_kv, tq, dk), lambda b, j, blk=blk: (0, blk(b, j), 0)))
        for blk in (prev, q_blk, nxt):
            in_specs.append(pl.BlockSpec((n_kv, VT_ROWS, tq), lambda b, j, blk=blk: (0, 0, blk(b, j))))
        args += [k, k, k, vt, vt, vt]
    else:
        in_specs += [
            pl.BlockSpec((n_kv, seq_len, dk), lambda b, j: (0, s_base + b, 0)),
            pl.BlockSpec((n_kv, VT_ROWS, seq_len), lambda b, j: (0, 0, s_base + b)),
        ]
        args += [k, vt]
    if has_ctx:
        in_specs += [
            pl.BlockSpec((n_kv, PAST_LEN, dk), lambda b, j: (0, b, 0)),
            pl.BlockSpec((n_kv, VT_ROWS, PAST_LEN), lambda b, j: (0, 0, b)),
        ]
        args += [kx, vx]
    if has_sink:
        in_specs.append(pl.BlockSpec(memory_space=pltpu.SMEM))
        args.append(sink)
    aliases = {}
    if has_alias:
        aliases = {len(args): 0}
        in_specs.append(pl.BlockSpec(memory_space=pl.ANY))
        args.append(other)
    kern = functools.partial(_attn_kernel, n_kv=n_kv, group=group, tq=tq, seq_len=seq_len,
                             has_ctx=has_ctx, has_sink=has_sink, has_alias=has_alias, window=window)
    return pl.pallas_call(
        kern, grid=(n_seq, n_qt), in_specs=in_specs,
        out_specs=pl.BlockSpec((tq, n_q * HEAD_DIM), lambda b, j: (q_blk(b, j), 0)),
        out_shape=jax.ShapeDtypeStruct((N_TOK, n_q * HEAD_DIM), BF16),
        input_output_aliases=aliases,
        compiler_params=pltpu.CompilerParams(
            dimension_semantics=("arbitrary", "arbitrary"), vmem_limit_bytes=VMEM_LIMIT),
        name=name,
    )(*args)


def _ffn_kernel(*refs, n_o, final):
    x_ref, xp_ref, xn_ref = refs[0:3]
    o_refs = [refs[3 + 3 * t:6 + 3 * t] for t in range(n_o)]
    pos = 3 + 3 * n_o
    wo_refs = refs[pos:pos + n_o]
    pos += n_o
    mod_ref, gffn_ref, wg_ref, wv_ref, cw_ref, cb_ref, wd_ref, gfin_ref = refs[pos:pos + 8]
    pos += 8
    out_refs = refs[pos:pos + (2 if final else 1)]
    h2e_ref, acc_ref = refs[pos + len(out_refs):]

    m = pl.program_id(0)
    c = pl.program_id(1)
    is_ctx = m < FFN_CTX_TILES
    part = (m - FFN_CTX_TILES) % FFN_LAT_TILES
    has_prev = jnp.logical_and(jnp.logical_not(is_ctx), part != 0)
    has_next = jnp.logical_and(jnp.logical_not(is_ctx), part != FFN_LAT_TILES - 1)
    g1 = mod_ref[2:3, :]
    sh2 = mod_ref[3:4, :]
    sc2 = mod_ref[4:5, :]
    g2 = mod_ref[5:6, :]

    @pl.when(c == 0)
    def _():
        wos = [w[...].astype(BF16) for w in wo_refs]

        def residual_and_norm(xv, ovs):
            attn = _dot(ovs[0], wos[0])
            for ov, wo in zip(ovs[1:], wos[1:]):
                attn = attn + _dot(ov, wo)
            x1 = xv + g1 * attn
            return x1, (_rms(x1, gffn_ref[...]) * (1.0 + sc2) + sh2).astype(BF16)

        for r in range(0, FFN_TM, FFN_SUB):
            x1, h2 = residual_and_norm(x_ref[r:r + FFN_SUB, :], [o[0][r:r + FFN_SUB, :] for o in o_refs])
            acc_ref[r:r + FFN_SUB, :] = x1
            h2e_ref[HALO + r:HALO + r + FFN_SUB, :] = h2
        _, h2h = residual_and_norm(
            jnp.concatenate([xp_ref[...], xn_ref[...]], axis=0),
            [jnp.concatenate([o[1][...], o[2][...]], axis=0) for o in o_refs])
        h2e_ref[0:HALO, :] = h2h[0:HALO]
        h2e_ref[HALO + FFN_TM:, :] = h2h[HALO:]

    w_up = jnp.concatenate([wg_ref[...].astype(BF16), wv_ref[...].astype(BF16)], axis=1)
    w_dn = wd_ref[...].astype(BF16)
    cw = cw_ref[...]
    cb = cb_ref[...]
    sub_rows = FFN_SUB + 2 * HALO
    n_sub = FFN_TM // FFN_SUB
    row8 = lax.broadcasted_iota(jnp.int32, (8, FF_CHUNK), 0)
    is_lat = jnp.logical_not(is_ctx)
    ups = {}

    def emit_up(r):
        ups[r] = _dot(h2e_ref[r * FFN_SUB:r * FFN_SUB + sub_rows, :], w_up)

    for r in range(min(UP_LOOKAHEAD, n_sub)):
        emit_up(r)
    for r in range(n_sub):
        if r + UP_LOOKAHEAD < n_sub:
            emit_up(r + UP_LOOKAHEAD)
        up = ups.pop(r)
        ge = up[:, :FF_CHUNK]
        val = up[HALO:HALO + FFN_SUB, FF_CHUNK:]
        g_prev = pltpu.roll(ge, 1, axis=0)[HALO:HALO + FFN_SUB]
        g_next = pltpu.roll(ge, sub_rows - 1, axis=0)[HALO:HALO + FFN_SUB]
        prev_ok = has_prev if r == 0 else is_lat
        next_ok = has_next if r == n_sub - 1 else is_lat
        g_prev = jnp.concatenate(
            [jnp.where(jnp.logical_or(row8 != 0, prev_ok), g_prev[0:8], 0.0), g_prev[8:]], axis=0)
        g_next = jnp.concatenate(
            [g_next[:-8], jnp.where(jnp.logical_or(row8 != 7, next_ok), g_next[-8:], 0.0)], axis=0)
        gate = g_prev * cw[0:1, :] + ge[HALO:HALO + FFN_SUB] * cw[1:2, :] + g_next * cw[2:3, :] + cb
        act = (gate * jax.nn.sigmoid(gate) * val).astype(BF16)
        acc_ref[r * FFN_SUB:(r + 1) * FFN_SUB, :] += g2 * _dot(act, w_dn)

    @pl.when(c == N_FF_CHUNKS - 1)
    def _():
        x2 = acc_ref[...]
        if final:
            x2 = _rms(x2, gfin_ref[...])

            @pl.when(is_ctx)
            def _():
                out_refs[0][...] = x2

            @pl.when(jnp.logical_not(is_ctx))
            def _():
                out_refs[1][...] = x2
        else:
            out_refs[0][...] = x2


def _ffn(x, os, mod, layer, g_ffn, w_out, w_up, conv_w, conv_b, w_down, g_final, *, final):
    nh = FFN_TM // HALO
    nblk = N_TOK // HALO
    n_tiles = N_TOK // FFN_TM
    n_o = len(os)
    tile = lambda m, c: (m, 0)
    prev = lambda m, c: (jnp.maximum(m * nh - 1, 0), 0)
    nxt = lambda m, c: (jnp.minimum((m + 1) * nh, nblk - 1), 0)
    cond = lambda m: jnp.where(m < FFN_CTX_TILES, 0, 1 + (m - FFN_CTX_TILES) // FFN_LAT_TILES)
    in_specs = [pl.BlockSpec((FFN_TM, D_MODEL), tile), pl.BlockSpec((HALO, D_MODEL), prev),
                pl.BlockSpec((HALO, D_MODEL), nxt)]
    args = [x, x, x]
    for o in os:
        w = o.shape[1]
        in_specs += [pl.BlockSpec((FFN_TM, w), tile), pl.BlockSpec((HALO, w), prev), pl.BlockSpec((HALO, w), nxt)]
        args += [o, o, o]
    w_rows = D_MODEL // n_o
    for t in range(n_o):
        in_specs.append(pl.BlockSpec((None, w_rows, D_MODEL), lambda m, c, t=t: (0, t, 0)))
        args.append(w_out)
    in_specs += [
        pl.BlockSpec((None, 6, D_MODEL), lambda m, c: (layer * 8 + cond(m), 0, 0)),
        pl.BlockSpec((None, 1, D_MODEL), lambda m, c: (layer, 0, 0)),
        pl.BlockSpec((None, D_MODEL, FF_CHUNK), lambda m, c: (layer, 0, c)),
        pl.BlockSpec((None, D_MODEL, FF_CHUNK), lambda m, c: (layer, 0, N_FF_CHUNKS + c)),
        pl.BlockSpec((None, 3, FF_CHUNK), lambda m, c: (layer, 0, c)),
        pl.BlockSpec((None, 1, FF_CHUNK), lambda m, c: (layer, 0, c)),
        pl.BlockSpec((None, FF_CHUNK, D_MODEL), lambda m, c: (layer, c, 0)),
        pl.BlockSpec((1, D_MODEL), lambda m, c: (0, 0)),
    ]
    args += [mod, g_ffn, w_up, w_up, conv_w, conv_b, w_down, g_final]
    if final:
        n_lat_tiles = n_tiles - FFN_CTX_TILES
        out_shape = [jax.ShapeDtypeStruct((N_CTX_TOK, D_MODEL), F32),
                     jax.ShapeDtypeStruct((N_TOK - N_CTX_TOK, D_MODEL), F32)]
        out_specs = [
            pl.BlockSpec((FFN_TM, D_MODEL), lambda m, c: (jnp.minimum(m, FFN_CTX_TILES - 1), 0)),
            pl.BlockSpec((FFN_TM, D_MODEL), lambda m, c: (jnp.clip(m - FFN_CTX_TILES, 0, n_lat_tiles - 1), 0)),
        ]
    else:
        out_shape = [jax.ShapeDtypeStruct((N_TOK, D_MODEL), F32)]
        out_specs = [pl.BlockSpec((FFN_TM, D_MODEL), tile)]
    return pl.pallas_call(
        functools.partial(_ffn_kernel, n_o=n_o, final=final),
        grid=(n_tiles, N_FF_CHUNKS), in_specs=in_specs, out_specs=out_specs, out_shape=out_shape,
        scratch_shapes=[pltpu.VMEM((FFN_TM + 2 * HALO, D_MODEL), BF16), pltpu.VMEM((FFN_TM, D_MODEL), F32)],
        compiler_params=pltpu.CompilerParams(
            dimension_semantics=("arbitrary", "arbitrary"), vmem_limit_bytes=FFN_VMEM_LIMIT),
        name="ffn_final" if final else "ffn",
    )(*args)


def _rope_tables(rot_dim):
    t = jnp.arange(LAT_LEN)
    row = (t // GRID_W).astype(F32)
    col = (t % GRID_W).astype(F32)
    d_axis = rot_dim // 2
    freqs = ROPE_THETA ** (-jnp.arange(0, d_axis, 2, dtype=F32) / d_axis)
    ang = jnp.concatenate([row[:, None] * freqs, col[:, None] * freqs], axis=-1)
    cos = jnp.repeat(jnp.cos(ang), 2, axis=-1)
    sin = jnp.repeat(jnp.sin(ang), 2, axis=-1) * jnp.tile(jnp.array([-1.0, 1.0], F32), rot_dim // 2)
    if rot_dim == HEAD_DIM:
        cos = jnp.tile(cos, (1, 2))
        sin = jnp.tile(sin, (1, 2))
    else:
        cos = jnp.concatenate([jnp.ones((LAT_LEN, QK_NOPE), F32), cos,
                               jnp.ones((LAT_LEN, 128 - QK_NOPE - rot_dim), F32)], axis=-1)
        sin = jnp.concatenate([jnp.zeros((LAT_LEN, QK_NOPE), F32), sin,
                               jnp.zeros((LAT_LEN, 128 - QK_NOPE - rot_dim), F32)], axis=-1)
    cos = jnp.concatenate([cos, jnp.ones((TM, 128), F32)], axis=0)
    sin = jnp.concatenate([sin, jnp.zeros((TM, 128), F32)], axis=0)
    return cos, sin


def kernel(x_prompt, x_sample, cache_a_k, cache_a_v, cache_b_ckv, cache_b_kpe, cache_c_k, cache_c_v, c, c_ctx, w_mod, b_mod, g_mix_norm, g_ffn_norm, w_in_e, g_qnorm_a, g_knorm_a, g_cq_b, w_uq_b, g_ckv_b, w_ukv_b, w_out_e, w_in_o, sink_c, w_out_o, w_up, conv_w, conv_b, w_down, g_final):
    depth = w_mod.shape[0]
    cond8 = jnp.concatenate([c_ctx[None, :], c, jnp.zeros((5, D_MODEL), F32)], axis=0)
    lane_pad = KB_PAD - QK_NOPE - QK_ROPE
    wkpe = jnp.pad(w_in_e[0][:, IN_E_MAIN:], ((0, 0), (QK_NOPE, lane_pad)))
    wuq = jnp.pad(w_uq_b[0].reshape(Q_LORA, N_HEADS_B, QK_NOPE + QK_ROPE),
                  ((0, 0), (0, 0), (0, lane_pad))).reshape(Q_LORA, N_HEADS_B * KB_PAD)
    wukv3 = w_ukv_b[0].reshape(KV_LORA, N_HEADS_B, QK_NOPE + V_DIM_B)
    wukk = jnp.pad(wukv3[:, :, :QK_NOPE], ((0, 0), (0, 0), (0, KB_PAD - QK_NOPE))
                   ).reshape(KV_LORA, N_HEADS_B * KB_PAD)
    wukv = wukv3[:, :, QK_NOPE:].reshape(KV_LORA, N_HEADS_B * V_DIM_B)
    gq = jnp.tile(g_qnorm_a[0], N_HEADS_A)[None, :]
    gk = jnp.tile(g_knorm_a[0], N_KV_A)[None, :]
    seg = jnp.arange(256) // HEAD_DIM
    ones = (seg[:, None] == seg[None, :]).astype(BF16)
    cosa, sina = _rope_tables(HEAD_DIM)
    cosb, sinb = _rope_tables(QK_ROPE)
    g_mix3 = g_mix_norm.reshape(depth, 1, D_MODEL)
    g_ffn3 = g_ffn_norm.reshape(depth, 1, D_MODEL)
    conv_b3 = conv_b.reshape(depth, 1, D_FF)
    g_fin2 = g_final[None, :]

    mod = _modulation(cond8, w_mod, b_mod).reshape(depth * 8, 6, D_MODEL)

    n_past = N_LAT_SEQ * PAST_LEN
    kpe_pad = jnp.pad(cache_b_kpe.reshape(n_past, QK_ROPE), ((0, 0), (QK_NOPE, lane_pad)))
    ka_c, va_c, kb_c, vb_c, kc_c, vc_c = _ctx_prep(
        cache_a_k.reshape(n_past, 128), cache_a_v.reshape(n_past, 128),
        cache_b_ckv.reshape(n_past, KV_LORA), kpe_pad,
        cache_c_k.reshape(n_past, 128), cache_c_v.reshape(n_past, 128), wukk, wukv)

    x0, qa, ka, va, qb, kb, vb, nak, nav, nckv, nkpe = _pre0(
        x_prompt.reshape(N_CTX_TOK, D_MODEL), x_sample.reshape(N_TOK - N_CTX_TOK, D_MODEL), mod, g_mix3,
        w_in_e, wkpe, gq, gk, g_cq_b.reshape(1, 1, Q_LORA), g_ckv_b.reshape(1, 1, KV_LORA),
        wuq, wukk, wukv, cosa, sina, cosb, sinb, ones)
    ctx_kw = dict(n_seq=N_CTX_SEQ, seq_len=CTX_LEN, tok_base=0, tq=CTX_LEN, window=0)
    lat_kw = dict(n_seq=N_LAT_SEQ, seq_len=LAT_LEN, tok_base=N_CTX_TOK)
    oa = _attention(qa, ka, va, None, None, None, None, name="attn_a_ctx", **ctx_kw)
    oa = _attention(qa, ka, va, ka_c, va_c, None, oa, tq=256, window=0, name="attn_a_lat", **lat_kw)
    ob = _attention(qb, kb, vb, None, None, None, None, name="attn_b_ctx", **ctx_kw)
    ob = _attention(qb, kb, vb, kb_c, vb_c, None, ob, tq=256, window=0, name="attn_b_lat", **lat_kw)
    (x1,) = _ffn(x0, [oa, ob], mod, 0, g_ffn3, w_out_e, w_up, conv_w, conv_b3, w_down, g_fin2, final=False)

    qc, kc, vc, nck, ncv = _pre1(x1, mod, g_mix3, w_in_o, cosa, sina)
    sink = sink_c[0]
    oc = _attention(qc, kc, vc, None, None, sink, None, name="attn_c_ctx", **ctx_kw)
    oc = _attention(qc, kc, vc, kc_c, vc_c, sink, oc, tq=128, window=WINDOW, name="attn_c_lat", **lat_kw)
    y_prompt, y_sample = _ffn(x1, [oc], mod, 1, g_ffn3, w_out_o, w_up, conv_w, conv_b3, w_down, g_fin2,
                              final=True)

    return (y_prompt.reshape(N_CTX_SEQ, CTX_LEN, D_MODEL), y_sample.reshape(N_LAT_SEQ, LAT_LEN, D_MODEL),
            nak.reshape(N_CTX_SEQ, 1, CTX_LEN, N_KV_A, HEAD_DIM),
            nav.reshape(N_CTX_SEQ, 1, CTX_LEN, N_KV_A, HEAD_DIM),
            nckv.reshape(N_CTX_SEQ, 1, CTX_LEN, KV_LORA),
            nkpe.reshape(N_CTX_SEQ, 1, CTX_LEN, QK_ROPE),
            nck.reshape(N_CTX_SEQ, 1, CTX_LEN, N_KV_C, HEAD_DIM),
            ncv.reshape(N_CTX_SEQ, 1, CTX_LEN, N_KV_C, HEAD_DIM))
```

```python
import functools

import jax
import jax.numpy as jnp
from jax import lax
from jax.experimental import pallas as pl
from jax.experimental.pallas import tpu as pltpu

F32 = jnp.float32
BF16 = jnp.bfloat16

D_MODEL = 1024
N_CTX_SEQ = 16
CTX_LEN = 256
N_LAT_SEQ = 2
LAT_LEN = 2048
PAST_LEN = 512
GRID_W = 64
ROPE_THETA = 10000.0
NORM_EPS = 1e-6
WINDOW = 128
NEG_INF = -1e30
HEAD_DIM = 64
N_HEADS_A, N_KV_A = 8, 2
N_HEADS_B = 8
Q_LORA, KV_LORA = 384, 256
QK_NOPE, QK_ROPE, V_DIM_B = 64, 32, 64
N_HEADS_C, N_KV_C = 16, 2
D_FF = 2816
IN_E_MAIN = N_HEADS_A * HEAD_DIM + 2 * N_KV_A * HEAD_DIM + Q_LORA + KV_LORA

N_CTX_TOK = N_CTX_SEQ * CTX_LEN
N_TOK = N_CTX_TOK + N_LAT_SEQ * LAT_LEN
TM = 256
N_TILES = N_TOK // TM
N_CTX_TILES = N_CTX_TOK // TM
LAT_TILES = LAT_LEN // TM
HALO = 16
FFN_TM = 1024
FFN_SUB = CTX_LEN
UP_LOOKAHEAD = 3
FFN_CTX_TILES = N_CTX_TOK // FFN_TM
FFN_LAT_TILES = LAT_LEN // FFN_TM
FF_CHUNK = 256
N_FF_CHUNKS = D_FF // FF_CHUNK
KB_PAD = 128
VT_ROWS = 80
Q_UNIT = 256
KEY_CHUNK = 512
SCORE_LOOKAHEAD = 5
VMEM_LIMIT = 56 * 1024 * 1024
FFN_VMEM_LIMIT = 60 * 1024 * 1024


def _dot(a, b):
    return jnp.dot(a, b, preferred_element_type=F32)


def _dot_nt(a, b):
    return lax.dot_general(a, b, (((1,), (1,)), ((), ())), preferred_element_type=F32)


def _rms(x, g):
    return x * lax.rsqrt(jnp.mean(x * x, axis=-1, keepdims=True) + NORM_EPS) * g


def _split_bf16(x):
    hi = x.astype(BF16)
    return hi, (x - hi.astype(F32)).astype(BF16)


def _head_rms(x, g, ones_ref):
    w = x.shape[1]
    hi, lo = _split_bf16(x * x)
    parts = []
    for c in range(0, w, 256):
        cw = min(256, w - c)
        ones = ones_ref[0:cw, 0:cw]
        parts.append(_dot(hi[:, c:c + cw], ones) + _dot(lo[:, c:c + cw], ones))
    ssum = parts[0] if len(parts) == 1 else jnp.concatenate(parts, axis=1)
    return x * lax.rsqrt(ssum * (1.0 / HEAD_DIM) + NORM_EPS) * g


def _swap_pairs(x):
    w = x.shape[1]
    up = pltpu.roll(x, w - 1, axis=1)
    dn = pltpu.roll(x, 1, axis=1)
    lane = lax.broadcasted_iota(jnp.int32, x.shape, 1)
    return jnp.where((lane & 1) == 0, up, dn)


def _rope(x, cos, sin_signed, reps):
    if reps > 1:
        cos = jnp.concatenate([cos] * reps, axis=1)
        sin_signed = jnp.concatenate([sin_signed] * reps, axis=1)
    return x * cos + _swap_pairs(x) * sin_signed


def _store_vt(vt_ref, v, n_heads):
    t = v.shape[0]
    vt = v.T.astype(BF16)
    ones = jnp.ones((VT_ROWS - 64, t), BF16)
    for hh in range(n_heads):
        vt_ref[hh, 0:64, :] = vt[hh * 64:(hh + 1) * 64]
        vt_ref[hh, 64:VT_ROWS, :] = ones


def _cond_row(i):
    return jnp.where(i < N_CTX_TILES, 0, 1 + (i - N_CTX_TILES) // LAT_TILES)


def _rope_block(i):
    return jnp.maximum(i - N_CTX_TILES, 0) % LAT_TILES


def _const_spec(shape):
    zeros = (0,) * len(shape)
    return pl.BlockSpec(shape, lambda *_: zeros)


def _layer_spec(shape, layer):
    idx = (layer,) + (0,) * len(shape)
    return pl.BlockSpec((None,) + tuple(shape), lambda *_: idx)


def _mod_kernel(cond_ref, w_ref, b_ref, o_ref):
    c = cond_ref[...]
    s_hi, s_lo = _split_bf16(c * jax.nn.sigmoid(c))
    w_hi, w_lo = _split_bf16(w_ref[0])
    r = _dot(jnp.concatenate([s_hi, s_lo], axis=0), w_hi)
    o_ref[0] = r[0:8] + r[8:16] + _dot(s_hi, w_lo) + b_ref[0]


def _modulation(cond8, w_mod, b_mod):
    depth, _, n = w_mod.shape
    tn = 1536
    return pl.pallas_call(
        _mod_kernel,
        grid=(depth, n // tn),
        in_specs=[
            pl.BlockSpec((8, D_MODEL), lambda l, j: (0, 0)),
            pl.BlockSpec((1, D_MODEL, tn), lambda l, j: (l, 0, j)),
            pl.BlockSpec((1, 1, tn), lambda l, j: (l, 0, j)),
        ],
        out_specs=pl.BlockSpec((1, 8, tn), lambda l, j: (l, 0, j)),
        out_shape=jax.ShapeDtypeStruct((depth, 8, n), F32),
        compiler_params=pltpu.CompilerParams(
            dimension_semantics=("arbitrary", "arbitrary"), vmem_limit_bytes=VMEM_LIMIT),
        name="modulation",
    )(cond8, w_mod, b_mod.reshape(depth, 1, n))


def _pre0_kernel(xp_ref, xs_ref, mod_ref, gmix_ref, win_ref, wkpe_ref, gq_ref, gk_ref, gcq_ref, gckv_ref,
                 wuq_ref, wukk_ref, wukv_ref, cosa_ref, sina_ref, cosb_ref, sinb_ref, ones_ref,
                 x0_ref, qa_ref, ka_ref, va_ref, qb_ref, kb_ref, vb_ref,
                 nak_ref, nav_ref, nckv_ref, nkpe_ref,
                 win_s, wuq_s, wukk_s, wukv_s):
    i = pl.program_id(0)

    @pl.when(i == 0)
    def _():
        win_s[:, 0:IN_E_MAIN] = win_ref[...].astype(BF16)
        win_s[:, IN_E_MAIN:] = wkpe_ref[...].astype(BF16)
        wuq_s[...] = wuq_ref[...].astype(BF16)
        wukk_s[...] = wukk_ref[...].astype(BF16)
        wukv_s[...] = wukv_ref[...].astype(BF16)

    x = jnp.where(i < N_CTX_TILES, xp_ref[...], xs_ref[...])
    x0_ref[...] = x
    sh1 = mod_ref[0:1, :]
    sc1 = mod_ref[1:2, :]
    h = _rms(x, gmix_ref[...]) * (1.0 + sc1) + sh1
    proj = _dot(h.astype(BF16), win_s[...])
    qa = _head_rms(proj[:, 0:512], gq_ref[...], ones_ref)
    ka = _head_rms(proj[:, 512:640], gk_ref[...], ones_ref)
    va = proj[:, 640:768]
    cq = _rms(proj[:, 768:1152], gcq_ref[...])
    ckv = _rms(proj[:, 1152:1408], gckv_ref[...])
    kpe = proj[:, 1408:1536]
    qb = _dot(cq.astype(BF16), wuq_s[...])

    ckv_b = ckv.astype(BF16)
    kbn = _dot(ckv_b, wukk_s[...])
    vb = _dot(ckv_b, wukv_s[...])
    _store_vt(va_ref, va, N_KV_A)
    _store_vt(vb_ref, vb, N_HEADS_B)

    def store_qk(qa, ka, qb, kpe):
        qa_s = qa * (HEAD_DIM ** -0.5)
        qb_s = qb * ((QK_NOPE + QK_ROPE) ** -0.5)
        for hh in range(N_HEADS_A):
            qa_ref[hh] = qa_s[:, hh * 64:(hh + 1) * 64].astype(BF16)
        for hh in range(N_KV_A):
            ka_ref[hh] = ka[:, hh * 64:(hh + 1) * 64].astype(BF16)
        for hh in range(N_HEADS_B):
            qb_ref[hh] = qb_s[:, hh * KB_PAD:(hh + 1) * KB_PAD].astype(BF16)
            kb_ref[hh] = (kbn[:, hh * KB_PAD:(hh + 1) * KB_PAD] + kpe).astype(BF16)

    @pl.when(i < N_CTX_TILES)
    def _():
        store_qk(qa, ka, qb, kpe)
        nak_ref[...] = ka
        nav_ref[...] = va
        nckv_ref[...] = ckv
        nkpe_ref[...] = kpe[:, QK_NOPE:QK_NOPE + QK_ROPE]

    @pl.when(i >= N_CTX_TILES)
    def _():
        cosa, sina = cosa_ref[...], sina_ref[...]
        cosb, sinb = cosb_ref[...], sinb_ref[...]
        store_qk(_rope(qa, cosa, sina, 4), _rope(ka, cosa, sina, 1),
                 _rope(qb, cosb, sinb, 8), _rope(kpe, cosb, sinb, 1))


def _pre0(xp, xs, mod, gmix, w_in_e, wkpe, gq, gk, gcq, gckv, wuq, wukk, wukv, cosa, sina, cosb, sinb, ones):
    tile = lambda i: (i, 0)
    head_tile = lambda i: (0, i, 0)
    vt_tile = lambda i: (0, 0, i)
    ctx_tile = lambda i: (jnp.minimum(i, N_CTX_TILES - 1), 0)
    lat_tile = lambda i: (jnp.maximum(i - N_CTX_TILES, 0), 0)
    rope_tile = lambda i: (_rope_block(i), 0)
    in_specs = [
        pl.BlockSpec((TM, D_MODEL), ctx_tile),
        pl.BlockSpec((TM, D_MODEL), lat_tile),
        pl.BlockSpec((None, 6, D_MODEL), lambda i: (_cond_row(i), 0, 0)),
        _layer_spec((1, D_MODEL), 0),
        _layer_spec((D_MODEL, IN_E_MAIN), 0),
        _const_spec(wkpe.shape),
        _const_spec(gq.shape), _const_spec(gk.shape), _layer_spec((1, Q_LORA), 0), _layer_spec((1, KV_LORA), 0),
        _const_spec(wuq.shape), _const_spec(wukk.shape), _const_spec(wukv.shape),
        pl.BlockSpec((TM, 128), rope_tile), pl.BlockSpec((TM, 128), rope_tile),
        pl.BlockSpec((TM, 128), rope_tile), pl.BlockSpec((TM, 128), rope_tile),
        _const_spec(ones.shape),
    ]
    out_shape = [
        jax.ShapeDtypeStruct((N_TOK, D_MODEL), F32),
        jax.ShapeDtypeStruct((N_HEADS_A, N_TOK, 64), BF16),
        jax.ShapeDtypeStruct((N_KV_A, N_TOK, 64), BF16),
        jax.ShapeDtypeStruct((N_KV_A, VT_ROWS, N_TOK), BF16),
        jax.ShapeDtypeStruct((N_HEADS_B, N_TOK, KB_PAD), BF16),
        jax.ShapeDtypeStruct((N_HEADS_B, N_TOK, KB_PAD), BF16),
        jax.ShapeDtypeStruct((N_HEADS_B, VT_ROWS, N_TOK), BF16),
        jax.ShapeDtypeStruct((N_CTX_TOK, 128), F32),
        jax.ShapeDtypeStruct((N_CTX_TOK, 128), F32),
        jax.ShapeDtypeStruct((N_CTX_TOK, KV_LORA), F32),
        jax.ShapeDtypeStruct((N_CTX_TOK, QK_ROPE), F32),
    ]
    out_specs = [
        pl.BlockSpec((TM, D_MODEL), tile),
        pl.BlockSpec((N_HEADS_A, TM, 64), head_tile),
        pl.BlockSpec((N_KV_A, TM, 64), head_tile),
        pl.BlockSpec((N_KV_A, VT_ROWS, TM), vt_tile),
        pl.BlockSpec((N_HEADS_B, TM, KB_PAD), head_tile),
        pl.BlockSpec((N_HEADS_B, TM, KB_PAD), head_tile),
        pl.BlockSpec((N_HEADS_B, VT_ROWS, TM), vt_tile),
        pl.BlockSpec((TM, 128), ctx_tile),
        pl.BlockSpec((TM, 128), ctx_tile),
        pl.BlockSpec((TM, KV_LORA), ctx_tile),
        pl.BlockSpec((TM, QK_ROPE), ctx_tile),
    ]
    scratch = [
        pltpu.VMEM((D_MODEL, IN_E_MAIN + KB_PAD), BF16),
        pltpu.VMEM(wuq.shape, BF16), pltpu.VMEM(wukk.shape, BF16), pltpu.VMEM(wukv.shape, BF16),
    ]
    return pl.pallas_call(
        _pre0_kernel, grid=(N_TILES,), in_specs=in_specs, out_specs=out_specs, out_shape=out_shape,
        scratch_shapes=scratch,
        compiler_params=pltpu.CompilerParams(
            dimension_semantics=("arbitrary",), vmem_limit_bytes=VMEM_LIMIT),
        name="pre0",
    )(xp, xs, mod, gmix, w_in_e, wkpe, gq, gk, gcq, gckv, wuq, wukk, wukv, cosa, sina, cosb, sinb, ones)


def _pre1_kernel(x_ref, mod_ref, gmix_ref, win_ref, cosa_ref, sina_ref,
                 q_ref, k_ref, v_ref, nk_ref, nv_ref, win_s):
    i = pl.program_id(0)

    @pl.when(i == 0)
    def _():
        win_s[...] = win_ref[...].astype(BF16)

    x = x_ref[...]
    sh1 = mod_ref[0:1, :]
    sc1 = mod_ref[1:2, :]
    h = _rms(x, gmix_ref[...]) * (1.0 + sc1) + sh1
    proj = _dot(h.astype(BF16), win_s[...])
    q = proj[:, 0:1024]
    k = proj[:, 1024:1152]
    v = proj[:, 1152:1280]
    _store_vt(v_ref, v, N_KV_C)

    def store_qk(q, k):
        q_s = q * (HEAD_DIM ** -0.5)
        for hh in range(N_HEADS_C):
            q_ref[hh] = q_s[:, hh * 64:(hh + 1) * 64].astype(BF16)
        for hh in range(N_KV_C):
            k_ref[hh] = k[:, hh * 64:(hh + 1) * 64].astype(BF16)

    @pl.when(i < N_CTX_TILES)
    def _():
        store_qk(q, k)
        nk_ref[...] = k
        nv_ref[...] = v

    @pl.when(i >= N_CTX_TILES)
    def _():
        cosa, sina = cosa_ref[...], sina_ref[...]
        store_qk(_rope(q, cosa, sina, 8), _rope(k, cosa, sina, 1))


def _pre1(x, mod, gmix, w_in_o, cosa, sina):
    n_in = w_in_o.shape[-1]
    tile = lambda i: (i, 0)
    head_tile = lambda i: (0, i, 0)
    vt_tile = lambda i: (0, 0, i)
    ctx_tile = lambda i: (jnp.minimum(i, N_CTX_TILES - 1), 0)
    rope_tile = lambda i: (_rope_block(i), 0)
    in_specs = [
        pl.BlockSpec((TM, D_MODEL), tile),
        pl.BlockSpec((None, 6, D_MODEL), lambda i: (8 + _cond_row(i), 0, 0)),
        _layer_spec((1, D_MODEL), 1),
        _layer_spec((D_MODEL, n_in), 0),
        pl.BlockSpec((TM, 128), rope_tile), pl.BlockSpec((TM, 128), rope_tile),
    ]
    out_shape = [
        jax.ShapeDtypeStruct((N_HEADS_C, N_TOK, 64), BF16),
        jax.ShapeDtypeStruct((N_KV_C, N_TOK, 64), BF16),
        jax.ShapeDtypeStruct((N_KV_C, VT_ROWS, N_TOK), BF16),
        jax.ShapeDtypeStruct((N_CTX_TOK, 128), F32),
        jax.ShapeDtypeStruct((N_CTX_TOK, 128), F32),
    ]
    out_specs = [
        pl.BlockSpec((N_HEADS_C, TM, 64), head_tile),
        pl.BlockSpec((N_KV_C, TM, 64), head_tile),
        pl.BlockSpec((N_KV_C, VT_ROWS, TM), vt_tile),
        pl.BlockSpec((TM, 128), ctx_tile),
        pl.BlockSpec((TM, 128), ctx_tile),
    ]
    return pl.pallas_call(
        _pre1_kernel, grid=(N_TILES,), in_specs=in_specs, out_specs=out_specs, out_shape=out_shape,
        scratch_shapes=[pltpu.VMEM((D_MODEL, n_in), BF16)],
        compiler_params=pltpu.CompilerParams(
            dimension_semantics=("arbitrary",), vmem_limit_bytes=VMEM_LIMIT),
        name="pre1",
    )(x, mod, gmix, w_in_o, cosa, sina)


def _ctx_kernel(ak_ref, av_ref, ckv_ref, kpe_ref, ck_ref, cv_ref, wukk_ref, wukv_ref,
                ka_ref, va_ref, kb_ref, vb_ref, kc_ref, vc_ref):
    ak, av, ck, cv = ak_ref[...], av_ref[...], ck_ref[...], cv_ref[...]
    for hh in range(2):
        sl = slice(hh * 64, (hh + 1) * 64)
        ka_ref[hh] = ak[:, sl].astype(BF16)
        kc_ref[hh] = ck[:, sl].astype(BF16)
    _store_vt(va_ref, av, 2)
    _store_vt(vc_ref, cv, 2)
    ckv_b = ckv_ref[...].astype(BF16)
    kbn = _dot(ckv_b, wukk_ref[...].astype(BF16))
    vb = _dot(ckv_b, wukv_ref[...].astype(BF16))
    kpe = kpe_ref[...]
    for hh in range(N_HEADS_B):
        kb_ref[hh] = (kbn[:, hh * KB_PAD:(hh + 1) * KB_PAD] + kpe).astype(BF16)
    _store_vt(vb_ref, vb, N_HEADS_B)


def _ctx_prep(ak, av, ckv, kpe_pad, ck, cv, wukk, wukv):
    n = N_LAT_SEQ * PAST_LEN
    row = lambda b: (b, 0)
    head_row = lambda b: (0, b, 0)
    vt_row = lambda b: (0, 0, b)
    in_specs = [
        pl.BlockSpec((PAST_LEN, 128), row), pl.BlockSpec((PAST_LEN, 128), row),
        pl.BlockSpec((PAST_LEN, KV_LORA), row), pl.BlockSpec((PAST_LEN, KB_PAD), row),
        pl.BlockSpec((PAST_LEN, 128), row), pl.BlockSpec((PAST_LEN, 128), row),
        _const_spec(wukk.shape), _const_spec(wukv.shape),
    ]
    out_shape = [
        jax.ShapeDtypeStruct((2, n, 64), BF16), jax.ShapeDtypeStruct((2, VT_ROWS, n), BF16),
        jax.ShapeDtypeStruct((N_HEADS_B, n, KB_PAD), BF16), jax.ShapeDtypeStruct((N_HEADS_B, VT_ROWS, n), BF16),
        jax.ShapeDtypeStruct((2, n, 64), BF16), jax.ShapeDtypeStruct((2, VT_ROWS, n), BF16),
    ]
    out_specs = [
        pl.BlockSpec((2, PAST_LEN, 64), head_row), pl.BlockSpec((2, VT_ROWS, PAST_LEN), vt_row),
        pl.BlockSpec((N_HEADS_B, PAST_LEN, KB_PAD), head_row), pl.BlockSpec((N_HEADS_B, VT_ROWS, PAST_LEN), vt_row),
        pl.BlockSpec((2, PAST_LEN, 64), head_row), pl.BlockSpec((2, VT_ROWS, PAST_LEN), vt_row),
    ]
    return pl.pallas_call(
        _ctx_kernel, grid=(N_LAT_SEQ,), in_specs=in_specs, out_specs=out_specs, out_shape=out_shape,
        compiler_params=pltpu.CompilerParams(
            dimension_semantics=("arbitrary",), vmem_limit_bytes=VMEM_LIMIT),
        name="ctx_prep",
    )(ak, av, ckv, kpe_pad, ck, cv, wukk, wukv)


def _softmax_units(units, lookahead):
    tasks = [(u, c) for u, unit in enumerate(units) for c in range(len(unit["chunks"]))]
    scores = {}

    def emit_scores(t):
        u, c = tasks[t]
        k, _, mask = units[u]["chunks"][c]
        s = _dot_nt(k, units[u]["q"])
        scores[t] = s if mask is None else jnp.where(mask, s, NEG_INF)

    for t in range(min(lookahead, len(tasks))):
        emit_scores(t)
    for t, (u, c) in enumerate(tasks):
        if t + lookahead < len(tasks):
            emit_scores(t + lookahead)
        unit = units[u]
        s = scores.pop(t)
        m, acc = unit["m"], unit["acc"]
        cmax = jnp.max(s, axis=0, keepdims=True)
        m_new = cmax if m is None else jnp.maximum(m, cmax)
        pv = _dot(unit["chunks"][c][1], jnp.exp(s - m_new).astype(BF16))
        unit["acc"] = pv if acc is None else acc * jnp.exp(m - m_new) + pv
        unit["m"] = m_new
    return [unit["acc"][0:64] * (1.0 / unit["acc"][64:65]) for unit in units]


def _attn_kernel(*refs, n_kv, group, tq, seq_len, has_ctx, has_sink, window):
    refs = list(refs)
    q_ref = refs.pop(0)
    if window:
        kp_ref, kc_ref, kn_ref, vp_ref, vc_ref, vn_ref = refs[:6]
        refs = refs[6:]
    else:
        k_ref, vt_ref = refs[:2]
        refs = refs[2:]
    if has_ctx:
        kx_ref, vx_ref = refs[:2]
        refs = refs[2:]
    if has_sink:
        sink_ref = refs.pop(0)
    o_ref = refs.pop(0)

    j = pl.program_id(1)
    dk = q_ref.shape[-1]
    heads_per_unit = Q_UNIT // tq
    lane = lax.broadcasted_iota(jnp.int32, (1, Q_UNIT), 1)
    if window:
        n_band = tq + 2 * window
        krow = lax.broadcasted_iota(jnp.int32, (n_band, Q_UNIT), 0)
        qcol = lax.broadcasted_iota(jnp.int32, (n_band, Q_UNIT), 1) & (tq - 1)
        rel = (krow - window) - qcol
        band_mask = ((jnp.abs(rel) <= window)
                     & ((krow >= window) | (j > 0))
                     & ((krow < window + tq) | (j < seq_len // tq - 1)))
    if has_sink:
        acc0 = jnp.where(lax.broadcasted_iota(jnp.int32, (VT_ROWS, Q_UNIT), 0) >= 64, 1.0, 0.0)

    units = []
    for hk in range(n_kv):
        chunks = []
        if window:
            chunks.append((jnp.concatenate([kp_ref[hk], kc_ref[hk], kn_ref[hk]], axis=0),
                           jnp.concatenate([vp_ref[hk], vc_ref[hk], vn_ref[hk]], axis=1), band_mask))
        else:
            for c in range(0, seq_len, KEY_CHUNK):
                n = min(KEY_CHUNK, seq_len - c)
                chunks.append((k_ref[hk, c:c + n, :], vt_ref[hk, :, c:c + n], None))
        if has_ctx:
            chunks.append((kx_ref[hk], vx_ref[hk], None))
        for u in range(group // heads_per_unit):
            h0 = hk * group + u * heads_per_unit
            unit = dict(q=q_ref[h0:h0 + heads_per_unit].reshape(Q_UNIT, dk), chunks=chunks, m=None, acc=None)
            if has_sink:
                m0 = jnp.full((1, Q_UNIT), sink_ref[h0], F32)
                for e in range(1, heads_per_unit):
                    m0 = jnp.where(lane >= e * tq, sink_ref[h0 + e], m0)
                unit.update(m=m0, acc=acc0)
            units.append(unit)
    outs = []
    for o in _softmax_units(units, SCORE_LOOKAHEAD):
        for e in range(heads_per_unit):
            outs.append(o[:, e * tq:(e + 1) * tq])
    o_ref[...] = jnp.concatenate(outs, axis=0).T.astype(BF16)


def _attention(q, k, vt, kx, vx, sink, *, n_seq, seq_len, tok_base, tq, window, name):
    n_q, _, dk = q.shape
    n_kv = k.shape[0]
    group = n_q // n_kv
    n_qt = seq_len // tq
    q_base = tok_base // tq
    s_base = tok_base // seq_len
    has_ctx = kx is not None
    has_sink = sink is not None
    q_blk = lambda b, j: q_base + b * n_qt + j
    in_specs = [pl.BlockSpec((n_q, tq, dk), lambda b, j: (0, q_blk(b, j), 0))]
    args = [q]
    if window:
        assert window == tq
        prev = lambda b, j: q_base + b * n_qt + jnp.maximum(j - 1, 0)
        nxt = lambda b, j: q_base + b * n_qt + jnp.minimum(j + 1, n_qt - 1)
        for blk in (prev, q_blk, nxt):
            in_specs.append(pl.BlockSpec((n_kv, tq, dk), lambda b, j, blk=blk: (0, blk(b, j), 0)))
        for blk in (prev, q_blk, nxt):
            in_specs.append(pl.BlockSpec((n_kv, VT_ROWS, tq), lambda b, j, blk=blk: (0, 0, blk(b, j))))
        args += [k, k, k, vt, vt, vt]
    else:
        in_specs += [
            pl.BlockSpec((n_kv, seq_len, dk), lambda b, j: (0, s_base + b, 0)),
            pl.BlockSpec((n_kv, VT_ROWS, seq_len), lambda b, j: (0, 0, s_base + b)),
        ]
        args += [k, vt]
    if has_ctx:
        in_specs += [
            pl.BlockSpec((n_kv, PAST_LEN, dk), lambda b, j: (0, b, 0)),
            pl.BlockSpec((n_kv, VT_ROWS, PAST_LEN), lambda b, j: (0, 0, b)),
        ]
        args += [kx, vx]
    if has_sink:
        in_specs.append(pl.BlockSpec(memory_space=pltpu.SMEM))
        args.append(sink)
    kern = functools.partial(_attn_kernel, n_kv=n_kv, group=group, tq=tq, seq_len=seq_len,
                             has_ctx=has_ctx, has_sink=has_sink, window=window)
    return pl.pallas_call(
        kern, grid=(n_seq, n_qt), in_specs=in_specs,
        out_specs=pl.BlockSpec((tq, n_q * HEAD_DIM), lambda b, j: (b * n_qt + j, 0)),
        out_shape=jax.ShapeDtypeStruct((n_seq * seq_len, n_q * HEAD_DIM), BF16),
        compiler_params=pltpu.CompilerParams(
            dimension_semantics=("arbitrary", "arbitrary"), vmem_limit_bytes=VMEM_LIMIT),
        name=name,
    )(*args)


def _ffn_kernel(*refs, n_o, final):
    x_ref, xp_ref, xn_ref = refs[0:3]
    o_refs = [refs[3 + 4 * t:7 + 4 * t] for t in range(n_o)]
    pos = 3 + 4 * n_o
    wo_refs = refs[pos:pos + n_o]
    pos += n_o
    mod_ref, gffn_ref, wg_ref, wv_ref, cw_ref, cb_ref, wd_ref, gfin_ref = refs[pos:pos + 8]
    pos += 8
    out_refs = refs[pos:pos + (2 if final else 1)]
    h2e_ref, acc_ref = refs[pos + len(out_refs):]

    m = pl.program_id(0)
    c = pl.program_id(1)
    is_ctx = m < FFN_CTX_TILES
    part = (m - FFN_CTX_TILES) % FFN_LAT_TILES
    has_prev = jnp.logical_and(jnp.logical_not(is_ctx), part != 0)
    has_next = jnp.logical_and(jnp.logical_not(is_ctx), part != FFN_LAT_TILES - 1)
    g1 = mod_ref[2:3, :]
    sh2 = mod_ref[3:4, :]
    sc2 = mod_ref[4:5, :]
    g2 = mod_ref[5:6, :]

    @pl.when(c == 0)
    def _():
        wos = [w[...].astype(BF16) for w in wo_refs]

        def residual_and_norm(xv, ovs):
            attn = _dot(ovs[0], wos[0])
            for ov, wo in zip(ovs[1:], wos[1:]):
                attn = attn + _dot(ov, wo)
            x1 = xv + g1 * attn
            return x1, (_rms(x1, gffn_ref[...]) * (1.0 + sc2) + sh2).astype(BF16)

        for r in range(0, FFN_TM, FFN_SUB):
            ovs = [jnp.where(is_ctx, o[0][r:r + FFN_SUB, :], o[1][r:r + FFN_SUB, :]) for o in o_refs]
            x1, h2 = residual_and_norm(x_ref[r:r + FFN_SUB, :], ovs)
            acc_ref[r:r + FFN_SUB, :] = x1
            h2e_ref[HALO + r:HALO + r + FFN_SUB, :] = h2
        _, h2h = residual_and_norm(
            jnp.concatenate([xp_ref[...], xn_ref[...]], axis=0),
            [jnp.concatenate([o[2][...], o[3][...]], axis=0) for o in o_refs])
        h2e_ref[0:HALO, :] = h2h[0:HALO]
        h2e_ref[HALO + FFN_TM:, :] = h2h[HALO:]

    w_up = jnp.concatenate([wg_ref[...].astype(BF16), wv_ref[...].astype(BF16)], axis=1)
    w_dn = wd_ref[...].astype(BF16)
    cw = cw_ref[...]
    cb = cb_ref[...]
    sub_rows = FFN_SUB + 2 * HALO
    n_sub = FFN_TM // FFN_SUB
    row8 = lax.broadcasted_iota(jnp.int32, (8, FF_CHUNK), 0)
    is_lat = jnp.logical_not(is_ctx)
    ups = {}

    def emit_up(r):
        ups[r] = _dot(h2e_ref[r * FFN_SUB:r * FFN_SUB + sub_rows, :], w_up)

    for r in range(min(UP_LOOKAHEAD, n_sub)):
        emit_up(r)
    for r in range(n_sub):
        if r + UP_LOOKAHEAD < n_sub:
            emit_up(r + UP_LOOKAHEAD)
        up = ups.pop(r)
        ge = up[:, :FF_CHUNK]
        val = up[HALO:HALO + FFN_SUB, FF_CHUNK:]
        g_prev = pltpu.roll(ge, 1, axis=0)[HALO:HALO + FFN_SUB]
        g_next = pltpu.roll(ge, sub_rows - 1, axis=0)[HALO:HALO + FFN_SUB]
        prev_ok = has_prev if r == 0 else is_lat
        next_ok = has_next if r == n_sub - 1 else is_lat
        g_prev = jnp.concatenate(
            [jnp.where(jnp.logical_or(row8 != 0, prev_ok), g_prev[0:8], 0.0), g_prev[8:]], axis=0)
        g_next = jnp.concatenate(
            [g_next[:-8], jnp.where(jnp.logical_or(row8 != 7, next_ok), g_next[-8:], 0.0)], axis=0)
        gate = g_prev * cw[0:1, :] + ge[HALO:HALO + FFN_SUB] * cw[1:2, :] + g_next * cw[2:3, :] + cb
        act = (gate * jax.nn.sigmoid(gate) * val).astype(BF16)
        acc_ref[r * FFN_SUB:(r + 1) * FFN_SUB, :] += g2 * _dot(act, w_dn)

    @pl.when(c == N_FF_CHUNKS - 1)
    def _():
        x2 = acc_ref[...]
        if final:
            x2 = _rms(x2, gfin_ref[...])

            @pl.when(is_ctx)
            def _():
                out_refs[0][...] = x2

            @pl.when(jnp.logical_not(is_ctx))
            def _():
                out_refs[1][...] = x2
        else:
            out_refs[0][...] = x2


def _ffn(x, os, mod, layer, g_ffn, w_out, w_up, conv_w, conv_b, w_down, g_final, *, final):
    nh = FFN_TM // HALO
    nblk = N_TOK // HALO
    n_tiles = N_TOK // FFN_TM
    n_o = len(os)
    tile = lambda m, c: (m, 0)
    prev = lambda m, c: (jnp.maximum(m * nh - 1, 0), 0)
    nxt = lambda m, c: (jnp.minimum((m + 1) * nh, nblk - 1), 0)
    cond = lambda m: jnp.where(m < FFN_CTX_TILES, 0, 1 + (m - FFN_CTX_TILES) // FFN_LAT_TILES)
    in_specs = [pl.BlockSpec((FFN_TM, D_MODEL), tile), pl.BlockSpec((HALO, D_MODEL), prev),
                pl.BlockSpec((HALO, D_MODEL), nxt)]
    args = [x, x, x]
    n_lat_tiles = n_tiles - FFN_CTX_TILES
    n_lat_blk = (N_TOK - N_CTX_TOK) // HALO
    ctx_tile = lambda m, c: (jnp.minimum(m, FFN_CTX_TILES - 1), 0)
    lat_tile = lambda m, c: (jnp.clip(m - FFN_CTX_TILES, 0, n_lat_tiles - 1), 0)
    lat_prev = lambda m, c: (jnp.clip((m - FFN_CTX_TILES) * nh - 1, 0, n_lat_blk - 1), 0)
    lat_next = lambda m, c: (jnp.clip((m - FFN_CTX_TILES + 1) * nh, 0, n_lat_blk - 1), 0)
    for o_ctx, o_lat in os:
        w = o_ctx.shape[1]
        in_specs += [pl.BlockSpec((FFN_TM, w), ctx_tile), pl.BlockSpec((FFN_TM, w), lat_tile),
                     pl.BlockSpec((HALO, w), lat_prev), pl.BlockSpec((HALO, w), lat_next)]
        args += [o_ctx, o_lat, o_lat, o_lat]
    w_rows = D_MODEL // n_o
    for t in range(n_o):
        in_specs.append(pl.BlockSpec((None, w_rows, D_MODEL), lambda m, c, t=t: (0, t, 0)))
        args.append(w_out)
    in_specs += [
        pl.BlockSpec((None, 6, D_MODEL), lambda m, c: (layer * 8 + cond(m), 0, 0)),
        pl.BlockSpec((None, 1, D_MODEL), lambda m, c: (layer, 0, 0)),
        pl.BlockSpec((None, D_MODEL, FF_CHUNK), lambda m, c: (layer, 0, c)),
        pl.BlockSpec((None, D_MODEL, FF_CHUNK), lambda m, c: (layer, 0, N_FF_CHUNKS + c)),
        pl.BlockSpec((None, 3, FF_CHUNK), lambda m, c: (layer, 0, c)),
        pl.BlockSpec((None, 1, FF_CHUNK), lambda m, c: (layer, 0, c)),
        pl.BlockSpec((None, FF_CHUNK, D_MODEL), lambda m, c: (layer, c, 0)),
        pl.BlockSpec((1, D_MODEL), lambda m, c: (0, 0)),
    ]
    args += [mod, g_ffn, w_up, w_up, conv_w, conv_b, w_down, g_final]
    if final:
        out_shape = [jax.ShapeDtypeStruct((N_CTX_TOK, D_MODEL), F32),
                     jax.ShapeDtypeStruct((N_TOK - N_CTX_TOK, D_MODEL), F32)]
        out_specs = [pl.BlockSpec((FFN_TM, D_MODEL), ctx_tile), pl.BlockSpec((FFN_TM, D_MODEL), lat_tile)]
    else:
        out_shape = [jax.ShapeDtypeStruct((N_TOK, D_MODEL), F32)]
        out_specs = [pl.BlockSpec((FFN_TM, D_MODEL), tile)]
    return pl.pallas_call(
        functools.partial(_ffn_kernel, n_o=n_o, final=final),
        grid=(n_tiles, N_FF_CHUNKS), in_specs=in_specs, out_specs=out_specs, out_shape=out_shape,
        scratch_shapes=[pltpu.VMEM((FFN_TM + 2 * HALO, D_MODEL), BF16), pltpu.VMEM((FFN_TM, D_MODEL), F32)],
        compiler_params=pltpu.CompilerParams(
            dimension_semantics=("arbitrary", "arbitrary"), vmem_limit_bytes=FFN_VMEM_LIMIT),
        name="ffn_final" if final else "ffn",
    )(*args)


def _rope_tables(rot_dim):
    t = jnp.arange(LAT_LEN)
    row = (t // GRID_W).astype(F32)
    col = (t % GRID_W).astype(F32)
    d_axis = rot_dim // 2
    freqs = ROPE_THETA ** (-jnp.arange(0, d_axis, 2, dtype=F32) / d_axis)
    ang = jnp.concatenate([row[:, None] * freqs, col[:, None] * freqs], axis=-1)
    cos = jnp.repeat(jnp.cos(ang), 2, axis=-1)
    sin = jnp.repeat(jnp.sin(ang), 2, axis=-1) * jnp.tile(jnp.array([-1.0, 1.0], F32), rot_dim // 2)
    if rot_dim == HEAD_DIM:
        cos = jnp.tile(cos, (1, 2))
        sin = jnp.tile(sin, (1, 2))
    else:
        cos = jnp.concatenate([jnp.ones((LAT_LEN, QK_NOPE), F32), cos,
                               jnp.ones((LAT_LEN, 128 - QK_NOPE - rot_dim), F32)], axis=-1)
        sin = jnp.concatenate([jnp.zeros((LAT_LEN, QK_NOPE), F32), sin,
                               jnp.zeros((LAT_LEN, 128 - QK_NOPE - rot_dim), F32)], axis=-1)
    return cos, sin


def kernel(x_prompt, x_sample, cache_a_k, cache_a_v, cache_b_ckv, cache_b_kpe, cache_c_k, cache_c_v, c, c_ctx, w_mod, b_mod, g_mix_norm, g_ffn_norm, w_in_e, g_qnorm_a, g_knorm_a, g_cq_b, w_uq_b, g_ckv_b, w_ukv_b, w_out_e, w_in_o, sink_c, w_out_o, w_up, conv_w, conv_b, w_down, g_final):
    depth = w_mod.shape[0]
    cond8 = jnp.concatenate([c_ctx[None, :], c, jnp.zeros((5, D_MODEL), F32)], axis=0)
    lane_pad = KB_PAD - QK_NOPE - QK_ROPE
    wkpe = jnp.pad(w_in_e[0][:, IN_E_MAIN:], ((0, 0), (QK_NOPE, lane_pad)))
    wuq = jnp.pad(w_uq_b[0].reshape(Q_LORA, N_HEADS_B, QK_NOPE + QK_ROPE),
                  ((0, 0), (0, 0), (0, lane_pad))).reshape(Q_LORA, N_HEADS_B * KB_PAD)
    wukv3 = w_ukv_b[0].reshape(KV_LORA, N_HEADS_B, QK_NOPE + V_DIM_B)
    wukk = jnp.pad(wukv3[:, :, :QK_NOPE], ((0, 0), (0, 0), (0, KB_PAD - QK_NOPE))
                   ).reshape(KV_LORA, N_HEADS_B * KB_PAD)
    wukv = wukv3[:, :, QK_NOPE:].reshape(KV_LORA, N_HEADS_B * V_DIM_B)
    gq = jnp.tile(g_qnorm_a[0], N_HEADS_A)[None, :]
    gk = jnp.tile(g_knorm_a[0], N_KV_A)[None, :]
    seg = jnp.arange(256) // HEAD_DIM
    ones = (seg[:, None] == seg[None, :]).astype(BF16)
    cosa, sina = _rope_tables(HEAD_DIM)
    cosb, sinb = _rope_tables(QK_ROPE)
    g_mix3 = g_mix_norm.reshape(depth, 1, D_MODEL)
    g_ffn3 = g_ffn_norm.reshape(depth, 1, D_MODEL)
    conv_b3 = conv_b.reshape(depth, 1, D_FF)
    g_fin2 = g_final[None, :]

    mod = _modulation(cond8, w_mod, b_mod).reshape(depth * 8, 6, D_MODEL)

    n_past = N_LAT_SEQ * PAST_LEN
    kpe_pad = jnp.pad(cache_b_kpe.reshape(n_past, QK_ROPE), ((0, 0), (QK_NOPE, lane_pad)))
    ka_c, va_c, kb_c, vb_c, kc_c, vc_c = _ctx_prep(
        cache_a_k.reshape(n_past, 128), cache_a_v.reshape(n_past, 128),
        cache_b_ckv.reshape(n_past, KV_LORA), kpe_pad,
        cache_c_k.reshape(n_past, 128), cache_c_v.reshape(n_past, 128), wukk, wukv)

    x0, qa, ka, va, qb, kb, vb, nak, nav, nckv, nkpe = _pre0(
        x_prompt.reshape(N_CTX_TOK, D_MODEL), x_sample.reshape(N_TOK - N_CTX_TOK, D_MODEL), mod, g_mix3,
        w_in_e, wkpe, gq, gk, g_cq_b.reshape(1, 1, Q_LORA), g_ckv_b.reshape(1, 1, KV_LORA),
        wuq, wukk, wukv, cosa, sina, cosb, sinb, ones)
    ctx_kw = dict(n_seq=N_CTX_SEQ, seq_len=CTX_LEN, tok_base=0, tq=CTX_LEN, window=0)
    lat_kw = dict(n_seq=N_LAT_SEQ, seq_len=LAT_LEN, tok_base=N_CTX_TOK)
    oa = (_attention(qa, ka, va, None, None, None, name="attn_a_ctx", **ctx_kw),
          _attention(qa, ka, va, ka_c, va_c, None, tq=256, window=0, name="attn_a_lat", **lat_kw))
    ob = (_attention(qb, kb, vb, None, None, None, name="attn_b_ctx", **ctx_kw),
          _attention(qb, kb, vb, kb_c, vb_c, None, tq=256, window=0, name="attn_b_lat", **lat_kw))
    (x1,) = _ffn(x0, [oa, ob], mod, 0, g_ffn3, w_out_e, w_up, conv_w, conv_b3, w_down, g_fin2, final=False)

    qc, kc, vc, nck, ncv = _pre1(x1, mod, g_mix3, w_in_o, cosa, sina)
    sink = sink_c[0]
    oc = (_attention(qc, kc, vc, None, None, sink, name="attn_c_ctx", **ctx_kw),
          _attention(qc, kc, vc, kc_c, vc_c, sink, tq=128, window=WINDOW, name="attn_c_lat", **lat_kw))
    y_prompt, y_sample = _ffn(x1, [oc], mod, 1, g_ffn3, w_out_o, w_up, conv_w, conv_b3, w_down, g_fin2,
                              final=True)

    return (y_prompt.reshape(N_CTX_SEQ, CTX_LEN, D_MODEL), y_sample.reshape(N_LAT_SEQ, LAT_LEN, D_MODEL),
            nak.reshape(N_CTX_SEQ, 1, CTX_LEN, N_KV_A, HEAD_DIM),
            nav.reshape(N_CTX_SEQ, 1, CTX_LEN, N_KV_A, HEAD_DIM),
            nckv.reshape(N_CTX_SEQ, 1, CTX_LEN, KV_LORA),
            nkpe.reshape(N_CTX_SEQ, 1, CTX_LEN, QK_ROPE),
            nck.reshape(N_CTX_SEQ, 1, CTX_LEN, N_KV_C, HEAD_DIM),
            ncv.reshape(N_CTX_SEQ, 1, CTX_LEN, N_KV_C, HEAD_DIM))
```

```python
import functools

import jax
import jax.numpy as jnp
from jax import lax
from jax.experimental import pallas as pl
from jax.experimental.pallas import tpu as pltpu

F32 = jnp.float32
BF16 = jnp.bfloat16

D_MODEL = 1024
N_CTX_SEQ = 16
CTX_LEN = 256
N_LAT_SEQ = 2
LAT_LEN = 2048
PAST_LEN = 512
GRID_W = 64
ROPE_THETA = 10000.0
NORM_EPS = 1e-6
WINDOW = 128
NEG_INF = -1e30
HEAD_DIM = 64
N_HEADS_A, N_KV_A = 8, 2
N_HEADS_B = 8
Q_LORA, KV_LORA = 384, 256
QK_NOPE, QK_ROPE, V_DIM_B = 64, 32, 64
N_HEADS_C, N_KV_C = 16, 2
D_FF = 2816
IN_E_MAIN = N_HEADS_A * HEAD_DIM + 2 * N_KV_A * HEAD_DIM + Q_LORA + KV_LORA

N_CTX_TOK = N_CTX_SEQ * CTX_LEN
N_TOK = N_CTX_TOK + N_LAT_SEQ * LAT_LEN
TM = 512
PRE_SUB = 256
N_TILES = N_TOK // TM
N_CTX_TILES = N_CTX_TOK // TM
LAT_TILES = LAT_LEN // TM
HALO = 16
FFN_TM = 1024
FFN_SUB = CTX_LEN
UP_LOOKAHEAD = 3
FFN_LAT_TILES = LAT_LEN // FFN_TM
FF_CHUNK = 256
N_FF_CHUNKS = D_FF // FF_CHUNK
FF_PER_STEP = 2
N_FF_STEPS = -(-N_FF_CHUNKS // FF_PER_STEP)
KB_PAD = 128
VT_ROWS = 80
SCORE_LOOKAHEAD = 5
VMEM_LIMIT = 56 * 1024 * 1024
FFN_VMEM_LIMIT = 60 * 1024 * 1024


def _dot(a, b):
    return jnp.dot(a, b, preferred_element_type=F32)


def _dot_nt(a, b):
    return lax.dot_general(a, b, (((1,), (1,)), ((), ())), preferred_element_type=F32)


def _rms(x, g):
    return x * lax.rsqrt(jnp.mean(x * x, axis=-1, keepdims=True) + NORM_EPS) * g


def _split_bf16(x):
    hi = x.astype(BF16)
    return hi, (x - hi.astype(F32)).astype(BF16)


def _head_rms(x, g, ones_ref):
    w = x.shape[1]
    hi, lo = _split_bf16(x * x)
    parts = []
    for c in range(0, w, 256):
        cw = min(256, w - c)
        ones = ones_ref[0:cw, 0:cw]
        parts.append(_dot(hi[:, c:c + cw], ones) + _dot(lo[:, c:c + cw], ones))
    ssum = parts[0] if len(parts) == 1 else jnp.concatenate(parts, axis=1)
    return x * lax.rsqrt(ssum * (1.0 / HEAD_DIM) + NORM_EPS) * g


def _swap_pairs(x):
    w = x.shape[1]
    up = pltpu.roll(x, w - 1, axis=1)
    dn = pltpu.roll(x, 1, axis=1)
    lane = lax.broadcasted_iota(jnp.int32, x.shape, 1)
    return jnp.where((lane & 1) == 0, up, dn)


def _rope(x, cos, sin_signed, reps):
    if reps > 1:
        cos = jnp.concatenate([cos] * reps, axis=1)
        sin_signed = jnp.concatenate([sin_signed] * reps, axis=1)
    return x * cos + _swap_pairs(x) * sin_signed


def _store_vt(vt_ref, v, n_heads, cols=slice(None)):
    t = v.shape[0]
    vt = v.T.astype(BF16)
    ones = jnp.ones((VT_ROWS - 64, t), BF16)
    for hh in range(n_heads):
        vt_ref[hh, 0:64, cols] = vt[hh * 64:(hh + 1) * 64]
        vt_ref[hh, 64:VT_ROWS, cols] = ones


def _cond_row(i):
    return jnp.where(i < N_CTX_TILES, 0, 1 + (i - N_CTX_TILES) // LAT_TILES)


def _rope_block(i):
    return jnp.maximum(i - N_CTX_TILES, 0) % LAT_TILES


def _const_spec(shape):
    zeros = (0,) * len(shape)
    return pl.BlockSpec(shape, lambda *_: zeros)


def _layer_spec(shape, layer):
    idx = (layer,) + (0,) * len(shape)
    return pl.BlockSpec((None,) + tuple(shape), lambda *_: idx)


def _mod_kernel(cond_ref, w_ref, b_ref, o_ref):
    c = cond_ref[...]
    s_hi, s_lo = _split_bf16(c * jax.nn.sigmoid(c))
    w_hi, w_lo = _split_bf16(w_ref[0])
    r = _dot(jnp.concatenate([s_hi, s_lo], axis=0), w_hi)
    o_ref[0] = r[0:8] + r[8:16] + _dot(s_hi, w_lo) + b_ref[0]


def _modulation(cond8, w_mod, b_mod):
    depth, _, n = w_mod.shape
    tn = 1536
    return pl.pallas_call(
        _mod_kernel,
        grid=(depth, n // tn),
        in_specs=[
            pl.BlockSpec((8, D_MODEL), lambda l, j: (0, 0)),
            pl.BlockSpec((1, D_MODEL, tn), lambda l, j: (l, 0, j)),
            pl.BlockSpec((1, 1, tn), lambda l, j: (l, 0, j)),
        ],
        out_specs=pl.BlockSpec((1, 8, tn), lambda l, j: (l, 0, j)),
        out_shape=jax.ShapeDtypeStruct((depth, 8, n), F32),
        compiler_params=pltpu.CompilerParams(
            dimension_semantics=("arbitrary", "arbitrary"), vmem_limit_bytes=VMEM_LIMIT),
        name="modulation",
    )(cond8, w_mod, b_mod.reshape(depth, 1, n))


def _pre0_kernel(xp_ref, xs_ref, mod_ref, gmix_ref, win_ref, wkpe_ref, gq_ref, gk_ref, gcq_ref, gckv_ref,
                 wuq_ref, wukk_ref, wukv_ref, cosa_ref, sina_ref, cosb_ref, sinb_ref, ones_ref,
                 qa_ref, ka_ref, va_ref, qb_ref, kb_ref, vb_ref,
                 nak_ref, nav_ref, nckv_ref, nkpe_ref,
                 win_s, wuq_s, wukk_s, wukv_s):
    i = pl.program_id(0)

    @pl.when(i == 0)
    def _():
        win_s[:, 0:IN_E_MAIN] = win_ref[...].astype(BF16)
        win_s[:, IN_E_MAIN:] = wkpe_ref[...].astype(BF16)
        wuq_s[...] = wuq_ref[...].astype(BF16)
        wukk_s[...] = wukk_ref[...].astype(BF16)
        wukv_s[...] = wukv_ref[...].astype(BF16)

    sh1 = mod_ref[0:1, :]
    sc1 = mod_ref[1:2, :]
    subs = [slice(r, r + PRE_SUB) for r in range(0, TM, PRE_SUB)]

    projs = []
    for sl in subs:
        x = jnp.where(i < N_CTX_TILES, xp_ref[sl, :], xs_ref[sl, :])
        h = _rms(x, gmix_ref[...]) * (1.0 + sc1) + sh1
        projs.append(_dot(h.astype(BF16), win_s[...]))

    parts = []
    for sl, proj in zip(subs, projs):
        qa = _head_rms(proj[:, 0:512], gq_ref[...], ones_ref)
        ka = _head_rms(proj[:, 512:640], gk_ref[...], ones_ref)
        va = proj[:, 640:768]
        cq = _rms(proj[:, 768:1152], gcq_ref[...])
        ckv = _rms(proj[:, 1152:1408], gckv_ref[...])
        kpe = proj[:, 1408:1536]
        qb = _dot(cq.astype(BF16), wuq_s[...])
        ckv_b = ckv.astype(BF16)
        kbn = _dot(ckv_b, wukk_s[...])
        vb = _dot(ckv_b, wukv_s[...])
        _store_vt(va_ref, va, N_KV_A, sl)
        _store_vt(vb_ref, vb, N_HEADS_B, sl)
        parts.append((qa, ka, va, qb, ckv, kpe, kbn))

    def store_qk(sl, qa, ka, qb, kpe, kbn):
        qa_s = qa * (HEAD_DIM ** -0.5)
        qb_s = qb * ((QK_NOPE + QK_ROPE) ** -0.5)
        for hh in range(N_HEADS_A):
            qa_ref[hh, sl, :] = qa_s[:, hh * 64:(hh + 1) * 64].astype(BF16)
        for hh in range(N_KV_A):
            ka_ref[hh, sl, :] = ka[:, hh * 64:(hh + 1) * 64].astype(BF16)
        for hh in range(N_HEADS_B):
            qb_ref[hh, sl, :] = qb_s[:, hh * KB_PAD:(hh + 1) * KB_PAD].astype(BF16)
            kb_ref[hh, sl, :] = (kbn[:, hh * KB_PAD:(hh + 1) * KB_PAD] + kpe).astype(BF16)

    @pl.when(i < N_CTX_TILES)
    def _():
        for sl, (qa, ka, va, qb, ckv, kpe, kbn) in zip(subs, parts):
            store_qk(sl, qa, ka, qb, kpe, kbn)
            nak_ref[sl, :] = ka
            nav_ref[sl, :] = va
            nckv_ref[sl, :] = ckv
            nkpe_ref[sl, :] = kpe[:, QK_NOPE:QK_NOPE + QK_ROPE]

    @pl.when(i >= N_CTX_TILES)
    def _():
        for sl, (qa, ka, va, qb, ckv, kpe, kbn) in zip(subs, parts):
            cosa, sina = cosa_ref[sl, :], sina_ref[sl, :]
            cosb, sinb = cosb_ref[sl, :], sinb_ref[sl, :]
            store_qk(sl, _rope(qa, cosa, sina, 4), _rope(ka, cosa, sina, 1),
                     _rope(qb, cosb, sinb, 8), _rope(kpe, cosb, sinb, 1), kbn)


def _pre0(xp, xs, mod, gmix, w_in_e, wkpe, gq, gk, gcq, gckv, wuq, wukk, wukv, cosa, sina, cosb, sinb, ones):
    tile = lambda i: (i, 0)
    head_tile = lambda i: (0, i, 0)
    vt_tile = lambda i: (0, 0, i)
    ctx_tile = lambda i: (jnp.minimum(i, N_CTX_TILES - 1), 0)
    lat_tile = lambda i: (jnp.maximum(i - N_CTX_TILES, 0), 0)
    rope_tile = lambda i: (_rope_block(i), 0)
    in_specs = [
        pl.BlockSpec((TM, D_MODEL), ctx_tile),
        pl.BlockSpec((TM, D_MODEL), lat_tile),
        pl.BlockSpec((None, 6, D_MODEL), lambda i: (_cond_row(i), 0, 0)),
        _layer_spec((1, D_MODEL), 0),
        _layer_spec((D_MODEL, IN_E_MAIN), 0),
        _const_spec(wkpe.shape),
        _const_spec(gq.shape), _const_spec(gk.shape), _layer_spec((1, Q_LORA), 0), _layer_spec((1, KV_LORA), 0),
        _const_spec(wuq.shape), _const_spec(wukk.shape), _const_spec(wukv.shape),
        pl.BlockSpec((TM, 128), rope_tile), pl.BlockSpec((TM, 128), rope_tile),
        pl.BlockSpec((TM, 128), rope_tile), pl.BlockSpec((TM, 128), rope_tile),
        _const_spec(ones.shape),
    ]
    out_shape = [
        jax.ShapeDtypeStruct((N_HEADS_A, N_TOK, 64), BF16),
        jax.ShapeDtypeStruct((N_KV_A, N_TOK, 64), BF16),
        jax.ShapeDtypeStruct((N_KV_A, VT_ROWS, N_TOK), BF16),
        jax.ShapeDtypeStruct((N_HEADS_B, N_TOK, KB_PAD), BF16),
        jax.ShapeDtypeStruct((N_HEADS_B, N_TOK, KB_PAD), BF16),
        jax.ShapeDtypeStruct((N_HEADS_B, VT_ROWS, N_TOK), BF16),
        jax.ShapeDtypeStruct((N_CTX_TOK, 128), F32),
        jax.ShapeDtypeStruct((N_CTX_TOK, 128), F32),
        jax.ShapeDtypeStruct((N_CTX_TOK, KV_LORA), F32),
        jax.ShapeDtypeStruct((N_CTX_TOK, QK_ROPE), F32),
    ]
    out_specs = [
        pl.BlockSpec((N_HEADS_A, TM, 64), head_tile),
        pl.BlockSpec((N_KV_A, TM, 64), head_tile),
        pl.BlockSpec((N_KV_A, VT_ROWS, TM), vt_tile),
        pl.BlockSpec((N_HEADS_B, TM, KB_PAD), head_tile),
        pl.BlockSpec((N_HEADS_B, TM, KB_PAD), head_tile),
        pl.BlockSpec((N_HEADS_B, VT_ROWS, TM), vt_tile),
        pl.BlockSpec((TM, 128), ctx_tile),
        pl.BlockSpec((TM, 128), ctx_tile),
        pl.BlockSpec((TM, KV_LORA), ctx_tile),
        pl.BlockSpec((TM, QK_ROPE), ctx_tile),
    ]
    scratch = [
        pltpu.VMEM((D_MODEL, IN_E_MAIN + KB_PAD), BF16),
        pltpu.VMEM(wuq.shape, BF16), pltpu.VMEM(wukk.shape, BF16), pltpu.VMEM(wukv.shape, BF16),
    ]
    return pl.pallas_call(
        _pre0_kernel, grid=(N_TILES,), in_specs=in_specs, out_specs=out_specs, out_shape=out_shape,
        scratch_shapes=scratch,
        compiler_params=pltpu.CompilerParams(
            dimension_semantics=("arbitrary",), vmem_limit_bytes=VMEM_LIMIT),
        name="pre0",
    )(xp, xs, mod, gmix, w_in_e, wkpe, gq, gk, gcq, gckv, wuq, wukk, wukv, cosa, sina, cosb, sinb, ones)


def _pre1_kernel(xp_ref, xs_ref, mod_ref, gmix_ref, win_ref, cosa_ref, sina_ref,
                 q_ref, k_ref, v_ref, nk_ref, nv_ref, win_s):
    i = pl.program_id(0)

    @pl.when(i == 0)
    def _():
        win_s[...] = win_ref[...].astype(BF16)

    sh1 = mod_ref[0:1, :]
    sc1 = mod_ref[1:2, :]
    subs = [slice(r, r + PRE_SUB) for r in range(0, TM, PRE_SUB)]
    projs = []
    for sl in subs:
        x = jnp.where(i < N_CTX_TILES, xp_ref[sl, :], xs_ref[sl, :])
        h = _rms(x, gmix_ref[...]) * (1.0 + sc1) + sh1
        projs.append(_dot(h.astype(BF16), win_s[...]))
    for sl, proj in zip(subs, projs):
        _store_vt(v_ref, proj[:, 1152:1280], N_KV_C, sl)

    def store_qk(sl, q, k):
        q_s = q * (HEAD_DIM ** -0.5)
        for hh in range(N_HEADS_C):
            q_ref[hh, sl, :] = q_s[:, hh * 64:(hh + 1) * 64].astype(BF16)
        for hh in range(N_KV_C):
            k_ref[hh, sl, :] = k[:, hh * 64:(hh + 1) * 64].astype(BF16)

    @pl.when(i < N_CTX_TILES)
    def _():
        for sl, proj in zip(subs, projs):
            store_qk(sl, proj[:, 0:1024], proj[:, 1024:1152])
            nk_ref[sl, :] = proj[:, 1024:1152]
            nv_ref[sl, :] = proj[:, 1152:1280]

    @pl.when(i >= N_CTX_TILES)
    def _():
        for sl, proj in zip(subs, projs):
            cosa, sina = cosa_ref[sl, :], sina_ref[sl, :]
            store_qk(sl, _rope(proj[:, 0:1024], cosa, sina, 8), _rope(proj[:, 1024:1152], cosa, sina, 1))


def _pre1(xp, xs, mod, gmix, w_in_o, cosa, sina):
    n_in = w_in_o.shape[-1]
    head_tile = lambda i: (0, i, 0)
    vt_tile = lambda i: (0, 0, i)
    ctx_tile = lambda i: (jnp.minimum(i, N_CTX_TILES - 1), 0)
    lat_tile = lambda i: (jnp.maximum(i - N_CTX_TILES, 0), 0)
    rope_tile = lambda i: (_rope_block(i), 0)
    in_specs = [
        pl.BlockSpec((TM, D_MODEL), ctx_tile),
        pl.BlockSpec((TM, D_MODEL), lat_tile),
        pl.BlockSpec((None, 6, D_MODEL), lambda i: (8 + _cond_row(i), 0, 0)),
        _layer_spec((1, D_MODEL), 1),
        _layer_spec((D_MODEL, n_in), 0),
        pl.BlockSpec((TM, 128), rope_tile), pl.BlockSpec((TM, 128), rope_tile),
    ]
    out_shape = [
        jax.ShapeDtypeStruct((N_HEADS_C, N_TOK, 64), BF16),
        jax.ShapeDtypeStruct((N_KV_C, N_TOK, 64), BF16),
        jax.ShapeDtypeStruct((N_KV_C, VT_ROWS, N_TOK), BF16),
        jax.ShapeDtypeStruct((N_CTX_TOK, 128), F32),
        jax.ShapeDtypeStruct((N_CTX_TOK, 128), F32),
    ]
    out_specs = [
        pl.BlockSpec((N_HEADS_C, TM, 64), head_tile),
        pl.BlockSpec((N_KV_C, TM, 64), head_tile),
        pl.BlockSpec((N_KV_C, VT_ROWS, TM), vt_tile),
        pl.BlockSpec((TM, 128), ctx_tile),
        pl.BlockSpec((TM, 128), ctx_tile),
    ]
    return pl.pallas_call(
        _pre1_kernel, grid=(N_TILES,), in_specs=in_specs, out_specs=out_specs, out_shape=out_shape,
        scratch_shapes=[pltpu.VMEM((D_MODEL, n_in), BF16)],
        compiler_params=pltpu.CompilerParams(
            dimension_semantics=("arbitrary",), vmem_limit_bytes=VMEM_LIMIT),
        name="pre1",
    )(xp, xs, mod, gmix, w_in_o, cosa, sina)


def _ctx_kernel(ak_ref, av_ref, ckv_ref, kpe_ref, ck_ref, cv_ref, wukk_ref, wukv_ref,
                ka_ref, va_ref, kb_ref, vb_ref, kc_ref, vc_ref):
    ak, av, ck, cv = ak_ref[...], av_ref[...], ck_ref[...], cv_ref[...]
    for hh in range(2):
        sl = slice(hh * 64, (hh + 1) * 64)
        ka_ref[hh] = ak[:, sl].astype(BF16)
        kc_ref[hh] = ck[:, sl].astype(BF16)
    _store_vt(va_ref, av, 2)
    _store_vt(vc_ref, cv, 2)
    ckv_b = ckv_ref[...].astype(BF16)
    kbn = _dot(ckv_b, wukk_ref[...].astype(BF16))
    vb = _dot(ckv_b, wukv_ref[...].astype(BF16))
    kpe = kpe_ref[...]
    for hh in range(N_HEADS_B):
        kb_ref[hh] = (kbn[:, hh * KB_PAD:(hh + 1) * KB_PAD] + kpe).astype(BF16)
    _store_vt(vb_ref, vb, N_HEADS_B)


def _ctx_prep(ak, av, ckv, kpe_pad, ck, cv, wukk, wukv):
    n = N_LAT_SEQ * PAST_LEN
    row = lambda b: (b, 0)
    head_row = lambda b: (0, b, 0)
    vt_row = lambda b: (0, 0, b)
    in_specs = [
        pl.BlockSpec((PAST_LEN, 128), row), pl.BlockSpec((PAST_LEN, 128), row),
        pl.BlockSpec((PAST_LEN, KV_LORA), row), pl.BlockSpec((PAST_LEN, KB_PAD), row),
        pl.BlockSpec((PAST_LEN, 128), row), pl.BlockSpec((PAST_LEN, 128), row),
        _const_spec(wukk.shape), _const_spec(wukv.shape),
    ]
    out_shape = [
        jax.ShapeDtypeStruct((2, n, 64), BF16), jax.ShapeDtypeStruct((2, VT_ROWS, n), BF16),
        jax.ShapeDtypeStruct((N_HEADS_B, n, KB_PAD), BF16), jax.ShapeDtypeStruct((N_HEADS_B, VT_ROWS, n), BF16),
        jax.ShapeDtypeStruct((2, n, 64), BF16), jax.ShapeDtypeStruct((2, VT_ROWS, n), BF16),
    ]
    out_specs = [
        pl.BlockSpec((2, PAST_LEN, 64), head_row), pl.BlockSpec((2, VT_ROWS, PAST_LEN), vt_row),
        pl.BlockSpec((N_HEADS_B, PAST_LEN, KB_PAD), head_row), pl.BlockSpec((N_HEADS_B, VT_ROWS, PAST_LEN), vt_row),
        pl.BlockSpec((2, PAST_LEN, 64), head_row), pl.BlockSpec((2, VT_ROWS, PAST_LEN), vt_row),
    ]
    return pl.pallas_call(
        _ctx_kernel, grid=(N_LAT_SEQ,), in_specs=in_specs, out_specs=out_specs, out_shape=out_shape,
        compiler_params=pltpu.CompilerParams(
            dimension_semantics=("arbitrary",), vmem_limit_bytes=VMEM_LIMIT),
        name="ctx_prep",
    )(ak, av, ckv, kpe_pad, ck, cv, wukk, wukv)


def _softmax_units(units, lookahead):
    tasks = [(u, c) for u, unit in enumerate(units) for c in range(len(unit["chunks"]))]
    scores = {}

    def emit_scores(t):
        u, c = tasks[t]
        k, _, mask = units[u]["chunks"][c]
        s = _dot_nt(k, units[u]["q"])
        scores[t] = s if mask is None else jnp.where(mask, s, NEG_INF)

    for t in range(min(lookahead, len(tasks))):
        emit_scores(t)
    for t, (u, c) in enumerate(tasks):
        if t + lookahead < len(tasks):
            emit_scores(t + lookahead)
        unit = units[u]
        s = scores.pop(t)
        m, acc = unit["m"], unit["acc"]
        cmax = jnp.max(s, axis=0, keepdims=True)
        m_new = cmax if m is None else jnp.maximum(m, cmax)
        pv = _dot(unit["chunks"][c][1], jnp.exp(s - m_new).astype(BF16))
        unit["acc"] = pv if acc is None else acc * jnp.exp(m - m_new) + pv
        unit["m"] = m_new
    return [unit["acc"][0:64] * (1.0 / unit["acc"][64:65]) for unit in units]


def _attn_kernel(*refs, n_kv, group, tq, seq_len, has_ctx, has_sink, window, q_unit, key_chunk):
    refs = list(refs)
    q_ref = refs.pop(0)
    if window:
        kp_ref, kc_ref, kn_ref, vp_ref, vc_ref, vn_ref = refs[:6]
        refs = refs[6:]
    else:
        k_ref, vt_ref = refs[:2]
        refs = refs[2:]
    if has_ctx:
        kx_ref, vx_ref = refs[:2]
        refs = refs[2:]
    if has_sink:
        sink_ref = refs.pop(0)
    o_ref = refs.pop(0)

    j = pl.program_id(1)
    dk = q_ref.shape[-1]
    heads_per_unit = q_unit // tq
    lane = lax.broadcasted_iota(jnp.int32, (1, q_unit), 1)
    if window:
        n_band = tq + 2 * window
        krow = lax.broadcasted_iota(jnp.int32, (n_band, q_unit), 0)
        qcol = lax.broadcasted_iota(jnp.int32, (n_band, q_unit), 1) & (tq - 1)
        rel = (krow - window) - qcol
        band_mask = ((jnp.abs(rel) <= window)
                     & ((krow >= window) | (j > 0))
                     & ((krow < window + tq) | (j < seq_len // tq - 1)))
    if has_sink:
        acc0 = jnp.where(lax.broadcasted_iota(jnp.int32, (VT_ROWS, q_unit), 0) >= 64, 1.0, 0.0)

    units = []
    for hk in range(n_kv):
        chunks = []
        if window:
            chunks.append((jnp.concatenate([kp_ref[hk], kc_ref[hk], kn_ref[hk]], axis=0),
                           jnp.concatenate([vp_ref[hk], vc_ref[hk], vn_ref[hk]], axis=1), band_mask))
        else:
            for c in range(0, seq_len, key_chunk):
                n = min(key_chunk, seq_len - c)
                chunks.append((k_ref[hk, c:c + n, :], vt_ref[hk, :, c:c + n], None))
        if has_ctx:
            for c in range(0, PAST_LEN, key_chunk):
                n = min(key_chunk, PAST_LEN - c)
                chunks.append((kx_ref[hk, c:c + n, :], vx_ref[hk, :, c:c + n], None))
        for u in range(group // heads_per_unit):
            h0 = hk * group + u * heads_per_unit
            unit = dict(q=q_ref[h0:h0 + heads_per_unit].reshape(q_unit, dk), chunks=chunks, m=None, acc=None)
            if has_sink:
                m0 = jnp.full((1, q_unit), sink_ref[h0], F32)
                for e in range(1, heads_per_unit):
                    m0 = jnp.where(lane >= e * tq, sink_ref[h0 + e], m0)
                unit.update(m=m0, acc=acc0)
            units.append(unit)
    outs = []
    for o in _softmax_units(units, SCORE_LOOKAHEAD):
        for e in range(heads_per_unit):
            outs.append(o[:, e * tq:(e + 1) * tq])
    o_ref[...] = jnp.concatenate(outs, axis=0).T.astype(BF16)


def _attention(q, k, vt, kx, vx, sink, *, n_seq, seq_len, tok_base, tq, window, q_unit, key_chunk, name):
    n_q, _, dk = q.shape
    n_kv = k.shape[0]
    group = n_q // n_kv
    n_qt = seq_len // tq
    q_base = tok_base // tq
    s_base = tok_base // seq_len
    has_ctx = kx is not None
    has_sink = sink is not None
    q_blk = lambda b, j: q_base + b * n_qt + j
    in_specs = [pl.BlockSpec((n_q, tq, dk), lambda b, j: (0, q_blk(b, j), 0))]
    args = [q]
    if window:
        assert window == tq
        prev = lambda b, j: q_base + b * n_qt + jnp.maximum(j - 1, 0)
        nxt = lambda b, j: q_base + b * n_qt + jnp.minimum(j + 1, n_qt - 1)
        for blk in (prev, q_blk, nxt):
            in_specs.append(pl.BlockSpec((n_kv, tq, dk), lambda b, j, blk=blk: (0, blk(b, j), 0)))
        for blk in (prev, q_blk, nxt):
            in_specs.append(pl.BlockSpec((n_kv, VT_ROWS, tq), lambda b, j, blk=blk: (0, 0, blk(b, j))))
        args += [k, k, k, vt, vt, vt]
    else:
        in_specs += [
            pl.BlockSpec((n_kv, seq_len, dk), lambda b, j: (0, s_base + b, 0)),
            pl.BlockSpec((n_kv, VT_ROWS, seq_len), lambda b, j: (0, 0, s_base + b)),
        ]
        args += [k, vt]
    if has_ctx:
        in_specs += [
            pl.BlockSpec((n_kv, PAST_LEN, dk), lambda b, j: (0, b, 0)),
            pl.BlockSpec((n_kv, VT_ROWS, PAST_LEN), lambda b, j: (0, 0, b)),
        ]
        args += [kx, vx]
    if has_sink:
        in_specs.append(pl.BlockSpec(memory_space=pltpu.SMEM))
        args.append(sink)
    kern = functools.partial(_attn_kernel, n_kv=n_kv, group=group, tq=tq, seq_len=seq_len,
                             has_ctx=has_ctx, has_sink=has_sink, window=window,
                             q_unit=q_unit, key_chunk=key_chunk)
    return pl.pallas_call(
        kern, grid=(n_seq, n_qt), in_specs=in_specs,
        out_specs=pl.BlockSpec((tq, n_q * HEAD_DIM), lambda b, j: (b * n_qt + j, 0)),
        out_shape=jax.ShapeDtypeStruct((n_seq * seq_len, n_q * HEAD_DIM), BF16),
        compiler_params=pltpu.CompilerParams(
            dimension_semantics=("arbitrary", "arbitrary"), vmem_limit_bytes=VMEM_LIMIT),
        name=name,
    )(*args)


def _ffn_kernel(*refs, n_o, is_ctx, final):
    halo = 0 if is_ctx else HALO
    it = iter(refs)
    x_ref = next(it)
    xh_refs = None if is_ctx else (next(it), next(it))
    o_refs, oh_refs = [], []
    for _ in range(n_o):
        o_refs.append(next(it))
        if not is_ctx:
            oh_refs.append((next(it), next(it)))
    wo_refs = [next(it) for _ in range(n_o)]
    mod_ref, gffn_ref = next(it), next(it)
    wg_refs = [next(it) for _ in range(FF_PER_STEP)]
    wv_refs = [next(it) for _ in range(FF_PER_STEP)]
    cw_refs = [next(it) for _ in range(FF_PER_STEP)]
    cb_refs = [next(it) for _ in range(FF_PER_STEP)]
    wd_refs = [next(it) for _ in range(FF_PER_STEP)]
    gfin_ref, out_ref, h2e_ref, acc_ref = next(it), next(it), next(it), next(it)

    m = pl.program_id(0)
    c = pl.program_id(1)
    g1 = mod_ref[2:3, :]
    sh2 = mod_ref[3:4, :]
    sc2 = mod_ref[4:5, :]
    g2 = mod_ref[5:6, :]
    n_sub = FFN_TM // FFN_SUB
    sub_rows = FFN_SUB + 2 * halo

    @pl.when(c == 0)
    def _():
        wos = [w[...].astype(BF16) for w in wo_refs]

        def residual_and_norm(xv, ovs):
            attn = _dot(ovs[0], wos[0])
            for ov, wo in zip(ovs[1:], wos[1:]):
                attn = attn + _dot(ov, wo)
            x1 = xv + g1 * attn
            return x1, (_rms(x1, gffn_ref[...]) * (1.0 + sc2) + sh2).astype(BF16)

        for r in range(0, FFN_TM, FFN_SUB):
            x1, h2 = residual_and_norm(x_ref[r:r + FFN_SUB, :], [o[r:r + FFN_SUB, :] for o in o_refs])
            acc_ref[r:r + FFN_SUB, :] = x1
            h2e_ref[halo + r:halo + r + FFN_SUB, :] = h2
        if not is_ctx:
            _, h2h = residual_and_norm(
                jnp.concatenate([xh_refs[0][...], xh_refs[1][...]], axis=0),
                [jnp.concatenate([oh[0][...], oh[1][...]], axis=0) for oh in oh_refs])
            h2e_ref[0:HALO, :] = h2h[0:HALO]
            h2e_ref[HALO + FFN_TM:, :] = h2h[HALO:]

    row8 = lax.broadcasted_iota(jnp.int32, (8, FF_CHUNK), 0)
    if not is_ctx:
        has_prev = m % FFN_LAT_TILES != 0
        has_next = m % FFN_LAT_TILES != FFN_LAT_TILES - 1

    def ff_chunk(j):
        w_up = jnp.concatenate([wg_refs[j][...].astype(BF16), wv_refs[j][...].astype(BF16)], axis=1)
        w_dn = wd_refs[j][...].astype(BF16)
        cw = cw_refs[j][...]
        cb = cb_refs[j][...]
        ups = {}

        def emit_up(r):
            ups[r] = _dot(h2e_ref[r * FFN_SUB:r * FFN_SUB + sub_rows, :], w_up)

        for r in range(min(UP_LOOKAHEAD, n_sub)):
            emit_up(r)
        for r in range(n_sub):
            if r + UP_LOOKAHEAD < n_sub:
                emit_up(r + UP_LOOKAHEAD)
            up = ups.pop(r)
            ge = up[:, :FF_CHUNK]
            val = up[halo:halo + FFN_SUB, FF_CHUNK:]
            g_prev = pltpu.roll(ge, 1, axis=0)[halo:halo + FFN_SUB]
            g_next = pltpu.roll(ge, sub_rows - 1, axis=0)[halo:halo + FFN_SUB]
            prev_ok = False if is_ctx else (has_prev if r == 0 else True)
            next_ok = False if is_ctx else (has_next if r == n_sub - 1 else True)
            if prev_ok is not True:
                g_prev = jnp.concatenate(
                    [jnp.where(jnp.logical_or(row8 != 0, prev_ok), g_prev[0:8], 0.0), g_prev[8:]], axis=0)
            if next_ok is not True:
                g_next = jnp.concatenate(
                    [g_next[:-8], jnp.where(jnp.logical_or(row8 != 7, next_ok), g_next[-8:], 0.0)], axis=0)
            gate = g_prev * cw[0:1, :] + ge[halo:halo + FFN_SUB] * cw[1:2, :] + g_next * cw[2:3, :] + cb
            act = (gate * jax.nn.sigmoid(gate) * val).astype(BF16)
            acc_ref[r * FFN_SUB:(r + 1) * FFN_SUB, :] += g2 * _dot(act, w_dn)

    n_tail = N_FF_CHUNKS % FF_PER_STEP
    for j in range(FF_PER_STEP):
        if n_tail and j >= n_tail:
            pl.when(c < N_FF_STEPS - 1)(functools.partial(ff_chunk, j))
        else:
            ff_chunk(j)

    @pl.when(c == N_FF_STEPS - 1)
    def _():
        x2 = acc_ref[...]
        out_ref[...] = _rms(x2, gfin_ref[...]) if final else x2


def _ffn(x, os, mod, layer, g_ffn, w_out, w_up, conv_w, conv_b, w_down, g_final, *, is_ctx, final):
    n_rows = x.shape[0]
    nh = FFN_TM // HALO
    nblk = n_rows // HALO
    n_o = len(os)
    halo = 0 if is_ctx else HALO
    tile = lambda m, c: (m, 0)
    prev = lambda m, c: (jnp.maximum(m * nh - 1, 0), 0)
    nxt = lambda m, c: (jnp.minimum((m + 1) * nh, nblk - 1), 0)
    chunk = lambda c, j: jnp.minimum(c * FF_PER_STEP + j, N_FF_CHUNKS - 1)
    if is_ctx:
        cond = lambda m: layer * 8
    else:
        cond = lambda m: layer * 8 + 1 + m // FFN_LAT_TILES

    def with_halo(arr):
        w = arr.shape[1]
        specs = [pl.BlockSpec((FFN_TM, w), tile)]
        if not is_ctx:
            specs += [pl.BlockSpec((HALO, w), prev), pl.BlockSpec((HALO, w), nxt)]
        return specs, [arr] * len(specs)

    in_specs, args = with_halo(x)
    for o in os:
        specs, arrs = with_halo(o)
        in_specs += specs
        args += arrs
    w_rows = D_MODEL // n_o
    for t in range(n_o):
        in_specs.append(pl.BlockSpec((None, w_rows, D_MODEL), lambda m, c, t=t: (0, t, 0)))
        args.append(w_out)
    in_specs += [
        pl.BlockSpec((None, 6, D_MODEL), lambda m, c: (cond(m), 0, 0)),
        pl.BlockSpec((None, 1, D_MODEL), lambda m, c: (layer, 0, 0)),
    ]
    args += [mod, g_ffn]
    steps = range(FF_PER_STEP)
    in_specs += [pl.BlockSpec((None, D_MODEL, FF_CHUNK), lambda m, c, j=j: (layer, 0, chunk(c, j))) for j in steps]
    in_specs += [pl.BlockSpec((None, D_MODEL, FF_CHUNK), lambda m, c, j=j: (layer, 0, N_FF_CHUNKS + chunk(c, j)))
                 for j in steps]
    in_specs += [pl.BlockSpec((None, 3, FF_CHUNK), lambda m, c, j=j: (layer, 0, chunk(c, j))) for j in steps]
    in_specs += [pl.BlockSpec((None, 1, FF_CHUNK), lambda m, c, j=j: (layer, 0, chunk(c, j))) for j in steps]
    in_specs += [pl.BlockSpec((None, FF_CHUNK, D_MODEL), lambda m, c, j=j: (layer, chunk(c, j), 0)) for j in steps]
    args += [w_up] * (2 * FF_PER_STEP) + [conv_w] * FF_PER_STEP + [conv_b] * FF_PER_STEP + [w_down] * FF_PER_STEP
    in_specs.append(pl.BlockSpec((1, D_MODEL), lambda m, c: (0, 0)))
    args.append(g_final)
    return pl.pallas_call(
        functools.partial(_ffn_kernel, n_o=n_o, is_ctx=is_ctx, final=final),
        grid=(n_rows // FFN_TM, N_FF_STEPS), in_specs=in_specs,
        out_specs=pl.BlockSpec((FFN_TM, D_MODEL), tile),
        out_shape=jax.ShapeDtypeStruct((n_rows, D_MODEL), F32),
        scratch_shapes=[pltpu.VMEM((FFN_TM + 2 * halo, D_MODEL), BF16), pltpu.VMEM((FFN_TM, D_MODEL), F32)],
        compiler_params=pltpu.CompilerParams(
            dimension_semantics=("arbitrary", "arbitrary"), vmem_limit_bytes=FFN_VMEM_LIMIT),
        name=("ffn_ctx" if is_ctx else "ffn_lat") + ("_final" if final else ""),
    )(*args)


def _rope_tables(rot_dim):
    t = jnp.arange(LAT_LEN)
    row = (t // GRID_W).astype(F32)
    col = (t % GRID_W).astype(F32)
    d_axis = rot_dim // 2
    freqs = ROPE_THETA ** (-jnp.arange(0, d_axis, 2, dtype=F32) / d_axis)
    ang = jnp.concatenate([row[:, None] * freqs, col[:, None] * freqs], axis=-1)
    cos = jnp.repeat(jnp.cos(ang), 2, axis=-1)
    sin = jnp.repeat(jnp.sin(ang), 2, axis=-1) * jnp.tile(jnp.array([-1.0, 1.0], F32), rot_dim // 2)
    if rot_dim == HEAD_DIM:
        cos = jnp.tile(cos, (1, 2))
        sin = jnp.tile(sin, (1, 2))
    else:
        cos = jnp.concatenate([jnp.ones((LAT_LEN, QK_NOPE), F32), cos,
                               jnp.ones((LAT_LEN, 128 - QK_NOPE - rot_dim), F32)], axis=-1)
        sin = jnp.concatenate([jnp.zeros((LAT_LEN, QK_NOPE), F32), sin,
                               jnp.zeros((LAT_LEN, 128 - QK_NOPE - rot_dim), F32)], axis=-1)
    return cos, sin


def kernel(x_prompt, x_sample, cache_a_k, cache_a_v, cache_b_ckv, cache_b_kpe, cache_c_k, cache_c_v, c, c_ctx, w_mod, b_mod, g_mix_norm, g_ffn_norm, w_in_e, g_qnorm_a, g_knorm_a, g_cq_b, w_uq_b, g_ckv_b, w_ukv_b, w_out_e, w_in_o, sink_c, w_out_o, w_up, conv_w, conv_b, w_down, g_final):
    depth = w_mod.shape[0]
    cond8 = jnp.concatenate([c_ctx[None, :], c, jnp.zeros((5, D_MODEL), F32)], axis=0)
    lane_pad = KB_PAD - QK_NOPE - QK_ROPE
    wkpe = jnp.pad(w_in_e[0][:, IN_E_MAIN:], ((0, 0), (QK_NOPE, lane_pad)))
    wuq = jnp.pad(w_uq_b[0].reshape(Q_LORA, N_HEADS_B, QK_NOPE + QK_ROPE),
                  ((0, 0), (0, 0), (0, lane_pad))).reshape(Q_LORA, N_HEADS_B * KB_PAD)
    wukv3 = w_ukv_b[0].reshape(KV_LORA, N_HEADS_B, QK_NOPE + V_DIM_B)
    wukk = jnp.pad(wukv3[:, :, :QK_NOPE], ((0, 0), (0, 0), (0, KB_PAD - QK_NOPE))
                   ).reshape(KV_LORA, N_HEADS_B * KB_PAD)
    wukv = wukv3[:, :, QK_NOPE:].reshape(KV_LORA, N_HEADS_B * V_DIM_B)
    gq = jnp.tile(g_qnorm_a[0], N_HEADS_A)[None, :]
    gk = jnp.tile(g_knorm_a[0], N_KV_A)[None, :]
    seg = jnp.arange(256) // HEAD_DIM
    ones = (seg[:, None] == seg[None, :]).astype(BF16)
    cosa, sina = _rope_tables(HEAD_DIM)
    cosb, sinb = _rope_tables(QK_ROPE)
    g_mix3 = g_mix_norm.reshape(depth, 1, D_MODEL)
    g_ffn3 = g_ffn_norm.reshape(depth, 1, D_MODEL)
    conv_b3 = conv_b.reshape(depth, 1, D_FF)
    g_fin2 = g_final[None, :]

    mod = _modulation(cond8, w_mod, b_mod).reshape(depth * 8, 6, D_MODEL)

    n_past = N_LAT_SEQ * PAST_LEN
    kpe_pad = jnp.pad(cache_b_kpe.reshape(n_past, QK_ROPE), ((0, 0), (QK_NOPE, lane_pad)))
    ka_c, va_c, kb_c, vb_c, kc_c, vc_c = _ctx_prep(
        cache_a_k.reshape(n_past, 128), cache_a_v.reshape(n_past, 128),
        cache_b_ckv.reshape(n_past, KV_LORA), kpe_pad,
        cache_c_k.reshape(n_past, 128), cache_c_v.reshape(n_past, 128), wukk, wukv)

    xp = x_prompt.reshape(N_CTX_TOK, D_MODEL)
    xs = x_sample.reshape(N_TOK - N_CTX_TOK, D_MODEL)
    qa, ka, va, qb, kb, vb, nak, nav, nckv, nkpe = _pre0(
        xp, xs, mod, g_mix3, w_in_e, wkpe, gq, gk, g_cq_b.reshape(1, 1, Q_LORA), g_ckv_b.reshape(1, 1, KV_LORA),
        wuq, wukk, wukv, cosa, sina, cosb, sinb, ones)
    ffn_w = (w_up, conv_w, conv_b3, w_down, g_fin2)
    ctx_kw = dict(n_seq=N_CTX_SEQ, seq_len=CTX_LEN, tok_base=0, tq=CTX_LEN, window=0, q_unit=256,
                  key_chunk=CTX_LEN)
    lat_kw = dict(n_seq=N_LAT_SEQ, seq_len=LAT_LEN, tok_base=N_CTX_TOK, key_chunk=512)
    oa = (_attention(qa, ka, va, None, None, None, name="attn_a_ctx", **ctx_kw),
          _attention(qa, ka, va, ka_c, va_c, None, tq=256, q_unit=256, window=0, name="attn_a_lat", **lat_kw))
    ob = (_attention(qb, kb, vb, None, None, None, name="attn_b_ctx", **ctx_kw),
          _attention(qb, kb, vb, kb_c, vb_c, None, tq=512, q_unit=512, window=0, name="attn_b_lat", **lat_kw))
    xp1 = _ffn(xp, [oa[0], ob[0]], mod, 0, g_ffn3, w_out_e, *ffn_w, is_ctx=True, final=False)
    xs1 = _ffn(xs, [oa[1], ob[1]], mod, 0, g_ffn3, w_out_e, *ffn_w, is_ctx=False, final=False)

    qc, kc, vc, nck, ncv = _pre1(xp1, xs1, mod, g_mix3, w_in_o, cosa, sina)
    sink = sink_c[0]
    oc = (_attention(qc, kc, vc, None, None, sink, name="attn_c_ctx", **ctx_kw),
          _attention(qc, kc, vc, kc_c, vc_c, sink, tq=128, q_unit=256, window=WINDOW, name="attn_c_lat",
                     **lat_kw))
    y_prompt = _ffn(xp1, [oc[0]], mod, 1, g_ffn3, w_out_o, *ffn_w, is_ctx=True, final=True)
    y_sample = _ffn(xs1, [oc[1]], mod, 1, g_ffn3, w_out_o, *ffn_w, is_ctx=False, final=True)

    return (y_prompt.reshape(N_CTX_SEQ, CTX_LEN, D_MODEL), y_sample.reshape(N_LAT_SEQ, LAT_LEN, D_MODEL),
            nak.reshape(N_CTX_SEQ, 1, CTX_LEN, N_KV_A, HEAD_DIM),
            nav.reshape(N_CTX_SEQ, 1, CTX_LEN, N_KV_A, HEAD_DIM),
            nckv.reshape(N_CTX_SEQ, 1, CTX_LEN, KV_LORA),
            nkpe.reshape(N_CTX_SEQ, 1, CTX_LEN, QK_ROPE),
            nck.reshape(N_CTX_SEQ, 1, CTX_LEN, N_KV_C, HEAD_DIM),
            ncv.reshape(N_CTX_SEQ, 1, CTX_LEN, N_KV_C, HEAD_DIM))
```

```python
import functools

import jax
import jax.numpy as jnp
import numpy as np
from jax import lax
from jax.experimental import pallas as pl
from jax.experimental.pallas import tpu as pltpu

F32 = jnp.float32
BF16 = jnp.bfloat16

D_MODEL = 1024
N_CTX_SEQ = 16
CTX_LEN = 256
N_LAT_SEQ = 2
LAT_LEN = 2048
PAST_LEN = 512
GRID_W = 64
ROPE_THETA = 10000.0
NORM_EPS = 1e-6
WINDOW = 128
NEG_INF = -1e30
HEAD_DIM = 64
N_HEADS_A, N_KV_A = 8, 2
N_HEADS_B = 8
Q_LORA, KV_LORA = 384, 256
QK_NOPE, QK_ROPE, V_DIM_B = 64, 32, 64
N_HEADS_C, N_KV_C = 16, 2
D_FF = 2816
IN_E_MAIN = N_HEADS_A * HEAD_DIM + 2 * N_KV_A * HEAD_DIM + Q_LORA + KV_LORA

N_CTX_TOK = N_CTX_SEQ * CTX_LEN
N_TOK = N_CTX_TOK + N_LAT_SEQ * LAT_LEN
TM = 512
PRE_SUB = 256
N_TILES = N_TOK // TM
N_CTX_TILES = N_CTX_TOK // TM
LAT_TILES = LAT_LEN // TM
HALO = 16
FFN_TM = 1024
FFN_SUB = CTX_LEN
UP_LOOKAHEAD = 3
FFN_LAT_TILES = LAT_LEN // FFN_TM
FF_CHUNK = 256
N_FF_CHUNKS = D_FF // FF_CHUNK
FF_PER_STEP = 3
N_FF_STEPS = -(-N_FF_CHUNKS // FF_PER_STEP)
KB_PAD = 128
VT_ROWS = 80
SCORE_LOOKAHEAD = 5
VMEM_LIMIT = 56 * 1024 * 1024
FFN_VMEM_LIMIT = 60 * 1024 * 1024


def _dot(a, b):
    return jnp.dot(a, b, preferred_element_type=F32)


def _dot_nt(a, b):
    return lax.dot_general(a, b, (((1,), (1,)), ((), ())), preferred_element_type=F32)


def _rms(x, g):
    return x * lax.rsqrt(jnp.mean(x * x, axis=-1, keepdims=True) + NORM_EPS) * g


def _split_bf16(x):
    hi = x.astype(BF16)
    return hi, (x - hi.astype(F32)).astype(BF16)


def _head_rms(x, g, ones_ref):
    w = x.shape[1]
    hi, lo = _split_bf16(x * x)
    parts = []
    for c in range(0, w, 256):
        cw = min(256, w - c)
        ones = ones_ref[0:cw, 0:cw]
        parts.append(_dot(hi[:, c:c + cw], ones) + _dot(lo[:, c:c + cw], ones))
    ssum = parts[0] if len(parts) == 1 else jnp.concatenate(parts, axis=1)
    return x * lax.rsqrt(ssum * (1.0 / HEAD_DIM) + NORM_EPS) * g


def _swap_pairs(x):
    w = x.shape[1]
    up = pltpu.roll(x, w - 1, axis=1)
    dn = pltpu.roll(x, 1, axis=1)
    lane = lax.broadcasted_iota(jnp.int32, x.shape, 1)
    return jnp.where((lane & 1) == 0, up, dn)


def _rope(x, cos, sin_signed, reps):
    if reps > 1:
        cos = jnp.concatenate([cos] * reps, axis=1)
        sin_signed = jnp.concatenate([sin_signed] * reps, axis=1)
    return x * cos + _swap_pairs(x) * sin_signed


def _store_vt(vt_ref, v, n_heads, cols=slice(None)):
    t = v.shape[0]
    vt = v.T.astype(BF16)
    ones = jnp.ones((VT_ROWS - 64, t), BF16)
    for hh in range(n_heads):
        vt_ref[hh, 0:64, cols] = vt[hh * 64:(hh + 1) * 64]
        vt_ref[hh, 64:VT_ROWS, cols] = ones


def _cond_row(i):
    return jnp.where(i < N_CTX_TILES, 0, 1 + (i - N_CTX_TILES) // LAT_TILES)


def _rope_block(i):
    return jnp.maximum(i - N_CTX_TILES, 0) % LAT_TILES


def _const_spec(shape):
    zeros = (0,) * len(shape)
    return pl.BlockSpec(shape, lambda *_: zeros)


def _layer_spec(shape, layer):
    idx = (layer,) + (0,) * len(shape)
    return pl.BlockSpec((None,) + tuple(shape), lambda *_: idx)


def _mod_kernel(cond_ref, w_ref, b_ref, o_ref):
    c = cond_ref[...]
    s_hi, s_lo = _split_bf16(c * jax.nn.sigmoid(c))
    w_hi, w_lo = _split_bf16(w_ref[0])
    r = _dot(jnp.concatenate([s_hi, s_lo], axis=0), w_hi)
    o_ref[0] = r[0:8] + r[8:16] + _dot(s_hi, w_lo) + b_ref[0]


def _modulation(cond8, w_mod, b_mod):
    depth, _, n = w_mod.shape
    tn = 1536
    return pl.pallas_call(
        _mod_kernel,
        grid=(depth, n // tn),
        in_specs=[
            pl.BlockSpec((8, D_MODEL), lambda l, j: (0, 0)),
            pl.BlockSpec((1, D_MODEL, tn), lambda l, j: (l, 0, j)),
            pl.BlockSpec((1, 1, tn), lambda l, j: (l, 0, j)),
        ],
        out_specs=pl.BlockSpec((1, 8, tn), lambda l, j: (l, 0, j)),
        out_shape=jax.ShapeDtypeStruct((depth, 8, n), F32),
        compiler_params=pltpu.CompilerParams(
            dimension_semantics=("arbitrary", "arbitrary"), vmem_limit_bytes=VMEM_LIMIT),
        name="modulation",
    )(cond8, w_mod, b_mod.reshape(depth, 1, n))


def _pre0_kernel(xp_ref, xs_ref, mod_ref, gmix_ref, win_ref, wkpe_ref, gq_ref, gk_ref, gcq_ref, gckv_ref,
                 wuq_ref, wukk_ref, wukv_ref, cosa_ref, sina_ref, cosb_ref, sinb_ref, ones_ref,
                 qa_ref, ka_ref, va_ref, qb_ref, kb_ref, vb_ref,
                 nak_ref, nav_ref, nckv_ref, nkpe_ref,
                 win_s, wuq_s, wukk_s, wukv_s):
    i = pl.program_id(0)

    @pl.when(i == 0)
    def _():
        win_s[:, 0:IN_E_MAIN] = win_ref[...].astype(BF16)
        win_s[:, IN_E_MAIN:] = wkpe_ref[...].astype(BF16)
        wuq_s[...] = wuq_ref[...].astype(BF16)
        wukk_s[...] = wukk_ref[...].astype(BF16)
        wukv_s[...] = wukv_ref[...].astype(BF16)

    sh1 = mod_ref[0:1, :]
    sc1 = mod_ref[1:2, :]
    subs = [slice(r, r + PRE_SUB) for r in range(0, TM, PRE_SUB)]

    projs = []
    for sl in subs:
        x = jnp.where(i < N_CTX_TILES, xp_ref[sl, :], xs_ref[sl, :])
        h = _rms(x, gmix_ref[...]) * (1.0 + sc1) + sh1
        projs.append(_dot(h.astype(BF16), win_s[...]))

    parts = []
    for sl, proj in zip(subs, projs):
        qa = _head_rms(proj[:, 0:512], gq_ref[...], ones_ref)
        ka = _head_rms(proj[:, 512:640], gk_ref[...], ones_ref)
        va = proj[:, 640:768]
        cq = _rms(proj[:, 768:1152], gcq_ref[...])
        ckv = _rms(proj[:, 1152:1408], gckv_ref[...])
        kpe = proj[:, 1408:1536]
        qb = _dot(cq.astype(BF16), wuq_s[...])
        ckv_b = ckv.astype(BF16)
        kbn = _dot(ckv_b, wukk_s[...])
        vb = _dot(ckv_b, wukv_s[...])
        _store_vt(va_ref, va, N_KV_A, sl)
        _store_vt(vb_ref, vb, N_HEADS_B, sl)
        parts.append((qa, ka, va, qb, ckv, kpe, kbn))

    def store_qk(sl, qa, ka, qb, kpe, kbn):
        qa_s = qa * (HEAD_DIM ** -0.5)
        qb_s = qb * ((QK_NOPE + QK_ROPE) ** -0.5)
        for hh in range(N_HEADS_A):
            qa_ref[hh, sl, :] = qa_s[:, hh * 64:(hh + 1) * 64].astype(BF16)
        for hh in range(N_KV_A):
            ka_ref[hh, sl, :] = ka[:, hh * 64:(hh + 1) * 64].astype(BF16)
        for hh in range(N_HEADS_B):
            qb_ref[hh, sl, :] = qb_s[:, hh * KB_PAD:(hh + 1) * KB_PAD].astype(BF16)
            kb_ref[hh, sl, :] = (kbn[:, hh * KB_PAD:(hh + 1) * KB_PAD] + kpe).astype(BF16)

    @pl.when(i < N_CTX_TILES)
    def _():
        for sl, (qa, ka, va, qb, ckv, kpe, kbn) in zip(subs, parts):
            store_qk(sl, qa, ka, qb, kpe, kbn)
            nak_ref[sl, :] = ka
            nav_ref[sl, :] = va
            nckv_ref[sl, :] = ckv
            nkpe_ref[sl, :] = kpe[:, QK_NOPE:QK_NOPE + QK_ROPE]

    @pl.when(i >= N_CTX_TILES)
    def _():
        for sl, (qa, ka, va, qb, ckv, kpe, kbn) in zip(subs, parts):
            cosa, sina = cosa_ref[sl, :], sina_ref[sl, :]
            cosb, sinb = cosb_ref[sl, :], sinb_ref[sl, :]
            store_qk(sl, _rope(qa, cosa, sina, 4), _rope(ka, cosa, sina, 1),
                     _rope(qb, cosb, sinb, 8), _rope(kpe, cosb, sinb, 1), kbn)


def _pre0(xp, xs, mod, gmix, w_in_e, wkpe, gq, gk, gcq, gckv, wuq, wukk, wukv, cosa, sina, cosb, sinb, ones):
    tile = lambda i: (i, 0)
    head_tile = lambda i: (0, i, 0)
    vt_tile = lambda i: (0, 0, i)
    ctx_tile = lambda i: (jnp.minimum(i, N_CTX_TILES - 1), 0)
    lat_tile = lambda i: (jnp.maximum(i - N_CTX_TILES, 0), 0)
    rope_tile = lambda i: (_rope_block(i), 0)
    in_specs = [
        pl.BlockSpec((TM, D_MODEL), ctx_tile),
        pl.BlockSpec((TM, D_MODEL), lat_tile),
        pl.BlockSpec((None, 6, D_MODEL), lambda i: (_cond_row(i), 0, 0)),
        _layer_spec((1, D_MODEL), 0),
        _layer_spec((D_MODEL, IN_E_MAIN), 0),
        _const_spec(wkpe.shape),
        _const_spec(gq.shape), _const_spec(gk.shape), _layer_spec((1, Q_LORA), 0), _layer_spec((1, KV_LORA), 0),
        _const_spec(wuq.shape), _const_spec(wukk.shape), _const_spec(wukv.shape),
        pl.BlockSpec((TM, 128), rope_tile), pl.BlockSpec((TM, 128), rope_tile),
        pl.BlockSpec((TM, 128), rope_tile), pl.BlockSpec((TM, 128), rope_tile),
        _const_spec(ones.shape),
    ]
    out_shape = [
        jax.ShapeDtypeStruct((N_HEADS_A, N_TOK, 64), BF16),
        jax.ShapeDtypeStruct((N_KV_A, N_TOK, 64), BF16),
        jax.ShapeDtypeStruct((N_KV_A, VT_ROWS, N_TOK), BF16),
        jax.ShapeDtypeStruct((N_HEADS_B, N_TOK, KB_PAD), BF16),
        jax.ShapeDtypeStruct((N_HEADS_B, N_TOK, KB_PAD), BF16),
        jax.ShapeDtypeStruct((N_HEADS_B, VT_ROWS, N_TOK), BF16),
        jax.ShapeDtypeStruct((N_CTX_TOK, 128), F32),
        jax.ShapeDtypeStruct((N_CTX_TOK, 128), F32),
        jax.ShapeDtypeStruct((N_CTX_TOK, KV_LORA), F32),
        jax.ShapeDtypeStruct((N_CTX_TOK, QK_ROPE), F32),
    ]
    out_specs = [
        pl.BlockSpec((N_HEADS_A, TM, 64), head_tile),
        pl.BlockSpec((N_KV_A, TM, 64), head_tile),
        pl.BlockSpec((N_KV_A, VT_ROWS, TM), vt_tile),
        pl.BlockSpec((N_HEADS_B, TM, KB_PAD), head_tile),
        pl.BlockSpec((N_HEADS_B, TM, KB_PAD), head_tile),
        pl.BlockSpec((N_HEADS_B, VT_ROWS, TM), vt_tile),
        pl.BlockSpec((TM, 128), ctx_tile),
        pl.BlockSpec((TM, 128), ctx_tile),
        pl.BlockSpec((TM, KV_LORA), ctx_tile),
        pl.BlockSpec((TM, QK_ROPE), ctx_tile),
    ]
    scratch = [
        pltpu.VMEM((D_MODEL, IN_E_MAIN + KB_PAD), BF16),
        pltpu.VMEM(wuq.shape, BF16), pltpu.VMEM(wukk.shape, BF16), pltpu.VMEM(wukv.shape, BF16),
    ]
    return pl.pallas_call(
        _pre0_kernel, grid=(N_TILES,), in_specs=in_specs, out_specs=out_specs, out_shape=out_shape,
        scratch_shapes=scratch,
        compiler_params=pltpu.CompilerParams(
            dimension_semantics=("arbitrary",), vmem_limit_bytes=VMEM_LIMIT),
        name="pre0",
    )(xp, xs, mod, gmix, w_in_e, wkpe, gq, gk, gcq, gckv, wuq, wukk, wukv, cosa, sina, cosb, sinb, ones)


def _pre1_kernel(xp_ref, xs_ref, mod_ref, gmix_ref, win_ref, cosa_ref, sina_ref,
                 q_ref, k_ref, v_ref, nk_ref, nv_ref, win_s):
    i = pl.program_id(0)

    @pl.when(i == 0)
    def _():
        win_s[...] = win_ref[...].astype(BF16)

    sh1 = mod_ref[0:1, :]
    sc1 = mod_ref[1:2, :]
    subs = [slice(r, r + PRE_SUB) for r in range(0, TM, PRE_SUB)]
    projs = []
    for sl in subs:
        x = jnp.where(i < N_CTX_TILES, xp_ref[sl, :], xs_ref[sl, :])
        h = _rms(x, gmix_ref[...]) * (1.0 + sc1) + sh1
        projs.append(_dot(h.astype(BF16), win_s[...]))
    for sl, proj in zip(subs, projs):
        _store_vt(v_ref, proj[:, 1152:1280], N_KV_C, sl)

    def store_qk(sl, q, k):
        q_s = q * (HEAD_DIM ** -0.5)
        for hh in range(N_HEADS_C):
            q_ref[hh, sl, :] = q_s[:, hh * 64:(hh + 1) * 64].astype(BF16)
        for hh in range(N_KV_C):
            k_ref[hh, sl, :] = k[:, hh * 64:(hh + 1) * 64].astype(BF16)

    @pl.when(i < N_CTX_TILES)
    def _():
        for sl, proj in zip(subs, projs):
            store_qk(sl, proj[:, 0:1024], proj[:, 1024:1152])
            nk_ref[sl, :] = proj[:, 1024:1152]
            nv_ref[sl, :] = proj[:, 1152:1280]

    @pl.when(i >= N_CTX_TILES)
    def _():
        for sl, proj in zip(subs, projs):
            cosa, sina = cosa_ref[sl, :], sina_ref[sl, :]
            store_qk(sl, _rope(proj[:, 0:1024], cosa, sina, 8), _rope(proj[:, 1024:1152], cosa, sina, 1))


def _pre1(xp, xs, mod, gmix, w_in_o, cosa, sina):
    n_in = w_in_o.shape[-1]
    head_tile = lambda i: (0, i, 0)
    vt_tile = lambda i: (0, 0, i)
    ctx_tile = lambda i: (jnp.minimum(i, N_CTX_TILES - 1), 0)
    lat_tile = lambda i: (jnp.maximum(i - N_CTX_TILES, 0), 0)
    rope_tile = lambda i: (_rope_block(i), 0)
    in_specs = [
        pl.BlockSpec((TM, D_MODEL), ctx_tile),
        pl.BlockSpec((TM, D_MODEL), lat_tile),
        pl.BlockSpec((None, 6, D_MODEL), lambda i: (8 + _cond_row(i), 0, 0)),
        _layer_spec((1, D_MODEL), 1),
        _layer_spec((D_MODEL, n_in), 0),
        pl.BlockSpec((TM, 128), rope_tile), pl.BlockSpec((TM, 128), rope_tile),
    ]
    out_shape = [
        jax.ShapeDtypeStruct((N_HEADS_C, N_TOK, 64), BF16),
        jax.ShapeDtypeStruct((N_KV_C, N_TOK, 64), BF16),
        jax.ShapeDtypeStruct((N_KV_C, VT_ROWS, N_TOK), BF16),
        jax.ShapeDtypeStruct((N_CTX_TOK, 128), F32),
        jax.ShapeDtypeStruct((N_CTX_TOK, 128), F32),
    ]
    out_specs = [
        pl.BlockSpec((N_HEADS_C, TM, 64), head_tile),
        pl.BlockSpec((N_KV_C, TM, 64), head_tile),
        pl.BlockSpec((N_KV_C, VT_ROWS, TM), vt_tile),
        pl.BlockSpec((TM, 128), ctx_tile),
        pl.BlockSpec((TM, 128), ctx_tile),
    ]
    return pl.pallas_call(
        _pre1_kernel, grid=(N_TILES,), in_specs=in_specs, out_specs=out_specs, out_shape=out_shape,
        scratch_shapes=[pltpu.VMEM((D_MODEL, n_in), BF16)],
        compiler_params=pltpu.CompilerParams(
            dimension_semantics=("arbitrary",), vmem_limit_bytes=VMEM_LIMIT),
        name="pre1",
    )(xp, xs, mod, gmix, w_in_o, cosa, sina)


def _ctx_kernel(ak_ref, av_ref, ckv_ref, kpe_ref, ck_ref, cv_ref, wukk_ref, wukv_ref,
                ka_ref, va_ref, kb_ref, vb_ref, kc_ref, vc_ref):
    ak, av, ck, cv = ak_ref[...], av_ref[...], ck_ref[...], cv_ref[...]
    for hh in range(2):
        sl = slice(hh * 64, (hh + 1) * 64)
        ka_ref[hh] = ak[:, sl].astype(BF16)
        kc_ref[hh] = ck[:, sl].astype(BF16)
    _store_vt(va_ref, av, 2)
    _store_vt(vc_ref, cv, 2)
    ckv_b = ckv_ref[...].astype(BF16)
    kbn = _dot(ckv_b, wukk_ref[...].astype(BF16))
    vb = _dot(ckv_b, wukv_ref[...].astype(BF16))
    kpe = kpe_ref[...]
    for hh in range(N_HEADS_B):
        kb_ref[hh] = (kbn[:, hh * KB_PAD:(hh + 1) * KB_PAD] + kpe).astype(BF16)
    _store_vt(vb_ref, vb, N_HEADS_B)


def _ctx_prep(ak, av, ckv, kpe_pad, ck, cv, wukk, wukv):
    n = N_LAT_SEQ * PAST_LEN
    row = lambda b: (b, 0)
    head_row = lambda b: (0, b, 0)
    vt_row = lambda b: (0, 0, b)
    in_specs = [
        pl.BlockSpec((PAST_LEN, 128), row), pl.BlockSpec((PAST_LEN, 128), row),
        pl.BlockSpec((PAST_LEN, KV_LORA), row), pl.BlockSpec((PAST_LEN, KB_PAD), row),
        pl.BlockSpec((PAST_LEN, 128), row), pl.BlockSpec((PAST_LEN, 128), row),
        _const_spec(wukk.shape), _const_spec(wukv.shape),
    ]
    out_shape = [
        jax.ShapeDtypeStruct((2, n, 64), BF16), jax.ShapeDtypeStruct((2, VT_ROWS, n), BF16),
        jax.ShapeDtypeStruct((N_HEADS_B, n, KB_PAD), BF16), jax.ShapeDtypeStruct((N_HEADS_B, VT_ROWS, n), BF16),
        jax.ShapeDtypeStruct((2, n, 64), BF16), jax.ShapeDtypeStruct((2, VT_ROWS, n), BF16),
    ]
    out_specs = [
        pl.BlockSpec((2, PAST_LEN, 64), head_row), pl.BlockSpec((2, VT_ROWS, PAST_LEN), vt_row),
        pl.BlockSpec((N_HEADS_B, PAST_LEN, KB_PAD), head_row), pl.BlockSpec((N_HEADS_B, VT_ROWS, PAST_LEN), vt_row),
        pl.BlockSpec((2, PAST_LEN, 64), head_row), pl.BlockSpec((2, VT_ROWS, PAST_LEN), vt_row),
    ]
    return pl.pallas_call(
        _ctx_kernel, grid=(N_LAT_SEQ,), in_specs=in_specs, out_specs=out_specs, out_shape=out_shape,
        compiler_params=pltpu.CompilerParams(
            dimension_semantics=("arbitrary",), vmem_limit_bytes=VMEM_LIMIT),
        name="ctx_prep",
    )(ak, av, ckv, kpe_pad, ck, cv, wukk, wukv)


def _softmax_units(units, lookahead):
    tasks = [(u, c) for u, unit in enumerate(units) for c in range(len(unit["chunks"]))]
    scores = {}

    def emit_scores(t):
        u, c = tasks[t]
        k, _, mask = units[u]["chunks"][c]
        s = _dot_nt(k, units[u]["q"])
        scores[t] = s if mask is None else jnp.where(mask, s, NEG_INF)

    for t in range(min(lookahead, len(tasks))):
        emit_scores(t)
    for t, (u, c) in enumerate(tasks):
        if t + lookahead < len(tasks):
            emit_scores(t + lookahead)
        unit = units[u]
        s = scores.pop(t)
        m, acc = unit["m"], unit["acc"]
        cmax = jnp.max(s, axis=0, keepdims=True)
        m_new = cmax if m is None else jnp.maximum(m, cmax)
        pv = _dot(unit["chunks"][c][1], jnp.exp(s - m_new).astype(BF16))
        unit["acc"] = pv if acc is None else acc * jnp.exp(m - m_new) + pv
        unit["m"] = m_new
    return [unit["acc"][0:64] * (1.0 / unit["acc"][64:65]) for unit in units]


def _attn_kernel(*refs, n_kv, group, tq, seq_len, has_ctx, has_sink, window, q_unit, key_chunk):
    refs = list(refs)
    q_ref = refs.pop(0)
    if window:
        kp_ref, kc_ref, kn_ref, vp_ref, vc_ref, vn_ref = refs[:6]
        refs = refs[6:]
    else:
        k_ref, vt_ref = refs[:2]
        refs = refs[2:]
    if has_ctx:
        kx_ref, vx_ref = refs[:2]
        refs = refs[2:]
    if has_sink:
        sink_ref = refs.pop(0)
    o_ref = refs.pop(0)

    j = pl.program_id(1)
    dk = q_ref.shape[-1]
    heads_per_unit = q_unit // tq
    lane = lax.broadcasted_iota(jnp.int32, (1, q_unit), 1)
    if window:
        n_band = tq + 2 * window
        krow = lax.broadcasted_iota(jnp.int32, (n_band, q_unit), 0)
        qcol = lax.broadcasted_iota(jnp.int32, (n_band, q_unit), 1) & (tq - 1)
        rel = (krow - window) - qcol
        band_mask = ((jnp.abs(rel) <= window)
                     & ((krow >= window) | (j > 0))
                     & ((krow < window + tq) | (j < seq_len // tq - 1)))
    if has_sink:
        acc0 = jnp.where(lax.broadcasted_iota(jnp.int32, (VT_ROWS, q_unit), 0) >= 64, 1.0, 0.0)

    units = []
    for hk in range(n_kv):
        chunks = []
        if window:
            chunks.append((jnp.concatenate([kp_ref[hk], kc_ref[hk], kn_ref[hk]], axis=0),
                           jnp.concatenate([vp_ref[hk], vc_ref[hk], vn_ref[hk]], axis=1), band_mask))
        else:
            for c in range(0, seq_len, key_chunk):
                n = min(key_chunk, seq_len - c)
                chunks.append((k_ref[hk, c:c + n, :], vt_ref[hk, :, c:c + n], None))
        if has_ctx:
            for c in range(0, PAST_LEN, key_chunk):
                n = min(key_chunk, PAST_LEN - c)
                chunks.append((kx_ref[hk, c:c + n, :], vx_ref[hk, :, c:c + n], None))
        for u in range(group // heads_per_unit):
            h0 = hk * group + u * heads_per_unit
            unit = dict(q=q_ref[h0:h0 + heads_per_unit].reshape(q_unit, dk), chunks=chunks, m=None, acc=None)
            if has_sink:
                m0 = jnp.full((1, q_unit), sink_ref[h0], F32)
                for e in range(1, heads_per_unit):
                    m0 = jnp.where(lane >= e * tq, sink_ref[h0 + e], m0)
                unit.update(m=m0, acc=acc0)
            units.append(unit)
    outs = []
    for o in _softmax_units(units, SCORE_LOOKAHEAD):
        for e in range(heads_per_unit):
            outs.append(o[:, e * tq:(e + 1) * tq])
    o_ref[...] = jnp.concatenate(outs, axis=0).T.astype(BF16)


def _attention(q, k, vt, kx, vx, sink, *, n_seq, seq_len, tok_base, tq, window, q_unit, key_chunk, name):
    n_q, _, dk = q.shape
    n_kv = k.shape[0]
    group = n_q // n_kv
    n_qt = seq_len // tq
    q_base = tok_base // tq
    s_base = tok_base // seq_len
    has_ctx = kx is not None
    has_sink = sink is not None
    q_blk = lambda b, j: q_base + b * n_qt + j
    in_specs = [pl.BlockSpec((n_q, tq, dk), lambda b, j: (0, q_blk(b, j), 0))]
    args = [q]
    if window:
        assert window == tq
        prev = lambda b, j: q_base + b * n_qt + jnp.maximum(j - 1, 0)
        nxt = lambda b, j: q_base + b * n_qt + jnp.minimum(j + 1, n_qt - 1)
        for blk in (prev, q_blk, nxt):
            in_specs.append(pl.BlockSpec((n_kv, tq, dk), lambda b, j, blk=blk: (0, blk(b, j), 0)))
        for blk in (prev, q_blk, nxt):
            in_specs.append(pl.BlockSpec((n_kv, VT_ROWS, tq), lambda b, j, blk=blk: (0, 0, blk(b, j))))
        args += [k, k, k, vt, vt, vt]
    else:
        in_specs += [
            pl.BlockSpec((n_kv, seq_len, dk), lambda b, j: (0, s_base + b, 0)),
            pl.BlockSpec((n_kv, VT_ROWS, seq_len), lambda b, j: (0, 0, s_base + b)),
        ]
        args += [k, vt]
    if has_ctx:
        in_specs += [
            pl.BlockSpec((n_kv, PAST_LEN, dk), lambda b, j: (0, b, 0)),
            pl.BlockSpec((n_kv, VT_ROWS, PAST_LEN), lambda b, j: (0, 0, b)),
        ]
        args += [kx, vx]
    if has_sink:
        in_specs.append(pl.BlockSpec(memory_space=pltpu.SMEM))
        args.append(sink)
    kern = functools.partial(_attn_kernel, n_kv=n_kv, group=group, tq=tq, seq_len=seq_len,
                             has_ctx=has_ctx, has_sink=has_sink, window=window,
                             q_unit=q_unit, key_chunk=key_chunk)
    return pl.pallas_call(
        kern, grid=(n_seq, n_qt), in_specs=in_specs,
        out_specs=pl.BlockSpec((tq, n_q * HEAD_DIM), lambda b, j: (b * n_qt + j, 0)),
        out_shape=jax.ShapeDtypeStruct((n_seq * seq_len, n_q * HEAD_DIM), BF16),
        compiler_params=pltpu.CompilerParams(
            dimension_semantics=("arbitrary", "arbitrary"), vmem_limit_bytes=VMEM_LIMIT),
        name=name,
    )(*args)


def _ffn_kernel(*refs, n_o, is_ctx, final):
    halo = 0 if is_ctx else HALO
    it = iter(refs)
    x_ref = next(it)
    xh_refs = None if is_ctx else (next(it), next(it))
    o_refs, oh_refs = [], []
    for _ in range(n_o):
        o_refs.append(next(it))
        if not is_ctx:
            oh_refs.append((next(it), next(it)))
    wo_refs = [next(it) for _ in range(n_o)]
    mod_ref, gffn_ref = next(it), next(it)
    wg_refs = [next(it) for _ in range(FF_PER_STEP)]
    wv_refs = [next(it) for _ in range(FF_PER_STEP)]
    cw_ref, cb_ref = next(it), next(it)
    wd_refs = [next(it) for _ in range(FF_PER_STEP)]
    gfin_ref, out_ref, h2e_ref, acc_ref = next(it), next(it), next(it), next(it)

    m = pl.program_id(0)
    c = pl.program_id(1)
    g1 = mod_ref[2:3, :]
    sh2 = mod_ref[3:4, :]
    sc2 = mod_ref[4:5, :]
    g2 = mod_ref[5:6, :]
    n_sub = FFN_TM // FFN_SUB
    sub_rows = FFN_SUB + 2 * halo

    @pl.when(c == 0)
    def _():
        wos = [w[...].astype(BF16) for w in wo_refs]

        def residual_and_norm(xv, ovs):
            attn = _dot(ovs[0], wos[0])
            for ov, wo in zip(ovs[1:], wos[1:]):
                attn = attn + _dot(ov, wo)
            x1 = xv + g1 * attn
            return x1, (_rms(x1, gffn_ref[...]) * (1.0 + sc2) + sh2).astype(BF16)

        for r in range(0, FFN_TM, FFN_SUB):
            x1, h2 = residual_and_norm(x_ref[r:r + FFN_SUB, :], [o[r:r + FFN_SUB, :] for o in o_refs])
            acc_ref[r:r + FFN_SUB, :] = x1
            h2e_ref[halo + r:halo + r + FFN_SUB, :] = h2
        if not is_ctx:
            _, h2h = residual_and_norm(
                jnp.concatenate([xh_refs[0][...], xh_refs[1][...]], axis=0),
                [jnp.concatenate([oh[0][...], oh[1][...]], axis=0) for oh in oh_refs])
            h2e_ref[0:HALO, :] = h2h[0:HALO]
            h2e_ref[HALO + FFN_TM:, :] = h2h[HALO:]

    row8 = lax.broadcasted_iota(jnp.int32, (8, FF_CHUNK), 0)
    if not is_ctx:
        has_prev = m % FFN_LAT_TILES != 0
        has_next = m % FFN_LAT_TILES != FFN_LAT_TILES - 1

    def ff_chunk(j):
        w_up = jnp.concatenate([wg_refs[j][...].astype(BF16), wv_refs[j][...].astype(BF16)], axis=1)
        w_dn = wd_refs[j][...].astype(BF16)
        cols = pl.ds(pl.multiple_of((c * FF_PER_STEP + j) * FF_CHUNK, FF_CHUNK), FF_CHUNK)
        cw = cw_ref[:, cols]
        cb = cb_ref[:, cols]
        ups = {}

        def emit_up(r):
            ups[r] = _dot(h2e_ref[r * FFN_SUB:r * FFN_SUB + sub_rows, :], w_up)

        for r in range(min(UP_LOOKAHEAD, n_sub)):
            emit_up(r)
        for r in range(n_sub):
            if r + UP_LOOKAHEAD < n_sub:
                emit_up(r + UP_LOOKAHEAD)
            up = ups.pop(r)
            ge = up[:, :FF_CHUNK]
            val = up[halo:halo + FFN_SUB, FF_CHUNK:]
            g_prev = pltpu.roll(ge, 1, axis=0)[halo:halo + FFN_SUB]
            g_next = pltpu.roll(ge, sub_rows - 1, axis=0)[halo:halo + FFN_SUB]
            prev_ok = False if is_ctx else (has_prev if r == 0 else True)
            next_ok = False if is_ctx else (has_next if r == n_sub - 1 else True)
            if prev_ok is not True:
                g_prev = jnp.concatenate(
                    [jnp.where(jnp.logical_or(row8 != 0, prev_ok), g_prev[0:8], 0.0), g_prev[8:]], axis=0)
            if next_ok is not True:
                g_next = jnp.concatenate(
                    [g_next[:-8], jnp.where(jnp.logical_or(row8 != 7, next_ok), g_next[-8:], 0.0)], axis=0)
            gate = g_prev * cw[0:1, :] + ge[halo:halo + FFN_SUB] * cw[1:2, :] + g_next * cw[2:3, :] + cb
            act = (gate * jax.nn.sigmoid(gate) * val).astype(BF16)
            acc_ref[r * FFN_SUB:(r + 1) * FFN_SUB, :] += g2 * _dot(act, w_dn)

    n_tail = N_FF_CHUNKS % FF_PER_STEP
    for j in range(FF_PER_STEP):
        if n_tail and j >= n_tail:
            pl.when(c < N_FF_STEPS - 1)(functools.partial(ff_chunk, j))
        else:
            ff_chunk(j)

    @pl.when(c == N_FF_STEPS - 1)
    def _():
        x2 = acc_ref[...]
        out_ref[...] = _rms(x2, gfin_ref[...]) if final else x2


def _ffn(x, os, mod, layer, g_ffn, w_out, w_up, conv_w, conv_b, w_down, g_final, *, is_ctx, final):
    n_rows = x.shape[0]
    nh = FFN_TM // HALO
    nblk = n_rows // HALO
    n_o = len(os)
    halo = 0 if is_ctx else HALO
    tile = lambda m, c: (m, 0)
    prev = lambda m, c: (jnp.maximum(m * nh - 1, 0), 0)
    nxt = lambda m, c: (jnp.minimum((m + 1) * nh, nblk - 1), 0)
    chunk = lambda c, j: jnp.minimum(c * FF_PER_STEP + j, N_FF_CHUNKS - 1)
    if is_ctx:
        cond = lambda m: layer * 8
    else:
        cond = lambda m: layer * 8 + 1 + m // FFN_LAT_TILES

    def with_halo(arr):
        w = arr.shape[1]
        specs = [pl.BlockSpec((FFN_TM, w), tile)]
        if not is_ctx:
            specs += [pl.BlockSpec((HALO, w), prev), pl.BlockSpec((HALO, w), nxt)]
        return specs, [arr] * len(specs)

    in_specs, args = with_halo(x)
    for o in os:
        specs, arrs = with_halo(o)
        in_specs += specs
        args += arrs
    w_rows = D_MODEL // n_o
    for t in range(n_o):
        in_specs.append(pl.BlockSpec((None, w_rows, D_MODEL), lambda m, c, t=t: (0, t, 0)))
        args.append(w_out)
    in_specs += [
        pl.BlockSpec((None, 6, D_MODEL), lambda m, c: (cond(m), 0, 0)),
        pl.BlockSpec((None, 1, D_MODEL), lambda m, c: (layer, 0, 0)),
    ]
    args += [mod, g_ffn]
    steps = range(FF_PER_STEP)
    in_specs += [pl.BlockSpec((None, D_MODEL, FF_CHUNK), lambda m, c, j=j: (layer, 0, chunk(c, j))) for j in steps]
    in_specs += [pl.BlockSpec((None, D_MODEL, FF_CHUNK), lambda m, c, j=j: (layer, 0, N_FF_CHUNKS + chunk(c, j)))
                 for j in steps]
    in_specs += [pl.BlockSpec((None, 3, D_FF), lambda m, c: (layer, 0, 0)),
                 pl.BlockSpec((None, 1, D_FF), lambda m, c: (layer, 0, 0))]
    in_specs += [pl.BlockSpec((None, FF_CHUNK, D_MODEL), lambda m, c, j=j: (layer, chunk(c, j), 0)) for j in steps]
    args += [w_up] * (2 * FF_PER_STEP) + [conv_w, conv_b] + [w_down] * FF_PER_STEP
    in_specs.append(pl.BlockSpec((1, D_MODEL), lambda m, c: (0, 0)))
    args.append(g_final)
    return pl.pallas_call(
        functools.partial(_ffn_kernel, n_o=n_o, is_ctx=is_ctx, final=final),
        grid=(n_rows // FFN_TM, N_FF_STEPS), in_specs=in_specs,
        out_specs=pl.BlockSpec((FFN_TM, D_MODEL), tile),
        out_shape=jax.ShapeDtypeStruct((n_rows, D_MODEL), F32),
        scratch_shapes=[pltpu.VMEM((FFN_TM + 2 * halo, D_MODEL), BF16), pltpu.VMEM((FFN_TM, D_MODEL), F32)],
        compiler_params=pltpu.CompilerParams(
            dimension_semantics=("arbitrary", "arbitrary"), vmem_limit_bytes=FFN_VMEM_LIMIT),
        name=("ffn_ctx" if is_ctx else "ffn_lat") + ("_final" if final else ""),
    )(*args)


def _rope_tables(rot_dim):
    f32 = np.float32
    t = np.arange(LAT_LEN)
    row = (t // GRID_W).astype(f32)
    col = (t % GRID_W).astype(f32)
    d_axis = rot_dim // 2
    freqs = (f32(ROPE_THETA) ** (-np.arange(0, d_axis, 2, dtype=f32) / f32(d_axis))).astype(f32)
    ang = np.concatenate([row[:, None] * freqs, col[:, None] * freqs], axis=-1)
    cos = np.repeat(np.cos(ang), 2, axis=-1).astype(f32)
    sin = (np.repeat(np.sin(ang), 2, axis=-1) * np.tile(np.array([-1.0, 1.0], f32), rot_dim // 2)).astype(f32)
    if rot_dim == HEAD_DIM:
        cos = np.tile(cos, (1, 2))
        sin = np.tile(sin, (1, 2))
    else:
        cos = np.concatenate([np.ones((LAT_LEN, QK_NOPE), f32), cos,
                              np.ones((LAT_LEN, 128 - QK_NOPE - rot_dim), f32)], axis=-1)
        sin = np.concatenate([np.zeros((LAT_LEN, QK_NOPE), f32), sin,
                              np.zeros((LAT_LEN, 128 - QK_NOPE - rot_dim), f32)], axis=-1)
    return jnp.asarray(cos), jnp.asarray(sin)


def kernel(x_prompt, x_sample, cache_a_k, cache_a_v, cache_b_ckv, cache_b_kpe, cache_c_k, cache_c_v, c, c_ctx, w_mod, b_mod, g_mix_norm, g_ffn_norm, w_in_e, g_qnorm_a, g_knorm_a, g_cq_b, w_uq_b, g_ckv_b, w_ukv_b, w_out_e, w_in_o, sink_c, w_out_o, w_up, conv_w, conv_b, w_down, g_final):
    depth = w_mod.shape[0]
    cond8 = jnp.concatenate([c_ctx[None, :], c, jnp.zeros((5, D_MODEL), F32)], axis=0)
    lane_pad = KB_PAD - QK_NOPE - QK_ROPE
    wkpe = jnp.pad(w_in_e[0][:, IN_E_MAIN:], ((0, 0), (QK_NOPE, lane_pad)))
    wuq = jnp.pad(w_uq_b[0].reshape(Q_LORA, N_HEADS_B, QK_NOPE + QK_ROPE),
                  ((0, 0), (0, 0), (0, lane_pad))).reshape(Q_LORA, N_HEADS_B * KB_PAD)
    wukv3 = w_ukv_b[0].reshape(KV_LORA, N_HEADS_B, QK_NOPE + V_DIM_B)
    wukk = jnp.pad(wukv3[:, :, :QK_NOPE], ((0, 0), (0, 0), (0, KB_PAD - QK_NOPE))
                   ).reshape(KV_LORA, N_HEADS_B * KB_PAD)
    wukv = wukv3[:, :, QK_NOPE:].reshape(KV_LORA, N_HEADS_B * V_DIM_B)
    gq = jnp.tile(g_qnorm_a[0], N_HEADS_A)[None, :]
    gk = jnp.tile(g_knorm_a[0], N_KV_A)[None, :]
    seg = np.arange(256) // HEAD_DIM
    ones = jnp.asarray(seg[:, None] == seg[None, :], dtype=BF16)
    cosa, sina = _rope_tables(HEAD_DIM)
    cosb, sinb = _rope_tables(QK_ROPE)
    g_mix3 = g_mix_norm.reshape(depth, 1, D_MODEL)
    g_ffn3 = g_ffn_norm.reshape(depth, 1, D_MODEL)
    conv_b3 = conv_b.reshape(depth, 1, D_FF)
    g_fin2 = g_final[None, :]

    mod = _modulation(cond8, w_mod, b_mod).reshape(depth * 8, 6, D_MODEL)

    n_past = N_LAT_SEQ * PAST_LEN
    kpe_pad = jnp.pad(cache_b_kpe.reshape(n_past, QK_ROPE), ((0, 0), (QK_NOPE, lane_pad)))
    ka_c, va_c, kb_c, vb_c, kc_c, vc_c = _ctx_prep(
        cache_a_k.reshape(n_past, 128), cache_a_v.reshape(n_past, 128),
        cache_b_ckv.reshape(n_past, KV_LORA), kpe_pad,
        cache_c_k.reshape(n_past, 128), cache_c_v.reshape(n_past, 128), wukk, wukv)

    xp = x_prompt.reshape(N_CTX_TOK, D_MODEL)
    xs = x_sample.reshape(N_TOK - N_CTX_TOK, D_MODEL)
    qa, ka, va, qb, kb, vb, nak, nav, nckv, nkpe = _pre0(
        xp, xs, mod, g_mix3, w_in_e, wkpe, gq, gk, g_cq_b.reshape(1, 1, Q_LORA), g_ckv_b.reshape(1, 1, KV_LORA),
        wuq, wukk, wukv, cosa, sina, cosb, sinb, ones)
    ffn_w = (w_up, conv_w, conv_b3, w_down, g_fin2)
    ctx_kw = dict(n_seq=N_CTX_SEQ, seq_len=CTX_LEN, tok_base=0, tq=CTX_LEN, window=0, q_unit=256,
                  key_chunk=CTX_LEN)
    lat_kw = dict(n_seq=N_LAT_SEQ, seq_len=LAT_LEN, tok_base=N_CTX_TOK, key_chunk=512)
    oa = (_attention(qa, ka, va, None, None, None, name="attn_a_ctx", **ctx_kw),
          _attention(qa, ka, va, ka_c, va_c, None, tq=256, q_unit=256, window=0, name="attn_a_lat", **lat_kw))
    ob = (_attention(qb, kb, vb, None, None, None, name="attn_b_ctx", **ctx_kw),
          _attention(qb, kb, vb, kb_c, vb_c, None, tq=256, q_unit=256, window=0, name="attn_b_lat", **lat_kw))
    xp1 = _ffn(xp, [oa[0], ob[0]], mod, 0, g_ffn3, w_out_e, *ffn_w, is_ctx=True, final=False)
    xs1 = _ffn(xs, [oa[1], ob[1]], mod, 0, g_ffn3, w_out_e, *ffn_w, is_ctx=False, final=False)

    qc, kc, vc, nck, ncv = _pre1(xp1, xs1, mod, g_mix3, w_in_o, cosa, sina)
    sink = sink_c[0]
    oc = (_attention(qc, kc, vc, None, None, sink, name="attn_c_ctx", **ctx_kw),
          _attention(qc, kc, vc, kc_c, vc_c, sink, tq=128, q_unit=256, window=WINDOW, name="attn_c_lat",
                     **lat_kw))
    y_prompt = _ffn(xp1, [oc[0]], mod, 1, g_ffn3, w_out_o, *ffn_w, is_ctx=True, final=True)
    y_sample = _ffn(xs1, [oc[1]], mod, 1, g_ffn3, w_out_o, *ffn_w, is_ctx=False, final=True)

    return (y_prompt.reshape(N_CTX_SEQ, CTX_LEN, D_MODEL), y_sample.reshape(N_LAT_SEQ, LAT_LEN, D_MODEL),
            nak.reshape(N_CTX_SEQ, 1, CTX_LEN, N_KV_A, HEAD_DIM),
            nav.reshape(N_CTX_SEQ, 1, CTX_LEN, N_KV_A, HEAD_DIM),
            nckv.reshape(N_CTX_SEQ, 1, CTX_LEN, KV_LORA),
            nkpe.reshape(N_CTX_SEQ, 1, CTX_LEN, QK_ROPE),
            nck.reshape(N_CTX_SEQ, 1, CTX_LEN, N_KV_C, HEAD_DIM),
            ncv.reshape(N_CTX_SEQ, 1, CTX_LEN, N_KV_C, HEAD_DIM))
```

```python
import functools

import jax
import jax.numpy as jnp
import numpy as np
from jax import lax
from jax.experimental import pallas as pl
from jax.experimental.pallas import tpu as pltpu

F32 = jnp.float32
BF16 = jnp.bfloat16

D_MODEL = 1024
N_CTX_SEQ = 16
CTX_LEN = 256
N_LAT_SEQ = 2
LAT_LEN = 2048
PAST_LEN = 512
GRID_W = 64
ROPE_THETA = 10000.0
NORM_EPS = 1e-6
WINDOW = 128
NEG_INF = -1e30
LOG2E = 1.4426950408889634
HEAD_DIM = 64
N_HEADS_A, N_KV_A = 8, 2
N_HEADS_B = 8
Q_LORA, KV_LORA = 384, 256
QK_NOPE, QK_ROPE, V_DIM_B = 64, 32, 64
N_HEADS_C, N_KV_C = 16, 2
D_FF = 2816
IN_E_MAIN = N_HEADS_A * HEAD_DIM + 2 * N_KV_A * HEAD_DIM + Q_LORA + KV_LORA

N_CTX_TOK = N_CTX_SEQ * CTX_LEN
N_TOK = N_CTX_TOK + N_LAT_SEQ * LAT_LEN
TM = 512
PRE_SUB = 256
N_TILES = N_TOK // TM
N_CTX_TILES = N_CTX_TOK // TM
LAT_TILES = LAT_LEN // TM
HALO = 16
FFN_TM = 1024
FFN_SUB = CTX_LEN
UP_LOOKAHEAD = 3
FFN_LAT_TILES = LAT_LEN // FFN_TM
FF_CHUNK = 256
N_FF_CHUNKS = D_FF // FF_CHUNK
FF_PER_STEP = 3
N_FF_STEPS = -(-N_FF_CHUNKS // FF_PER_STEP)
KB_PAD = 128
VT_ROWS = 80
SCORE_LOOKAHEAD = 5
VMEM_LIMIT = 56 * 1024 * 1024
FFN_VMEM_LIMIT = 60 * 1024 * 1024


def _dot(a, b):
    return jnp.dot(a, b, preferred_element_type=F32)


def _dot_nt(a, b):
    return lax.dot_general(a, b, (((1,), (1,)), ((), ())), preferred_element_type=F32)


def _rms(x, g):
    return x * lax.rsqrt(jnp.mean(x * x, axis=-1, keepdims=True) + NORM_EPS) * g


def _split_bf16(x):
    hi = x.astype(BF16)
    return hi, (x - hi.astype(F32)).astype(BF16)


def _head_rms(x, g, ones_ref):
    w = x.shape[1]
    hi, lo = _split_bf16(x * x)
    parts = []
    for c in range(0, w, 256):
        cw = min(256, w - c)
        ones = ones_ref[0:cw, 0:cw]
        parts.append(_dot(hi[:, c:c + cw], ones) + _dot(lo[:, c:c + cw], ones))
    ssum = parts[0] if len(parts) == 1 else jnp.concatenate(parts, axis=1)
    return x * lax.rsqrt(ssum * (1.0 / HEAD_DIM) + NORM_EPS) * g


def _swap_pairs(x):
    w = x.shape[1]
    up = pltpu.roll(x, w - 1, axis=1)
    dn = pltpu.roll(x, 1, axis=1)
    lane = lax.broadcasted_iota(jnp.int32, x.shape, 1)
    return jnp.where((lane & 1) == 0, up, dn)


def _rope(x, cos, sin_signed, reps):
    if reps > 1:
        cos = jnp.concatenate([cos] * reps, axis=1)
        sin_signed = jnp.concatenate([sin_signed] * reps, axis=1)
    return x * cos + _swap_pairs(x) * sin_signed


def _store_vt(vt_ref, v, n_heads, cols=slice(None)):
    t = v.shape[0]
    vt = v.T.astype(BF16)
    ones = jnp.ones((VT_ROWS - 64, t), BF16)
    for hh in range(n_heads):
        vt_ref[hh, 0:64, cols] = vt[hh * 64:(hh + 1) * 64]
        vt_ref[hh, 64:VT_ROWS, cols] = ones


def _cond_row(i):
    return jnp.where(i < N_CTX_TILES, 0, 1 + (i - N_CTX_TILES) // LAT_TILES)


def _rope_block(i):
    return jnp.maximum(i - N_CTX_TILES, 0) % LAT_TILES


def _const_spec(shape):
    zeros = (0,) * len(shape)
    return pl.BlockSpec(shape, lambda *_: zeros)


def _layer_spec(shape, layer):
    idx = (layer,) + (0,) * len(shape)
    return pl.BlockSpec((None,) + tuple(shape), lambda *_: idx)


def _mod_kernel(cond_ref, w_ref, b_ref, o_ref):
    c = cond_ref[...]
    s_hi, s_lo = _split_bf16(c * jax.nn.sigmoid(c))
    w_hi, w_lo = _split_bf16(w_ref[0])
    r = _dot(jnp.concatenate([s_hi, s_lo], axis=0), w_hi)
    o_ref[0] = r[0:8] + r[8:16] + _dot(s_hi, w_lo) + b_ref[0]


def _modulation(cond8, w_mod, b_mod):
    depth, _, n = w_mod.shape
    tn = 1536
    return pl.pallas_call(
        _mod_kernel,
        grid=(depth, n // tn),
        in_specs=[
            pl.BlockSpec((8, D_MODEL), lambda l, j: (0, 0)),
            pl.BlockSpec((1, D_MODEL, tn), lambda l, j: (l, 0, j)),
            pl.BlockSpec((1, 1, tn), lambda l, j: (l, 0, j)),
        ],
        out_specs=pl.BlockSpec((1, 8, tn), lambda l, j: (l, 0, j)),
        out_shape=jax.ShapeDtypeStruct((depth, 8, n), F32),
        compiler_params=pltpu.CompilerParams(
            dimension_semantics=("arbitrary", "arbitrary"), vmem_limit_bytes=VMEM_LIMIT),
        name="modulation",
    )(cond8, w_mod, b_mod.reshape(depth, 1, n))


def _pre0_kernel(xp_ref, xs_ref, mod_ref, gmix_ref, win_ref, wkpe_ref, gq_ref, gk_ref, gcq_ref, gckv_ref,
                 wuq_ref, wukk_ref, wukv_ref, cosa_ref, sina_ref, cosb_ref, sinb_ref, ones_ref,
                 qa_ref, ka_ref, va_ref, qb_ref, kb_ref, vb_ref,
                 nak_ref, nav_ref, nckv_ref, nkpe_ref,
                 win_s, wuq_s, wukk_s, wukv_s):
    i = pl.program_id(0)

    @pl.when(i == 0)
    def _():
        win_s[:, 0:IN_E_MAIN] = win_ref[...].astype(BF16)
        win_s[:, IN_E_MAIN:] = wkpe_ref[...].astype(BF16)
        wuq_s[...] = wuq_ref[...].astype(BF16)
        wukk_s[...] = wukk_ref[...].astype(BF16)
        wukv_s[...] = wukv_ref[...].astype(BF16)

    sh1 = mod_ref[0:1, :]
    sc1 = mod_ref[1:2, :]
    subs = [slice(r, r + PRE_SUB) for r in range(0, TM, PRE_SUB)]

    projs = []
    for sl in subs:
        x = jnp.where(i < N_CTX_TILES, xp_ref[sl, :], xs_ref[sl, :])
        h = _rms(x, gmix_ref[...]) * (1.0 + sc1) + sh1
        projs.append(_dot(h.astype(BF16), win_s[...]))

    parts = []
    for sl, proj in zip(subs, projs):
        qa = _head_rms(proj[:, 0:512], gq_ref[...], ones_ref)
        ka = _head_rms(proj[:, 512:640], gk_ref[...], ones_ref)
        va = proj[:, 640:768]
        cq = _rms(proj[:, 768:1152], gcq_ref[...])
        ckv = _rms(proj[:, 1152:1408], gckv_ref[...])
        kpe = proj[:, 1408:1536]
        qb = _dot(cq.astype(BF16), wuq_s[...])
        ckv_b = ckv.astype(BF16)
        kbn = _dot(ckv_b, wukk_s[...])
        vb = _dot(ckv_b, wukv_s[...])
        _store_vt(va_ref, va, N_KV_A, sl)
        _store_vt(vb_ref, vb, N_HEADS_B, sl)
        parts.append((qa, ka, va, qb, ckv, kpe, kbn))

    def store_qk(sl, qa, ka, qb, kpe, kbn):
        qa_s = qa * (LOG2E * HEAD_DIM ** -0.5)
        qb_s = qb * (LOG2E * (QK_NOPE + QK_ROPE) ** -0.5)
        for hh in range(N_HEADS_A):
            qa_ref[hh, sl, :] = qa_s[:, hh * 64:(hh + 1) * 64].astype(BF16)
        for hh in range(N_KV_A):
            ka_ref[hh, sl, :] = ka[:, hh * 64:(hh + 1) * 64].astype(BF16)
        for hh in range(N_HEADS_B):
            qb_ref[hh, sl, :] = qb_s[:, hh * KB_PAD:(hh + 1) * KB_PAD].astype(BF16)
            kb_ref[hh, sl, :] = (kbn[:, hh * KB_PAD:(hh + 1) * KB_PAD] + kpe).astype(BF16)

    @pl.when(i < N_CTX_TILES)
    def _():
        for sl, (qa, ka, va, qb, ckv, kpe, kbn) in zip(subs, parts):
            store_qk(sl, qa, ka, qb, kpe, kbn)
            nak_ref[sl, :] = ka
            nav_ref[sl, :] = va
            nckv_ref[sl, :] = ckv
            nkpe_ref[sl, :] = kpe[:, QK_NOPE:QK_NOPE + QK_ROPE]

    @pl.when(i >= N_CTX_TILES)
    def _():
        for sl, (qa, ka, va, qb, ckv, kpe, kbn) in zip(subs, parts):
            cosa, sina = cosa_ref[sl, :], sina_ref[sl, :]
            cosb, sinb = cosb_ref[sl, :], sinb_ref[sl, :]
            store_qk(sl, _rope(qa, cosa, sina, 4), _rope(ka, cosa, sina, 1),
                     _rope(qb, cosb, sinb, 8), _rope(kpe, cosb, sinb, 1), kbn)


def _pre0(xp, xs, mod, gmix, w_in_e, wkpe, gq, gk, gcq, gckv, wuq, wukk, wukv, cosa, sina, cosb, sinb, ones):
    tile = lambda i: (i, 0)
    head_tile = lambda i: (0, i, 0)
    vt_tile = lambda i: (0, 0, i)
    ctx_tile = lambda i: (jnp.minimum(i, N_CTX_TILES - 1), 0)
    lat_tile = lambda i: (jnp.maximum(i - N_CTX_TILES, 0), 0)
    rope_tile = lambda i: (_rope_block(i), 0)
    in_specs = [
        pl.BlockSpec((TM, D_MODEL), ctx_tile),
        pl.BlockSpec((TM, D_MODEL), lat_tile),
        pl.BlockSpec((None, 6, D_MODEL), lambda i: (_cond_row(i), 0, 0)),
        _layer_spec((1, D_MODEL), 0),
        _layer_spec((D_MODEL, IN_E_MAIN), 0),
        _const_spec(wkpe.shape),
        _const_spec(gq.shape), _const_spec(gk.shape), _layer_spec((1, Q_LORA), 0), _layer_spec((1, KV_LORA), 0),
        _const_spec(wuq.shape), _const_spec(wukk.shape), _const_spec(wukv.shape),
        pl.BlockSpec((TM, 128), rope_tile), pl.BlockSpec((TM, 128), rope_tile),
        pl.BlockSpec((TM, 128), rope_tile), pl.BlockSpec((TM, 128), rope_tile),
        _const_spec(ones.shape),
    ]
    out_shape = [
        jax.ShapeDtypeStruct((N_HEADS_A, N_TOK, 64), BF16),
        jax.ShapeDtypeStruct((N_KV_A, N_TOK, 64), BF16),
        jax.ShapeDtypeStruct((N_KV_A, VT_ROWS, N_TOK), BF16),
        jax.ShapeDtypeStruct((N_HEADS_B, N_TOK, KB_PAD), BF16),
        jax.ShapeDtypeStruct((N_HEADS_B, N_TOK, KB_PAD), BF16),
        jax.ShapeDtypeStruct((N_HEADS_B, VT_ROWS, N_TOK), BF16),
        jax.ShapeDtypeStruct((N_CTX_TOK, 128), F32),
        jax.ShapeDtypeStruct((N_CTX_TOK, 128), F32),
        jax.ShapeDtypeStruct((N_CTX_TOK, KV_LORA), F32),
        jax.ShapeDtypeStruct((N_CTX_TOK, QK_ROPE), F32),
    ]
    out_specs = [
        pl.BlockSpec((N_HEADS_A, TM, 64), head_tile),
        pl.BlockSpec((N_KV_A, TM, 64), head_tile),
        pl.BlockSpec((N_KV_A, VT_ROWS, TM), vt_tile),
        pl.BlockSpec((N_HEADS_B, TM, KB_PAD), head_tile),
        pl.BlockSpec((N_HEADS_B, TM, KB_PAD), head_tile),
        pl.BlockSpec((N_HEADS_B, VT_ROWS, TM), vt_tile),
        pl.BlockSpec((TM, 128), ctx_tile),
        pl.BlockSpec((TM, 128), ctx_tile),
        pl.BlockSpec((TM, KV_LORA), ctx_tile),
        pl.BlockSpec((TM, QK_ROPE), ctx_tile),
    ]
    scratch = [
        pltpu.VMEM((D_MODEL, IN_E_MAIN + KB_PAD), BF16),
        pltpu.VMEM(wuq.shape, BF16), pltpu.VMEM(wukk.shape, BF16), pltpu.VMEM(wukv.shape, BF16),
    ]
    return pl.pallas_call(
        _pre0_kernel, grid=(N_TILES,), in_specs=in_specs, out_specs=out_specs, out_shape=out_shape,
        scratch_shapes=scratch,
        compiler_params=pltpu.CompilerParams(
            dimension_semantics=("arbitrary",), vmem_limit_bytes=VMEM_LIMIT),
        name="pre0",
    )(xp, xs, mod, gmix, w_in_e, wkpe, gq, gk, gcq, gckv, wuq, wukk, wukv, cosa, sina, cosb, sinb, ones)


def _pre1_kernel(xp_ref, xs_ref, mod_ref, gmix_ref, win_ref, cosa_ref, sina_ref,
                 q_ref, k_ref, v_ref, nk_ref, nv_ref, win_s):
    i = pl.program_id(0)

    @pl.when(i == 0)
    def _():
        win_s[...] = win_ref[...].astype(BF16)

    sh1 = mod_ref[0:1, :]
    sc1 = mod_ref[1:2, :]
    subs = [slice(r, r + PRE_SUB) for r in range(0, TM, PRE_SUB)]
    projs = []
    for sl in subs:
        x = jnp.where(i < N_CTX_TILES, xp_ref[sl, :], xs_ref[sl, :])
        h = _rms(x, gmix_ref[...]) * (1.0 + sc1) + sh1
        projs.append(_dot(h.astype(BF16), win_s[...]))
    for sl, proj in zip(subs, projs):
        _store_vt(v_ref, proj[:, 1152:1280], N_KV_C, sl)

    def store_qk(sl, q, k):
        q_s = q * (LOG2E * HEAD_DIM ** -0.5)
        for hh in range(N_HEADS_C):
            q_ref[hh, sl, :] = q_s[:, hh * 64:(hh + 1) * 64].astype(BF16)
        for hh in range(N_KV_C):
            k_ref[hh, sl, :] = k[:, hh * 64:(hh + 1) * 64].astype(BF16)

    @pl.when(i < N_CTX_TILES)
    def _():
        for sl, proj in zip(subs, projs):
            store_qk(sl, proj[:, 0:1024], proj[:, 1024:1152])
            nk_ref[sl, :] = proj[:, 1024:1152]
            nv_ref[sl, :] = proj[:, 1152:1280]

    @pl.when(i >= N_CTX_TILES)
    def _():
        for sl, proj in zip(subs, projs):
            cosa, sina = cosa_ref[sl, :], sina_ref[sl, :]
            store_qk(sl, _rope(proj[:, 0:1024], cosa, sina, 8), _rope(proj[:, 1024:1152], cosa, sina, 1))


def _pre1(xp, xs, mod, gmix, w_in_o, cosa, sina):
    n_in = w_in_o.shape[-1]
    head_tile = lambda i: (0, i, 0)
    vt_tile = lambda i: (0, 0, i)
    ctx_tile = lambda i: (jnp.minimum(i, N_CTX_TILES - 1), 0)
    lat_tile = lambda i: (jnp.maximum(i - N_CTX_TILES, 0), 0)
    rope_tile = lambda i: (_rope_block(i), 0)
    in_specs = [
        pl.BlockSpec((TM, D_MODEL), ctx_tile),
        pl.BlockSpec((TM, D_MODEL), lat_tile),
        pl.BlockSpec((None, 6, D_MODEL), lambda i: (8 + _cond_row(i), 0, 0)),
        _layer_spec((1, D_MODEL), 1),
        _layer_spec((D_MODEL, n_in), 0),
        pl.BlockSpec((TM, 128), rope_tile), pl.BlockSpec((TM, 128), rope_tile),
    ]
    out_shape = [
        jax.ShapeDtypeStruct((N_HEADS_C, N_TOK, 64), BF16),
        jax.ShapeDtypeStruct((N_KV_C, N_TOK, 64), BF16),
        jax.ShapeDtypeStruct((N_KV_C, VT_ROWS, N_TOK), BF16),
        jax.ShapeDtypeStruct((N_CTX_TOK, 128), F32),
        jax.ShapeDtypeStruct((N_CTX_TOK, 128), F32),
    ]
    out_specs = [
        pl.BlockSpec((N_HEADS_C, TM, 64), head_tile),
        pl.BlockSpec((N_KV_C, TM, 64), head_tile),
        pl.BlockSpec((N_KV_C, VT_ROWS, TM), vt_tile),
        pl.BlockSpec((TM, 128), ctx_tile),
        pl.BlockSpec((TM, 128), ctx_tile),
    ]
    return pl.pallas_call(
        _pre1_kernel, grid=(N_TILES,), in_specs=in_specs, out_specs=out_specs, out_shape=out_shape,
        scratch_shapes=[pltpu.VMEM((D_MODEL, n_in), BF16)],
        compiler_params=pltpu.CompilerParams(
            dimension_semantics=("arbitrary",), vmem_limit_bytes=VMEM_LIMIT),
        name="pre1",
    )(xp, xs, mod, gmix, w_in_o, cosa, sina)


def _ctx_kernel(ak_ref, av_ref, ckv_ref, kpe_ref, ck_ref, cv_ref, wukk_ref, wukv_ref,
                ka_ref, va_ref, kb_ref, vb_ref, kc_ref, vc_ref):
    ak, av, ck, cv = ak_ref[...], av_ref[...], ck_ref[...], cv_ref[...]
    for hh in range(2):
        sl = slice(hh * 64, (hh + 1) * 64)
        ka_ref[hh] = ak[:, sl].astype(BF16)
        kc_ref[hh] = ck[:, sl].astype(BF16)
    _store_vt(va_ref, av, 2)
    _store_vt(vc_ref, cv, 2)
    ckv_b = ckv_ref[...].astype(BF16)
    kbn = _dot(ckv_b, wukk_ref[...].astype(BF16))
    vb = _dot(ckv_b, wukv_ref[...].astype(BF16))
    kpe = kpe_ref[...]
    for hh in range(N_HEADS_B):
        kb_ref[hh] = (kbn[:, hh * KB_PAD:(hh + 1) * KB_PAD] + kpe).astype(BF16)
    _store_vt(vb_ref, vb, N_HEADS_B)


def _ctx_prep(ak, av, ckv, kpe_pad, ck, cv, wukk, wukv):
    n = N_LAT_SEQ * PAST_LEN
    row = lambda b: (b, 0)
    head_row = lambda b: (0, b, 0)
    vt_row = lambda b: (0, 0, b)
    in_specs = [
        pl.BlockSpec((PAST_LEN, 128), row), pl.BlockSpec((PAST_LEN, 128), row),
        pl.BlockSpec((PAST_LEN, KV_LORA), row), pl.BlockSpec((PAST_LEN, KB_PAD), row),
        pl.BlockSpec((PAST_LEN, 128), row), pl.BlockSpec((PAST_LEN, 128), row),
        _const_spec(wukk.shape), _const_spec(wukv.shape),
    ]
    out_shape = [
        jax.ShapeDtypeStruct((2, n, 64), BF16), jax.ShapeDtypeStruct((2, VT_ROWS, n), BF16),
        jax.ShapeDtypeStruct((N_HEADS_B, n, KB_PAD), BF16), jax.ShapeDtypeStruct((N_HEADS_B, VT_ROWS, n), BF16),
        jax.ShapeDtypeStruct((2, n, 64), BF16), jax.ShapeDtypeStruct((2, VT_ROWS, n), BF16),
    ]
    out_specs = [
        pl.BlockSpec((2, PAST_LEN, 64), head_row), pl.BlockSpec((2, VT_ROWS, PAST_LEN), vt_row),
        pl.BlockSpec((N_HEADS_B, PAST_LEN, KB_PAD), head_row), pl.BlockSpec((N_HEADS_B, VT_ROWS, PAST_LEN), vt_row),
        pl.BlockSpec((2, PAST_LEN, 64), head_row), pl.BlockSpec((2, VT_ROWS, PAST_LEN), vt_row),
    ]
    return pl.pallas_call(
        _ctx_kernel, grid=(N_LAT_SEQ,), in_specs=in_specs, out_specs=out_specs, out_shape=out_shape,
        compiler_params=pltpu.CompilerParams(
            dimension_semantics=("arbitrary",), vmem_limit_bytes=VMEM_LIMIT),
        name="ctx_prep",
    )(ak, av, ckv, kpe_pad, ck, cv, wukk, wukv)


def _softmax_units(units, lookahead):
    tasks = [(u, c) for u, unit in enumerate(units) for c in range(len(unit["chunks"]))]
    scores = {}

    def emit_scores(t):
        u, c = tasks[t]
        k, _, mask = units[u]["chunks"][c]
        s = _dot_nt(k, units[u]["q"])
        scores[t] = s if mask is None else jnp.where(mask, s, NEG_INF)

    for t in range(min(lookahead, len(tasks))):
        emit_scores(t)
    for t, (u, c) in enumerate(tasks):
        if t + lookahead < len(tasks):
            emit_scores(t + lookahead)
        unit = units[u]
        s = scores.pop(t)
        m, acc = unit["m"], unit["acc"]
        cmax = jnp.max(s, axis=0, keepdims=True)
        m_new = cmax if m is None else jnp.maximum(m, cmax)
        pv = _dot(unit["chunks"][c][1], jnp.exp2(s - m_new).astype(BF16))
        unit["acc"] = pv if acc is None else acc * jnp.exp2(m - m_new) + pv
        unit["m"] = m_new
    return [unit["acc"][0:64] * (1.0 / unit["acc"][64:65]) for unit in units]


def _attn_kernel(*refs, n_kv, group, tq, seq_len, has_ctx, has_sink, window, q_unit, key_chunk):
    refs = list(refs)
    q_ref = refs.pop(0)
    if window:
        kp_ref, kc_ref, kn_ref, vp_ref, vc_ref, vn_ref = refs[:6]
        refs = refs[6:]
    else:
        k_ref, vt_ref = refs[:2]
        refs = refs[2:]
    if has_ctx:
        kx_ref, vx_ref = refs[:2]
        refs = refs[2:]
    if has_sink:
        sink_ref = refs.pop(0)
    o_ref = refs.pop(0)

    j = pl.program_id(1)
    dk = q_ref.shape[-1]
    heads_per_unit = q_unit // tq
    lane = lax.broadcasted_iota(jnp.int32, (1, q_unit), 1)
    if window:
        n_band = tq + 2 * window
        krow = lax.broadcasted_iota(jnp.int32, (n_band, q_unit), 0)
        qcol = lax.broadcasted_iota(jnp.int32, (n_band, q_unit), 1) & (tq - 1)
        rel = (krow - window) - qcol
        band_mask = ((jnp.abs(rel) <= window)
                     & ((krow >= window) | (j > 0))
                     & ((krow < window + tq) | (j < seq_len // tq - 1)))
    if has_sink:
        acc0 = jnp.where(lax.broadcasted_iota(jnp.int32, (VT_ROWS, q_unit), 0) >= 64, 1.0, 0.0)

    units = []
    for hk in range(n_kv):
        chunks = []
        if window:
            chunks.append((jnp.concatenate([kp_ref[hk], kc_ref[hk], kn_ref[hk]], axis=0),
                           jnp.concatenate([vp_ref[hk], vc_ref[hk], vn_ref[hk]], axis=1), band_mask))
        else:
            for c in range(0, seq_len, key_chunk):
                n = min(key_chunk, seq_len - c)
                chunks.append((k_ref[hk, c:c + n, :], vt_ref[hk, :, c:c + n], None))
        if has_ctx:
            for c in range(0, PAST_LEN, key_chunk):
                n = min(key_chunk, PAST_LEN - c)
                chunks.append((kx_ref[hk, c:c + n, :], vx_ref[hk, :, c:c + n], None))
        for u in range(group // heads_per_unit):
            h0 = hk * group + u * heads_per_unit
            unit = dict(q=q_ref[h0:h0 + heads_per_unit].reshape(q_unit, dk), chunks=chunks, m=None, acc=None)
            if has_sink:
                m0 = jnp.full((1, q_unit), sink_ref[h0] * LOG2E, F32)
                for e in range(1, heads_per_unit):
                    m0 = jnp.where(lane >= e * tq, sink_ref[h0 + e] * LOG2E, m0)
                unit.update(m=m0, acc=acc0)
            units.append(unit)
    outs = []
    for o in _softmax_units(units, SCORE_LOOKAHEAD):
        for e in range(heads_per_unit):
            outs.append(o[:, e * tq:(e + 1) * tq])
    o_ref[...] = jnp.concatenate(outs, axis=0).T.astype(BF16)


def _attention(q, k, vt, kx, vx, sink, *, n_seq, seq_len, tok_base, tq, window, q_unit, key_chunk, name):
    n_q, _, dk = q.shape
    n_kv = k.shape[0]
    group = n_q // n_kv
    n_qt = seq_len // tq
    q_base = tok_base // tq
    s_base = tok_base // seq_len
    has_ctx = kx is not None
    has_sink = sink is not None
    q_blk = lambda b, j: q_base + b * n_qt + j
    in_specs = [pl.BlockSpec((n_q, tq, dk), lambda b, j: (0, q_blk(b, j), 0))]
    args = [q]
    if window:
        assert window == tq
        prev = lambda b, j: q_base + b * n_qt + jnp.maximum(j - 1, 0)
        nxt = lambda b, j: q_base + b * n_qt + jnp.minimum(j + 1, n_qt - 1)
        for blk in (prev, q_blk, nxt):
            in_specs.append(pl.BlockSpec((n_kv, tq, dk), lambda b, j, blk=blk: (0, blk(b, j), 0)))
        for blk in (prev, q_blk, nxt):
            in_specs.append(pl.BlockSpec((n_kv, VT_ROWS, tq), lambda b, j, blk=blk: (0, 0, blk(b, j))))
        args += [k, k, k, vt, vt, vt]
    else:
        in_specs += [
            pl.BlockSpec((n_kv, seq_len, dk), lambda b, j: (0, s_base + b, 0)),
            pl.BlockSpec((n_kv, VT_ROWS, seq_len), lambda b, j: (0, 0, s_base + b)),
        ]
        args += [k, vt]
    if has_ctx:
        in_specs += [
            pl.BlockSpec((n_kv, PAST_LEN, dk), lambda b, j: (0, b, 0)),
            pl.BlockSpec((n_kv, VT_ROWS, PAST_LEN), lambda b, j: (0, 0, b)),
        ]
        args += [kx, vx]
    if has_sink:
        in_specs.append(pl.BlockSpec(memory_space=pltpu.SMEM))
        args.append(sink)
    kern = functools.partial(_attn_kernel, n_kv=n_kv, group=group, tq=tq, seq_len=seq_len,
                             has_ctx=has_ctx, has_sink=has_sink, window=window,
                             q_unit=q_unit, key_chunk=key_chunk)
    return pl.pallas_call(
        kern, grid=(n_seq, n_qt), in_specs=in_specs,
        out_specs=pl.BlockSpec((tq, n_q * HEAD_DIM), lambda b, j: (b * n_qt + j, 0)),
        out_shape=jax.ShapeDtypeStruct((n_seq * seq_len, n_q * HEAD_DIM), BF16),
        compiler_params=pltpu.CompilerParams(
            dimension_semantics=("arbitrary", "arbitrary"), vmem_limit_bytes=VMEM_LIMIT),
        name=name,
    )(*args)


def _ffn_kernel(*refs, n_o, is_ctx, final):
    halo = 0 if is_ctx else HALO
    it = iter(refs)
    x_ref = next(it)
    xh_refs = None if is_ctx else (next(it), next(it))
    o_refs, oh_refs = [], []
    for _ in range(n_o):
        o_refs.append(next(it))
        if not is_ctx:
            oh_refs.append((next(it), next(it)))
    wo_refs = [next(it) for _ in range(n_o)]
    mod_ref, gffn_ref = next(it), next(it)
    wg_refs = [next(it) for _ in range(FF_PER_STEP)]
    wv_refs = [next(it) for _ in range(FF_PER_STEP)]
    cw_ref, cb_ref = next(it), next(it)
    wd_refs = [next(it) for _ in range(FF_PER_STEP)]
    gfin_ref, out_ref, h2e_ref, acc_ref = next(it), next(it), next(it), next(it)

    m = pl.program_id(0)
    c = pl.program_id(1)
    g1 = mod_ref[2:3, :]
    sh2 = mod_ref[3:4, :]
    sc2 = mod_ref[4:5, :]
    g2 = mod_ref[5:6, :]
    n_sub = FFN_TM // FFN_SUB
    sub_rows = FFN_SUB + 2 * halo

    @pl.when(c == 0)
    def _():
        wos = [w[...].astype(BF16) for w in wo_refs]

        def residual_and_norm(xv, ovs):
            attn = _dot(ovs[0], wos[0])
            for ov, wo in zip(ovs[1:], wos[1:]):
                attn = attn + _dot(ov, wo)
            x1 = xv + g1 * attn
            return x1, (_rms(x1, gffn_ref[...]) * (1.0 + sc2) + sh2).astype(BF16)

        for r in range(0, FFN_TM, FFN_SUB):
            x1, h2 = residual_and_norm(x_ref[r:r + FFN_SUB, :], [o[r:r + FFN_SUB, :] for o in o_refs])
            acc_ref[r:r + FFN_SUB, :] = x1
            h2e_ref[halo + r:halo + r + FFN_SUB, :] = h2
        if not is_ctx:
            _, h2h = residual_and_norm(
                jnp.concatenate([xh_refs[0][...], xh_refs[1][...]], axis=0),
                [jnp.concatenate([oh[0][...], oh[1][...]], axis=0) for oh in oh_refs])
            h2e_ref[0:HALO, :] = h2h[0:HALO]
            h2e_ref[HALO + FFN_TM:, :] = h2h[HALO:]

    row8 = lax.broadcasted_iota(jnp.int32, (8, FF_CHUNK), 0)
    if not is_ctx:
        has_prev = m % FFN_LAT_TILES != 0
        has_next = m % FFN_LAT_TILES != FFN_LAT_TILES - 1

    def ff_chunk(j):
        w_up = jnp.concatenate([wg_refs[j][...].astype(BF16), wv_refs[j][...].astype(BF16)], axis=1)
        w_dn = wd_refs[j][...].astype(BF16)
        cols = pl.ds(pl.multiple_of((c * FF_PER_STEP + j) * FF_CHUNK, FF_CHUNK), FF_CHUNK)
        cw = cw_ref[:, cols]
        cb = cb_ref[:, cols]
        ups = {}

        def emit_up(r):
            ups[r] = _dot(h2e_ref[r * FFN_SUB:r * FFN_SUB + sub_rows, :], w_up)

        for r in range(min(UP_LOOKAHEAD, n_sub)):
            emit_up(r)
        for r in range(n_sub):
            if r + UP_LOOKAHEAD < n_sub:
                emit_up(r + UP_LOOKAHEAD)
            up = ups.pop(r)
            ge = up[:, :FF_CHUNK]
            val = up[halo:halo + FFN_SUB, FF_CHUNK:]
            g_prev = pltpu.roll(ge, 1, axis=0)[halo:halo + FFN_SUB]
            g_next = pltpu.roll(ge, sub_rows - 1, axis=0)[halo:halo + FFN_SUB]
            prev_ok = False if is_ctx else (has_prev if r == 0 else True)
            next_ok = False if is_ctx else (has_next if r == n_sub - 1 else True)
            if prev_ok is not True:
                g_prev = jnp.concatenate(
                    [jnp.where(jnp.logical_or(row8 != 0, prev_ok), g_prev[0:8], 0.0), g_prev[8:]], axis=0)
            if next_ok is not True:
                g_next = jnp.concatenate(
                    [g_next[:-8], jnp.where(jnp.logical_or(row8 != 7, next_ok), g_next[-8:], 0.0)], axis=0)
            gate = g_prev * cw[0:1, :] + ge[halo:halo + FFN_SUB] * cw[1:2, :] + g_next * cw[2:3, :] + cb
            act = (gate * jax.nn.sigmoid(gate) * val).astype(BF16)
            acc_ref[r * FFN_SUB:(r + 1) * FFN_SUB, :] += g2 * _dot(act, w_dn)

    n_tail = N_FF_CHUNKS % FF_PER_STEP
    for j in range(FF_PER_STEP):
        if n_tail and j >= n_tail:
            pl.when(c < N_FF_STEPS - 1)(functools.partial(ff_chunk, j))
        else:
            ff_chunk(j)

    @pl.when(c == N_FF_STEPS - 1)
    def _():
        x2 = acc_ref[...]
        out_ref[...] = _rms(x2, gfin_ref[...]) if final else x2


def _ffn(x, os, mod, layer, g_ffn, w_out, w_up, conv_w, conv_b, w_down, g_final, *, is_ctx, final):
    n_rows = x.shape[0]
    nh = FFN_TM // HALO
    nblk = n_rows // HALO
    n_o = len(os)
    halo = 0 if is_ctx else HALO
    tile = lambda m, c: (m, 0)
    prev = lambda m, c: (jnp.maximum(m * nh - 1, 0), 0)
    nxt = lambda m, c: (jnp.minimum((m + 1) * nh, nblk - 1), 0)
    chunk = lambda c, j: jnp.minimum(c * FF_PER_STEP + j, N_FF_CHUNKS - 1)
    if is_ctx:
        cond = lambda m: layer * 8
    else:
        cond = lambda m: layer * 8 + 1 + m // FFN_LAT_TILES

    def with_halo(arr):
        w = arr.shape[1]
        specs = [pl.BlockSpec((FFN_TM, w), tile)]
        if not is_ctx:
            specs += [pl.BlockSpec((HALO, w), prev), pl.BlockSpec((HALO, w), nxt)]
        return specs, [arr] * len(specs)

    in_specs, args = with_halo(x)
    for o in os:
        specs, arrs = with_halo(o)
        in_specs += specs
        args += arrs
    w_rows = D_MODEL // n_o
    for t in range(n_o):
        in_specs.append(pl.BlockSpec((None, w_rows, D_MODEL), lambda m, c, t=t: (0, t, 0)))
        args.append(w_out)
    in_specs += [
        pl.BlockSpec((None, 6, D_MODEL), lambda m, c: (cond(m), 0, 0)),
        pl.BlockSpec((None, 1, D_MODEL), lambda m, c: (layer, 0, 0)),
    ]
    args += [mod, g_ffn]
    steps = range(FF_PER_STEP)
    in_specs += [pl.BlockSpec((None, D_MODEL, FF_CHUNK), lambda m, c, j=j: (layer, 0, chunk(c, j))) for j in steps]
    in_specs += [pl.BlockSpec((None, D_MODEL, FF_CHUNK), lambda m, c, j=j: (layer, 0, N_FF_CHUNKS + chunk(c, j)))
                 for j in steps]
    in_specs += [pl.BlockSpec((None, 3, D_FF), lambda m, c: (layer, 0, 0)),
                 pl.BlockSpec((None, 1, D_FF), lambda m, c: (layer, 0, 0))]
    in_specs += [pl.BlockSpec((None, FF_CHUNK, D_MODEL), lambda m, c, j=j: (layer, chunk(c, j), 0)) for j in steps]
    args += [w_up] * (2 * FF_PER_STEP) + [conv_w, conv_b] + [w_down] * FF_PER_STEP
    in_specs.append(pl.BlockSpec((1, D_MODEL), lambda m, c: (0, 0)))
    args.append(g_final)
    return pl.pallas_call(
        functools.partial(_ffn_kernel, n_o=n_o, is_ctx=is_ctx, final=final),
        grid=(n_rows // FFN_TM, N_FF_STEPS), in_specs=in_specs,
        out_specs=pl.BlockSpec((FFN_TM, D_MODEL), tile),
        out_shape=jax.ShapeDtypeStruct((n_rows, D_MODEL), F32),
        scratch_shapes=[pltpu.VMEM((FFN_TM + 2 * halo, D_MODEL), BF16), pltpu.VMEM((FFN_TM, D_MODEL), F32)],
        compiler_params=pltpu.CompilerParams(
            dimension_semantics=("arbitrary", "arbitrary"), vmem_limit_bytes=FFN_VMEM_LIMIT),
        name=("ffn_ctx" if is_ctx else "ffn_lat") + ("_final" if final else ""),
    )(*args)


def _rope_tables(rot_dim):
    f32 = np.float32
    t = np.arange(LAT_LEN)
    row = (t // GRID_W).astype(f32)
    col = (t % GRID_W).astype(f32)
    d_axis = rot_dim // 2
    freqs = (f32(ROPE_THETA) ** (-np.arange(0, d_axis, 2, dtype=f32) / f32(d_axis))).astype(f32)
    ang = np.concatenate([row[:, None] * freqs, col[:, None] * freqs], axis=-1)
    cos = np.repeat(np.cos(ang), 2, axis=-1).astype(f32)
    sin = (np.repeat(np.sin(ang), 2, axis=-1) * np.tile(np.array([-1.0, 1.0], f32), rot_dim // 2)).astype(f32)
    if rot_dim == HEAD_DIM:
        cos = np.tile(cos, (1, 2))
        sin = np.tile(sin, (1, 2))
    else:
        cos = np.concatenate([np.ones((LAT_LEN, QK_NOPE), f32), cos,
                              np.ones((LAT_LEN, 128 - QK_NOPE - rot_dim), f32)], axis=-1)
        sin = np.concatenate([np.zeros((LAT_LEN, QK_NOPE), f32), sin,
                              np.zeros((LAT_LEN, 128 - QK_NOPE - rot_dim), f32)], axis=-1)
    return jnp.asarray(cos), jnp.asarray(sin)


def kernel(x_prompt, x_sample, cache_a_k, cache_a_v, cache_b_ckv, cache_b_kpe, cache_c_k, cache_c_v, c, c_ctx, w_mod, b_mod, g_mix_norm, g_ffn_norm, w_in_e, g_qnorm_a, g_knorm_a, g_cq_b, w_uq_b, g_ckv_b, w_ukv_b, w_out_e, w_in_o, sink_c, w_out_o, w_up, conv_w, conv_b, w_down, g_final):
    depth = w_mod.shape[0]
    cond8 = jnp.concatenate([c_ctx[None, :], c, jnp.zeros((5, D_MODEL), F32)], axis=0)
    lane_pad = KB_PAD - QK_NOPE - QK_ROPE
    wkpe = jnp.pad(w_in_e[0][:, IN_E_MAIN:], ((0, 0), (QK_NOPE, lane_pad)))
    wuq = jnp.pad(w_uq_b[0].reshape(Q_LORA, N_HEADS_B, QK_NOPE + QK_ROPE),
                  ((0, 0), (0, 0), (0, lane_pad))).reshape(Q_LORA, N_HEADS_B * KB_PAD)
    wukv3 = w_ukv_b[0].reshape(KV_LORA, N_HEADS_B, QK_NOPE + V_DIM_B)
    wukk = jnp.pad(wukv3[:, :, :QK_NOPE], ((0, 0), (0, 0), (0, KB_PAD - QK_NOPE))
                   ).reshape(KV_LORA, N_HEADS_B * KB_PAD)
    wukv = wukv3[:, :, QK_NOPE:].reshape(KV_LORA, N_HEADS_B * V_DIM_B)
    gq = jnp.tile(g_qnorm_a[0], N_HEADS_A)[None, :]
    gk = jnp.tile(g_knorm_a[0], N_KV_A)[None, :]
    seg = np.arange(256) // HEAD_DIM
    ones = jnp.asarray(seg[:, None] == seg[None, :], dtype=BF16)
    cosa, sina = _rope_tables(HEAD_DIM)
    cosb, sinb = _rope_tables(QK_ROPE)
    g_mix3 = g_mix_norm.reshape(depth, 1, D_MODEL)
    g_ffn3 = g_ffn_norm.reshape(depth, 1, D_MODEL)
    conv_b3 = conv_b.reshape(depth, 1, D_FF)
    g_fin2 = g_final[None, :]

    mod = _modulation(cond8, w_mod, b_mod).reshape(depth * 8, 6, D_MODEL)

    n_past = N_LAT_SEQ * PAST_LEN
    kpe_pad = jnp.pad(cache_b_kpe.reshape(n_past, QK_ROPE), ((0, 0), (QK_NOPE, lane_pad)))
    ka_c, va_c, kb_c, vb_c, kc_c, vc_c = _ctx_prep(
        cache_a_k.reshape(n_past, 128), cache_a_v.reshape(n_past, 128),
        cache_b_ckv.reshape(n_past, KV_LORA), kpe_pad,
        cache_c_k.reshape(n_past, 128), cache_c_v.reshape(n_past, 128), wukk, wukv)

    xp = x_prompt.reshape(N_CTX_TOK, D_MODEL)
    xs = x_sample.reshape(N_TOK - N_CTX_TOK, D_MODEL)
    qa, ka, va, qb, kb, vb, nak, nav, nckv, nkpe = _pre0(
        xp, xs, mod, g_mix3, w_in_e, wkpe, gq, gk, g_cq_b.reshape(1, 1, Q_LORA), g_ckv_b.reshape(1, 1, KV_LORA),
        wuq, wukk, wukv, cosa, sina, cosb, sinb, ones)
    ffn_w = (w_up, conv_w, conv_b3, w_down, g_fin2)
    ctx_kw = dict(n_seq=N_CTX_SEQ, seq_len=CTX_LEN, tok_base=0, tq=CTX_LEN, window=0, q_unit=256,
                  key_chunk=CTX_LEN)
    lat_kw = dict(n_seq=N_LAT_SEQ, seq_len=LAT_LEN, tok_base=N_CTX_TOK, q_unit=256)
    dense_kw = dict(tq=256, window=0, key_chunk=256, **lat_kw)
    oa = (_attention(qa, ka, va, None, None, None, name="attn_a_ctx", **ctx_kw),
          _attention(qa, ka, va, ka_c, va_c, None, name="attn_a_lat", **dense_kw))
    ob = (_attention(qb, kb, vb, None, None, None, name="attn_b_ctx", **ctx_kw),
          _attention(qb, kb, vb, kb_c, vb_c, None, name="attn_b_lat", **dense_kw))
    xp1 = _ffn(xp, [oa[0], ob[0]], mod, 0, g_ffn3, w_out_e, *ffn_w, is_ctx=True, final=False)
    xs1 = _ffn(xs, [oa[1], ob[1]], mod, 0, g_ffn3, w_out_e, *ffn_w, is_ctx=False, final=False)

    qc, kc, vc, nck, ncv = _pre1(xp1, xs1, mod, g_mix3, w_in_o, cosa, sina)
    sink = sink_c[0]
    oc = (_attention(qc, kc, vc, None, None, sink, name="attn_c_ctx", **ctx_kw),
          _attention(qc, kc, vc, kc_c, vc_c, sink, tq=128, window=WINDOW, key_chunk=PAST_LEN,
                     name="attn_c_lat", **lat_kw))
    y_prompt = _ffn(xp1, [oc[0]], mod, 1, g_ffn3, w_out_o, *ffn_w, is_ctx=True, final=True)
    y_sample = _ffn(xs1, [oc[1]], mod, 1, g_ffn3, w_out_o, *ffn_w, is_ctx=False, final=True)

    return (y_prompt.reshape(N_CTX_SEQ, CTX_LEN, D_MODEL), y_sample.reshape(N_LAT_SEQ, LAT_LEN, D_MODEL),
            nak.reshape(N_CTX_SEQ, 1, CTX_LEN, N_KV_A, HEAD_DIM),
            nav.reshape(N_CTX_SEQ, 1, CTX_LEN, N_KV_A, HEAD_DIM),
            nckv.reshape(N_CTX_SEQ, 1, CTX_LEN, KV_LORA),
            nkpe.reshape(N_CTX_SEQ, 1, CTX_LEN, QK_ROPE),
            nck.reshape(N_CTX_SEQ, 1, CTX_LEN, N_KV_C, HEAD_DIM),
            ncv.reshape(N_CTX_SEQ, 1, CTX_LEN, N_KV_C, HEAD_DIM))
```

```python
import functools

import jax
import jax.numpy as jnp
import numpy as np
from jax import lax
from jax.experimental import pallas as pl
from jax.experimental.pallas import tpu as pltpu

F32 = jnp.float32
BF16 = jnp.bfloat16

D_MODEL = 1024
N_CTX_SEQ = 16
CTX_LEN = 256
N_LAT_SEQ = 2
LAT_LEN = 2048
PAST_LEN = 512
GRID_W = 64
ROPE_THETA = 10000.0
NORM_EPS = 1e-6
WINDOW = 128
NEG_INF = -1e30
LOG2E = 1.4426950408889634
HEAD_DIM = 64
N_HEADS_A, N_KV_A = 8, 2
N_HEADS_B = 8
Q_LORA, KV_LORA = 384, 256
QK_NOPE, QK_ROPE, V_DIM_B = 64, 32, 64
N_HEADS_C, N_KV_C = 16, 2
D_FF = 2816
IN_E_MAIN = N_HEADS_A * HEAD_DIM + 2 * N_KV_A * HEAD_DIM + Q_LORA + KV_LORA

N_CTX_TOK = N_CTX_SEQ * CTX_LEN
N_TOK = N_CTX_TOK + N_LAT_SEQ * LAT_LEN
TM = 512
PRE_SUB = 256
N_TILES = N_TOK // TM
N_CTX_TILES = N_CTX_TOK // TM
LAT_TILES = LAT_LEN // TM
HALO = 16
FFN_TM = 1024
FFN_SUB = CTX_LEN
UP_LOOKAHEAD = 3
FFN_LAT_TILES = LAT_LEN // FFN_TM
FF_CHUNK = 256
N_FF_CHUNKS = D_FF // FF_CHUNK
FF_PER_STEP = 3
N_FF_STEPS = -(-N_FF_CHUNKS // FF_PER_STEP)
KB_PAD = 128
VT_ROWS = 80
SCORE_LOOKAHEAD = 5
VMEM_LIMIT = 56 * 1024 * 1024
FFN_VMEM_LIMIT = 60 * 1024 * 1024


def _dot(a, b):
    return jnp.dot(a, b, preferred_element_type=F32)


def _dot_nt(a, b):
    return lax.dot_general(a, b, (((1,), (1,)), ((), ())), preferred_element_type=F32)


def _rms(x, g):
    return x * lax.rsqrt(jnp.mean(x * x, axis=-1, keepdims=True) + NORM_EPS) * g


def _split_bf16(x):
    hi = x.astype(BF16)
    return hi, (x - hi.astype(F32)).astype(BF16)


def _head_rms(x, g, ones_ref):
    w = x.shape[1]
    hi, lo = _split_bf16(x * x)
    parts = []
    for c in range(0, w, 256):
        cw = min(256, w - c)
        ones = ones_ref[0:cw, 0:cw]
        parts.append(_dot(hi[:, c:c + cw], ones) + _dot(lo[:, c:c + cw], ones))
    ssum = parts[0] if len(parts) == 1 else jnp.concatenate(parts, axis=1)
    return x * lax.rsqrt(ssum * (1.0 / HEAD_DIM) + NORM_EPS) * g


def _swap_pairs(x):
    w = x.shape[1]
    up = pltpu.roll(x, w - 1, axis=1)
    dn = pltpu.roll(x, 1, axis=1)
    lane = lax.broadcasted_iota(jnp.int32, x.shape, 1)
    return jnp.where((lane & 1) == 0, up, dn)


def _rope(x, cos, sin_signed, reps):
    if reps > 1:
        cos = jnp.concatenate([cos] * reps, axis=1)
        sin_signed = jnp.concatenate([sin_signed] * reps, axis=1)
    return x * cos + _swap_pairs(x) * sin_signed


def _store_vt(vt_ref, v, n_heads, cols=slice(None)):
    t = v.shape[0]
    vt = v.T.astype(BF16)
    ones = jnp.ones((VT_ROWS - 64, t), BF16)
    for hh in range(n_heads):
        vt_ref[hh, 0:64, cols] = vt[hh * 64:(hh + 1) * 64]
        vt_ref[hh, 64:VT_ROWS, cols] = ones


def _cond_row(i):
    return jnp.where(i < N_CTX_TILES, 0, 1 + (i - N_CTX_TILES) // LAT_TILES)


def _rope_block(i):
    return jnp.where(i < N_CTX_TILES, LAT_TILES, (i - N_CTX_TILES) % LAT_TILES)


def _const_spec(shape):
    zeros = (0,) * len(shape)
    return pl.BlockSpec(shape, lambda *_: zeros)


def _layer_spec(shape, layer):
    idx = (layer,) + (0,) * len(shape)
    return pl.BlockSpec((None,) + tuple(shape), lambda *_: idx)


def _mod_kernel(cond_ref, w_ref, b_ref, o_ref):
    c = cond_ref[...]
    s_hi, s_lo = _split_bf16(c * jax.nn.sigmoid(c))
    w_hi, w_lo = _split_bf16(w_ref[0])
    r = _dot(jnp.concatenate([s_hi, s_lo], axis=0), w_hi)
    o_ref[0] = r[0:8] + r[8:16] + _dot(s_hi, w_lo) + b_ref[0]


def _modulation(cond8, w_mod, b_mod):
    depth, _, n = w_mod.shape
    tn = 1536
    return pl.pallas_call(
        _mod_kernel,
        grid=(depth, n // tn),
        in_specs=[
            pl.BlockSpec((8, D_MODEL), lambda l, j: (0, 0)),
            pl.BlockSpec((1, D_MODEL, tn), lambda l, j: (l, 0, j)),
            pl.BlockSpec((1, 1, tn), lambda l, j: (l, 0, j)),
        ],
        out_specs=pl.BlockSpec((1, 8, tn), lambda l, j: (l, 0, j)),
        out_shape=jax.ShapeDtypeStruct((depth, 8, n), F32),
        compiler_params=pltpu.CompilerParams(
            dimension_semantics=("arbitrary", "arbitrary"), vmem_limit_bytes=VMEM_LIMIT),
        name="modulation",
    )(cond8, w_mod, b_mod.reshape(depth, 1, n))


def _pre0_kernel(xp_ref, xs_ref, mod_ref, gmix_ref, win_ref, wkpe_ref, gq_ref, gk_ref, gcq_ref, gckv_ref,
                 wuq_ref, wukk_ref, wukv_ref, cosa_ref, sina_ref, cosb_ref, sinb_ref, ones_ref,
                 qa_ref, ka_ref, va_ref, qb_ref, kb_ref, vb_ref,
                 nak_ref, nav_ref, nckv_ref, nkpe_ref,
                 win_s, wuq_s, wukk_s, wukv_s):
    i = pl.program_id(0)

    @pl.when(i == 0)
    def _():
        win_s[:, 0:IN_E_MAIN] = win_ref[...].astype(BF16)
        win_s[:, IN_E_MAIN:] = wkpe_ref[...].astype(BF16)
        wuq_s[...] = wuq_ref[...].astype(BF16)
        wukk_s[...] = wukk_ref[...].astype(BF16)
        wukv_s[...] = wukv_ref[...].astype(BF16)

    sh1 = mod_ref[0:1, :]
    sc1 = mod_ref[1:2, :]
    subs = [slice(r, r + PRE_SUB) for r in range(0, TM, PRE_SUB)]

    projs = []
    for sl in subs:
        x = jnp.where(i < N_CTX_TILES, xp_ref[sl, :], xs_ref[sl, :])
        h = _rms(x, gmix_ref[...]) * (1.0 + sc1) + sh1
        projs.append(_dot(h.astype(BF16), win_s[...]))

    parts = []
    for sl, proj in zip(subs, projs):
        qa = _head_rms(proj[:, 0:512], gq_ref[...], ones_ref)
        ka = _head_rms(proj[:, 512:640], gk_ref[...], ones_ref)
        va = proj[:, 640:768]
        cq = _rms(proj[:, 768:1152], gcq_ref[...])
        ckv = _rms(proj[:, 1152:1408], gckv_ref[...])
        kpe = proj[:, 1408:1536]
        qb = _dot(cq.astype(BF16), wuq_s[...])
        ckv_b = ckv.astype(BF16)
        kbn = _dot(ckv_b, wukk_s[...])
        vb = _dot(ckv_b, wukv_s[...])
        _store_vt(va_ref, va, N_KV_A, sl)
        _store_vt(vb_ref, vb, N_HEADS_B, sl)
        parts.append((ka, va, ckv, kpe))
        cosa, sina = cosa_ref[sl, :], sina_ref[sl, :]
        cosb, sinb = cosb_ref[sl, :], sinb_ref[sl, :]
        qa_s = _rope(qa, cosa, sina, 4) * (LOG2E * HEAD_DIM ** -0.5)
        ka_r = _rope(ka, cosa, sina, 1)
        qb_s = _rope(qb, cosb, sinb, 8) * (LOG2E * (QK_NOPE + QK_ROPE) ** -0.5)
        kpe_r = _rope(kpe, cosb, sinb, 1)
        for hh in range(N_HEADS_A):
            qa_ref[hh, sl, :] = qa_s[:, hh * 64:(hh + 1) * 64].astype(BF16)
        for hh in range(N_KV_A):
            ka_ref[hh, sl, :] = ka_r[:, hh * 64:(hh + 1) * 64].astype(BF16)
        for hh in range(N_HEADS_B):
            qb_ref[hh, sl, :] = qb_s[:, hh * KB_PAD:(hh + 1) * KB_PAD].astype(BF16)
            kb_ref[hh, sl, :] = (kbn[:, hh * KB_PAD:(hh + 1) * KB_PAD] + kpe_r).astype(BF16)

    @pl.when(i < N_CTX_TILES)
    def _():
        for sl, (ka, va, ckv, kpe) in zip(subs, parts):
            nak_ref[sl, :] = ka
            nav_ref[sl, :] = va
            nckv_ref[sl, :] = ckv
            nkpe_ref[sl, :] = kpe[:, QK_NOPE:QK_NOPE + QK_ROPE]


def _pre0(xp, xs, mod, gmix, w_in_e, wkpe, gq, gk, gcq, gckv, wuq, wukk, wukv, cosa, sina, cosb, sinb, ones):
    tile = lambda i: (i, 0)
    head_tile = lambda i: (0, i, 0)
    vt_tile = lambda i: (0, 0, i)
    ctx_tile = lambda i: (jnp.minimum(i, N_CTX_TILES - 1), 0)
    lat_tile = lambda i: (jnp.maximum(i - N_CTX_TILES, 0), 0)
    rope_tile = lambda i: (_rope_block(i), 0)
    in_specs = [
        pl.BlockSpec((TM, D_MODEL), ctx_tile),
        pl.BlockSpec((TM, D_MODEL), lat_tile),
        pl.BlockSpec((None, 6, D_MODEL), lambda i: (_cond_row(i), 0, 0)),
        _layer_spec((1, D_MODEL), 0),
        _layer_spec((D_MODEL, IN_E_MAIN), 0),
        _const_spec(wkpe.shape),
        _const_spec(gq.shape), _const_spec(gk.shape), _layer_spec((1, Q_LORA), 0), _layer_spec((1, KV_LORA), 0),
        _const_spec(wuq.shape), _const_spec(wukk.shape), _const_spec(wukv.shape),
        pl.BlockSpec((TM, 128), rope_tile), pl.BlockSpec((TM, 128), rope_tile),
        pl.BlockSpec((TM, 128), rope_tile), pl.BlockSpec((TM, 128), rope_tile),
        _const_spec(ones.shape),
    ]
    out_shape = [
        jax.ShapeDtypeStruct((N_HEADS_A, N_TOK, 64), BF16),
        jax.ShapeDtypeStruct((N_KV_A, N_TOK, 64), BF16),
        jax.ShapeDtypeStruct((N_KV_A, VT_ROWS, N_TOK), BF16),
        jax.ShapeDtypeStruct((N_HEADS_B, N_TOK, KB_PAD), BF16),
        jax.ShapeDtypeStruct((N_HEADS_B, N_TOK, KB_PAD), BF16),
        jax.ShapeDtypeStruct((N_HEADS_B, VT_ROWS, N_TOK), BF16),
        jax.ShapeDtypeStruct((N_CTX_TOK, 128), F32),
        jax.ShapeDtypeStruct((N_CTX_TOK, 128), F32),
        jax.ShapeDtypeStruct((N_CTX_TOK, KV_LORA), F32),
        jax.ShapeDtypeStruct((N_CTX_TOK, QK_ROPE), F32),
    ]
    out_specs = [
        pl.BlockSpec((N_HEADS_A, TM, 64), head_tile),
        pl.BlockSpec((N_KV_A, TM, 64), head_tile),
        pl.BlockSpec((N_KV_A, VT_ROWS, TM), vt_tile),
        pl.BlockSpec((N_HEADS_B, TM, KB_PAD), head_tile),
        pl.BlockSpec((N_HEADS_B, TM, KB_PAD), head_tile),
        pl.BlockSpec((N_HEADS_B, VT_ROWS, TM), vt_tile),
        pl.BlockSpec((TM, 128), ctx_tile),
        pl.BlockSpec((TM, 128), ctx_tile),
        pl.BlockSpec((TM, KV_LORA), ctx_tile),
        pl.BlockSpec((TM, QK_ROPE), ctx_tile),
    ]
    scratch = [
        pltpu.VMEM((D_MODEL, IN_E_MAIN + KB_PAD), BF16),
        pltpu.VMEM(wuq.shape, BF16), pltpu.VMEM(wukk.shape, BF16), pltpu.VMEM(wukv.shape, BF16),
    ]
    return pl.pallas_call(
        _pre0_kernel, grid=(N_TILES,), in_specs=in_specs, out_specs=out_specs, out_shape=out_shape,
        scratch_shapes=scratch,
        compiler_params=pltpu.CompilerParams(
            dimension_semantics=("arbitrary",), vmem_limit_bytes=VMEM_LIMIT),
        name="pre0",
    )(xp, xs, mod, gmix, w_in_e, wkpe, gq, gk, gcq, gckv, wuq, wukk, wukv, cosa, sina, cosb, sinb, ones)


def _pre1_kernel(xp_ref, xs_ref, mod_ref, gmix_ref, win_ref, cosa_ref, sina_ref,
                 q_ref, k_ref, v_ref, nk_ref, nv_ref, win_s):
    i = pl.program_id(0)

    @pl.when(i == 0)
    def _():
        win_s[...] = win_ref[...].astype(BF16)

    sh1 = mod_ref[0:1, :]
    sc1 = mod_ref[1:2, :]
    subs = [slice(r, r + PRE_SUB) for r in range(0, TM, PRE_SUB)]
    projs = []
    for sl in subs:
        x = jnp.where(i < N_CTX_TILES, xp_ref[sl, :], xs_ref[sl, :])
        h = _rms(x, gmix_ref[...]) * (1.0 + sc1) + sh1
        projs.append(_dot(h.astype(BF16), win_s[...]))
    for sl, proj in zip(subs, projs):
        cosa, sina = cosa_ref[sl, :], sina_ref[sl, :]
        q_s = _rope(proj[:, 0:1024], cosa, sina, 8) * (LOG2E * HEAD_DIM ** -0.5)
        k = _rope(proj[:, 1024:1152], cosa, sina, 1)
        for hh in range(N_HEADS_C):
            q_ref[hh, sl, :] = q_s[:, hh * 64:(hh + 1) * 64].astype(BF16)
        for hh in range(N_KV_C):
            k_ref[hh, sl, :] = k[:, hh * 64:(hh + 1) * 64].astype(BF16)
        _store_vt(v_ref, proj[:, 1152:1280], N_KV_C, sl)

    @pl.when(i < N_CTX_TILES)
    def _():
        for sl, proj in zip(subs, projs):
            nk_ref[sl, :] = proj[:, 1024:1152]
            nv_ref[sl, :] = proj[:, 1152:1280]


def _pre1(xp, xs, mod, gmix, w_in_o, cosa, sina):
    n_in = w_in_o.shape[-1]
    head_tile = lambda i: (0, i, 0)
    vt_tile = lambda i: (0, 0, i)
    ctx_tile = lambda i: (jnp.minimum(i, N_CTX_TILES - 1), 0)
    lat_tile = lambda i: (jnp.maximum(i - N_CTX_TILES, 0), 0)
    rope_tile = lambda i: (_rope_block(i), 0)
    in_specs = [
        pl.BlockSpec((TM, D_MODEL), ctx_tile),
        pl.BlockSpec((TM, D_MODEL), lat_tile),
        pl.BlockSpec((None, 6, D_MODEL), lambda i: (8 + _cond_row(i), 0, 0)),
        _layer_spec((1, D_MODEL), 1),
        _layer_spec((D_MODEL, n_in), 0),
        pl.BlockSpec((TM, 128), rope_tile), pl.BlockSpec((TM, 128), rope_tile),
    ]
    out_shape = [
        jax.ShapeDtypeStruct((N_HEADS_C, N_TOK, 64), BF16),
        jax.ShapeDtypeStruct((N_KV_C, N_TOK, 64), BF16),
        jax.ShapeDtypeStruct((N_KV_C, VT_ROWS, N_TOK), BF16),
        jax.ShapeDtypeStruct((N_CTX_TOK, 128), F32),
        jax.ShapeDtypeStruct((N_CTX_TOK, 128), F32),
    ]
    out_specs = [
        pl.BlockSpec((N_HEADS_C, TM, 64), head_tile),
        pl.BlockSpec((N_KV_C, TM, 64), head_tile),
        pl.BlockSpec((N_KV_C, VT_ROWS, TM), vt_tile),
        pl.BlockSpec((TM, 128), ctx_tile),
        pl.BlockSpec((TM, 128), ctx_tile),
    ]
    return pl.pallas_call(
        _pre1_kernel, grid=(N_TILES,), in_specs=in_specs, out_specs=out_specs, out_shape=out_shape,
        scratch_shapes=[pltpu.VMEM((D_MODEL, n_in), BF16)],
        compiler_params=pltpu.CompilerParams(
            dimension_semantics=("arbitrary",), vmem_limit_bytes=VMEM_LIMIT),
        name="pre1",
    )(xp, xs, mod, gmix, w_in_o, cosa, sina)


def _ctx_kernel(ak_ref, av_ref, ckv_ref, kpe_ref, ck_ref, cv_ref, wukk_ref, wukv_ref,
                ka_ref, va_ref, kb_ref, vb_ref, kc_ref, vc_ref):
    ak, av, ck, cv = ak_ref[...], av_ref[...], ck_ref[...], cv_ref[...]
    for hh in range(2):
        sl = slice(hh * 64, (hh + 1) * 64)
        ka_ref[hh] = ak[:, sl].astype(BF16)
        kc_ref[hh] = ck[:, sl].astype(BF16)
    _store_vt(va_ref, av, 2)
    _store_vt(vc_ref, cv, 2)
    ckv_b = ckv_ref[...].astype(BF16)
    kbn = _dot(ckv_b, wukk_ref[...].astype(BF16))
    vb = _dot(ckv_b, wukv_ref[...].astype(BF16))
    kpe = kpe_ref[...]
    for hh in range(N_HEADS_B):
        kb_ref[hh] = (kbn[:, hh * KB_PAD:(hh + 1) * KB_PAD] + kpe).astype(BF16)
    _store_vt(vb_ref, vb, N_HEADS_B)


def _ctx_prep(ak, av, ckv, kpe_pad, ck, cv, wukk, wukv):
    n = N_LAT_SEQ * PAST_LEN
    row = lambda b: (b, 0)
    head_row = lambda b: (0, b, 0)
    vt_row = lambda b: (0, 0, b)
    in_specs = [
        pl.BlockSpec((PAST_LEN, 128), row), pl.BlockSpec((PAST_LEN, 128), row),
        pl.BlockSpec((PAST_LEN, KV_LORA), row), pl.BlockSpec((PAST_LEN, KB_PAD), row),
        pl.BlockSpec((PAST_LEN, 128), row), pl.BlockSpec((PAST_LEN, 128), row),
        _const_spec(wukk.shape), _const_spec(wukv.shape),
    ]
    out_shape = [
        jax.ShapeDtypeStruct((2, n, 64), BF16), jax.ShapeDtypeStruct((2, VT_ROWS, n), BF16),
        jax.ShapeDtypeStruct((N_HEADS_B, n, KB_PAD), BF16), jax.ShapeDtypeStruct((N_HEADS_B, VT_ROWS, n), BF16),
        jax.ShapeDtypeStruct((2, n, 64), BF16), jax.ShapeDtypeStruct((2, VT_ROWS, n), BF16),
    ]
    out_specs = [
        pl.BlockSpec((2, PAST_LEN, 64), head_row), pl.BlockSpec((2, VT_ROWS, PAST_LEN), vt_row),
        pl.BlockSpec((N_HEADS_B, PAST_LEN, KB_PAD), head_row), pl.BlockSpec((N_HEADS_B, VT_ROWS, PAST_LEN), vt_row),
        pl.BlockSpec((2, PAST_LEN, 64), head_row), pl.BlockSpec((2, VT_ROWS, PAST_LEN), vt_row),
    ]
    return pl.pallas_call(
        _ctx_kernel, grid=(N_LAT_SEQ,), in_specs=in_specs, out_specs=out_specs, out_shape=out_shape,
        compiler_params=pltpu.CompilerParams(
            dimension_semantics=("arbitrary",), vmem_limit_bytes=VMEM_LIMIT),
        name="ctx_prep",
    )(ak, av, ckv, kpe_pad, ck, cv, wukk, wukv)


def _softmax_units(units, lookahead):
    tasks = [(u, c) for u, unit in enumerate(units) for c in range(len(unit["chunks"]))]
    scores = {}

    def emit_scores(t):
        u, c = tasks[t]
        k, _, mask = units[u]["chunks"][c]
        s = _dot_nt(k, units[u]["q"])
        scores[t] = s if mask is None else jnp.where(mask, s, NEG_INF)

    for t in range(min(lookahead, len(tasks))):
        emit_scores(t)
    for t, (u, c) in enumerate(tasks):
        if t + lookahead < len(tasks):
            emit_scores(t + lookahead)
        unit = units[u]
        s = scores.pop(t)
        m, acc = unit["m"], unit["acc"]
        cmax = jnp.max(s, axis=0, keepdims=True)
        m_new = cmax if m is None else jnp.maximum(m, cmax)
        pv = _dot(unit["chunks"][c][1], jnp.exp2(s - m_new).astype(BF16))
        unit["acc"] = pv if acc is None else acc * jnp.exp2(m - m_new) + pv
        unit["m"] = m_new
    return [unit["acc"][0:64] * (1.0 / unit["acc"][64:65]) for unit in units]


def _attn_kernel(*refs, n_kv, group, tq, seq_len, seqs, has_ctx, has_sink, window, q_unit, key_chunk):
    refs = list(refs)
    q_ref = refs.pop(0)
    if window:
        kp_ref, kc_ref, kn_ref, vp_ref, vc_ref, vn_ref = refs[:6]
        refs = refs[6:]
    else:
        k_ref, vt_ref = refs[:2]
        refs = refs[2:]
    if has_ctx:
        kx_ref, vx_ref = refs[:2]
        refs = refs[2:]
    if has_sink:
        sink_ref = refs.pop(0)
    o_ref = refs.pop(0)

    j = pl.program_id(1)
    dk = q_ref.shape[-1]
    heads_per_unit = q_unit // tq
    lane = lax.broadcasted_iota(jnp.int32, (1, q_unit), 1)
    if window:
        n_band = tq + 2 * window
        krow = lax.broadcasted_iota(jnp.int32, (n_band, q_unit), 0)
        qcol = lax.broadcasted_iota(jnp.int32, (n_band, q_unit), 1) & (tq - 1)
        rel = (krow - window) - qcol
        band_mask = ((jnp.abs(rel) <= window)
                     & ((krow >= window) | (j > 0))
                     & ((krow < window + tq) | (j < seq_len // tq - 1)))
    if has_sink:
        acc0 = jnp.where(lax.broadcasted_iota(jnp.int32, (VT_ROWS, q_unit), 0) >= 64, 1.0, 0.0)

    units = []
    for sq in range(seqs):
        base = sq * seq_len
        for hk in range(n_kv):
            chunks = []
            if window:
                chunks.append((jnp.concatenate([kp_ref[hk], kc_ref[hk], kn_ref[hk]], axis=0),
                               jnp.concatenate([vp_ref[hk], vc_ref[hk], vn_ref[hk]], axis=1), band_mask))
            else:
                for c in range(base, base + seq_len, key_chunk):
                    n = min(key_chunk, base + seq_len - c)
                    chunks.append((k_ref[hk, c:c + n, :], vt_ref[hk, :, c:c + n], None))
            if has_ctx:
                for c in range(0, PAST_LEN, key_chunk):
                    n = min(key_chunk, PAST_LEN - c)
                    chunks.append((kx_ref[hk, c:c + n, :], vx_ref[hk, :, c:c + n], None))
            for u in range(group // heads_per_unit):
                h0 = hk * group + u * heads_per_unit
                q = q_ref[h0:h0 + heads_per_unit, base:base + tq, :].reshape(q_unit, dk)
                unit = dict(q=q, chunks=chunks, m=None, acc=None)
                if has_sink:
                    m0 = jnp.full((1, q_unit), sink_ref[h0] * LOG2E, F32)
                    for e in range(1, heads_per_unit):
                        m0 = jnp.where(lane >= e * tq, sink_ref[h0 + e] * LOG2E, m0)
                    unit.update(m=m0, acc=acc0)
                units.append(unit)
    results = _softmax_units(units, SCORE_LOOKAHEAD)
    per_seq = len(results) // seqs
    for sq in range(seqs):
        outs = []
        for o in results[sq * per_seq:(sq + 1) * per_seq]:
            for e in range(heads_per_unit):
                outs.append(o[:, e * tq:(e + 1) * tq])
        o_ref[sq * seq_len:sq * seq_len + tq, :] = jnp.concatenate(outs, axis=0).T.astype(BF16)


def _attention(q, k, vt, kx, vx, sink, *, n_seq, seq_len, tok_base, tq, window, q_unit, key_chunk, name,
               seqs=1):
    n_q, _, dk = q.shape
    n_kv = k.shape[0]
    group = n_q // n_kv
    n_qt = seq_len // tq
    has_ctx = kx is not None
    has_sink = sink is not None
    assert seqs == 1 or (n_qt == 1 and not window and not has_ctx)
    q_base = tok_base // (seqs * tq)
    s_base = tok_base // (seqs * seq_len)
    q_blk = lambda b, j: q_base + b * n_qt + j
    in_specs = [pl.BlockSpec((n_q, seqs * tq, dk), lambda b, j: (0, q_blk(b, j), 0))]
    args = [q]
    if window:
        assert window == tq
        prev = lambda b, j: q_base + b * n_qt + jnp.maximum(j - 1, 0)
        nxt = lambda b, j: q_base + b * n_qt + jnp.minimum(j + 1, n_qt - 1)
        for blk in (prev, q_blk, nxt):
            in_specs.append(pl.BlockSpec((n_kv, tq, dk), lambda b, j, blk=blk: (0, blk(b, j), 0)))
        for blk in (prev, q_blk, nxt):
            in_specs.append(pl.BlockSpec((n_kv, VT_ROWS, tq), lambda b, j, blk=blk: (0, 0, blk(b, j))))
        args += [k, k, k, vt, vt, vt]
    else:
        in_specs += [
            pl.BlockSpec((n_kv, seqs * seq_len, dk), lambda b, j: (0, s_base + b, 0)),
            pl.BlockSpec((n_kv, VT_ROWS, seqs * seq_len), lambda b, j: (0, 0, s_base + b)),
        ]
        args += [k, vt]
    if has_ctx:
        in_specs += [
            pl.BlockSpec((n_kv, PAST_LEN, dk), lambda b, j: (0, b, 0)),
            pl.BlockSpec((n_kv, VT_ROWS, PAST_LEN), lambda b, j: (0, 0, b)),
        ]
        args += [kx, vx]
    if has_sink:
        in_specs.append(pl.BlockSpec(memory_space=pltpu.SMEM))
        args.append(sink)
    kern = functools.partial(_attn_kernel, n_kv=n_kv, group=group, tq=tq, seq_len=seq_len, seqs=seqs,
                             has_ctx=has_ctx, has_sink=has_sink, window=window,
                             q_unit=q_unit, key_chunk=key_chunk)
    return pl.pallas_call(
        kern, grid=(n_seq // seqs, n_qt), in_specs=in_specs,
        out_specs=pl.BlockSpec((seqs * tq, n_q * HEAD_DIM), lambda b, j: (b * n_qt + j, 0)),
        out_shape=jax.ShapeDtypeStruct((n_seq * seq_len, n_q * HEAD_DIM), BF16),
        compiler_params=pltpu.CompilerParams(
            dimension_semantics=("arbitrary", "arbitrary"), vmem_limit_bytes=VMEM_LIMIT),
        name=name,
    )(*args)


def _ffn_kernel(*refs, n_o, is_ctx, final):
    halo = 0 if is_ctx else HALO
    it = iter(refs)
    x_ref = next(it)
    xh_refs = None if is_ctx else (next(it), next(it))
    o_refs, oh_refs = [], []
    for _ in range(n_o):
        o_refs.append(next(it))
        if not is_ctx:
            oh_refs.append((next(it), next(it)))
    wo_refs = [next(it) for _ in range(n_o)]
    mod_ref, gffn_ref = next(it), next(it)
    wg_refs = [next(it) for _ in range(FF_PER_STEP)]
    wv_refs = [next(it) for _ in range(FF_PER_STEP)]
    cw_ref, cb_ref = next(it), next(it)
    wd_refs = [next(it) for _ in range(FF_PER_STEP)]
    gfin_ref, out_ref, h2e_ref, acc_ref = next(it), next(it), next(it), next(it)

    m = pl.program_id(0)
    c = pl.program_id(1)
    g1 = mod_ref[2:3, :]
    sh2 = mod_ref[3:4, :]
    sc2 = mod_ref[4:5, :]
    g2 = mod_ref[5:6, :]
    n_sub = FFN_TM // FFN_SUB
    sub_rows = FFN_SUB + 2 * halo

    @pl.when(c == 0)
    def _():
        wos = [w[...].astype(BF16) for w in wo_refs]

        def residual_and_norm(xv, ovs):
            attn = _dot(ovs[0], wos[0])
            for ov, wo in zip(ovs[1:], wos[1:]):
                attn = attn + _dot(ov, wo)
            x1 = xv + g1 * attn
            return x1, (_rms(x1, gffn_ref[...]) * (1.0 + sc2) + sh2).astype(BF16)

        for r in range(0, FFN_TM, FFN_SUB):
            x1, h2 = residual_and_norm(x_ref[r:r + FFN_SUB, :], [o[r:r + FFN_SUB, :] for o in o_refs])
            acc_ref[r:r + FFN_SUB, :] = x1
            h2e_ref[halo + r:halo + r + FFN_SUB, :] = h2
        if not is_ctx:
            _, h2h = residual_and_norm(
                jnp.concatenate([xh_refs[0][...], xh_refs[1][...]], axis=0),
                [jnp.concatenate([oh[0][...], oh[1][...]], axis=0) for oh in oh_refs])
            h2e_ref[0:HALO, :] = h2h[0:HALO]
            h2e_ref[HALO + FFN_TM:, :] = h2h[HALO:]

    row8 = lax.broadcasted_iota(jnp.int32, (8, FF_CHUNK), 0)
    if not is_ctx:
        has_prev = m % FFN_LAT_TILES != 0
        has_next = m % FFN_LAT_TILES != FFN_LAT_TILES - 1

    def ff_chunk(j):
        w_up = jnp.concatenate([wg_refs[j][...].astype(BF16), wv_refs[j][...].astype(BF16)], axis=1)
        w_dn = wd_refs[j][...].astype(BF16)
        cols = pl.ds(pl.multiple_of((c * FF_PER_STEP + j) * FF_CHUNK, FF_CHUNK), FF_CHUNK)
        cw = cw_ref[:, cols]
        cb = cb_ref[:, cols]
        ups = {}

        def emit_up(r):
            ups[r] = _dot(h2e_ref[r * FFN_SUB:r * FFN_SUB + sub_rows, :], w_up)

        for r in range(min(UP_LOOKAHEAD, n_sub)):
            emit_up(r)
        for r in range(n_sub):
            if r + UP_LOOKAHEAD < n_sub:
                emit_up(r + UP_LOOKAHEAD)
            up = ups.pop(r)
            ge = up[:, :FF_CHUNK]
            val = up[halo:halo + FFN_SUB, FF_CHUNK:]
            g_prev = pltpu.roll(ge, 1, axis=0)[halo:halo + FFN_SUB]
            g_next = pltpu.roll(ge, sub_rows - 1, axis=0)[halo:halo + FFN_SUB]
            prev_ok = False if is_ctx else (has_prev if r == 0 else True)
            next_ok = False if is_ctx else (has_next if r == n_sub - 1 else True)
            if prev_ok is not True:
                g_prev = jnp.concatenate(
                    [jnp.where(jnp.logical_or(row8 != 0, prev_ok), g_prev[0:8], 0.0), g_prev[8:]], axis=0)
            if next_ok is not True:
                g_next = jnp.concatenate(
                    [g_next[:-8], jnp.where(jnp.logical_or(row8 != 7, next_ok), g_next[-8:], 0.0)], axis=0)
            gate = g_prev * cw[0:1, :] + ge[halo:halo + FFN_SUB] * cw[1:2, :] + g_next * cw[2:3, :] + cb
            act = (gate * jax.nn.sigmoid(gate) * val).astype(BF16)
            acc_ref[r * FFN_SUB:(r + 1) * FFN_SUB, :] += g2 * _dot(act, w_dn)

    n_tail = N_FF_CHUNKS % FF_PER_STEP
    for j in range(FF_PER_STEP):
        if n_tail and j >= n_tail:
            pl.when(c < N_FF_STEPS - 1)(functools.partial(ff_chunk, j))
        else:
            ff_chunk(j)

    @pl.when(c == N_FF_STEPS - 1)
    def _():
        x2 = acc_ref[...]
        out_ref[...] = _rms(x2, gfin_ref[...]) if final else x2


def _ffn(x, os, mod, layer, g_ffn, w_out, w_up, conv_w, conv_b, w_down, g_final, *, is_ctx, final):
    n_rows = x.shape[0]
    nh = FFN_TM // HALO
    nblk = n_rows // HALO
    n_o = len(os)
    halo = 0 if is_ctx else HALO
    tile = lambda m, c: (m, 0)
    prev = lambda m, c: (jnp.maximum(m * nh - 1, 0), 0)
    nxt = lambda m, c: (jnp.minimum((m + 1) * nh, nblk - 1), 0)
    chunk = lambda c, j: jnp.minimum(c * FF_PER_STEP + j, N_FF_CHUNKS - 1)
    if is_ctx:
        cond = lambda m: layer * 8
    else:
        cond = lambda m: layer * 8 + 1 + m // FFN_LAT_TILES

    def with_halo(arr):
        w = arr.shape[1]
        specs = [pl.BlockSpec((FFN_TM, w), tile)]
        if not is_ctx:
            specs += [pl.BlockSpec((HALO, w), prev), pl.BlockSpec((HALO, w), nxt)]
        return specs, [arr] * len(specs)

    in_specs, args = with_halo(x)
    for o in os:
        specs, arrs = with_halo(o)
        in_specs += specs
        args += arrs
    w_rows = D_MODEL // n_o
    for t in range(n_o):
        in_specs.append(pl.BlockSpec((None, w_rows, D_MODEL), lambda m, c, t=t: (0, t, 0)))
        args.append(w_out)
    in_specs += [
        pl.BlockSpec((None, 6, D_MODEL), lambda m, c: (cond(m), 0, 0)),
        pl.BlockSpec((None, 1, D_MODEL), lambda m, c: (layer, 0, 0)),
    ]
    args += [mod, g_ffn]
    steps = range(FF_PER_STEP)
    in_specs += [pl.BlockSpec((None, D_MODEL, FF_CHUNK), lambda m, c, j=j: (layer, 0, chunk(c, j))) for j in steps]
    in_specs += [pl.BlockSpec((None, D_MODEL, FF_CHUNK), lambda m, c, j=j: (layer, 0, N_FF_CHUNKS + chunk(c, j)))
                 for j in steps]
    in_specs += [pl.BlockSpec((None, 3, D_FF), lambda m, c: (layer, 0, 0)),
                 pl.BlockSpec((None, 1, D_FF), lambda m, c: (layer, 0, 0))]
    in_specs += [pl.BlockSpec((None, FF_CHUNK, D_MODEL), lambda m, c, j=j: (layer, chunk(c, j), 0)) for j in steps]
    args += [w_up] * (2 * FF_PER_STEP) + [conv_w, conv_b] + [w_down] * FF_PER_STEP
    in_specs.append(pl.BlockSpec((1, D_MODEL), lambda m, c: (0, 0)))
    args.append(g_final)
    return pl.pallas_call(
        functools.partial(_ffn_kernel, n_o=n_o, is_ctx=is_ctx, final=final),
        grid=(n_rows // FFN_TM, N_FF_STEPS), in_specs=in_specs,
        out_specs=pl.BlockSpec((FFN_TM, D_MODEL), tile),
        out_shape=jax.ShapeDtypeStruct((n_rows, D_MODEL), F32),
        scratch_shapes=[pltpu.VMEM((FFN_TM + 2 * halo, D_MODEL), BF16), pltpu.VMEM((FFN_TM, D_MODEL), F32)],
        compiler_params=pltpu.CompilerParams(
            dimension_semantics=("arbitrary", "arbitrary"), vmem_limit_bytes=FFN_VMEM_LIMIT),
        name=("ffn_ctx" if is_ctx else "ffn_lat") + ("_final" if final else ""),
    )(*args)


def _rope_tables(rot_dim):
    f32 = np.float32
    t = np.arange(LAT_LEN)
    row = (t // GRID_W).astype(f32)
    col = (t % GRID_W).astype(f32)
    d_axis = rot_dim // 2
    freqs = (f32(ROPE_THETA) ** (-np.arange(0, d_axis, 2, dtype=f32) / f32(d_axis))).astype(f32)
    ang = np.concatenate([row[:, None] * freqs, col[:, None] * freqs], axis=-1)
    cos = np.repeat(np.cos(ang), 2, axis=-1).astype(f32)
    sin = (np.repeat(np.sin(ang), 2, axis=-1) * np.tile(np.array([-1.0, 1.0], f32), rot_dim // 2)).astype(f32)
    if rot_dim == HEAD_DIM:
        cos = np.tile(cos, (1, 2))
        sin = np.tile(sin, (1, 2))
    else:
        cos = np.concatenate([np.ones((LAT_LEN, QK_NOPE), f32), cos,
                              np.ones((LAT_LEN, 128 - QK_NOPE - rot_dim), f32)], axis=-1)
        sin = np.concatenate([np.zeros((LAT_LEN, QK_NOPE), f32), sin,
                              np.zeros((LAT_LEN, 128 - QK_NOPE - rot_dim), f32)], axis=-1)
    cos = np.concatenate([cos, np.ones((TM, 128), f32)], axis=0)
    sin = np.concatenate([sin, np.zeros((TM, 128), f32)], axis=0)
    return jnp.asarray(cos), jnp.asarray(sin)


def kernel(x_prompt, x_sample, cache_a_k, cache_a_v, cache_b_ckv, cache_b_kpe, cache_c_k, cache_c_v, c, c_ctx, w_mod, b_mod, g_mix_norm, g_ffn_norm, w_in_e, g_qnorm_a, g_knorm_a, g_cq_b, w_uq_b, g_ckv_b, w_ukv_b, w_out_e, w_in_o, sink_c, w_out_o, w_up, conv_w, conv_b, w_down, g_final):
    depth = w_mod.shape[0]
    cond8 = jnp.concatenate([c_ctx[None, :], c, jnp.zeros((5, D_MODEL), F32)], axis=0)
    lane_pad = KB_PAD - QK_NOPE - QK_ROPE
    wkpe = jnp.pad(w_in_e[0][:, IN_E_MAIN:], ((0, 0), (QK_NOPE, lane_pad)))
    wuq = jnp.pad(w_uq_b[0].reshape(Q_LORA, N_HEADS_B, QK_NOPE + QK_ROPE),
                  ((0, 0), (0, 0), (0, lane_pad))).reshape(Q_LORA, N_HEADS_B * KB_PAD)
    wukv3 = w_ukv_b[0].reshape(KV_LORA, N_HEADS_B, QK_NOPE + V_DIM_B)
    wukk = jnp.pad(wukv3[:, :, :QK_NOPE], ((0, 0), (0, 0), (0, KB_PAD - QK_NOPE))
                   ).reshape(KV_LORA, N_HEADS_B * KB_PAD)
    wukv = wukv3[:, :, QK_NOPE:].reshape(KV_LORA, N_HEADS_B * V_DIM_B)
    gq = jnp.tile(g_qnorm_a[0], N_HEADS_A)[None, :]
    gk = jnp.tile(g_knorm_a[0], N_KV_A)[None, :]
    seg = np.arange(256) // HEAD_DIM
    ones = jnp.asarray(seg[:, None] == seg[None, :], dtype=BF16)
    cosa, sina = _rope_tables(HEAD_DIM)
    cosb, sinb = _rope_tables(QK_ROPE)
    g_mix3 = g_mix_norm.reshape(depth, 1, D_MODEL)
    g_ffn3 = g_ffn_norm.reshape(depth, 1, D_MODEL)
    conv_b3 = conv_b.reshape(depth, 1, D_FF)
    g_fin2 = g_final[None, :]

    mod = _modulation(cond8, w_mod, b_mod).reshape(depth * 8, 6, D_MODEL)

    n_past = N_LAT_SEQ * PAST_LEN
    kpe_pad = jnp.pad(cache_b_kpe.reshape(n_past, QK_ROPE), ((0, 0), (QK_NOPE, lane_pad)))
    ka_c, va_c, kb_c, vb_c, kc_c, vc_c = _ctx_prep(
        cache_a_k.reshape(n_past, 128), cache_a_v.reshape(n_past, 128),
        cache_b_ckv.reshape(n_past, KV_LORA), kpe_pad,
        cache_c_k.reshape(n_past, 128), cache_c_v.reshape(n_past, 128), wukk, wukv)

    xp = x_prompt.reshape(N_CTX_TOK, D_MODEL)
    xs = x_sample.reshape(N_TOK - N_CTX_TOK, D_MODEL)
    qa, ka, va, qb, kb, vb, nak, nav, nckv, nkpe = _pre0(
        xp, xs, mod, g_mix3, w_in_e, wkpe, gq, gk, g_cq_b.reshape(1, 1, Q_LORA), g_ckv_b.reshape(1, 1, KV_LORA),
        wuq, wukk, wukv, cosa, sina, cosb, sinb, ones)
    ffn_w = (w_up, conv_w, conv_b3, w_down, g_fin2)
    ctx_kw = dict(n_seq=N_CTX_SEQ, seq_len=CTX_LEN, tok_base=0, tq=CTX_LEN, window=0, q_unit=256,
                  key_chunk=CTX_LEN, seqs=8)
    lat_kw = dict(n_seq=N_LAT_SEQ, seq_len=LAT_LEN, tok_base=N_CTX_TOK, q_unit=256)
    dense_kw = dict(tq=256, window=0, key_chunk=256, **lat_kw)
    oa = (_attention(qa, ka, va, None, None, None, name="attn_a_ctx", **ctx_kw),
          _attention(qa, ka, va, ka_c, va_c, None, name="attn_a_lat", **dense_kw))
    ob = (_attention(qb, kb, vb, None, None, None, name="attn_b_ctx", **ctx_kw),
          _attention(qb, kb, vb, kb_c, vb_c, None, name="attn_b_lat", **dense_kw))
    xp1 = _ffn(xp, [oa[0], ob[0]], mod, 0, g_ffn3, w_out_e, *ffn_w, is_ctx=True, final=False)
    xs1 = _ffn(xs, [oa[1], ob[1]], mod, 0, g_ffn3, w_out_e, *ffn_w, is_ctx=False, final=False)

    qc, kc, vc, nck, ncv = _pre1(xp1, xs1, mod, g_mix3, w_in_o, cosa, sina)
    sink = sink_c[0]
    oc = (_attention(qc, kc, vc, None, None, sink, name="attn_c_ctx", **ctx_kw),
          _attention(qc, kc, vc, kc_c, vc_c, sink, tq=128, window=WINDOW, key_chunk=PAST_LEN,
                     name="attn_c_lat", **lat_kw))
    y_prompt = _ffn(xp1, [oc[0]], mod, 1, g_ffn3, w_out_o, *ffn_w, is_ctx=True, final=True)
    y_sample = _ffn(xs1, [oc[1]], mod, 1, g_ffn3, w_out_o, *ffn_w, is_ctx=False, final=True)

    return (y_prompt.reshape(N_CTX_SEQ, CTX_LEN, D_MODEL), y_sample.reshape(N_LAT_SEQ, LAT_LEN, D_MODEL),
            nak.reshape(N_CTX_SEQ, 1, CTX_LEN, N_KV_A, HEAD_DIM),
            nav.reshape(N_CTX_SEQ, 1, CTX_LEN, N_KV_A, HEAD_DIM),
            nckv.reshape(N_CTX_SEQ, 1, CTX_LEN, KV_LORA),
            nkpe.reshape(N_CTX_SEQ, 1, CTX_LEN, QK_ROPE),
            nck.reshape(N_CTX_SEQ, 1, CTX_LEN, N_KV_C, HEAD_DIM),
            ncv.reshape(N_CTX_SEQ, 1, CTX_LEN, N_KV_C, HEAD_DIM))
```

```python
import functools

import jax
import jax.numpy as jnp
import numpy as np
from jax import lax
from jax.experimental import pallas as pl
from jax.experimental.pallas import tpu as pltpu

F32 = jnp.float32
BF16 = jnp.bfloat16

D_MODEL = 1024
N_CTX_SEQ = 16
CTX_LEN = 256
N_LAT_SEQ = 2
LAT_LEN = 2048
PAST_LEN = 512
GRID_W = 64
ROPE_THETA = 10000.0
NORM_EPS = 1e-6
WINDOW = 128
NEG_INF = -1e30
LOG2E = 1.4426950408889634
HEAD_DIM = 64
N_HEADS_A, N_KV_A = 8, 2
N_HEADS_B = 8
Q_LORA, KV_LORA = 384, 256
QK_NOPE, QK_ROPE, V_DIM_B = 64, 32, 64
N_HEADS_C, N_KV_C = 16, 2
D_FF = 2816
IN_E_MAIN = N_HEADS_A * HEAD_DIM + 2 * N_KV_A * HEAD_DIM + Q_LORA + KV_LORA

N_CTX_TOK = N_CTX_SEQ * CTX_LEN
N_TOK = N_CTX_TOK + N_LAT_SEQ * LAT_LEN
TM = 512
PRE_SUB = 256
N_TILES = N_TOK // TM
N_CTX_TILES = N_CTX_TOK // TM
LAT_TILES = LAT_LEN // TM
HALO = 16
FFN_TM = 1024
FFN_SUB = CTX_LEN
FFN_LAT_SUB = 256
UP_LOOKAHEAD = 3
FFN_LAT_TILES = LAT_LEN // FFN_TM
FF_CHUNK = 256
N_FF_CHUNKS = D_FF // FF_CHUNK
FF_PER_STEP = 3
N_FF_STEPS = -(-N_FF_CHUNKS // FF_PER_STEP)
KB_PAD = 128
VT_ROWS = 80
SCORE_LOOKAHEAD = 5
VMEM_LIMIT = 56 * 1024 * 1024
FFN_VMEM_LIMIT = 60 * 1024 * 1024


def _dot(a, b):
    return jnp.dot(a, b, preferred_element_type=F32)


def _dot_nt(a, b):
    return lax.dot_general(a, b, (((1,), (1,)), ((), ())), preferred_element_type=F32)


def _rms(x, g):
    return x * lax.rsqrt(jnp.mean(x * x, axis=-1, keepdims=True) + NORM_EPS) * g


def _split_bf16(x):
    hi = x.astype(BF16)
    return hi, (x - hi.astype(F32)).astype(BF16)


def _head_rms(x, g, ones_ref):
    w = x.shape[1]
    hi, lo = _split_bf16(x * x)
    parts = []
    for c in range(0, w, 256):
        cw = min(256, w - c)
        ones = ones_ref[0:cw, 0:cw]
        parts.append(_dot(hi[:, c:c + cw], ones) + _dot(lo[:, c:c + cw], ones))
    ssum = parts[0] if len(parts) == 1 else jnp.concatenate(parts, axis=1)
    return x * lax.rsqrt(ssum * (1.0 / HEAD_DIM) + NORM_EPS) * g


def _swap_pairs(x):
    w = x.shape[1]
    up = pltpu.roll(x, w - 1, axis=1)
    dn = pltpu.roll(x, 1, axis=1)
    lane = lax.broadcasted_iota(jnp.int32, x.shape, 1)
    return jnp.where((lane & 1) == 0, up, dn)


def _rope(x, cos, sin_signed, reps):
    if reps > 1:
        cos = jnp.concatenate([cos] * reps, axis=1)
        sin_signed = jnp.concatenate([sin_signed] * reps, axis=1)
    return x * cos + _swap_pairs(x) * sin_signed


def _store_vt(vt_ref, v, n_heads, cols=slice(None)):
    t = v.shape[0]
    vt = v.T.astype(BF16)
    ones = jnp.ones((VT_ROWS - 64, t), BF16)
    for hh in range(n_heads):
        vt_ref[hh, 0:64, cols] = vt[hh * 64:(hh + 1) * 64]
        vt_ref[hh, 64:VT_ROWS, cols] = ones


def _cond_row(i):
    return jnp.where(i < N_CTX_TILES, 0, 1 + (i - N_CTX_TILES) // LAT_TILES)


def _rope_block(i):
    return jnp.where(i < N_CTX_TILES, LAT_TILES, (i - N_CTX_TILES) % LAT_TILES)


def _const_spec(shape):
    zeros = (0,) * len(shape)
    return pl.BlockSpec(shape, lambda *_: zeros)


def _layer_spec(shape, layer):
    idx = (layer,) + (0,) * len(shape)
    return pl.BlockSpec((None,) + tuple(shape), lambda *_: idx)


def _mod_kernel(cond_ref, w_ref, b_ref, o_ref):
    c = cond_ref[...]
    s_hi, s_lo = _split_bf16(c * jax.nn.sigmoid(c))
    w_hi, w_lo = _split_bf16(w_ref[0])
    r = _dot(jnp.concatenate([s_hi, s_lo], axis=0), w_hi)
    o_ref[0] = r[0:8] + r[8:16] + _dot(s_hi, w_lo) + b_ref[0]


def _modulation(cond8, w_mod, b_mod):
    depth, _, n = w_mod.shape
    tn = 3072
    return pl.pallas_call(
        _mod_kernel,
        grid=(depth, n // tn),
        in_specs=[
            pl.BlockSpec((8, D_MODEL), lambda l, j: (0, 0)),
            pl.BlockSpec((1, D_MODEL, tn), lambda l, j: (l, 0, j)),
            pl.BlockSpec((1, 1, tn), lambda l, j: (l, 0, j)),
        ],
        out_specs=pl.BlockSpec((1, 8, tn), lambda l, j: (l, 0, j)),
        out_shape=jax.ShapeDtypeStruct((depth, 8, n), F32),
        compiler_params=pltpu.CompilerParams(
            dimension_semantics=("arbitrary", "arbitrary"), vmem_limit_bytes=VMEM_LIMIT),
        name="modulation",
    )(cond8, w_mod, b_mod.reshape(depth, 1, n))


def _pre0_kernel(xp_ref, xs_ref, mod_ref, gmix_ref, win_ref, gq_ref, gk_ref, gcq_ref, gckv_ref,
                 wuq_ref, wukk_ref, wukv_ref, cosa_ref, sina_ref, cosb_ref, sinb_ref, ones_ref,
                 qa_ref, ka_ref, va_ref, qb_ref, kb_ref, vb_ref,
                 nak_ref, nav_ref, nckv_ref, nkpe_ref,
                 win_s, wuq_s, wukk_s, wukv_s):
    i = pl.program_id(0)

    @pl.when(i == 0)
    def _():
        win_s[:, 0:IN_E_MAIN] = win_ref[:, 0:IN_E_MAIN].astype(BF16)
        win_s[:, IN_E_MAIN:] = jnp.concatenate(
            [jnp.zeros((D_MODEL, QK_NOPE), F32), win_ref[:, IN_E_MAIN:],
             jnp.zeros((D_MODEL, KB_PAD - QK_NOPE - QK_ROPE), F32)], axis=1).astype(BF16)
        wuq_s[...] = wuq_ref[...].astype(BF16)
        wukk_s[...] = wukk_ref[...].astype(BF16)
        wukv_s[...] = wukv_ref[...].astype(BF16)

    sh1 = mod_ref[0:1, :]
    sc1 = mod_ref[1:2, :]
    subs = [slice(r, r + PRE_SUB) for r in range(0, TM, PRE_SUB)]

    projs = []
    for sl in subs:
        x = jnp.where(i < N_CTX_TILES, xp_ref[sl, :], xs_ref[sl, :])
        h = _rms(x, gmix_ref[...]) * (1.0 + sc1) + sh1
        projs.append(_dot(h.astype(BF16), win_s[...]))

    parts = []
    for sl, proj in zip(subs, projs):
        qa = _head_rms(proj[:, 0:512], gq_ref[...], ones_ref)
        ka = _head_rms(proj[:, 512:640], gk_ref[...], ones_ref)
        va = proj[:, 640:768]
        cq = _rms(proj[:, 768:1152], gcq_ref[...])
        ckv = _rms(proj[:, 1152:1408], gckv_ref[...])
        kpe = proj[:, 1408:1536]
        qb = _dot(cq.astype(BF16), wuq_s[...])
        ckv_b = ckv.astype(BF16)
        kbn = _dot(ckv_b, wukk_s[...])
        vb = _dot(ckv_b, wukv_s[...])
        _store_vt(va_ref, va, N_KV_A, sl)
        _store_vt(vb_ref, vb, N_HEADS_B, sl)
        parts.append((ka, va, ckv, kpe))
        cosa, sina = cosa_ref[sl, :], sina_ref[sl, :]
        cosb, sinb = cosb_ref[sl, :], sinb_ref[sl, :]
        qa_s = _rope(qa, cosa, sina, 4) * (LOG2E * HEAD_DIM ** -0.5)
        ka_r = _rope(ka, cosa, sina, 1)
        qb_s = _rope(qb, cosb, sinb, 8) * (LOG2E * (QK_NOPE + QK_ROPE) ** -0.5)
        kpe_r = _rope(kpe, cosb, sinb, 1)
        for hh in range(N_HEADS_A):
            qa_ref[hh, sl, :] = qa_s[:, hh * 64:(hh + 1) * 64].astype(BF16)
        for hh in range(N_KV_A):
            ka_ref[hh, sl, :] = ka_r[:, hh * 64:(hh + 1) * 64].astype(BF16)
        for hh in range(N_HEADS_B):
            qb_ref[hh, sl, :] = qb_s[:, hh * KB_PAD:(hh + 1) * KB_PAD].astype(BF16)
            kb_ref[hh, sl, :] = (kbn[:, hh * KB_PAD:(hh + 1) * KB_PAD] + kpe_r).astype(BF16)

    @pl.when(i < N_CTX_TILES)
    def _():
        for sl, (ka, va, ckv, kpe) in zip(subs, parts):
            nak_ref[sl, :] = ka
            nav_ref[sl, :] = va
            nckv_ref[sl, :] = ckv
            nkpe_ref[sl, :] = kpe[:, QK_NOPE:QK_NOPE + QK_ROPE]


def _pre0(xp, xs, mod, gmix, w_in_e, gq, gk, gcq, gckv, wuq, wukk, wukv, cosa, sina, cosb, sinb, ones):
    tile = lambda i: (i, 0)
    head_tile = lambda i: (0, i, 0)
    vt_tile = lambda i: (0, 0, i)
    ctx_tile = lambda i: (jnp.minimum(i, N_CTX_TILES - 1), 0)
    lat_tile = lambda i: (jnp.maximum(i - N_CTX_TILES, 0), 0)
    rope_tile = lambda i: (_rope_block(i), 0)
    in_specs = [
        pl.BlockSpec((TM, D_MODEL), ctx_tile),
        pl.BlockSpec((TM, D_MODEL), lat_tile),
        pl.BlockSpec((None, 6, D_MODEL), lambda i: (_cond_row(i), 0, 0)),
        _layer_spec((1, D_MODEL), 0),
        _layer_spec(w_in_e.shape[1:], 0),
        _const_spec(gq.shape), _const_spec(gk.shape), _layer_spec((1, Q_LORA), 0), _layer_spec((1, KV_LORA), 0),
        _const_spec(wuq.shape), _const_spec(wukk.shape), _const_spec(wukv.shape),
        pl.BlockSpec((TM, 128), rope_tile), pl.BlockSpec((TM, 128), rope_tile),
        pl.BlockSpec((TM, 128), rope_tile), pl.BlockSpec((TM, 128), rope_tile),
        _const_spec(ones.shape),
    ]
    out_shape = [
        jax.ShapeDtypeStruct((N_HEADS_A, N_TOK, 64), BF16),
        jax.ShapeDtypeStruct((N_KV_A, N_TOK, 64), BF16),
        jax.ShapeDtypeStruct((N_KV_A, VT_ROWS, N_TOK), BF16),
        jax.ShapeDtypeStruct((N_HEADS_B, N_TOK, KB_PAD), BF16),
        jax.ShapeDtypeStruct((N_HEADS_B, N_TOK, KB_PAD), BF16),
        jax.ShapeDtypeStruct((N_HEADS_B, VT_ROWS, N_TOK), BF16),
        jax.ShapeDtypeStruct((N_CTX_TOK, 128), F32),
        jax.ShapeDtypeStruct((N_CTX_TOK, 128), F32),
        jax.ShapeDtypeStruct((N_CTX_TOK, KV_LORA), F32),
        jax.ShapeDtypeStruct((N_CTX_TOK, QK_ROPE), F32),
    ]
    out_specs = [
        pl.BlockSpec((N_HEADS_A, TM, 64), head_tile),
        pl.BlockSpec((N_KV_A, TM, 64), head_tile),
        pl.BlockSpec((N_KV_A, VT_ROWS, TM), vt_tile),
        pl.BlockSpec((N_HEADS_B, TM, KB_PAD), head_tile),
        pl.BlockSpec((N_HEADS_B, TM, KB_PAD), head_tile),
        pl.BlockSpec((N_HEADS_B, VT_ROWS, TM), vt_tile),
        pl.BlockSpec((TM, 128), ctx_tile),
        pl.BlockSpec((TM, 128), ctx_tile),
        pl.BlockSpec((TM, KV_LORA), ctx_tile),
        pl.BlockSpec((TM, QK_ROPE), ctx_tile),
    ]
    scratch = [
        pltpu.VMEM((D_MODEL, IN_E_MAIN + KB_PAD), BF16),
        pltpu.VMEM(wuq.shape, BF16), pltpu.VMEM(wukk.shape, BF16), pltpu.VMEM(wukv.shape, BF16),
    ]
    return pl.pallas_call(
        _pre0_kernel, grid=(N_TILES,), in_specs=in_specs, out_specs=out_specs, out_shape=out_shape,
        scratch_shapes=scratch,
        compiler_params=pltpu.CompilerParams(
            dimension_semantics=("arbitrary",), vmem_limit_bytes=VMEM_LIMIT),
        name="pre0",
    )(xp, xs, mod, gmix, w_in_e, gq, gk, gcq, gckv, wuq, wukk, wukv, cosa, sina, cosb, sinb, ones)


def _pre1_kernel(xp_ref, xs_ref, mod_ref, gmix_ref, win_ref, cosa_ref, sina_ref,
                 q_ref, k_ref, v_ref, nk_ref, nv_ref, win_s):
    i = pl.program_id(0)

    @pl.when(i == 0)
    def _():
        win_s[...] = win_ref[...].astype(BF16)

    sh1 = mod_ref[0:1, :]
    sc1 = mod_ref[1:2, :]
    subs = [slice(r, r + PRE_SUB) for r in range(0, TM, PRE_SUB)]
    projs = []
    for sl in subs:
        x = jnp.where(i < N_CTX_TILES, xp_ref[sl, :], xs_ref[sl, :])
        h = _rms(x, gmix_ref[...]) * (1.0 + sc1) + sh1
        projs.append(_dot(h.astype(BF16), win_s[...]))
    for sl, proj in zip(subs, projs):
        cosa, sina = cosa_ref[sl, :], sina_ref[sl, :]
        q_s = _rope(proj[:, 0:1024], cosa, sina, 8) * (LOG2E * HEAD_DIM ** -0.5)
        k = _rope(proj[:, 1024:1152], cosa, sina, 1)
        for hh in range(N_HEADS_C):
            q_ref[hh, sl, :] = q_s[:, hh * 64:(hh + 1) * 64].astype(BF16)
        for hh in range(N_KV_C):
            k_ref[hh, sl, :] = k[:, hh * 64:(hh + 1) * 64].astype(BF16)
        _store_vt(v_ref, proj[:, 1152:1280], N_KV_C, sl)

    @pl.when(i < N_CTX_TILES)
    def _():
        for sl, proj in zip(subs, projs):
            nk_ref[sl, :] = proj[:, 1024:1152]
            nv_ref[sl, :] = proj[:, 1152:1280]


def _pre1(xp, xs, mod, gmix, w_in_o, cosa, sina):
    n_in = w_in_o.shape[-1]
    head_tile = lambda i: (0, i, 0)
    vt_tile = lambda i: (0, 0, i)
    ctx_tile = lambda i: (jnp.minimum(i, N_CTX_TILES - 1), 0)
    lat_tile = lambda i: (jnp.maximum(i - N_CTX_TILES, 0), 0)
    rope_tile = lambda i: (_rope_block(i), 0)
    in_specs = [
        pl.BlockSpec((TM, D_MODEL), ctx_tile),
        pl.BlockSpec((TM, D_MODEL), lat_tile),
        pl.BlockSpec((None, 6, D_MODEL), lambda i: (8 + _cond_row(i), 0, 0)),
        _layer_spec((1, D_MODEL), 1),
        _layer_spec((D_MODEL, n_in), 0),
        pl.BlockSpec((TM, 128), rope_tile), pl.BlockSpec((TM, 128), rope_tile),
    ]
    out_shape = [
        jax.ShapeDtypeStruct((N_HEADS_C, N_TOK, 64), BF16),
        jax.ShapeDtypeStruct((N_KV_C, N_TOK, 64), BF16),
        jax.ShapeDtypeStruct((N_KV_C, VT_ROWS, N_TOK), BF16),
        jax.ShapeDtypeStruct((N_CTX_TOK, 128), F32),
        jax.ShapeDtypeStruct((N_CTX_TOK, 128), F32),
    ]
    out_specs = [
        pl.BlockSpec((N_HEADS_C, TM, 64), head_tile),
        pl.BlockSpec((N_KV_C, TM, 64), head_tile),
        pl.BlockSpec((N_KV_C, VT_ROWS, TM), vt_tile),
        pl.BlockSpec((TM, 128), ctx_tile),
        pl.BlockSpec((TM, 128), ctx_tile),
    ]
    return pl.pallas_call(
        _pre1_kernel, grid=(N_TILES,), in_specs=in_specs, out_specs=out_specs, out_shape=out_shape,
        scratch_shapes=[pltpu.VMEM((D_MODEL, n_in), BF16)],
        compiler_params=pltpu.CompilerParams(
            dimension_semantics=("arbitrary",), vmem_limit_bytes=VMEM_LIMIT),
        name="pre1",
    )(xp, xs, mod, gmix, w_in_o, cosa, sina)


def _ctx_kernel(ak_ref, av_ref, ckv_ref, kpe_ref, ck_ref, cv_ref, wukk_ref, wukv_ref,
                ka_ref, va_ref, kb_ref, vb_ref, kc_ref, vc_ref):
    ak, av, ck, cv = ak_ref[...], av_ref[...], ck_ref[...], cv_ref[...]
    for hh in range(2):
        sl = slice(hh * 64, (hh + 1) * 64)
        ka_ref[hh] = ak[:, sl].astype(BF16)
        kc_ref[hh] = ck[:, sl].astype(BF16)
    _store_vt(va_ref, av, 2)
    _store_vt(vc_ref, cv, 2)
    ckv_b = ckv_ref[...].astype(BF16)
    kbn = _dot(ckv_b, wukk_ref[...].astype(BF16))
    vb = _dot(ckv_b, wukv_ref[...].astype(BF16))
    kpe = kpe_ref[...]
    for hh in range(N_HEADS_B):
        kb_ref[hh] = (kbn[:, hh * KB_PAD:(hh + 1) * KB_PAD] + kpe).astype(BF16)
    _store_vt(vb_ref, vb, N_HEADS_B)


def _ctx_prep(ak, av, ckv, kpe_pad, ck, cv, wukk, wukv):
    n = N_LAT_SEQ * PAST_LEN
    row = lambda b: (b, 0)
    head_row = lambda b: (0, b, 0)
    vt_row = lambda b: (0, 0, b)
    in_specs = [
        pl.BlockSpec((PAST_LEN, 128), row), pl.BlockSpec((PAST_LEN, 128), row),
        pl.BlockSpec((PAST_LEN, KV_LORA), row), pl.BlockSpec((PAST_LEN, KB_PAD), row),
        pl.BlockSpec((PAST_LEN, 128), row), pl.BlockSpec((PAST_LEN, 128), row),
        _const_spec(wukk.shape), _const_spec(wukv.shape),
    ]
    out_shape = [
        jax.ShapeDtypeStruct((2, n, 64), BF16), jax.ShapeDtypeStruct((2, VT_ROWS, n), BF16),
        jax.ShapeDtypeStruct((N_HEADS_B, n, KB_PAD), BF16), jax.ShapeDtypeStruct((N_HEADS_B, VT_ROWS, n), BF16),
        jax.ShapeDtypeStruct((2, n, 64), BF16), jax.ShapeDtypeStruct((2, VT_ROWS, n), BF16),
    ]
    out_specs = [
        pl.BlockSpec((2, PAST_LEN, 64), head_row), pl.BlockSpec((2, VT_ROWS, PAST_LEN), vt_row),
        pl.BlockSpec((N_HEADS_B, PAST_LEN, KB_PAD), head_row), pl.BlockSpec((N_HEADS_B, VT_ROWS, PAST_LEN), vt_row),
        pl.BlockSpec((2, PAST_LEN, 64), head_row), pl.BlockSpec((2, VT_ROWS, PAST_LEN), vt_row),
    ]
    return pl.pallas_call(
        _ctx_kernel, grid=(N_LAT_SEQ,), in_specs=in_specs, out_specs=out_specs, out_shape=out_shape,
        compiler_params=pltpu.CompilerParams(
            dimension_semantics=("arbitrary",), vmem_limit_bytes=VMEM_LIMIT),
        name="ctx_prep",
    )(ak, av, ckv, kpe_pad, ck, cv, wukk, wukv)


def _softmax_units(units, lookahead):
    tasks = [(u, c) for u, unit in enumerate(units) for c in range(len(unit["chunks"]))]
    scores = {}

    def emit_scores(t):
        u, c = tasks[t]
        k, _, mask = units[u]["chunks"][c]
        s = _dot_nt(k, units[u]["q"])
        scores[t] = s if mask is None else jnp.where(mask, s, NEG_INF)

    for t in range(min(lookahead, len(tasks))):
        emit_scores(t)
    for t, (u, c) in enumerate(tasks):
        if t + lookahead < len(tasks):
            emit_scores(t + lookahead)
        unit = units[u]
        s = scores.pop(t)
        m, acc = unit["m"], unit["acc"]
        cmax = jnp.max(s, axis=0, keepdims=True)
        m_new = cmax if m is None else jnp.maximum(m, cmax)
        pv = _dot(unit["chunks"][c][1], jnp.exp2(s - m_new).astype(BF16))
        unit["acc"] = pv if acc is None else acc * jnp.exp2(m - m_new) + pv
        unit["m"] = m_new
    return [unit["acc"][0:64] * (1.0 / unit["acc"][64:65]) for unit in units]


def _attn_kernel(*refs, n_kv, group, tq, seq_len, seqs, has_ctx, has_sink, window, q_unit, key_chunk):
    refs = list(refs)
    q_ref = refs.pop(0)
    if window:
        n_blk = seqs + 2
        kb_refs, vb_refs = refs[:n_blk], refs[n_blk:2 * n_blk]
        refs = refs[2 * n_blk:]
    else:
        k_ref, vt_ref = refs[:2]
        refs = refs[2:]
    if has_ctx:
        kx_ref, vx_ref = refs[:2]
        refs = refs[2:]
    if has_sink:
        sink_ref = refs.pop(0)
    o_ref = refs.pop(0)

    j = pl.program_id(1)
    dk = q_ref.shape[-1]
    heads_per_unit = q_unit // tq
    lane = lax.broadcasted_iota(jnp.int32, (1, q_unit), 1)
    if window:
        n_band = tq + 2 * window
        krow = lax.broadcasted_iota(jnp.int32, (n_band, q_unit), 0)
        qcol = lax.broadcasted_iota(jnp.int32, (n_band, q_unit), 1) & (tq - 1)
        rel = (krow - window) - qcol
        in_band = jnp.abs(rel) <= window
        last_j = seq_len // (tq * seqs) - 1
        band_masks = []
        for sq in range(seqs):
            mask = in_band
            if sq == 0:
                mask = mask & ((krow >= window) | (j > 0))
            if sq == seqs - 1:
                mask = mask & ((krow < window + tq) | (j < last_j))
            band_masks.append(mask)
    if has_sink:
        acc0 = jnp.where(lax.broadcasted_iota(jnp.int32, (VT_ROWS, q_unit), 0) >= 64, 1.0, 0.0)

    units = []
    for sq in range(seqs):
        base = sq * (tq if window else seq_len)
        for hk in range(n_kv):
            chunks = []
            if window:
                chunks.append((jnp.concatenate([r[hk] for r in kb_refs[sq:sq + 3]], axis=0),
                               jnp.concatenate([r[hk] for r in vb_refs[sq:sq + 3]], axis=1), band_masks[sq]))
            else:
                for c in range(base, base + seq_len, key_chunk):
                    n = min(key_chunk, base + seq_len - c)
                    chunks.append((k_ref[hk, c:c + n, :], vt_ref[hk, :, c:c + n], None))
            if has_ctx:
                for c in range(0, PAST_LEN, key_chunk):
                    n = min(key_chunk, PAST_LEN - c)
                    chunks.append((kx_ref[hk, c:c + n, :], vx_ref[hk, :, c:c + n], None))
            for u in range(group // heads_per_unit):
                h0 = hk * group + u * heads_per_unit
                q = q_ref[h0:h0 + heads_per_unit, base:base + tq, :].reshape(q_unit, dk)
                unit = dict(q=q, chunks=chunks, m=None, acc=None)
                if has_sink:
                    m0 = jnp.full((1, q_unit), sink_ref[h0] * LOG2E, F32)
                    for e in range(1, heads_per_unit):
                        m0 = jnp.where(lane >= e * tq, sink_ref[h0 + e] * LOG2E, m0)
                    unit.update(m=m0, acc=acc0)
                units.append(unit)
    results = _softmax_units(units, SCORE_LOOKAHEAD)
    per_seq = len(results) // seqs
    for sq in range(seqs):
        outs = []
        for o in results[sq * per_seq:(sq + 1) * per_seq]:
            for e in range(heads_per_unit):
                outs.append(o[:, e * tq:(e + 1) * tq])
        base = sq * (tq if window else seq_len)
        o_ref[base:base + tq, :] = jnp.concatenate(outs, axis=0).T.astype(BF16)


def _attention(q, k, vt, kx, vx, sink, *, n_seq, seq_len, tok_base, tq, window, q_unit, key_chunk, name,
               seqs=1):
    n_q, _, dk = q.shape
    n_kv = k.shape[0]
    group = n_q // n_kv
    has_ctx = kx is not None
    has_sink = sink is not None
    if window:
        n_qt = seq_len // (tq * seqs)
    else:
        n_qt = seq_len // tq
        assert seqs == 1 or (n_qt == 1 and not has_ctx)
    q_base = tok_base // (seqs * tq)
    s_base = tok_base // (seqs * seq_len)
    q_blk = lambda b, j: q_base + b * n_qt + j
    in_specs = [pl.BlockSpec((n_q, seqs * tq, dk), lambda b, j: (0, q_blk(b, j), 0))]
    args = [q]
    if window:
        assert window == tq
        n_kb = seq_len // tq
        kb_base = tok_base // tq

        def key_blk(b, j, off):
            return kb_base + b * n_kb + jnp.clip(j * seqs - 1 + off, 0, n_kb - 1)

        for off in range(seqs + 2):
            in_specs.append(pl.BlockSpec((n_kv, tq, dk), lambda b, j, off=off: (0, key_blk(b, j, off), 0)))
        for off in range(seqs + 2):
            in_specs.append(pl.BlockSpec((n_kv, VT_ROWS, tq), lambda b, j, off=off: (0, 0, key_blk(b, j, off))))
        args += [k] * (seqs + 2) + [vt] * (seqs + 2)
    else:
        in_specs += [
            pl.BlockSpec((n_kv, seqs * seq_len, dk), lambda b, j: (0, s_base + b, 0)),
            pl.BlockSpec((n_kv, VT_ROWS, seqs * seq_len), lambda b, j: (0, 0, s_base + b)),
        ]
        args += [k, vt]
    if has_ctx:
        in_specs += [
            pl.BlockSpec((n_kv, PAST_LEN, dk), lambda b, j: (0, b, 0)),
            pl.BlockSpec((n_kv, VT_ROWS, PAST_LEN), lambda b, j: (0, 0, b)),
        ]
        args += [kx, vx]
    if has_sink:
        in_specs.append(pl.BlockSpec(memory_space=pltpu.SMEM))
        args.append(sink)
    kern = functools.partial(_attn_kernel, n_kv=n_kv, group=group, tq=tq, seq_len=seq_len, seqs=seqs,
                             has_ctx=has_ctx, has_sink=has_sink, window=window,
                             q_unit=q_unit, key_chunk=key_chunk)
    return pl.pallas_call(
        kern, grid=(n_seq if window else n_seq // seqs, n_qt), in_specs=in_specs,
        out_specs=pl.BlockSpec((seqs * tq, n_q * HEAD_DIM), lambda b, j: (b * n_qt + j, 0)),
        out_shape=jax.ShapeDtypeStruct((n_seq * seq_len, n_q * HEAD_DIM), BF16),
        compiler_params=pltpu.CompilerParams(
            dimension_semantics=("arbitrary", "arbitrary"), vmem_limit_bytes=VMEM_LIMIT),
        name=name,
    )(*args)


def _ffn_kernel(*refs, n_o, is_ctx, final):
    halo = 0 if is_ctx else HALO
    it = iter(refs)
    x_ref = next(it)
    xh_refs = None if is_ctx else (next(it), next(it))
    o_refs, oh_refs = [], []
    for _ in range(n_o):
        o_refs.append(next(it))
        if not is_ctx:
            oh_refs.append((next(it), next(it)))
    wo_refs = [next(it) for _ in range(n_o)]
    mod_ref, gffn_ref = next(it), next(it)
    wg_refs = [next(it) for _ in range(FF_PER_STEP)]
    wv_refs = [next(it) for _ in range(FF_PER_STEP)]
    cw_ref, cb_ref = next(it), next(it)
    wd_refs = [next(it) for _ in range(FF_PER_STEP)]
    gfin_ref, out_ref, h2e_ref, acc_ref = next(it), next(it), next(it), next(it)

    m = pl.program_id(0)
    c = pl.program_id(1)
    g1 = mod_ref[2:3, :]
    sh2 = mod_ref[3:4, :]
    sc2 = mod_ref[4:5, :]
    g2 = mod_ref[5:6, :]
    sub = FFN_SUB if is_ctx else FFN_LAT_SUB
    n_sub = FFN_TM // sub
    sub_rows = sub + 2 * halo

    @pl.when(c == 0)
    def _():
        wos = [w[...].astype(BF16) for w in wo_refs]

        def residual_and_norm(xv, ovs):
            attn = _dot(ovs[0], wos[0])
            for ov, wo in zip(ovs[1:], wos[1:]):
                attn = attn + _dot(ov, wo)
            x1 = xv + g1 * attn
            return x1, (_rms(x1, gffn_ref[...]) * (1.0 + sc2) + sh2).astype(BF16)

        for r in range(0, FFN_TM, FFN_SUB):
            x1, h2 = residual_and_norm(x_ref[r:r + FFN_SUB, :], [o[r:r + FFN_SUB, :] for o in o_refs])
            acc_ref[r:r + FFN_SUB, :] = x1
            h2e_ref[halo + r:halo + r + FFN_SUB, :] = h2
        if not is_ctx:
            _, h2h = residual_and_norm(
                jnp.concatenate([xh_refs[0][...], xh_refs[1][...]], axis=0),
                [jnp.concatenate([oh[0][...], oh[1][...]], axis=0) for oh in oh_refs])
            h2e_ref[0:HALO, :] = h2h[0:HALO]
            h2e_ref[HALO + FFN_TM:, :] = h2h[HALO:]

    row8 = lax.broadcasted_iota(jnp.int32, (8, FF_CHUNK), 0)
    if not is_ctx:
        has_prev = m % FFN_LAT_TILES != 0
        has_next = m % FFN_LAT_TILES != FFN_LAT_TILES - 1

    def ff_chunk(j):
        w_up = jnp.concatenate([wg_refs[j][...].astype(BF16), wv_refs[j][...].astype(BF16)], axis=1)
        w_dn = wd_refs[j][...].astype(BF16)
        cols = pl.ds(pl.multiple_of((c * FF_PER_STEP + j) * FF_CHUNK, FF_CHUNK), FF_CHUNK)
        cw = cw_ref[:, cols]
        cb = cb_ref[:, cols]
        ups = {}

        def emit_up(r):
            ups[r] = _dot(h2e_ref[r * sub:r * sub + sub_rows, :], w_up)

        for r in range(min(UP_LOOKAHEAD, n_sub)):
            emit_up(r)
        for r in range(n_sub):
            if r + UP_LOOKAHEAD < n_sub:
                emit_up(r + UP_LOOKAHEAD)
            up = ups.pop(r)
            ge = up[:, :FF_CHUNK]
            val = up[halo:halo + sub, FF_CHUNK:]
            g_prev = pltpu.roll(ge, 1, axis=0)[halo:halo + sub]
            g_next = pltpu.roll(ge, sub_rows - 1, axis=0)[halo:halo + sub]
            prev_ok = False if is_ctx else (has_prev if r == 0 else True)
            next_ok = False if is_ctx else (has_next if r == n_sub - 1 else True)
            if prev_ok is not True:
                g_prev = jnp.concatenate(
                    [jnp.where(jnp.logical_or(row8 != 0, prev_ok), g_prev[0:8], 0.0), g_prev[8:]], axis=0)
            if next_ok is not True:
                g_next = jnp.concatenate(
                    [g_next[:-8], jnp.where(jnp.logical_or(row8 != 7, next_ok), g_next[-8:], 0.0)], axis=0)
            gate = g_prev * cw[0:1, :] + ge[halo:halo + sub] * cw[1:2, :] + g_next * cw[2:3, :] + cb
            act = (gate * jax.nn.sigmoid(gate) * val).astype(BF16)
            acc_ref[r * sub:(r + 1) * sub, :] += g2 * _dot(act, w_dn)

    n_tail = N_FF_CHUNKS % FF_PER_STEP
    for j in range(FF_PER_STEP):
        if n_tail and j >= n_tail:
            pl.when(c < N_FF_STEPS - 1)(functools.partial(ff_chunk, j))
        else:
            ff_chunk(j)

    @pl.when(c == N_FF_STEPS - 1)
    def _():
        x2 = acc_ref[...]
        out_ref[...] = _rms(x2, gfin_ref[...]) if final else x2


def _ffn(x, os, mod, layer, g_ffn, w_out, w_up, conv_w, conv_b, w_down, g_final, *, is_ctx, final):
    n_rows = x.shape[0]
    nh = FFN_TM // HALO
    nblk = n_rows // HALO
    n_o = len(os)
    halo = 0 if is_ctx else HALO
    tile = lambda m, c: (m, 0)
    prev = lambda m, c: (jnp.maximum(m * nh - 1, 0), 0)
    nxt = lambda m, c: (jnp.minimum((m + 1) * nh, nblk - 1), 0)
    chunk = lambda c, j: jnp.minimum(c * FF_PER_STEP + j, N_FF_CHUNKS - 1)
    if is_ctx:
        cond = lambda m: layer * 8
    else:
        cond = lambda m: layer * 8 + 1 + m // FFN_LAT_TILES

    def with_halo(arr):
        w = arr.shape[1]
        specs = [pl.BlockSpec((FFN_TM, w), tile)]
        if not is_ctx:
            specs += [pl.BlockSpec((HALO, w), prev), pl.BlockSpec((HALO, w), nxt)]
        return specs, [arr] * len(specs)

    in_specs, args = with_halo(x)
    for o in os:
        specs, arrs = with_halo(o)
        in_specs += specs
        args += arrs
    w_rows = D_MODEL // n_o
    for t in range(n_o):
        in_specs.append(pl.BlockSpec((None, w_rows, D_MODEL), lambda m, c, t=t: (0, t, 0)))
        args.append(w_out)
    in_specs += [
        pl.BlockSpec((None, 6, D_MODEL), lambda m, c: (cond(m), 0, 0)),
        pl.BlockSpec((None, 1, D_MODEL), lambda m, c: (layer, 0, 0)),
    ]
    args += [mod, g_ffn]
    steps = range(FF_PER_STEP)
    in_specs += [pl.BlockSpec((None, D_MODEL, FF_CHUNK), lambda m, c, j=j: (layer, 0, chunk(c, j))) for j in steps]
    in_specs += [pl.BlockSpec((None, D_MODEL, FF_CHUNK), lambda m, c, j=j: (layer, 0, N_FF_CHUNKS + chunk(c, j)))
                 for j in steps]
    in_specs += [pl.BlockSpec((None, 3, D_FF), lambda m, c: (layer, 0, 0)),
                 pl.BlockSpec((None, 1, D_FF), lambda m, c: (layer, 0, 0))]
    in_specs += [pl.BlockSpec((None, FF_CHUNK, D_MODEL), lambda m, c, j=j: (layer, chunk(c, j), 0)) for j in steps]
    args += [w_up] * (2 * FF_PER_STEP) + [conv_w, conv_b] + [w_down] * FF_PER_STEP
    in_specs.append(pl.BlockSpec((1, D_MODEL), lambda m, c: (0, 0)))
    args.append(g_final)
    return pl.pallas_call(
        functools.partial(_ffn_kernel, n_o=n_o, is_ctx=is_ctx, final=final),
        grid=(n_rows // FFN_TM, N_FF_STEPS), in_specs=in_specs,
        out_specs=pl.BlockSpec((FFN_TM, D_MODEL), tile),
        out_shape=jax.ShapeDtypeStruct((n_rows, D_MODEL), F32),
        scratch_shapes=[pltpu.VMEM((FFN_TM + 2 * halo, D_MODEL), BF16), pltpu.VMEM((FFN_TM, D_MODEL), F32)],
        compiler_params=pltpu.CompilerParams(
            dimension_semantics=("arbitrary", "arbitrary"), vmem_limit_bytes=FFN_VMEM_LIMIT),
        name=("ffn_ctx" if is_ctx else "ffn_lat") + ("_final" if final else ""),
    )(*args)


def _rope_tables(rot_dim):
    f32 = np.float32
    t = np.arange(LAT_LEN)
    row = (t // GRID_W).astype(f32)
    col = (t % GRID_W).astype(f32)
    d_axis = rot_dim // 2
    freqs = (f32(ROPE_THETA) ** (-np.arange(0, d_axis, 2, dtype=f32) / f32(d_axis))).astype(f32)
    ang = np.concatenate([row[:, None] * freqs, col[:, None] * freqs], axis=-1)
    cos = np.repeat(np.cos(ang), 2, axis=-1).astype(f32)
    sin = (np.repeat(np.sin(ang), 2, axis=-1) * np.tile(np.array([-1.0, 1.0], f32), rot_dim // 2)).astype(f32)
    if rot_dim == HEAD_DIM:
        cos = np.tile(cos, (1, 2))
        sin = np.tile(sin, (1, 2))
    else:
        cos = np.concatenate([np.ones((LAT_LEN, QK_NOPE), f32), cos,
                              np.ones((LAT_LEN, 128 - QK_NOPE - rot_dim), f32)], axis=-1)
        sin = np.concatenate([np.zeros((LAT_LEN, QK_NOPE), f32), sin,
                              np.zeros((LAT_LEN, 128 - QK_NOPE - rot_dim), f32)], axis=-1)
    cos = np.concatenate([cos, np.ones((TM, 128), f32)], axis=0)
    sin = np.concatenate([sin, np.zeros((TM, 128), f32)], axis=0)
    return jnp.asarray(cos), jnp.asarray(sin)


def kernel(x_prompt, x_sample, cache_a_k, cache_a_v, cache_b_ckv, cache_b_kpe, cache_c_k, cache_c_v, c, c_ctx, w_mod, b_mod, g_mix_norm, g_ffn_norm, w_in_e, g_qnorm_a, g_knorm_a, g_cq_b, w_uq_b, g_ckv_b, w_ukv_b, w_out_e, w_in_o, sink_c, w_out_o, w_up, conv_w, conv_b, w_down, g_final):
    depth = w_mod.shape[0]
    cond8 = jnp.concatenate([c_ctx[None, :], c, jnp.zeros((5, D_MODEL), F32)], axis=0)
    lane_pad = KB_PAD - QK_NOPE - QK_ROPE
    wuq = jnp.pad(w_uq_b[0].reshape(Q_LORA, N_HEADS_B, QK_NOPE + QK_ROPE),
                  ((0, 0), (0, 0), (0, lane_pad))).reshape(Q_LORA, N_HEADS_B * KB_PAD)
    wukv3 = w_ukv_b[0].reshape(KV_LORA, N_HEADS_B, QK_NOPE + V_DIM_B)
    wukk = jnp.pad(wukv3[:, :, :QK_NOPE], ((0, 0), (0, 0), (0, KB_PAD - QK_NOPE))
                   ).reshape(KV_LORA, N_HEADS_B * KB_PAD)
    wukv = wukv3[:, :, QK_NOPE:].reshape(KV_LORA, N_HEADS_B * V_DIM_B)
    gq = jnp.tile(g_qnorm_a[0], N_HEADS_A)[None, :]
    gk = jnp.tile(g_knorm_a[0], N_KV_A)[None, :]
    seg = np.arange(256) // HEAD_DIM
    ones = jnp.asarray(seg[:, None] == seg[None, :], dtype=BF16)
    cosa, sina = _rope_tables(HEAD_DIM)
    cosb, sinb = _rope_tables(QK_ROPE)
    g_mix3 = g_mix_norm.reshape(depth, 1, D_MODEL)
    g_ffn3 = g_ffn_norm.reshape(depth, 1, D_MODEL)
    conv_b3 = conv_b.reshape(depth, 1, D_FF)
    g_fin2 = g_final[None, :]

    mod = _modulation(cond8, w_mod, b_mod).reshape(depth * 8, 6, D_MODEL)

    n_past = N_LAT_SEQ * PAST_LEN
    kpe_pad = jnp.pad(cache_b_kpe.reshape(n_past, QK_ROPE), ((0, 0), (QK_NOPE, lane_pad)))
    ka_c, va_c, kb_c, vb_c, kc_c, vc_c = _ctx_prep(
        cache_a_k.reshape(n_past, 128), cache_a_v.reshape(n_past, 128),
        cache_b_ckv.reshape(n_past, KV_LORA), kpe_pad,
        cache_c_k.reshape(n_past, 128), cache_c_v.reshape(n_past, 128), wukk, wukv)

    xp = x_prompt.reshape(N_CTX_TOK, D_MODEL)
    xs = x_sample.reshape(N_TOK - N_CTX_TOK, D_MODEL)
    qa, ka, va, qb, kb, vb, nak, nav, nckv, nkpe = _pre0(
        xp, xs, mod, g_mix3, w_in_e, gq, gk, g_cq_b.reshape(1, 1, Q_LORA), g_ckv_b.reshape(1, 1, KV_LORA),
        wuq, wukk, wukv, cosa, sina, cosb, sinb, ones)
    ffn_w = (w_up, conv_w, conv_b3, w_down, g_fin2)
    ctx_kw = dict(n_seq=N_CTX_SEQ, seq_len=CTX_LEN, tok_base=0, tq=CTX_LEN, window=0, q_unit=256,
                  key_chunk=CTX_LEN, seqs=8)
    lat_kw = dict(n_seq=N_LAT_SEQ, seq_len=LAT_LEN, tok_base=N_CTX_TOK, q_unit=256)
    dense_kw = dict(tq=256, window=0, key_chunk=256, **lat_kw)
    oa = (_attention(qa, ka, va, None, None, None, name="attn_a_ctx", **ctx_kw),
          _attention(qa, ka, va, ka_c, va_c, None, name="attn_a_lat", **dense_kw))
    ob = (_attention(qb, kb, vb, None, None, None, name="attn_b_ctx", **ctx_kw),
          _attention(qb, kb, vb, kb_c, vb_c, None, name="attn_b_lat", **dense_kw))
    xp1 = _ffn(xp, [oa[0], ob[0]], mod, 0, g_ffn3, w_out_e, *ffn_w, is_ctx=True, final=False)
    xs1 = _ffn(xs, [oa[1], ob[1]], mod, 0, g_ffn3, w_out_e, *ffn_w, is_ctx=False, final=False)

    qc, kc, vc, nck, ncv = _pre1(xp1, xs1, mod, g_mix3, w_in_o, cosa, sina)
    sink = sink_c[0]
    oc = (_attention(qc, kc, vc, None, None, sink, name="attn_c_ctx", **ctx_kw),
          _attention(qc, kc, vc, kc_c, vc_c, sink, tq=128, window=WINDOW, key_chunk=PAST_LEN, seqs=4,
                     name="attn_c_lat", **lat_kw))
    y_prompt = _ffn(xp1, [oc[0]], mod, 1, g_ffn3, w_out_o, *ffn_w, is_ctx=True, final=True)
    y_sample = _ffn(xs1, [oc[1]], mod, 1, g_ffn3, w_out_o, *ffn_w, is_ctx=False, final=True)

    return (y_prompt.reshape(N_CTX_SEQ, CTX_LEN, D_MODEL), y_sample.reshape(N_LAT_SEQ, LAT_LEN, D_MODEL),
            nak.reshape(N_CTX_SEQ, 1, CTX_LEN, N_KV_A, HEAD_DIM),
            nav.reshape(N_CTX_SEQ, 1, CTX_LEN, N_KV_A, HEAD_DIM),
            nckv.reshape(N_CTX_SEQ, 1, CTX_LEN, KV_LORA),
            nkpe.reshape(N_CTX_SEQ, 1, CTX_LEN, QK_ROPE),
            nck.reshape(N_CTX_SEQ, 1, CTX_LEN, N_KV_C, HEAD_DIM),
            ncv.reshape(N_CTX_SEQ, 1, CTX_LEN, N_KV_C, HEAD_DIM))
```

```python
import functools

import jax
import jax.numpy as jnp
import numpy as np
from jax import lax
from jax.experimental import pallas as pl
from jax.experimental.pallas import tpu as pltpu

F32 = jnp.float32
BF16 = jnp.bfloat16

D_MODEL = 1024
N_CTX_SEQ = 16
CTX_LEN = 256
N_LAT_SEQ = 2
LAT_LEN = 2048
PAST_LEN = 512
GRID_W = 64
ROPE_THETA = 10000.0
NORM_EPS = 1e-6
WINDOW = 128
NEG_INF = -1e30
LOG2E = 1.4426950408889634
HEAD_DIM = 64
N_HEADS_A, N_KV_A = 8, 2
N_HEADS_B = 8
Q_LORA, KV_LORA = 384, 256
QK_NOPE, QK_ROPE, V_DIM_B = 64, 32, 64
N_HEADS_C, N_KV_C = 16, 2
D_FF = 2816
IN_E_MAIN = N_HEADS_A * HEAD_DIM + 2 * N_KV_A * HEAD_DIM + Q_LORA + KV_LORA

N_CTX_TOK = N_CTX_SEQ * CTX_LEN
N_TOK = N_CTX_TOK + N_LAT_SEQ * LAT_LEN
TM = 512
PRE_SUB = 256
N_TILES = N_TOK // TM
N_CTX_TILES = N_CTX_TOK // TM
LAT_TILES = LAT_LEN // TM
HALO = 16
FFN_TM = 1024
FFN_SUB = CTX_LEN
FFN_LAT_SUB = 256
UP_LOOKAHEAD = 3
FFN_LAT_TILES = LAT_LEN // FFN_TM
FF_CHUNK = 256
N_FF_CHUNKS = D_FF // FF_CHUNK
FF_PER_STEP = 3
N_FF_STEPS = -(-N_FF_CHUNKS // FF_PER_STEP)
KB_PAD = 128
VT_ROWS = 80
SCORE_LOOKAHEAD = 5
VMEM_LIMIT = 56 * 1024 * 1024
FFN_VMEM_LIMIT = 60 * 1024 * 1024


def _dot(a, b):
    return jnp.dot(a, b, preferred_element_type=F32)


def _dot_nt(a, b):
    return lax.dot_general(a, b, (((1,), (1,)), ((), ())), preferred_element_type=F32)


def _rms(x, g):
    return x * lax.rsqrt(jnp.mean(x * x, axis=-1, keepdims=True) + NORM_EPS) * g


def _split_bf16(x):
    hi = x.astype(BF16)
    return hi, (x - hi.astype(F32)).astype(BF16)


def _head_rms(x, g, ones_ref):
    w = x.shape[1]
    hi, lo = _split_bf16(x * x)
    parts = []
    for c in range(0, w, 256):
        cw = min(256, w - c)
        ones = ones_ref[0:cw, 0:cw]
        parts.append(_dot(hi[:, c:c + cw], ones) + _dot(lo[:, c:c + cw], ones))
    ssum = parts[0] if len(parts) == 1 else jnp.concatenate(parts, axis=1)
    return x * lax.rsqrt(ssum * (1.0 / HEAD_DIM) + NORM_EPS) * g


def _swap_pairs(x):
    w = x.shape[1]
    up = pltpu.roll(x, w - 1, axis=1)
    dn = pltpu.roll(x, 1, axis=1)
    lane = lax.broadcasted_iota(jnp.int32, x.shape, 1)
    return jnp.where((lane & 1) == 0, up, dn)


def _rope(x, cos, sin_signed, reps):
    if reps > 1:
        cos = jnp.concatenate([cos] * reps, axis=1)
        sin_signed = jnp.concatenate([sin_signed] * reps, axis=1)
    return x * cos + _swap_pairs(x) * sin_signed


def _store_vt(vt_ref, v, n_heads, cols=slice(None)):
    t = v.shape[0]
    vt = v.T.astype(BF16)
    ones = jnp.ones((VT_ROWS - 64, t), BF16)
    for hh in range(n_heads):
        vt_ref[hh, 0:64, cols] = vt[hh * 64:(hh + 1) * 64]
        vt_ref[hh, 64:VT_ROWS, cols] = ones


def _cond_row(i):
    return jnp.where(i < N_CTX_TILES, 0, 1 + (i - N_CTX_TILES) // LAT_TILES)


def _rope_block(i):
    return jnp.where(i < N_CTX_TILES, LAT_TILES, (i - N_CTX_TILES) % LAT_TILES)


def _const_spec(shape):
    zeros = (0,) * len(shape)
    return pl.BlockSpec(shape, lambda *_: zeros)


def _layer_spec(shape, layer):
    idx = (layer,) + (0,) * len(shape)
    return pl.BlockSpec((None,) + tuple(shape), lambda *_: idx)


def _mod_kernel(cond_ref, w_ref, b_ref, o_ref):
    c = cond_ref[...]
    s_hi, s_lo = _split_bf16(c * jax.nn.sigmoid(c))
    w_hi, w_lo = _split_bf16(w_ref[0])
    r = _dot(jnp.concatenate([s_hi, s_lo], axis=0), w_hi)
    o_ref[0] = r[0:8] + r[8:16] + _dot(s_hi, w_lo) + b_ref[0]


def _modulation(cond8, w_mod, b_mod):
    depth, _, n = w_mod.shape
    tn = 3072
    return pl.pallas_call(
        _mod_kernel,
        grid=(depth, n // tn),
        in_specs=[
            pl.BlockSpec((8, D_MODEL), lambda l, j: (0, 0)),
            pl.BlockSpec((1, D_MODEL, tn), lambda l, j: (l, 0, j)),
            pl.BlockSpec((1, 1, tn), lambda l, j: (l, 0, j)),
        ],
        out_specs=pl.BlockSpec((1, 8, tn), lambda l, j: (l, 0, j)),
        out_shape=jax.ShapeDtypeStruct((depth, 8, n), F32),
        compiler_params=pltpu.CompilerParams(
            dimension_semantics=("arbitrary", "arbitrary"), vmem_limit_bytes=VMEM_LIMIT),
        name="modulation",
    )(cond8, w_mod, b_mod.reshape(depth, 1, n))


def _pre0_kernel(xp_ref, xs_ref, mod_ref, gmix_ref, win_ref, gq_ref, gk_ref, gcq_ref, gckv_ref,
                 wuq_ref, wukk_ref, wukv_ref, cosa_ref, sina_ref, cosb_ref, sinb_ref, ones_ref,
                 qa_ref, ka_ref, va_ref, qb_ref, kb_ref, vb_ref,
                 nak_ref, nav_ref, nckv_ref, nkpe_ref,
                 win_s, wuq_s, wukk_s, wukv_s):
    i = pl.program_id(0)

    @pl.when(i == 0)
    def _():
        win_s[:, 0:IN_E_MAIN] = win_ref[:, 0:IN_E_MAIN].astype(BF16)
        win_s[:, IN_E_MAIN:] = jnp.concatenate(
            [jnp.zeros((D_MODEL, QK_NOPE), F32), win_ref[:, IN_E_MAIN:],
             jnp.zeros((D_MODEL, KB_PAD - QK_NOPE - QK_ROPE), F32)], axis=1).astype(BF16)
        wuq_s[...] = wuq_ref[...].astype(BF16)
        wukk_s[...] = wukk_ref[...].astype(BF16)
        wukv_s[...] = wukv_ref[...].astype(BF16)

    sh1 = mod_ref[0:1, :]
    sc1 = mod_ref[1:2, :]
    subs = [slice(r, r + PRE_SUB) for r in range(0, TM, PRE_SUB)]

    projs = []
    for sl in subs:
        x = jnp.where(i < N_CTX_TILES, xp_ref[sl, :], xs_ref[sl, :])
        h = _rms(x, gmix_ref[...]) * (1.0 + sc1) + sh1
        projs.append(_dot(h.astype(BF16), win_s[...]))

    parts = []
    for sl, proj in zip(subs, projs):
        qa = _head_rms(proj[:, 0:512], gq_ref[...], ones_ref)
        ka = _head_rms(proj[:, 512:640], gk_ref[...], ones_ref)
        va = proj[:, 640:768]
        cq = _rms(proj[:, 768:1152], gcq_ref[...])
        ckv = _rms(proj[:, 1152:1408], gckv_ref[...])
        kpe = proj[:, 1408:1536]
        qb = _dot(cq.astype(BF16), wuq_s[...])
        ckv_b = ckv.astype(BF16)
        kbn = _dot(ckv_b, wukk_s[...])
        vb = _dot(ckv_b, wukv_s[...])
        _store_vt(va_ref, va, N_KV_A, sl)
        _store_vt(vb_ref, vb, N_HEADS_B, sl)
        parts.append((ka, va, ckv, kpe))
        cosa, sina = cosa_ref[sl, :], sina_ref[sl, :]
        cosb, sinb = cosb_ref[sl, :], sinb_ref[sl, :]
        qa_s = _rope(qa, cosa, sina, 4) * (LOG2E * HEAD_DIM ** -0.5)
        ka_r = _rope(ka, cosa, sina, 1)
        qb_s = _rope(qb, cosb, sinb, 8) * (LOG2E * (QK_NOPE + QK_ROPE) ** -0.5)
        kpe_r = _rope(kpe, cosb, sinb, 1)
        for hh in range(N_HEADS_A):
            qa_ref[hh, sl, :] = qa_s[:, hh * 64:(hh + 1) * 64].astype(BF16)
        for hh in range(N_KV_A):
            ka_ref[hh, sl, :] = ka_r[:, hh * 64:(hh + 1) * 64].astype(BF16)
        for hh in range(N_HEADS_B):
            qb_ref[hh, sl, :] = qb_s[:, hh * KB_PAD:(hh + 1) * KB_PAD].astype(BF16)
            kb_ref[hh, sl, :] = (kbn[:, hh * KB_PAD:(hh + 1) * KB_PAD] + kpe_r).astype(BF16)

    @pl.when(i < N_CTX_TILES)
    def _():
        for sl, (ka, va, ckv, kpe) in zip(subs, parts):
            nak_ref[sl, :] = ka
            nav_ref[sl, :] = va
            nckv_ref[sl, :] = ckv
            nkpe_ref[sl, :] = kpe[:, QK_NOPE:QK_NOPE + QK_ROPE]


def _pre0(xp, xs, mod, gmix, w_in_e, gq, gk, gcq, gckv, wuq, wukk, wukv, cosa, sina, cosb, sinb, ones):
    tile = lambda i: (i, 0)
    head_tile = lambda i: (0, i, 0)
    vt_tile = lambda i: (0, 0, i)
    ctx_tile = lambda i: (jnp.minimum(i, N_CTX_TILES - 1), 0)
    lat_tile = lambda i: (jnp.maximum(i - N_CTX_TILES, 0), 0)
    rope_tile = lambda i: (_rope_block(i), 0)
    in_specs = [
        pl.BlockSpec((TM, D_MODEL), ctx_tile),
        pl.BlockSpec((TM, D_MODEL), lat_tile),
        pl.BlockSpec((None, 6, D_MODEL), lambda i: (_cond_row(i), 0, 0)),
        _layer_spec((1, D_MODEL), 0),
        _layer_spec(w_in_e.shape[1:], 0),
        _const_spec(gq.shape), _const_spec(gk.shape), _layer_spec((1, Q_LORA), 0), _layer_spec((1, KV_LORA), 0),
        _const_spec(wuq.shape), _const_spec(wukk.shape), _const_spec(wukv.shape),
        pl.BlockSpec((TM, 128), rope_tile), pl.BlockSpec((TM, 128), rope_tile),
        pl.BlockSpec((TM, 128), rope_tile), pl.BlockSpec((TM, 128), rope_tile),
        _const_spec(ones.shape),
    ]
    out_shape = [
        jax.ShapeDtypeStruct((N_HEADS_A, N_TOK, 64), BF16),
        jax.ShapeDtypeStruct((N_KV_A, N_TOK, 64), BF16),
        jax.ShapeDtypeStruct((N_KV_A, VT_ROWS, N_TOK), BF16),
        jax.ShapeDtypeStruct((N_HEADS_B, N_TOK, KB_PAD), BF16),
        jax.ShapeDtypeStruct((N_HEADS_B, N_TOK, KB_PAD), BF16),
        jax.ShapeDtypeStruct((N_HEADS_B, VT_ROWS, N_TOK), BF16),
        jax.ShapeDtypeStruct((N_CTX_TOK, 128), F32),
        jax.ShapeDtypeStruct((N_CTX_TOK, 128), F32),
        jax.ShapeDtypeStruct((N_CTX_TOK, KV_LORA), F32),
        jax.ShapeDtypeStruct((N_CTX_TOK, QK_ROPE), F32),
    ]
    out_specs = [
        pl.BlockSpec((N_HEADS_A, TM, 64), head_tile),
        pl.BlockSpec((N_KV_A, TM, 64), head_tile),
        pl.BlockSpec((N_KV_A, VT_ROWS, TM), vt_tile),
        pl.BlockSpec((N_HEADS_B, TM, KB_PAD), head_tile),
        pl.BlockSpec((N_HEADS_B, TM, KB_PAD), head_tile),
        pl.BlockSpec((N_HEADS_B, VT_ROWS, TM), vt_tile),
        pl.BlockSpec((TM, 128), ctx_tile),
        pl.BlockSpec((TM, 128), ctx_tile),
        pl.BlockSpec((TM, KV_LORA), ctx_tile),
        pl.BlockSpec((TM, QK_ROPE), ctx_tile),
    ]
    scratch = [
        pltpu.VMEM((D_MODEL, IN_E_MAIN + KB_PAD), BF16),
        pltpu.VMEM(wuq.shape, BF16), pltpu.VMEM(wukk.shape, BF16), pltpu.VMEM(wukv.shape, BF16),
    ]
    return pl.pallas_call(
        _pre0_kernel, grid=(N_TILES,), in_specs=in_specs, out_specs=out_specs, out_shape=out_shape,
        scratch_shapes=scratch,
        compiler_params=pltpu.CompilerParams(
            dimension_semantics=("arbitrary",), vmem_limit_bytes=VMEM_LIMIT),
        name="pre0",
    )(xp, xs, mod, gmix, w_in_e, gq, gk, gcq, gckv, wuq, wukk, wukv, cosa, sina, cosb, sinb, ones)


def _pre1_kernel(xp_ref, xs_ref, mod_ref, gmix_ref, win_ref, cosa_ref, sina_ref,
                 q_ref, k_ref, v_ref, nk_ref, nv_ref, win_s):
    i = pl.program_id(0)

    @pl.when(i == 0)
    def _():
        win_s[...] = win_ref[...].astype(BF16)

    sh1 = mod_ref[0:1, :]
    sc1 = mod_ref[1:2, :]
    subs = [slice(r, r + PRE_SUB) for r in range(0, TM, PRE_SUB)]
    projs = []
    for sl in subs:
        x = jnp.where(i < N_CTX_TILES, xp_ref[sl, :], xs_ref[sl, :])
        h = _rms(x, gmix_ref[...]) * (1.0 + sc1) + sh1
        projs.append(_dot(h.astype(BF16), win_s[...]))
    for sl, proj in zip(subs, projs):
        cosa, sina = cosa_ref[sl, :], sina_ref[sl, :]
        q_s = _rope(proj[:, 0:1024], cosa, sina, 8) * (LOG2E * HEAD_DIM ** -0.5)
        k = _rope(proj[:, 1024:1152], cosa, sina, 1)
        for hh in range(N_HEADS_C):
            q_ref[hh, sl, :] = q_s[:, hh * 64:(hh + 1) * 64].astype(BF16)
        for hh in range(N_KV_C):
            k_ref[hh, sl, :] = k[:, hh * 64:(hh + 1) * 64].astype(BF16)
        _store_vt(v_ref, proj[:, 1152:1280], N_KV_C, sl)

    @pl.when(i < N_CTX_TILES)
    def _():
        for sl, proj in zip(subs, projs):
            nk_ref[sl, :] = proj[:, 1024:1152]
            nv_ref[sl, :] = proj[:, 1152:1280]


def _pre1(xp, xs, mod, gmix, w_in_o, cosa, sina):
    n_in = w_in_o.shape[-1]
    head_tile = lambda i: (0, i, 0)
    vt_tile = lambda i: (0, 0, i)
    ctx_tile = lambda i: (jnp.minimum(i, N_CTX_TILES - 1), 0)
    lat_tile = lambda i: (jnp.maximum(i - N_CTX_TILES, 0), 0)
    rope_tile = lambda i: (_rope_block(i), 0)
    in_specs = [
        pl.BlockSpec((TM, D_MODEL), ctx_tile),
        pl.BlockSpec((TM, D_MODEL), lat_tile),
        pl.BlockSpec((None, 6, D_MODEL), lambda i: (8 + _cond_row(i), 0, 0)),
        _layer_spec((1, D_MODEL), 1),
        _layer_spec((D_MODEL, n_in), 0),
        pl.BlockSpec((TM, 128), rope_tile), pl.BlockSpec((TM, 128), rope_tile),
    ]
    out_shape = [
        jax.ShapeDtypeStruct((N_HEADS_C, N_TOK, 64), BF16),
        jax.ShapeDtypeStruct((N_KV_C, N_TOK, 64), BF16),
        jax.ShapeDtypeStruct((N_KV_C, VT_ROWS, N_TOK), BF16),
        jax.ShapeDtypeStruct((N_CTX_TOK, 128), F32),
        jax.ShapeDtypeStruct((N_CTX_TOK, 128), F32),
    ]
    out_specs = [
        pl.BlockSpec((N_HEADS_C, TM, 64), head_tile),
        pl.BlockSpec((N_KV_C, TM, 64), head_tile),
        pl.BlockSpec((N_KV_C, VT_ROWS, TM), vt_tile),
        pl.BlockSpec((TM, 128), ctx_tile),
        pl.BlockSpec((TM, 128), ctx_tile),
    ]
    return pl.pallas_call(
        _pre1_kernel, grid=(N_TILES,), in_specs=in_specs, out_specs=out_specs, out_shape=out_shape,
        scratch_shapes=[pltpu.VMEM((D_MODEL, n_in), BF16)],
        compiler_params=pltpu.CompilerParams(
            dimension_semantics=("arbitrary",), vmem_limit_bytes=VMEM_LIMIT),
        name="pre1",
    )(xp, xs, mod, gmix, w_in_o, cosa, sina)


def _ctx_kernel(ak_ref, av_ref, ckv_ref, kpe_ref, ck_ref, cv_ref, wukk_ref, wukv_ref,
                ka_ref, va_ref, kb_ref, vb_ref, kc_ref, vc_ref):
    ak, av, ck, cv = ak_ref[...], av_ref[...], ck_ref[...], cv_ref[...]
    for hh in range(2):
        sl = slice(hh * 64, (hh + 1) * 64)
        ka_ref[hh] = ak[:, sl].astype(BF16)
        kc_ref[hh] = ck[:, sl].astype(BF16)
    _store_vt(va_ref, av, 2)
    _store_vt(vc_ref, cv, 2)
    ckv_b = ckv_ref[...].astype(BF16)
    kbn = _dot(ckv_b, wukk_ref[...].astype(BF16))
    vb = _dot(ckv_b, wukv_ref[...].astype(BF16))
    kpe = kpe_ref[...]
    for hh in range(N_HEADS_B):
        kb_ref[hh] = (kbn[:, hh * KB_PAD:(hh + 1) * KB_PAD] + kpe).astype(BF16)
    _store_vt(vb_ref, vb, N_HEADS_B)


def _ctx_prep(ak, av, ckv, kpe_pad, ck, cv, wukk, wukv):
    n = N_LAT_SEQ * PAST_LEN
    row = lambda b: (b, 0)
    head_row = lambda b: (0, b, 0)
    vt_row = lambda b: (0, 0, b)
    in_specs = [
        pl.BlockSpec((PAST_LEN, 128), row), pl.BlockSpec((PAST_LEN, 128), row),
        pl.BlockSpec((PAST_LEN, KV_LORA), row), pl.BlockSpec((PAST_LEN, KB_PAD), row),
        pl.BlockSpec((PAST_LEN, 128), row), pl.BlockSpec((PAST_LEN, 128), row),
        _const_spec(wukk.shape), _const_spec(wukv.shape),
    ]
    out_shape = [
        jax.ShapeDtypeStruct((2, n, 64), BF16), jax.ShapeDtypeStruct((2, VT_ROWS, n), BF16),
        jax.ShapeDtypeStruct((N_HEADS_B, n, KB_PAD), BF16), jax.ShapeDtypeStruct((N_HEADS_B, VT_ROWS, n), BF16),
        jax.ShapeDtypeStruct((2, n, 64), BF16), jax.ShapeDtypeStruct((2, VT_ROWS, n), BF16),
    ]
    out_specs = [
        pl.BlockSpec((2, PAST_LEN, 64), head_row), pl.BlockSpec((2, VT_ROWS, PAST_LEN), vt_row),
        pl.BlockSpec((N_HEADS_B, PAST_LEN, KB_PAD), head_row), pl.BlockSpec((N_HEADS_B, VT_ROWS, PAST_LEN), vt_row),
        pl.BlockSpec((2, PAST_LEN, 64), head_row), pl.BlockSpec((2, VT_ROWS, PAST_LEN), vt_row),
    ]
    return pl.pallas_call(
        _ctx_kernel, grid=(N_LAT_SEQ,), in_specs=in_specs, out_specs=out_specs, out_shape=out_shape,
        compiler_params=pltpu.CompilerParams(
            dimension_semantics=("arbitrary",), vmem_limit_bytes=VMEM_LIMIT),
        name="ctx_prep",
    )(ak, av, ckv, kpe_pad, ck, cv, wukk, wukv)


def _softmax_units(units, lookahead):
    tasks = [(u, c) for u, unit in enumerate(units) for c in range(len(unit["chunks"]))]
    scores = {}

    def emit_scores(t):
        u, c = tasks[t]
        k, _, mask = units[u]["chunks"][c]
        s = _dot_nt(k, units[u]["q"])
        scores[t] = s if mask is None else jnp.where(mask, s, NEG_INF)

    for t in range(min(lookahead, len(tasks))):
        emit_scores(t)
    for t, (u, c) in enumerate(tasks):
        if t + lookahead < len(tasks):
            emit_scores(t + lookahead)
        unit = units[u]
        s = scores.pop(t)
        m, acc = unit["m"], unit["acc"]
        cmax = jnp.max(s, axis=0, keepdims=True)
        m_new = cmax if m is None else jnp.maximum(m, cmax)
        pv = _dot(unit["chunks"][c][1], jnp.exp2(s - m_new).astype(BF16))
        unit["acc"] = pv if acc is None else acc * jnp.exp2(m - m_new) + pv
        unit["m"] = m_new
    return [unit["acc"][0:64] * (1.0 / unit["acc"][64:65]) for unit in units]


def _attn_kernel(*refs, n_kv, group, tq, seq_len, seqs, has_ctx, has_sink, window, q_unit, key_chunk):
    refs = list(refs)
    q_ref = refs.pop(0)
    if window:
        n_blk = seqs + 2
        kb_refs, vb_refs = refs[:n_blk], refs[n_blk:2 * n_blk]
        refs = refs[2 * n_blk:]
    else:
        k_ref, vt_ref = refs[:2]
        refs = refs[2:]
    if has_ctx:
        kx_ref, vx_ref = refs[:2]
        refs = refs[2:]
    if has_sink:
        sink_ref = refs.pop(0)
    o_ref = refs.pop(0)

    j = pl.program_id(1)
    dk = q_ref.shape[-1]
    heads_per_unit = q_unit // tq
    lane = lax.broadcasted_iota(jnp.int32, (1, q_unit), 1)
    if window:
        n_band = tq + 2 * window
        krow = lax.broadcasted_iota(jnp.int32, (n_band, q_unit), 0)
        qcol = lax.broadcasted_iota(jnp.int32, (n_band, q_unit), 1) & (tq - 1)
        rel = (krow - window) - qcol
        in_band = jnp.abs(rel) <= window
        last_j = seq_len // (tq * seqs) - 1
        band_masks = []
        for sq in range(seqs):
            mask = in_band
            if sq == 0:
                mask = mask & ((krow >= window) | (j > 0))
            if sq == seqs - 1:
                mask = mask & ((krow < window + tq) | (j < last_j))
            band_masks.append(mask)
    if has_sink:
        acc0 = jnp.where(lax.broadcasted_iota(jnp.int32, (VT_ROWS, q_unit), 0) >= 64, 1.0, 0.0)

    units = []
    for sq in range(seqs):
        base = sq * (tq if window else seq_len)
        for hk in range(n_kv):
            chunks = []
            if window:
                chunks.append((jnp.concatenate([r[hk] for r in kb_refs[sq:sq + 3]], axis=0),
                               jnp.concatenate([r[hk] for r in vb_refs[sq:sq + 3]], axis=1), band_masks[sq]))
            else:
                for c in range(base, base + seq_len, key_chunk):
                    n = min(key_chunk, base + seq_len - c)
                    chunks.append((k_ref[hk, c:c + n, :], vt_ref[hk, :, c:c + n], None))
            if has_ctx:
                for c in range(0, PAST_LEN, key_chunk):
                    n = min(key_chunk, PAST_LEN - c)
                    chunks.append((kx_ref[hk, c:c + n, :], vx_ref[hk, :, c:c + n], None))
            for u in range(group // heads_per_unit):
                h0 = hk * group + u * heads_per_unit
                q = q_ref[h0:h0 + heads_per_unit, base:base + tq, :].reshape(q_unit, dk)
                unit = dict(q=q, chunks=chunks, m=None, acc=None)
                if has_sink:
                    m0 = jnp.full((1, q_unit), sink_ref[h0] * LOG2E, F32)
                    for e in range(1, heads_per_unit):
                        m0 = jnp.where(lane >= e * tq, sink_ref[h0 + e] * LOG2E, m0)
                    unit.update(m=m0, acc=acc0)
                units.append(unit)
    results = _softmax_units(units, SCORE_LOOKAHEAD)
    per_seq = len(results) // seqs
    for sq in range(seqs):
        outs = []
        for o in results[sq * per_seq:(sq + 1) * per_seq]:
            for e in range(heads_per_unit):
                outs.append(o[:, e * tq:(e + 1) * tq])
        base = sq * (tq if window else seq_len)
        o_ref[base:base + tq, :] = jnp.concatenate(outs, axis=0).T.astype(BF16)


def _attention(q, k, vt, kx, vx, sink, *, n_seq, seq_len, tok_base, tq, window, q_unit, key_chunk, name,
               seqs=1):
    n_q, _, dk = q.shape
    n_kv = k.shape[0]
    group = n_q // n_kv
    has_ctx = kx is not None
    has_sink = sink is not None
    if window:
        n_qt = seq_len // (tq * seqs)
    else:
        n_qt = seq_len // tq
        assert seqs == 1 or (n_qt == 1 and not has_ctx)
    q_base = tok_base // (seqs * tq)
    s_base = tok_base // (seqs * seq_len)
    q_blk = lambda b, j: q_base + b * n_qt + j
    in_specs = [pl.BlockSpec((n_q, seqs * tq, dk), lambda b, j: (0, q_blk(b, j), 0))]
    args = [q]
    if window:
        assert window == tq
        n_kb = seq_len // tq
        kb_base = tok_base // tq

        def key_blk(b, j, off):
            return kb_base + b * n_kb + jnp.clip(j * seqs - 1 + off, 0, n_kb - 1)

        for off in range(seqs + 2):
            in_specs.append(pl.BlockSpec((n_kv, tq, dk), lambda b, j, off=off: (0, key_blk(b, j, off), 0)))
        for off in range(seqs + 2):
            in_specs.append(pl.BlockSpec((n_kv, VT_ROWS, tq), lambda b, j, off=off: (0, 0, key_blk(b, j, off))))
        args += [k] * (seqs + 2) + [vt] * (seqs + 2)
    else:
        in_specs += [
            pl.BlockSpec((n_kv, seqs * seq_len, dk), lambda b, j: (0, s_base + b, 0)),
            pl.BlockSpec((n_kv, VT_ROWS, seqs * seq_len), lambda b, j: (0, 0, s_base + b)),
        ]
        args += [k, vt]
    if has_ctx:
        in_specs += [
            pl.BlockSpec((n_kv, PAST_LEN, dk), lambda b, j: (0, b, 0)),
            pl.BlockSpec((n_kv, VT_ROWS, PAST_LEN), lambda b, j: (0, 0, b)),
        ]
        args += [kx, vx]
    if has_sink:
        in_specs.append(pl.BlockSpec(memory_space=pltpu.SMEM))
        args.append(sink)
    kern = functools.partial(_attn_kernel, n_kv=n_kv, group=group, tq=tq, seq_len=seq_len, seqs=seqs,
                             has_ctx=has_ctx, has_sink=has_sink, window=window,
                             q_unit=q_unit, key_chunk=key_chunk)
    return pl.pallas_call(
        kern, grid=(n_seq if window else n_seq // seqs, n_qt), in_specs=in_specs,
        out_specs=pl.BlockSpec((seqs * tq, n_q * HEAD_DIM), lambda b, j: (b * n_qt + j, 0)),
        out_shape=jax.ShapeDtypeStruct((n_seq * seq_len, n_q * HEAD_DIM), BF16),
        compiler_params=pltpu.CompilerParams(
            dimension_semantics=("arbitrary", "arbitrary"), vmem_limit_bytes=VMEM_LIMIT),
        name=name,
    )(*args)


def _ffn_kernel(*refs, n_o, is_ctx, final):
    halo = 0 if is_ctx else HALO
    it = iter(refs)
    x_ref = next(it)
    xh_refs = None if is_ctx else (next(it), next(it))
    o_refs, oh_refs = [], []
    for _ in range(n_o):
        o_refs.append(next(it))
        if not is_ctx:
            oh_refs.append((next(it), next(it)))
    wo_refs = [next(it) for _ in range(n_o)]
    mod_ref, gffn_ref = next(it), next(it)
    wg_refs = [next(it) for _ in range(FF_PER_STEP)]
    wv_refs = [next(it) for _ in range(FF_PER_STEP)]
    cw_ref, cb_ref = next(it), next(it)
    wd_refs = [next(it) for _ in range(FF_PER_STEP)]
    gfin_ref, out_ref, h2e_ref, acc_ref = next(it), next(it), next(it), next(it)

    m = pl.program_id(0)
    c = pl.program_id(1)
    g1 = mod_ref[2:3, :]
    sh2 = mod_ref[3:4, :]
    sc2 = mod_ref[4:5, :]
    g2 = mod_ref[5:6, :]
    sub = FFN_SUB if is_ctx else FFN_LAT_SUB
    n_sub = FFN_TM // sub
    sub_rows = sub + 2 * halo

    @pl.when(c == 0)
    def _():
        wos = [w[...].astype(BF16) for w in wo_refs]

        def residual_and_norm(xv, ovs):
            attn = _dot(ovs[0], wos[0])
            for ov, wo in zip(ovs[1:], wos[1:]):
                attn = attn + _dot(ov, wo)
            x1 = xv + g1 * attn
            return x1, (_rms(x1, gffn_ref[...]) * (1.0 + sc2) + sh2).astype(BF16)

        for r in range(0, FFN_TM, FFN_SUB):
            x1, h2 = residual_and_norm(x_ref[r:r + FFN_SUB, :], [o[r:r + FFN_SUB, :] for o in o_refs])
            acc_ref[r:r + FFN_SUB, :] = x1
            h2e_ref[halo + r:halo + r + FFN_SUB, :] = h2
        if not is_ctx:
            _, h2h = residual_and_norm(
                jnp.concatenate([xh_refs[0][...], xh_refs[1][...]], axis=0),
                [jnp.concatenate([oh[0][...], oh[1][...]], axis=0) for oh in oh_refs])
            h2e_ref[0:HALO, :] = h2h[0:HALO]
            h2e_ref[HALO + FFN_TM:, :] = h2h[HALO:]

    row8 = lax.broadcasted_iota(jnp.int32, (8, FF_CHUNK), 0)
    if not is_ctx:
        has_prev = m % FFN_LAT_TILES != 0
        has_next = m % FFN_LAT_TILES != FFN_LAT_TILES - 1

    def ff_chunks(n_chunks):
        w_up, w_dn, cw, cb = [], [], [], []
        for j in range(n_chunks):
            w_up.append(jnp.concatenate([wg_refs[j][...].astype(BF16), wv_refs[j][...].astype(BF16)],
                                        axis=1))
            w_dn.append(wd_refs[j][...].astype(BF16))
            cols = pl.ds(pl.multiple_of((c * FF_PER_STEP + j) * FF_CHUNK, FF_CHUNK), FF_CHUNK)
            cw.append(cw_ref[:, cols])
            cb.append(cb_ref[:, cols])
        tasks = [(j, r) for j in range(n_chunks) for r in range(n_sub)]
        ups = {}

        def emit_up(t):
            j, r = tasks[t]
            ups[t] = _dot(h2e_ref[r * sub:r * sub + sub_rows, :], w_up[j])

        for t in range(min(UP_LOOKAHEAD, len(tasks))):
            emit_up(t)
        for t, (j, r) in enumerate(tasks):
            if t + UP_LOOKAHEAD < len(tasks):
                emit_up(t + UP_LOOKAHEAD)
            up = ups.pop(t)
            ge = up[:, :FF_CHUNK]
            val = up[halo:halo + sub, FF_CHUNK:]
            g_prev = pltpu.roll(ge, 1, axis=0)[halo:halo + sub]
            g_next = pltpu.roll(ge, sub_rows - 1, axis=0)[halo:halo + sub]
            prev_ok = False if is_ctx else (has_prev if r == 0 else True)
            next_ok = False if is_ctx else (has_next if r == n_sub - 1 else True)
            if prev_ok is not True:
                g_prev = jnp.concatenate(
                    [jnp.where(jnp.logical_or(row8 != 0, prev_ok), g_prev[0:8], 0.0), g_prev[8:]], axis=0)
            if next_ok is not True:
                g_next = jnp.concatenate(
                    [g_next[:-8], jnp.where(jnp.logical_or(row8 != 7, next_ok), g_next[-8:], 0.0)], axis=0)
            gate = (g_prev * cw[j][0:1, :] + ge[halo:halo + sub] * cw[j][1:2, :] + g_next * cw[j][2:3, :]
                    + cb[j])
            act = (gate * jax.nn.sigmoid(gate) * val).astype(BF16)
            acc_ref[r * sub:(r + 1) * sub, :] += g2 * _dot(act, w_dn[j])

    n_tail = N_FF_CHUNKS % FF_PER_STEP
    if n_tail:
        pl.when(c < N_FF_STEPS - 1)(functools.partial(ff_chunks, FF_PER_STEP))
    else:
        ff_chunks(FF_PER_STEP)

    @pl.when(c == N_FF_STEPS - 1)
    def _():
        if n_tail:
            ff_chunks(n_tail)
        x2 = acc_ref[...]
        out_ref[...] = _rms(x2, gfin_ref[...]) if final else x2


def _ffn(x, os, mod, layer, g_ffn, w_out, w_up, conv_w, conv_b, w_down, g_final, *, is_ctx, final):
    n_rows = x.shape[0]
    nh = FFN_TM // HALO
    nblk = n_rows // HALO
    n_o = len(os)
    halo = 0 if is_ctx else HALO
    tile = lambda m, c: (m, 0)
    prev = lambda m, c: (jnp.maximum(m * nh - 1, 0), 0)
    nxt = lambda m, c: (jnp.minimum((m + 1) * nh, nblk - 1), 0)
    chunk = lambda c, j: jnp.minimum(c * FF_PER_STEP + j, N_FF_CHUNKS - 1)
    if is_ctx:
        cond = lambda m: layer * 8
    else:
        cond = lambda m: layer * 8 + 1 + m // FFN_LAT_TILES

    def with_halo(arr):
        w = arr.shape[1]
        specs = [pl.BlockSpec((FFN_TM, w), tile)]
        if not is_ctx:
            specs += [pl.BlockSpec((HALO, w), prev), pl.BlockSpec((HALO, w), nxt)]
        return specs, [arr] * len(specs)

    in_specs, args = with_halo(x)
    for o in os:
        specs, arrs = with_halo(o)
        in_specs += specs
        args += arrs
    w_rows = D_MODEL // n_o
    for t in range(n_o):
        in_specs.append(pl.BlockSpec((None, w_rows, D_MODEL), lambda m, c, t=t: (0, t, 0)))
        args.append(w_out)
    in_specs += [
        pl.BlockSpec((None, 6, D_MODEL), lambda m, c: (cond(m), 0, 0)),
        pl.BlockSpec((None, 1, D_MODEL), lambda m, c: (layer, 0, 0)),
    ]
    args += [mod, g_ffn]
    steps = range(FF_PER_STEP)
    in_specs += [pl.BlockSpec((None, D_MODEL, FF_CHUNK), lambda m, c, j=j: (layer, 0, chunk(c, j))) for j in steps]
    in_specs += [pl.BlockSpec((None, D_MODEL, FF_CHUNK), lambda m, c, j=j: (layer, 0, N_FF_CHUNKS + chunk(c, j)))
                 for j in steps]
    in_specs += [pl.BlockSpec((None, 3, D_FF), lambda m, c: (layer, 0, 0)),
                 pl.BlockSpec((None, 1, D_FF), lambda m, c: (layer, 0, 0))]
    in_specs += [pl.BlockSpec((None, FF_CHUNK, D_MODEL), lambda m, c, j=j: (layer, chunk(c, j), 0)) for j in steps]
    args += [w_up] * (2 * FF_PER_STEP) + [conv_w, conv_b] + [w_down] * FF_PER_STEP
    in_specs.append(pl.BlockSpec((1, D_MODEL), lambda m, c: (0, 0)))
    args.append(g_final)
    return pl.pallas_call(
        functools.partial(_ffn_kernel, n_o=n_o, is_ctx=is_ctx, final=final),
        grid=(n_rows // FFN_TM, N_FF_STEPS), in_specs=in_specs,
        out_specs=pl.BlockSpec((FFN_TM, D_MODEL), tile),
        out_shape=jax.ShapeDtypeStruct((n_rows, D_MODEL), F32),
        scratch_shapes=[pltpu.VMEM((FFN_TM + 2 * halo, D_MODEL), BF16), pltpu.VMEM((FFN_TM, D_MODEL), F32)],
        compiler_params=pltpu.CompilerParams(
            dimension_semantics=("arbitrary", "arbitrary"), vmem_limit_bytes=FFN_VMEM_LIMIT),
        name=("ffn_ctx" if is_ctx else "ffn_lat") + ("_final" if final else ""),
    )(*args)


def _rope_tables(rot_dim):
    f32 = np.float32
    t = np.arange(LAT_LEN)
    row = (t // GRID_W).astype(f32)
    col = (t % GRID_W).astype(f32)
    d_axis = rot_dim // 2
    freqs = (f32(ROPE_THETA) ** (-np.arange(0, d_axis, 2, dtype=f32) / f32(d_axis))).astype(f32)
    ang = np.concatenate([row[:, None] * freqs, col[:, None] * freqs], axis=-1)
    cos = np.repeat(np.cos(ang), 2, axis=-1).astype(f32)
    sin = (np.repeat(np.sin(ang), 2, axis=-1) * np.tile(np.array([-1.0, 1.0], f32), rot_dim // 2)).astype(f32)
    if rot_dim == HEAD_DIM:
        cos = np.tile(cos, (1, 2))
        sin = np.tile(sin, (1, 2))
    else:
        cos = np.concatenate([np.ones((LAT_LEN, QK_NOPE), f32), cos,
                              np.ones((LAT_LEN, 128 - QK_NOPE - rot_dim), f32)], axis=-1)
        sin = np.concatenate([np.zeros((LAT_LEN, QK_NOPE), f32), sin,
                              np.zeros((LAT_LEN, 128 - QK_NOPE - rot_dim), f32)], axis=-1)
    cos = np.concatenate([cos, np.ones((TM, 128), f32)], axis=0)
    sin = np.concatenate([sin, np.zeros((TM, 128), f32)], axis=0)
    return jnp.asarray(cos), jnp.asarray(sin)


def kernel(x_prompt, x_sample, cache_a_k, cache_a_v, cache_b_ckv, cache_b_kpe, cache_c_k, cache_c_v, c, c_ctx, w_mod, b_mod, g_mix_norm, g_ffn_norm, w_in_e, g_qnorm_a, g_knorm_a, g_cq_b, w_uq_b, g_ckv_b, w_ukv_b, w_out_e, w_in_o, sink_c, w_out_o, w_up, conv_w, conv_b, w_down, g_final):
    depth = w_mod.shape[0]
    cond8 = jnp.concatenate([c_ctx[None, :], c, jnp.zeros((5, D_MODEL), F32)], axis=0)
    lane_pad = KB_PAD - QK_NOPE - QK_ROPE
    wuq = jnp.pad(w_uq_b[0].reshape(Q_LORA, N_HEADS_B, QK_NOPE + QK_ROPE),
                  ((0, 0), (0, 0), (0, lane_pad))).reshape(Q_LORA, N_HEADS_B * KB_PAD)
    wukv3 = w_ukv_b[0].reshape(KV_LORA, N_HEADS_B, QK_NOPE + V_DIM_B)
    wukk = jnp.pad(wukv3[:, :, :QK_NOPE], ((0, 0), (0, 0), (0, KB_PAD - QK_NOPE))
                   ).reshape(KV_LORA, N_HEADS_B * KB_PAD)
    wukv = wukv3[:, :, QK_NOPE:].reshape(KV_LORA, N_HEADS_B * V_DIM_B)
    gq = jnp.tile(g_qnorm_a[0], N_HEADS_A)[None, :]
    gk = jnp.tile(g_knorm_a[0], N_KV_A)[None, :]
    seg = np.arange(256) // HEAD_DIM
    ones = jnp.asarray(seg[:, None] == seg[None, :], dtype=BF16)
    cosa, sina = _rope_tables(HEAD_DIM)
    cosb, sinb = _rope_tables(QK_ROPE)
    g_mix3 = g_mix_norm.reshape(depth, 1, D_MODEL)
    g_ffn3 = g_ffn_norm.reshape(depth, 1, D_MODEL)
    conv_b3 = conv_b.reshape(depth, 1, D_FF)
    g_fin2 = g_final[None, :]

    mod = _modulation(cond8, w_mod, b_mod).reshape(depth * 8, 6, D_MODEL)

    n_past = N_LAT_SEQ * PAST_LEN
    kpe_pad = jnp.pad(cache_b_kpe.reshape(n_past, QK_ROPE), ((0, 0), (QK_NOPE, lane_pad)))
    ka_c, va_c, kb_c, vb_c, kc_c, vc_c = _ctx_prep(
        cache_a_k.reshape(n_past, 128), cache_a_v.reshape(n_past, 128),
        cache_b_ckv.reshape(n_past, KV_LORA), kpe_pad,
        cache_c_k.reshape(n_past, 128), cache_c_v.reshape(n_past, 128), wukk, wukv)

    xp = x_prompt.reshape(N_CTX_TOK, D_MODEL)
    xs = x_sample.reshape(N_TOK - N_CTX_TOK, D_MODEL)
    qa, ka, va, qb, kb, vb, nak, nav, nckv, nkpe = _pre0(
        xp, xs, mod, g_mix3, w_in_e, gq, gk, g_cq_b.reshape(1, 1, Q_LORA), g_ckv_b.reshape(1, 1, KV_LORA),
        wuq, wukk, wukv, cosa, sina, cosb, sinb, ones)
    ffn_w = (w_up, conv_w, conv_b3, w_down, g_fin2)
    ctx_kw = dict(n_seq=N_CTX_SEQ, seq_len=CTX_LEN, tok_base=0, tq=CTX_LEN, window=0, q_unit=256,
                  key_chunk=CTX_LEN, seqs=8)
    lat_kw = dict(n_seq=N_LAT_SEQ, seq_len=LAT_LEN, tok_base=N_CTX_TOK, q_unit=256)
    dense_kw = dict(tq=256, window=0, key_chunk=256, **lat_kw)
    oa = (_attention(qa, ka, va, None, None, None, name="attn_a_ctx", **ctx_kw),
          _attention(qa, ka, va, ka_c, va_c, None, name="attn_a_lat", **dense_kw))
    ob = (_attention(qb, kb, vb, None, None, None, name="attn_b_ctx", **ctx_kw),
          _attention(qb, kb, vb, kb_c, vb_c, None, name="attn_b_lat", **dense_kw))
    xp1 = _ffn(xp, [oa[0], ob[0]], mod, 0, g_ffn3, w_out_e, *ffn_w, is_ctx=True, final=False)
    xs1 = _ffn(xs, [oa[1], ob[1]], mod, 0, g_ffn3, w_out_e, *ffn_w, is_ctx=False, final=False)

    qc, kc, vc, nck, ncv = _pre1(xp1, xs1, mod, g_mix3, w_in_o, cosa, sina)
    sink = sink_c[0]
    oc = (_attention(qc, kc, vc, None, None, sink, name="attn_c_ctx", **ctx_kw),
          _attention(qc, kc, vc, kc_c, vc_c, sink, tq=128, window=WINDOW, key_chunk=PAST_LEN, seqs=4,
                     name="attn_c_lat", **lat_kw))
    y_prompt = _ffn(xp1, [oc[0]], mod, 1, g_ffn3, w_out_o, *ffn_w, is_ctx=True, final=True)
    y_sample = _ffn(xs1, [oc[1]], mod, 1, g_ffn3, w_out_o, *ffn_w, is_ctx=False, final=True)

    return (y_prompt.reshape(N_CTX_SEQ, CTX_LEN, D_MODEL), y_sample.reshape(N_LAT_SEQ, LAT_LEN, D_MODEL),
            nak.reshape(N_CTX_SEQ, 1, CTX_LEN, N_KV_A, HEAD_DIM),
            nav.reshape(N_CTX_SEQ, 1, CTX_LEN, N_KV_A, HEAD_DIM),
            nckv.reshape(N_CTX_SEQ, 1, CTX_LEN, KV_LORA),
            nkpe.reshape(N_CTX_SEQ, 1, CTX_LEN, QK_ROPE),
            nck.reshape(N_CTX_SEQ, 1, CTX_LEN, N_KV_C, HEAD_DIM),
            ncv.reshape(N_CTX_SEQ, 1, CTX_LEN, N_KV_C, HEAD_DIM))
```

```python
import functools

import jax
import jax.numpy as jnp
import numpy as np
from jax import lax
from jax.experimental import pallas as pl
from jax.experimental.pallas import tpu as pltpu

F32 = jnp.float32
BF16 = jnp.bfloat16

D_MODEL = 1024
N_CTX_SEQ = 16
CTX_LEN = 256
N_LAT_SEQ = 2
LAT_LEN = 2048
PAST_LEN = 512
GRID_W = 64
ROPE_THETA = 10000.0
NORM_EPS = 1e-6
WINDOW = 128
NEG_INF = -1e30
LOG2E = 1.4426950408889634
HEAD_DIM = 64
N_HEADS_A, N_KV_A = 8, 2
N_HEADS_B = 8
Q_LORA, KV_LORA = 384, 256
QK_NOPE, QK_ROPE, V_DIM_B = 64, 32, 64
N_HEADS_C, N_KV_C = 16, 2
D_FF = 2816
IN_E_MAIN = N_HEADS_A * HEAD_DIM + 2 * N_KV_A * HEAD_DIM + Q_LORA + KV_LORA

N_CTX_TOK = N_CTX_SEQ * CTX_LEN
N_TOK = N_CTX_TOK + N_LAT_SEQ * LAT_LEN
TM = 512
PRE_SUB = 256
N_TILES = N_TOK // TM
N_CTX_TILES = N_CTX_TOK // TM
LAT_TILES = LAT_LEN // TM
HALO = 16
FFN_TM = 1024
FFN_SUB = CTX_LEN
FFN_LAT_SUB = 256
UP_LOOKAHEAD = 3
FFN_LAT_TILES = LAT_LEN // FFN_TM
FF_CHUNK = 256
N_FF_CHUNKS = D_FF // FF_CHUNK
FF_PER_STEP = 3
N_FF_STEPS = -(-N_FF_CHUNKS // FF_PER_STEP)
KB_PAD = 128
VT_ROWS = 80
MOD_SLAB = 256
SCORE_LOOKAHEAD = 5
VMEM_LIMIT = 56 * 1024 * 1024
FFN_VMEM_LIMIT = 60 * 1024 * 1024


def _dot(a, b):
    return jnp.dot(a, b, preferred_element_type=F32)


def _dot_nt(a, b):
    return lax.dot_general(a, b, (((1,), (1,)), ((), ())), preferred_element_type=F32)


def _rms(x, g):
    return x * lax.rsqrt(jnp.mean(x * x, axis=-1, keepdims=True) + NORM_EPS) * g


def _split_bf16(x):
    hi = x.astype(BF16)
    return hi, (x - hi.astype(F32)).astype(BF16)


def _head_rms(x, g, ones_ref):
    w = x.shape[1]
    hi, lo = _split_bf16(x * x)
    parts = []
    for c in range(0, w, 256):
        cw = min(256, w - c)
        ones = ones_ref[0:cw, 0:cw]
        parts.append(_dot(hi[:, c:c + cw], ones) + _dot(lo[:, c:c + cw], ones))
    ssum = parts[0] if len(parts) == 1 else jnp.concatenate(parts, axis=1)
    return x * lax.rsqrt(ssum * (1.0 / HEAD_DIM) + NORM_EPS) * g


def _swap_pairs(x):
    w = x.shape[1]
    up = pltpu.roll(x, w - 1, axis=1)
    dn = pltpu.roll(x, 1, axis=1)
    lane = lax.broadcasted_iota(jnp.int32, x.shape, 1)
    return jnp.where((lane & 1) == 0, up, dn)


def _rope(x, cos, sin_signed, reps):
    if reps > 1:
        cos = jnp.concatenate([cos] * reps, axis=1)
        sin_signed = jnp.concatenate([sin_signed] * reps, axis=1)
    return x * cos + _swap_pairs(x) * sin_signed


def _store_vt(vt_ref, v, n_heads, cols=slice(None)):
    t = v.shape[0]
    vt = v.T.astype(BF16)
    ones = jnp.ones((VT_ROWS - 64, t), BF16)
    for hh in range(n_heads):
        vt_ref[hh, 0:64, cols] = vt[hh * 64:(hh + 1) * 64]
        vt_ref[hh, 64:VT_ROWS, cols] = ones


def _cond_row(i):
    return jnp.where(i < N_CTX_TILES, 0, 1 + (i - N_CTX_TILES) // LAT_TILES)


def _rope_block(i):
    return jnp.where(i < N_CTX_TILES, LAT_TILES, (i - N_CTX_TILES) % LAT_TILES)


def _const_spec(shape):
    zeros = (0,) * len(shape)
    return pl.BlockSpec(shape, lambda *_: zeros)


def _layer_spec(shape, layer):
    idx = (layer,) + (0,) * len(shape)
    return pl.BlockSpec((None,) + tuple(shape), lambda *_: idx)


def _mod_kernel(cond_ref, *refs):
    *w_refs, b_ref, o_ref = refs
    c = cond_ref[...]
    s_hi, s_lo = _split_bf16(c * jax.nn.sigmoid(c))
    s_both = jnp.concatenate([s_hi, s_lo], axis=0)
    out = b_ref[0]
    for t, w_ref in enumerate(w_refs):
        rows = slice(t * MOD_SLAB, (t + 1) * MOD_SLAB)
        w_hi, w_lo = _split_bf16(w_ref[0])
        r = _dot(s_both[:, rows], w_hi)
        out = out + r[0:8] + r[8:16] + _dot(s_hi[:, rows], w_lo)
    o_ref[0] = out


def _modulation(cond8, w_mod, b_mod):
    depth, _, n = w_mod.shape
    tn = 3072
    n_slabs = D_MODEL // MOD_SLAB
    return pl.pallas_call(
        _mod_kernel,
        grid=(depth, n // tn),
        in_specs=[pl.BlockSpec((8, D_MODEL), lambda l, j: (0, 0))]
        + [pl.BlockSpec((1, MOD_SLAB, tn), lambda l, j, t=t: (l, t, j)) for t in range(n_slabs)]
        + [pl.BlockSpec((1, 1, tn), lambda l, j: (l, 0, j))],
        out_specs=pl.BlockSpec((1, 8, tn), lambda l, j: (l, 0, j)),
        out_shape=jax.ShapeDtypeStruct((depth, 8, n), F32),
        compiler_params=pltpu.CompilerParams(
            dimension_semantics=("arbitrary", "arbitrary"), vmem_limit_bytes=VMEM_LIMIT),
        name="modulation",
    )(cond8, *([w_mod] * n_slabs), b_mod.reshape(depth, 1, n))


def _pre0_kernel(xp_ref, xs_ref, mod_ref, gmix_ref, win_ref, gq_ref, gk_ref, gcq_ref, gckv_ref,
                 wuq_ref, wukk_ref, wukv_ref, cosa_ref, sina_ref, cosb_ref, sinb_ref, ones_ref,
                 qa_ref, ka_ref, va_ref, qb_ref, kb_ref, vb_ref,
                 nak_ref, nav_ref, nckv_ref, nkpe_ref,
                 win_s, wuq_s, wukk_s, wukv_s):
    i = pl.program_id(0)

    @pl.when(i == 0)
    def _():
        win_s[:, 0:IN_E_MAIN] = win_ref[:, 0:IN_E_MAIN].astype(BF16)
        win_s[:, IN_E_MAIN:] = jnp.concatenate(
            [jnp.zeros((D_MODEL, QK_NOPE), F32), win_ref[:, IN_E_MAIN:],
             jnp.zeros((D_MODEL, KB_PAD - QK_NOPE - QK_ROPE), F32)], axis=1).astype(BF16)
        wuq_s[...] = wuq_ref[...].astype(BF16)
        wukk_s[...] = wukk_ref[...].astype(BF16)
        wukv_s[...] = wukv_ref[...].astype(BF16)

    sh1 = mod_ref[0:1, :]
    sc1 = mod_ref[1:2, :]
    subs = [slice(r, r + PRE_SUB) for r in range(0, TM, PRE_SUB)]

    projs = []
    for sl in subs:
        x = jnp.where(i < N_CTX_TILES, xp_ref[sl, :], xs_ref[sl, :])
        h = _rms(x, gmix_ref[...]) * (1.0 + sc1) + sh1
        projs.append(_dot(h.astype(BF16), win_s[...]))

    parts = []
    for sl, proj in zip(subs, projs):
        qa = _head_rms(proj[:, 0:512], gq_ref[...], ones_ref)
        ka = _head_rms(proj[:, 512:640], gk_ref[...], ones_ref)
        va = proj[:, 640:768]
        cq = _rms(proj[:, 768:1152], gcq_ref[...])
        ckv = _rms(proj[:, 1152:1408], gckv_ref[...])
        kpe = proj[:, 1408:1536]
        qb = _dot(cq.astype(BF16), wuq_s[...])
        ckv_b = ckv.astype(BF16)
        kbn = _dot(ckv_b, wukk_s[...])
        vb = _dot(ckv_b, wukv_s[...])
        _store_vt(va_ref, va, N_KV_A, sl)
        _store_vt(vb_ref, vb, N_HEADS_B, sl)
        parts.append((ka, va, ckv, kpe))
        cosa, sina = cosa_ref[sl, :], sina_ref[sl, :]
        cosb, sinb = cosb_ref[sl, :], sinb_ref[sl, :]
        qa_s = _rope(qa, cosa, sina, 4) * (LOG2E * HEAD_DIM ** -0.5)
        ka_r = _rope(ka, cosa, sina, 1)
        qb_s = _rope(qb, cosb, sinb, 8) * (LOG2E * (QK_NOPE + QK_ROPE) ** -0.5)
        kpe_r = _rope(kpe, cosb, sinb, 1)
        for hh in range(N_HEADS_A):
            qa_ref[hh, sl, :] = qa_s[:, hh * 64:(hh + 1) * 64].astype(BF16)
        for hh in range(N_KV_A):
            ka_ref[hh, sl, :] = ka_r[:, hh * 64:(hh + 1) * 64].astype(BF16)
        for hh in range(N_HEADS_B):
            qb_ref[hh, sl, :] = qb_s[:, hh * KB_PAD:(hh + 1) * KB_PAD].astype(BF16)
            kb_ref[hh, sl, :] = (kbn[:, hh * KB_PAD:(hh + 1) * KB_PAD] + kpe_r).astype(BF16)

    @pl.when(i < N_CTX_TILES)
    def _():
        for sl, (ka, va, ckv, kpe) in zip(subs, parts):
            nak_ref[sl, :] = ka
            nav_ref[sl, :] = va
            nckv_ref[sl, :] = ckv
            nkpe_ref[sl, :] = kpe[:, QK_NOPE:QK_NOPE + QK_ROPE]


def _pre0(xp, xs, mod, gmix, w_in_e, gq, gk, gcq, gckv, wuq, wukk, wukv, cosa, sina, cosb, sinb, ones):
    tile = lambda i: (i, 0)
    head_tile = lambda i: (0, i, 0)
    vt_tile = lambda i: (0, 0, i)
    ctx_tile = lambda i: (jnp.minimum(i, N_CTX_TILES - 1), 0)
    lat_tile = lambda i: (jnp.maximum(i - N_CTX_TILES, 0), 0)
    rope_tile = lambda i: (_rope_block(i), 0)
    in_specs = [
        pl.BlockSpec((TM, D_MODEL), ctx_tile),
        pl.BlockSpec((TM, D_MODEL), lat_tile),
        pl.BlockSpec((None, 6, D_MODEL), lambda i: (_cond_row(i), 0, 0)),
        _layer_spec((1, D_MODEL), 0),
        _layer_spec(w_in_e.shape[1:], 0),
        _const_spec(gq.shape), _const_spec(gk.shape), _layer_spec((1, Q_LORA), 0), _layer_spec((1, KV_LORA), 0),
        _const_spec(wuq.shape), _const_spec(wukk.shape), _const_spec(wukv.shape),
        pl.BlockSpec((TM, 128), rope_tile), pl.BlockSpec((TM, 128), rope_tile),
        pl.BlockSpec((TM, 128), rope_tile), pl.BlockSpec((TM, 128), rope_tile),
        _const_spec(ones.shape),
    ]
    out_shape = [
        jax.ShapeDtypeStruct((N_HEADS_A, N_TOK, 64), BF16),
        jax.ShapeDtypeStruct((N_KV_A, N_TOK, 64), BF16),
        jax.ShapeDtypeStruct((N_KV_A, VT_ROWS, N_TOK), BF16),
        jax.ShapeDtypeStruct((N_HEADS_B, N_TOK, KB_PAD), BF16),
        jax.ShapeDtypeStruct((N_HEADS_B, N_TOK, KB_PAD), BF16),
        jax.ShapeDtypeStruct((N_HEADS_B, VT_ROWS, N_TOK), BF16),
        jax.ShapeDtypeStruct((N_CTX_TOK, 128), F32),
        jax.ShapeDtypeStruct((N_CTX_TOK, 128), F32),
        jax.ShapeDtypeStruct((N_CTX_TOK, KV_LORA), F32),
        jax.ShapeDtypeStruct((N_CTX_TOK, QK_ROPE), F32),
    ]
    out_specs = [
        pl.BlockSpec((N_HEADS_A, TM, 64), head_tile),
        pl.BlockSpec((N_KV_A, TM, 64), head_tile),
        pl.BlockSpec((N_KV_A, VT_ROWS, TM), vt_tile),
        pl.BlockSpec((N_HEADS_B, TM, KB_PAD), head_tile),
        pl.BlockSpec((N_HEADS_B, TM, KB_PAD), head_tile),
        pl.BlockSpec((N_HEADS_B, VT_ROWS, TM), vt_tile),
        pl.BlockSpec((TM, 128), ctx_tile),
        pl.BlockSpec((TM, 128), ctx_tile),
        pl.BlockSpec((TM, KV_LORA), ctx_tile),
        pl.BlockSpec((TM, QK_ROPE), ctx_tile),
    ]
    scratch = [
        pltpu.VMEM((D_MODEL, IN_E_MAIN + KB_PAD), BF16),
        pltpu.VMEM(wuq.shape, BF16), pltpu.VMEM(wukk.shape, BF16), pltpu.VMEM(wukv.shape, BF16),
    ]
    return pl.pallas_call(
        _pre0_kernel, grid=(N_TILES,), in_specs=in_specs, out_specs=out_specs, out_shape=out_shape,
        scratch_shapes=scratch,
        compiler_params=pltpu.CompilerParams(
            dimension_semantics=("arbitrary",), vmem_limit_bytes=VMEM_LIMIT),
        name="pre0",
    )(xp, xs, mod, gmix, w_in_e, gq, gk, gcq, gckv, wuq, wukk, wukv, cosa, sina, cosb, sinb, ones)


def _pre1_kernel(xp_ref, xs_ref, mod_ref, gmix_ref, win_ref, cosa_ref, sina_ref,
                 q_ref, k_ref, v_ref, nk_ref, nv_ref, win_s):
    i = pl.program_id(0)

    @pl.when(i == 0)
    def _():
        win_s[...] = win_ref[...].astype(BF16)

    sh1 = mod_ref[0:1, :]
    sc1 = mod_ref[1:2, :]
    subs = [slice(r, r + PRE_SUB) for r in range(0, TM, PRE_SUB)]
    projs = []
    for sl in subs:
        x = jnp.where(i < N_CTX_TILES, xp_ref[sl, :], xs_ref[sl, :])
        h = _rms(x, gmix_ref[...]) * (1.0 + sc1) + sh1
        projs.append(_dot(h.astype(BF16), win_s[...]))
    for sl, proj in zip(subs, projs):
        cosa, sina = cosa_ref[sl, :], sina_ref[sl, :]
        q_s = _rope(proj[:, 0:1024], cosa, sina, 8) * (LOG2E * HEAD_DIM ** -0.5)
        k = _rope(proj[:, 1024:1152], cosa, sina, 1)
        for hh in range(N_HEADS_C):
            q_ref[hh, sl, :] = q_s[:, hh * 64:(hh + 1) * 64].astype(BF16)
        for hh in range(N_KV_C):
            k_ref[hh, sl, :] = k[:, hh * 64:(hh + 1) * 64].astype(BF16)
        _store_vt(v_ref, proj[:, 1152:1280], N_KV_C, sl)

    @pl.when(i < N_CTX_TILES)
    def _():
        for sl, proj in zip(subs, projs):
            nk_ref[sl, :] = proj[:, 1024:1152]
            nv_ref[sl, :] = proj[:, 1152:1280]


def _pre1(xp, xs, mod, gmix, w_in_o, cosa, sina):
    n_in = w_in_o.shape[-1]
    head_tile = lambda i: (0, i, 0)
    vt_tile = lambda i: (0, 0, i)
    ctx_tile = lambda i: (jnp.minimum(i, N_CTX_TILES - 1), 0)
    lat_tile = lambda i: (jnp.maximum(i - N_CTX_TILES, 0), 0)
    rope_tile = lambda i: (_rope_block(i), 0)
    in_specs = [
        pl.BlockSpec((TM, D_MODEL), ctx_tile),
        pl.BlockSpec((TM, D_MODEL), lat_tile),
        pl.BlockSpec((None, 6, D_MODEL), lambda i: (8 + _cond_row(i), 0, 0)),
        _layer_spec((1, D_MODEL), 1),
        _layer_spec((D_MODEL, n_in), 0),
        pl.BlockSpec((TM, 128), rope_tile), pl.BlockSpec((TM, 128), rope_tile),
    ]
    out_shape = [
        jax.ShapeDtypeStruct((N_HEADS_C, N_TOK, 64), BF16),
        jax.ShapeDtypeStruct((N_KV_C, N_TOK, 64), BF16),
        jax.ShapeDtypeStruct((N_KV_C, VT_ROWS, N_TOK), BF16),
        jax.ShapeDtypeStruct((N_CTX_TOK, 128), F32),
        jax.ShapeDtypeStruct((N_CTX_TOK, 128), F32),
    ]
    out_specs = [
        pl.BlockSpec((N_HEADS_C, TM, 64), head_tile),
        pl.BlockSpec((N_KV_C, TM, 64), head_tile),
        pl.BlockSpec((N_KV_C, VT_ROWS, TM), vt_tile),
        pl.BlockSpec((TM, 128), ctx_tile),
        pl.BlockSpec((TM, 128), ctx_tile),
    ]
    return pl.pallas_call(
        _pre1_kernel, grid=(N_TILES,), in_specs=in_specs, out_specs=out_specs, out_shape=out_shape,
        scratch_shapes=[pltpu.VMEM((D_MODEL, n_in), BF16)],
        compiler_params=pltpu.CompilerParams(
            dimension_semantics=("arbitrary",), vmem_limit_bytes=VMEM_LIMIT),
        name="pre1",
    )(xp, xs, mod, gmix, w_in_o, cosa, sina)


def _ctx_kernel(ak_ref, av_ref, ckv_ref, kpe_ref, ck_ref, cv_ref, wukk_ref, wukv_ref,
                ka_ref, va_ref, kb_ref, vb_ref, kc_ref, vc_ref):
    ak, av, ck, cv = ak_ref[...], av_ref[...], ck_ref[...], cv_ref[...]
    for hh in range(2):
        sl = slice(hh * 64, (hh + 1) * 64)
        ka_ref[hh] = ak[:, sl].astype(BF16)
        kc_ref[hh] = ck[:, sl].astype(BF16)
    _store_vt(va_ref, av, 2)
    _store_vt(vc_ref, cv, 2)
    ckv_b = ckv_ref[...].astype(BF16)
    kbn = _dot(ckv_b, wukk_ref[...].astype(BF16))
    vb = _dot(ckv_b, wukv_ref[...].astype(BF16))
    kpe = kpe_ref[...]
    for hh in range(N_HEADS_B):
        kb_ref[hh] = (kbn[:, hh * KB_PAD:(hh + 1) * KB_PAD] + kpe).astype(BF16)
    _store_vt(vb_ref, vb, N_HEADS_B)


def _ctx_prep(ak, av, ckv, kpe_pad, ck, cv, wukk, wukv):
    n = N_LAT_SEQ * PAST_LEN
    row = lambda b: (b, 0)
    head_row = lambda b: (0, b, 0)
    vt_row = lambda b: (0, 0, b)
    in_specs = [
        pl.BlockSpec((PAST_LEN, 128), row), pl.BlockSpec((PAST_LEN, 128), row),
        pl.BlockSpec((PAST_LEN, KV_LORA), row), pl.BlockSpec((PAST_LEN, KB_PAD), row),
        pl.BlockSpec((PAST_LEN, 128), row), pl.BlockSpec((PAST_LEN, 128), row),
        _const_spec(wukk.shape), _const_spec(wukv.shape),
    ]
    out_shape = [
        jax.ShapeDtypeStruct((2, n, 64), BF16), jax.ShapeDtypeStruct((2, VT_ROWS, n), BF16),
        jax.ShapeDtypeStruct((N_HEADS_B, n, KB_PAD), BF16), jax.ShapeDtypeStruct((N_HEADS_B, VT_ROWS, n), BF16),
        jax.ShapeDtypeStruct((2, n, 64), BF16), jax.ShapeDtypeStruct((2, VT_ROWS, n), BF16),
    ]
    out_specs = [
        pl.BlockSpec((2, PAST_LEN, 64), head_row), pl.BlockSpec((2, VT_ROWS, PAST_LEN), vt_row),
        pl.BlockSpec((N_HEADS_B, PAST_LEN, KB_PAD), head_row), pl.BlockSpec((N_HEADS_B, VT_ROWS, PAST_LEN), vt_row),
        pl.BlockSpec((2, PAST_LEN, 64), head_row), pl.BlockSpec((2, VT_ROWS, PAST_LEN), vt_row),
    ]
    return pl.pallas_call(
        _ctx_kernel, grid=(N_LAT_SEQ,), in_specs=in_specs, out_specs=out_specs, out_shape=out_shape,
        compiler_params=pltpu.CompilerParams(
            dimension_semantics=("arbitrary",), vmem_limit_bytes=VMEM_LIMIT),
        name="ctx_prep",
    )(ak, av, ckv, kpe_pad, ck, cv, wukk, wukv)


def _softmax_units(units, lookahead):
    tasks = [(u, c) for u, unit in enumerate(units) for c in range(len(unit["chunks"]))]
    scores = {}

    def emit_scores(t):
        u, c = tasks[t]
        k, _, mask = units[u]["chunks"][c]
        s = _dot_nt(k, units[u]["q"])
        scores[t] = s if mask is None else jnp.where(mask, s, NEG_INF)

    for t in range(min(lookahead, len(tasks))):
        emit_scores(t)
    for t, (u, c) in enumerate(tasks):
        if t + lookahead < len(tasks):
            emit_scores(t + lookahead)
        unit = units[u]
        s = scores.pop(t)
        m, acc = unit["m"], unit["acc"]
        cmax = jnp.max(s, axis=0, keepdims=True)
        m_new = cmax if m is None else jnp.maximum(m, cmax)
        pv = _dot(unit["chunks"][c][1], jnp.exp2(s - m_new).astype(BF16))
        unit["acc"] = pv if acc is None else acc * jnp.exp2(m - m_new) + pv
        unit["m"] = m_new
    return [unit["acc"][0:64] * (1.0 / unit["acc"][64:65]) for unit in units]


def _attn_kernel(*refs, n_kv, group, tq, seq_len, seqs, has_ctx, has_sink, window, q_unit, key_chunk):
    refs = list(refs)
    q_ref = refs.pop(0)
    if window:
        n_blk = seqs + 2
        kb_refs, vb_refs = refs[:n_blk], refs[n_blk:2 * n_blk]
        refs = refs[2 * n_blk:]
    else:
        k_ref, vt_ref = refs[:2]
        refs = refs[2:]
    if has_ctx:
        kx_ref, vx_ref = refs[:2]
        refs = refs[2:]
    if has_sink:
        sink_ref = refs.pop(0)
    o_ref = refs.pop(0)

    j = pl.program_id(1)
    dk = q_ref.shape[-1]
    heads_per_unit = q_unit // tq
    lane = lax.broadcasted_iota(jnp.int32, (1, q_unit), 1)
    if window:
        n_band = tq + 2 * window
        krow = lax.broadcasted_iota(jnp.int32, (n_band, q_unit), 0)
        qcol = lax.broadcasted_iota(jnp.int32, (n_band, q_unit), 1) & (tq - 1)
        rel = (krow - window) - qcol
        in_band = jnp.abs(rel) <= window
        last_j = seq_len // (tq * seqs) - 1
        band_masks = []
        for sq in range(seqs):
            mask = in_band
            if sq == 0:
                mask = mask & ((krow >= window) | (j > 0))
            if sq == seqs - 1:
                mask = mask & ((krow < window + tq) | (j < last_j))
            band_masks.append(mask)
    if has_sink:
        acc0 = jnp.where(lax.broadcasted_iota(jnp.int32, (VT_ROWS, q_unit), 0) >= 64, 1.0, 0.0)

    units = []
    for sq in range(seqs):
        base = sq * (tq if window else seq_len)
        for hk in range(n_kv):
            chunks = []
            if window:
                chunks.append((jnp.concatenate([r[hk] for r in kb_refs[sq:sq + 3]], axis=0),
                               jnp.concatenate([r[hk] for r in vb_refs[sq:sq + 3]], axis=1), band_masks[sq]))
            else:
                for c in range(base, base + seq_len, key_chunk):
                    n = min(key_chunk, base + seq_len - c)
                    chunks.append((k_ref[hk, c:c + n, :], vt_ref[hk, :, c:c + n], None))
            if has_ctx:
                for c in range(0, PAST_LEN, key_chunk):
                    n = min(key_chunk, PAST_LEN - c)
                    chunks.append((kx_ref[hk, c:c + n, :], vx_ref[hk, :, c:c + n], None))
            for u in range(group // heads_per_unit):
                h0 = hk * group + u * heads_per_unit
                q = q_ref[h0:h0 + heads_per_unit, base:base + tq, :].reshape(q_unit, dk)
                unit = dict(q=q, chunks=chunks, m=None, acc=None)
                if has_sink:
                    m0 = jnp.full((1, q_unit), sink_ref[h0] * LOG2E, F32)
                    for e in range(1, heads_per_unit):
                        m0 = jnp.where(lane >= e * tq, sink_ref[h0 + e] * LOG2E, m0)
                    unit.update(m=m0, acc=acc0)
                units.append(unit)
    results = _softmax_units(units, SCORE_LOOKAHEAD)
    per_seq = len(results) // seqs
    for sq in range(seqs):
        outs = []
        for o in results[sq * per_seq:(sq + 1) * per_seq]:
            for e in range(heads_per_unit):
                outs.append(o[:, e * tq:(e + 1) * tq])
        base = sq * (tq if window else seq_len)
        o_ref[base:base + tq, :] = jnp.concatenate(outs, axis=0).T.astype(BF16)


def _attention(q, k, vt, kx, vx, sink, *, n_seq, seq_len, tok_base, tq, window, q_unit, key_chunk, name,
               seqs=1):
    n_q, _, dk = q.shape
    n_kv = k.shape[0]
    group = n_q // n_kv
    has_ctx = kx is not None
    has_sink = sink is not None
    if window:
        n_qt = seq_len // (tq * seqs)
    else:
        n_qt = seq_len // tq
        assert seqs == 1 or (n_qt == 1 and not has_ctx)
    q_base = tok_base // (seqs * tq)
    s_base = tok_base // (seqs * seq_len)
    q_blk = lambda b, j: q_base + b * n_qt + j
    in_specs = [pl.BlockSpec((n_q, seqs * tq, dk), lambda b, j: (0, q_blk(b, j), 0))]
    args = [q]
    if window:
        assert window == tq
        n_kb = seq_len // tq
        kb_base = tok_base // tq

        def key_blk(b, j, off):
            return kb_base + b * n_kb + jnp.clip(j * seqs - 1 + off, 0, n_kb - 1)

        for off in range(seqs + 2):
            in_specs.append(pl.BlockSpec((n_kv, tq, dk), lambda b, j, off=off: (0, key_blk(b, j, off), 0)))
        for off in range(seqs + 2):
            in_specs.append(pl.BlockSpec((n_kv, VT_ROWS, tq), lambda b, j, off=off: (0, 0, key_blk(b, j, off))))
        args += [k] * (seqs + 2) + [vt] * (seqs + 2)
    else:
        in_specs += [
            pl.BlockSpec((n_kv, seqs * seq_len, dk), lambda b, j: (0, s_base + b, 0)),
            pl.BlockSpec((n_kv, VT_ROWS, seqs * seq_len), lambda b, j: (0, 0, s_base + b)),
        ]
        args += [k, vt]
    if has_ctx:
        in_specs += [
            pl.BlockSpec((n_kv, PAST_LEN, dk), lambda b, j: (0, b, 0)),
            pl.BlockSpec((n_kv, VT_ROWS, PAST_LEN), lambda b, j: (0, 0, b)),
        ]
        args += [kx, vx]
    if has_sink:
        in_specs.append(pl.BlockSpec(memory_space=pltpu.SMEM))
        args.append(sink)
    kern = functools.partial(_attn_kernel, n_kv=n_kv, group=group, tq=tq, seq_len=seq_len, seqs=seqs,
                             has_ctx=has_ctx, has_sink=has_sink, window=window,
                             q_unit=q_unit, key_chunk=key_chunk)
    return pl.pallas_call(
        kern, grid=(n_seq if window else n_seq // seqs, n_qt), in_specs=in_specs,
        out_specs=pl.BlockSpec((seqs * tq, n_q * HEAD_DIM), lambda b, j: (b * n_qt + j, 0)),
        out_shape=jax.ShapeDtypeStruct((n_seq * seq_len, n_q * HEAD_DIM), BF16),
        compiler_params=pltpu.CompilerParams(
            dimension_semantics=("arbitrary", "arbitrary"), vmem_limit_bytes=VMEM_LIMIT),
        name=name,
    )(*args)


def _ffn_kernel(*refs, n_o, is_ctx, final):
    halo = 0 if is_ctx else HALO
    it = iter(refs)
    x_ref = next(it)
    xh_refs = None if is_ctx else (next(it), next(it))
    o_refs, oh_refs = [], []
    for _ in range(n_o):
        o_refs.append(next(it))
        if not is_ctx:
            oh_refs.append((next(it), next(it)))
    wo_refs = [next(it) for _ in range(n_o)]
    mod_ref, gffn_ref = next(it), next(it)
    wg_refs = [next(it) for _ in range(FF_PER_STEP)]
    wv_refs = [next(it) for _ in range(FF_PER_STEP)]
    cw_ref, cb_ref = next(it), next(it)
    wd_refs = [next(it) for _ in range(FF_PER_STEP)]
    gfin_ref, out_ref, h2e_ref, acc_ref = next(it), next(it), next(it), next(it)

    m = pl.program_id(0)
    c = pl.program_id(1)
    g1 = mod_ref[2:3, :]
    sh2 = mod_ref[3:4, :]
    sc2 = mod_ref[4:5, :]
    g2 = mod_ref[5:6, :]
    sub = FFN_SUB if is_ctx else FFN_LAT_SUB
    n_sub = FFN_TM // sub
    sub_rows = sub + 2 * halo

    @pl.when(c == 0)
    def _():
        wos = [w[...].astype(BF16) for w in wo_refs]

        def residual_and_norm(xv, ovs):
            attn = _dot(ovs[0], wos[0])
            for ov, wo in zip(ovs[1:], wos[1:]):
                attn = attn + _dot(ov, wo)
            x1 = xv + g1 * attn
            return x1, (_rms(x1, gffn_ref[...]) * (1.0 + sc2) + sh2).astype(BF16)

        for r in range(0, FFN_TM, FFN_SUB):
            x1, h2 = residual_and_norm(x_ref[r:r + FFN_SUB, :], [o[r:r + FFN_SUB, :] for o in o_refs])
            acc_ref[r:r + FFN_SUB, :] = x1
            h2e_ref[halo + r:halo + r + FFN_SUB, :] = h2
        if not is_ctx:
            _, h2h = residual_and_norm(
                jnp.concatenate([xh_refs[0][...], xh_refs[1][...]], axis=0),
                [jnp.concatenate([oh[0][...], oh[1][...]], axis=0) for oh in oh_refs])
            h2e_ref[0:HALO, :] = h2h[0:HALO]
            h2e_ref[HALO + FFN_TM:, :] = h2h[HALO:]

    row8 = lax.broadcasted_iota(jnp.int32, (8, FF_CHUNK), 0)
    if not is_ctx:
        has_prev = m % FFN_LAT_TILES != 0
        has_next = m % FFN_LAT_TILES != FFN_LAT_TILES - 1

    def ff_chunks(n_chunks):
        w_up, w_dn, cw, cb = [], [], [], []
        for j in range(n_chunks):
            w_up.append(jnp.concatenate([wg_refs[j][...].astype(BF16), wv_refs[j][...].astype(BF16)],
                                        axis=1))
            w_dn.append(wd_refs[j][...].astype(BF16))
            cols = pl.ds(pl.multiple_of((c * FF_PER_STEP + j) * FF_CHUNK, FF_CHUNK), FF_CHUNK)
            cw.append(cw_ref[:, cols])
            cb.append(cb_ref[:, cols])
        tasks = [(j, r) for j in range(n_chunks) for r in range(n_sub)]
        ups = {}

        def emit_up(t):
            j, r = tasks[t]
            ups[t] = _dot(h2e_ref[r * sub:r * sub + sub_rows, :], w_up[j])

        for t in range(min(UP_LOOKAHEAD, len(tasks))):
            emit_up(t)
        for t, (j, r) in enumerate(tasks):
            if t + UP_LOOKAHEAD < len(tasks):
                emit_up(t + UP_LOOKAHEAD)
            up = ups.pop(t)
            ge = up[:, :FF_CHUNK]
            val = up[halo:halo + sub, FF_CHUNK:]
            g_prev = pltpu.roll(ge, 1, axis=0)[halo:halo + sub]
            g_next = pltpu.roll(ge, sub_rows - 1, axis=0)[halo:halo + sub]
            prev_ok = False if is_ctx else (has_prev if r == 0 else True)
            next_ok = False if is_ctx else (has_next if r == n_sub - 1 else True)
            if prev_ok is not True:
                g_prev = jnp.concatenate(
                    [jnp.where(jnp.logical_or(row8 != 0, prev_ok), g_prev[0:8], 0.0), g_prev[8:]], axis=0)
            if next_ok is not True:
                g_next = jnp.concatenate(
                    [g_next[:-8], jnp.where(jnp.logical_or(row8 != 7, next_ok), g_next[-8:], 0.0)], axis=0)
            gate = (g_prev * cw[j][0:1, :] + ge[halo:halo + sub] * cw[j][1:2, :] + g_next * cw[j][2:3, :]
                    + cb[j])
            act = (gate * jax.nn.sigmoid(gate) * val).astype(BF16)
            acc_ref[r * sub:(r + 1) * sub, :] += g2 * _dot(act, w_dn[j])

    n_tail = N_FF_CHUNKS % FF_PER_STEP
    if n_tail:
        pl.when(c < N_FF_STEPS - 1)(functools.partial(ff_chunks, FF_PER_STEP))
    else:
        ff_chunks(FF_PER_STEP)

    @pl.when(c == N_FF_STEPS - 1)
    def _():
        if n_tail:
            ff_chunks(n_tail)
        x2 = acc_ref[...]
        out_ref[...] = _rms(x2, gfin_ref[...]) if final else x2


def _ffn(x, os, mod, layer, g_ffn, w_out, w_up, conv_w, conv_b, w_down, g_final, *, is_ctx, final):
    n_rows = x.shape[0]
    nh = FFN_TM // HALO
    nblk = n_rows // HALO
    n_o = len(os)
    halo = 0 if is_ctx else HALO
    tile = lambda m, c: (m, 0)
    prev = lambda m, c: (jnp.maximum(m * nh - 1, 0), 0)
    nxt = lambda m, c: (jnp.minimum((m + 1) * nh, nblk - 1), 0)
    chunk = lambda c, j: jnp.minimum(c * FF_PER_STEP + j, N_FF_CHUNKS - 1)
    if is_ctx:
        cond = lambda m: layer * 8
    else:
        cond = lambda m: layer * 8 + 1 + m // FFN_LAT_TILES

    def with_halo(arr):
        w = arr.shape[1]
        specs = [pl.BlockSpec((FFN_TM, w), tile)]
        if not is_ctx:
            specs += [pl.BlockSpec((HALO, w), prev), pl.BlockSpec((HALO, w), nxt)]
        return specs, [arr] * len(specs)

    in_specs, args = with_halo(x)
    for o in os:
        specs, arrs = with_halo(o)
        in_specs += specs
        args += arrs
    w_rows = D_MODEL // n_o
    for t in range(n_o):
        in_specs.append(pl.BlockSpec((None, w_rows, D_MODEL), lambda m, c, t=t: (0, t, 0)))
        args.append(w_out)
    in_specs += [
        pl.BlockSpec((None, 6, D_MODEL), lambda m, c: (cond(m), 0, 0)),
        pl.BlockSpec((None, 1, D_MODEL), lambda m, c: (layer, 0, 0)),
    ]
    args += [mod, g_ffn]
    steps = range(FF_PER_STEP)
    in_specs += [pl.BlockSpec((None, D_MODEL, FF_CHUNK), lambda m, c, j=j: (layer, 0, chunk(c, j))) for j in steps]
    in_specs += [pl.BlockSpec((None, D_MODEL, FF_CHUNK), lambda m, c, j=j: (layer, 0, N_FF_CHUNKS + chunk(c, j)))
                 for j in steps]
    in_specs += [pl.BlockSpec((None, 3, D_FF), lambda m, c: (layer, 0, 0)),
                 pl.BlockSpec((None, 1, D_FF), lambda m, c: (layer, 0, 0))]
    in_specs += [pl.BlockSpec((None, FF_CHUNK, D_MODEL), lambda m, c, j=j: (layer, chunk(c, j), 0)) for j in steps]
    args += [w_up] * (2 * FF_PER_STEP) + [conv_w, conv_b] + [w_down] * FF_PER_STEP
    in_specs.append(pl.BlockSpec((1, D_MODEL), lambda m, c: (0, 0)))
    args.append(g_final)
    return pl.pallas_call(
        functools.partial(_ffn_kernel, n_o=n_o, is_ctx=is_ctx, final=final),
        grid=(n_rows // FFN_TM, N_FF_STEPS), in_specs=in_specs,
        out_specs=pl.BlockSpec((FFN_TM, D_MODEL), tile),
        out_shape=jax.ShapeDtypeStruct((n_rows, D_MODEL), F32),
        scratch_shapes=[pltpu.VMEM((FFN_TM + 2 * halo, D_MODEL), BF16), pltpu.VMEM((FFN_TM, D_MODEL), F32)],
        compiler_params=pltpu.CompilerParams(
            dimension_semantics=("arbitrary", "arbitrary"), vmem_limit_bytes=FFN_VMEM_LIMIT),
        name=("ffn_ctx" if is_ctx else "ffn_lat") + ("_final" if final else ""),
    )(*args)


def _rope_tables(rot_dim):
    f32 = np.float32
    t = np.arange(LAT_LEN)
    row = (t // GRID_W).astype(f32)
    col = (t % GRID_W).astype(f32)
    d_axis = rot_dim // 2
    freqs = (f32(ROPE_THETA) ** (-np.arange(0, d_axis, 2, dtype=f32) / f32(d_axis))).astype(f32)
    ang = np.concatenate([row[:, None] * freqs, col[:, None] * freqs], axis=-1)
    cos = np.repeat(np.cos(ang), 2, axis=-1).astype(f32)
    sin = (np.repeat(np.sin(ang), 2, axis=-1) * np.tile(np.array([-1.0, 1.0], f32), rot_dim // 2)).astype(f32)
    if rot_dim == HEAD_DIM:
        cos = np.tile(cos, (1, 2))
        sin = np.tile(sin, (1, 2))
    else:
        cos = np.concatenate([np.ones((LAT_LEN, QK_NOPE), f32), cos,
                              np.ones((LAT_LEN, 128 - QK_NOPE - rot_dim), f32)], axis=-1)
        sin = np.concatenate([np.zeros((LAT_LEN, QK_NOPE), f32), sin,
                              np.zeros((LAT_LEN, 128 - QK_NOPE - rot_dim), f32)], axis=-1)
    cos = np.concatenate([cos, np.ones((TM, 128), f32)], axis=0)
    sin = np.concatenate([sin, np.zeros((TM, 128), f32)], axis=0)
    return jnp.asarray(cos), jnp.asarray(sin)


def kernel(x_prompt, x_sample, cache_a_k, cache_a_v, cache_b_ckv, cache_b_kpe, cache_c_k, cache_c_v, c, c_ctx, w_mod, b_mod, g_mix_norm, g_ffn_norm, w_in_e, g_qnorm_a, g_knorm_a, g_cq_b, w_uq_b, g_ckv_b, w_ukv_b, w_out_e, w_in_o, sink_c, w_out_o, w_up, conv_w, conv_b, w_down, g_final):
    depth = w_mod.shape[0]
    cond8 = jnp.concatenate([c_ctx[None, :], c, jnp.zeros((5, D_MODEL), F32)], axis=0)
    lane_pad = KB_PAD - QK_NOPE - QK_ROPE
    wuq = jnp.pad(w_uq_b[0].reshape(Q_LORA, N_HEADS_B, QK_NOPE + QK_ROPE),
                  ((0, 0), (0, 0), (0, lane_pad))).reshape(Q_LORA, N_HEADS_B * KB_PAD)
    wukv3 = w_ukv_b[0].reshape(KV_LORA, N_HEADS_B, QK_NOPE + V_DIM_B)
    wukk = jnp.pad(wukv3[:, :, :QK_NOPE], ((0, 0), (0, 0), (0, KB_PAD - QK_NOPE))
                   ).reshape(KV_LORA, N_HEADS_B * KB_PAD)
    wukv = wukv3[:, :, QK_NOPE:].reshape(KV_LORA, N_HEADS_B * V_DIM_B)
    gq = jnp.tile(g_qnorm_a[0], N_HEADS_A)[None, :]
    gk = jnp.tile(g_knorm_a[0], N_KV_A)[None, :]
    seg = np.arange(256) // HEAD_DIM
    ones = jnp.asarray(seg[:, None] == seg[None, :], dtype=BF16)
    cosa, sina = _rope_tables(HEAD_DIM)
    cosb, sinb = _rope_tables(QK_ROPE)
    g_mix3 = g_mix_norm.reshape(depth, 1, D_MODEL)
    g_ffn3 = g_ffn_norm.reshape(depth, 1, D_MODEL)
    conv_b3 = conv_b.reshape(depth, 1, D_FF)
    g_fin2 = g_final[None, :]

    mod = _modulation(cond8, w_mod, b_mod).reshape(depth * 8, 6, D_MODEL)

    n_past = N_LAT_SEQ * PAST_LEN
    kpe_pad = jnp.pad(cache_b_kpe.reshape(n_past, QK_ROPE), ((0, 0), (QK_NOPE, lane_pad)))
    ka_c, va_c, kb_c, vb_c, kc_c, vc_c = _ctx_prep(
        cache_a_k.reshape(n_past, 128), cache_a_v.reshape(n_past, 128),
        cache_b_ckv.reshape(n_past, KV_LORA), kpe_pad,
        cache_c_k.reshape(n_past, 128), cache_c_v.reshape(n_past, 128), wukk, wukv)

    xp = x_prompt.reshape(N_CTX_TOK, D_MODEL)
    xs = x_sample.reshape(N_TOK - N_CTX_TOK, D_MODEL)
    qa, ka, va, qb, kb, vb, nak, nav, nckv, nkpe = _pre0(
        xp, xs, mod, g_mix3, w_in_e, gq, gk, g_cq_b.reshape(1, 1, Q_LORA), g_ckv_b.reshape(1, 1, KV_LORA),
        wuq, wukk, wukv, cosa, sina, cosb, sinb, ones)
    ffn_w = (w_up, conv_w, conv_b3, w_down, g_fin2)
    ctx_kw = dict(n_seq=N_CTX_SEQ, seq_len=CTX_LEN, tok_base=0, tq=CTX_LEN, window=0, q_unit=256,
                  key_chunk=CTX_LEN, seqs=8)
    lat_kw = dict(n_seq=N_LAT_SEQ, seq_len=LAT_LEN, tok_base=N_CTX_TOK, q_unit=256)
    dense_kw = dict(tq=256, window=0, key_chunk=256, **lat_kw)
    oa = (_attention(qa, ka, va, None, None, None, name="attn_a_ctx", **ctx_kw),
          _attention(qa, ka, va, ka_c, va_c, None, name="attn_a_lat", **dense_kw))
    ob = (_attention(qb, kb, vb, None, None, None, name="attn_b_ctx", **ctx_kw),
          _attention(qb, kb, vb, kb_c, vb_c, None, name="attn_b_lat", **dense_kw))
    xp1 = _ffn(xp, [oa[0], ob[0]], mod, 0, g_ffn3, w_out_e, *ffn_w, is_ctx=True, final=False)
    xs1 = _ffn(xs, [oa[1], ob[1]], mod, 0, g_ffn3, w_out_e, *ffn_w, is_ctx=False, final=False)

    qc, kc, vc, nck, ncv = _pre1(xp1, xs1, mod, g_mix3, w_in_o, cosa, sina)
    sink = sink_c[0]
    oc = (_attention(qc, kc, vc, None, None, sink, name="attn_c_ctx", **ctx_kw),
          _attention(qc, kc, vc, kc_c, vc_c, sink, tq=128, window=WINDOW, key_chunk=PAST_LEN, seqs=4,
                     name="attn_c_lat", **lat_kw))
    y_prompt = _ffn(xp1, [oc[0]], mod, 1, g_ffn3, w_out_o, *ffn_w, is_ctx=True, final=True)
    y_sample = _ffn(xs1, [oc[1]], mod, 1, g_ffn3, w_out_o, *ffn_w, is_ctx=False, final=True)

    return (y_prompt.reshape(N_CTX_SEQ, CTX_LEN, D_MODEL), y_sample.reshape(N_LAT_SEQ, LAT_LEN, D_MODEL),
            nak.reshape(N_CTX_SEQ, 1, CTX_LEN, N_KV_A, HEAD_DIM),
            nav.reshape(N_CTX_SEQ, 1, CTX_LEN, N_KV_A, HEAD_DIM),
            nckv.reshape(N_CTX_SEQ, 1, CTX_LEN, KV_LORA),
            nkpe.reshape(N_CTX_SEQ, 1, CTX_LEN, QK_ROPE),
            nck.reshape(N_CTX_SEQ, 1, CTX_LEN, N_KV_C, HEAD_DIM),
            ncv.reshape(N_CTX_SEQ, 1, CTX_LEN, N_KV_C, HEAD_DIM))
```

```python
import functools

import jax
import jax.numpy as jnp
import numpy as np
from jax import lax
from jax.experimental import pallas as pl
from jax.experimental.pallas import tpu as pltpu

F32 = jnp.float32
BF16 = jnp.bfloat16

D_MODEL = 1024
N_CTX_SEQ = 16
CTX_LEN = 256
N_LAT_SEQ = 2
LAT_LEN = 2048
PAST_LEN = 512
GRID_W = 64
ROPE_THETA = 10000.0
NORM_EPS = 1e-6
WINDOW = 128
NEG_INF = -1e30
LOG2E = 1.4426950408889634
HEAD_DIM = 64
N_HEADS_A, N_KV_A = 8, 2
N_HEADS_B = 8
Q_LORA, KV_LORA = 384, 256
QK_NOPE, QK_ROPE, V_DIM_B = 64, 32, 64
N_HEADS_C, N_KV_C = 16, 2
D_FF = 2816
IN_E_MAIN = N_HEADS_A * HEAD_DIM + 2 * N_KV_A * HEAD_DIM + Q_LORA + KV_LORA

N_CTX_TOK = N_CTX_SEQ * CTX_LEN
N_TOK = N_CTX_TOK + N_LAT_SEQ * LAT_LEN
TM = 512
PRE_SUB = 256
N_TILES = N_TOK // TM
N_CTX_TILES = N_CTX_TOK // TM
LAT_TILES = LAT_LEN // TM
HALO = 16
FFN_TM = 1024
FFN_SUB = CTX_LEN
FFN_LAT_SUB = 256
UP_LOOKAHEAD = 3
FFN_LAT_TILES = LAT_LEN // FFN_TM
FF_CHUNK = 256
N_FF_CHUNKS = D_FF // FF_CHUNK
FF_PER_STEP = 3
N_FF_STEPS = -(-N_FF_CHUNKS // FF_PER_STEP)
KB_PAD = 128
VT_ROWS = 80
SCORE_LOOKAHEAD = 5
VMEM_LIMIT = 56 * 1024 * 1024
FFN_VMEM_LIMIT = 60 * 1024 * 1024


def _dot(a, b):
    return jnp.dot(a, b, preferred_element_type=F32)


def _dot_nt(a, b):
    return lax.dot_general(a, b, (((1,), (1,)), ((), ())), preferred_element_type=F32)


def _rms(x, g):
    return x * lax.rsqrt(jnp.mean(x * x, axis=-1, keepdims=True) + NORM_EPS) * g


def _split_bf16(x):
    hi = x.astype(BF16)
    return hi, (x - hi.astype(F32)).astype(BF16)


def _head_rms(x, g, ones_ref):
    w = x.shape[1]
    hi, lo = _split_bf16(x * x)
    parts = []
    for c in range(0, w, 256):
        cw = min(256, w - c)
        ones = ones_ref[0:cw, 0:cw]
        parts.append(_dot(hi[:, c:c + cw], ones) + _dot(lo[:, c:c + cw], ones))
    ssum = parts[0] if len(parts) == 1 else jnp.concatenate(parts, axis=1)
    return x * lax.rsqrt(ssum * (1.0 / HEAD_DIM) + NORM_EPS) * g


def _swap_pairs(x):
    w = x.shape[1]
    up = pltpu.roll(x, w - 1, axis=1)
    dn = pltpu.roll(x, 1, axis=1)
    lane = lax.broadcasted_iota(jnp.int32, x.shape, 1)
    return jnp.where((lane & 1) == 0, up, dn)


def _rope(x, cos, sin_signed, reps):
    if reps > 1:
        cos = jnp.concatenate([cos] * reps, axis=1)
        sin_signed = jnp.concatenate([sin_signed] * reps, axis=1)
    return x * cos + _swap_pairs(x) * sin_signed


def _store_vt(vt_ref, v, n_heads, cols=slice(None)):
    t = v.shape[0]
    vt = v.T.astype(BF16)
    ones = jnp.ones((VT_ROWS - 64, t), BF16)
    for hh in range(n_heads):
        vt_ref[hh, 0:64, cols] = vt[hh * 64:(hh + 1) * 64]
        vt_ref[hh, 64:VT_ROWS, cols] = ones


def _cond_row(i):
    return jnp.where(i < N_CTX_TILES, 0, 1 + (i - N_CTX_TILES) // LAT_TILES)


def _rope_block(i):
    return jnp.where(i < N_CTX_TILES, LAT_TILES, (i - N_CTX_TILES) % LAT_TILES)


def _const_spec(shape):
    zeros = (0,) * len(shape)
    return pl.BlockSpec(shape, lambda *_: zeros)


def _layer_spec(shape, layer):
    idx = (layer,) + (0,) * len(shape)
    return pl.BlockSpec((None,) + tuple(shape), lambda *_: idx)


def _mod_kernel(cond_ref, w_ref, b_ref, o_ref):
    c = cond_ref[...]
    s_hi, s_lo = _split_bf16(c * jax.nn.sigmoid(c))
    w_hi, w_lo = _split_bf16(w_ref[0])
    r = _dot(jnp.concatenate([s_hi, s_lo], axis=0), w_hi)
    o_ref[0] = r[0:8] + r[8:16] + _dot(s_hi, w_lo) + b_ref[0]


def _modulation(cond8, w_mod, b_mod):
    depth, _, n = w_mod.shape
    tn = 3072
    return pl.pallas_call(
        _mod_kernel,
        grid=(depth, n // tn),
        in_specs=[
            pl.BlockSpec((8, D_MODEL), lambda l, j: (0, 0)),
            pl.BlockSpec((1, D_MODEL, tn), lambda l, j: (l, 0, j)),
            pl.BlockSpec((1, 1, tn), lambda l, j: (l, 0, j)),
        ],
        out_specs=pl.BlockSpec((1, 8, tn), lambda l, j: (l, 0, j)),
        out_shape=jax.ShapeDtypeStruct((depth, 8, n), F32),
        compiler_params=pltpu.CompilerParams(
            dimension_semantics=("arbitrary", "arbitrary"), vmem_limit_bytes=VMEM_LIMIT),
        name="modulation",
    )(cond8, w_mod, b_mod.reshape(depth, 1, n))


def _pre0_kernel(xp_ref, xs_ref, mod_ref, gmix_ref, win_ref, gq_ref, gk_ref, gcq_ref, gckv_ref,
                 wuq_ref, wukk_ref, wukv_ref, cosa_ref, sina_ref, cosb_ref, sinb_ref, ones_ref,
                 qa_ref, ka_ref, va_ref, qb_ref, kb_ref, vb_ref,
                 nak_ref, nav_ref, nckv_ref, nkpe_ref,
                 win_s, wuq_s, wukk_s, wukv_s):
    i = pl.program_id(0)

    @pl.when(i == 0)
    def _():
        win_s[:, 0:IN_E_MAIN] = win_ref[:, 0:IN_E_MAIN].astype(BF16)
        win_s[:, IN_E_MAIN:] = jnp.concatenate(
            [jnp.zeros((D_MODEL, QK_NOPE), F32), win_ref[:, IN_E_MAIN:],
             jnp.zeros((D_MODEL, KB_PAD - QK_NOPE - QK_ROPE), F32)], axis=1).astype(BF16)
        wuq_s[...] = wuq_ref[...].astype(BF16)
        wukk_s[...] = wukk_ref[...].astype(BF16)
        wukv_s[...] = wukv_ref[...].astype(BF16)

    sh1 = mod_ref[0:1, :]
    sc1 = mod_ref[1:2, :]
    subs = [slice(r, r + PRE_SUB) for r in range(0, TM, PRE_SUB)]

    projs = []
    for sl in subs:
        x = jnp.where(i < N_CTX_TILES, xp_ref[sl, :], xs_ref[sl, :])
        h = _rms(x, gmix_ref[...]) * (1.0 + sc1) + sh1
        projs.append(_dot(h.astype(BF16), win_s[...]))

    parts = []
    for sl, proj in zip(subs, projs):
        qa = _head_rms(proj[:, 0:512], gq_ref[...], ones_ref)
        ka = _head_rms(proj[:, 512:640], gk_ref[...], ones_ref)
        va = proj[:, 640:768]
        cq = _rms(proj[:, 768:1152], gcq_ref[...])
        ckv = _rms(proj[:, 1152:1408], gckv_ref[...])
        kpe = proj[:, 1408:1536]
        qb = _dot(cq.astype(BF16), wuq_s[...])
        ckv_b = ckv.astype(BF16)
        kbn = _dot(ckv_b, wukk_s[...])
        vb = _dot(ckv_b, wukv_s[...])
        _store_vt(va_ref, va, N_KV_A, sl)
        _store_vt(vb_ref, vb, N_HEADS_B, sl)
        parts.append((ka, va, ckv, kpe))
        cosa, sina = cosa_ref[sl, :], sina_ref[sl, :]
        cosb, sinb = cosb_ref[sl, :], sinb_ref[sl, :]
        qa_s = _rope(qa, cosa, sina, 4) * (LOG2E * HEAD_DIM ** -0.5)
        ka_r = _rope(ka, cosa, sina, 1)
        qb_s = _rope(qb, cosb, sinb, 8) * (LOG2E * (QK_NOPE + QK_ROPE) ** -0.5)
        kpe_r = _rope(kpe, cosb, sinb, 1)
        for hh in range(N_HEADS_A):
            qa_ref[hh, sl, :] = qa_s[:, hh * 64:(hh + 1) * 64].astype(BF16)
        for hh in range(N_KV_A):
            ka_ref[hh, sl, :] = ka_r[:, hh * 64:(hh + 1) * 64].astype(BF16)
        for hh in range(N_HEADS_B):
            qb_ref[hh, sl, :] = qb_s[:, hh * KB_PAD:(hh + 1) * KB_PAD].astype(BF16)
            kb_ref[hh, sl, :] = (kbn[:, hh * KB_PAD:(hh + 1) * KB_PAD] + kpe_r).astype(BF16)

    @pl.when(i < N_CTX_TILES)
    def _():
        for sl, (ka, va, ckv, kpe) in zip(subs, parts):
            nak_ref[sl, :] = ka
            nav_ref[sl, :] = va
            nckv_ref[sl, :] = ckv
            nkpe_ref[sl, :] = kpe[:, QK_NOPE:QK_NOPE + QK_ROPE]


def _pre0(xp, xs, mod, gmix, w_in_e, gq, gk, gcq, gckv, wuq, wukk, wukv, cosa, sina, cosb, sinb, ones):
    tile = lambda i: (i, 0)
    head_tile = lambda i: (0, i, 0)
    vt_tile = lambda i: (0, 0, i)
    ctx_tile = lambda i: (jnp.minimum(i, N_CTX_TILES - 1), 0)
    lat_tile = lambda i: (jnp.maximum(i - N_CTX_TILES, 0), 0)
    rope_tile = lambda i: (_rope_block(i), 0)
    in_specs = [
        pl.BlockSpec((TM, D_MODEL), ctx_tile),
        pl.BlockSpec((TM, D_MODEL), lat_tile),
        pl.BlockSpec((None, 6, D_MODEL), lambda i: (_cond_row(i), 0, 0)),
        _layer_spec((1, D_MODEL), 0),
        _layer_spec(w_in_e.shape[1:], 0),
        _const_spec(gq.shape), _const_spec(gk.shape), _layer_spec((1, Q_LORA), 0), _layer_spec((1, KV_LORA), 0),
        _const_spec(wuq.shape), _const_spec(wukk.shape), _const_spec(wukv.shape),
        pl.BlockSpec((TM, 128), rope_tile), pl.BlockSpec((TM, 128), rope_tile),
        pl.BlockSpec((TM, 128), rope_tile), pl.BlockSpec((TM, 128), rope_tile),
        _const_spec(ones.shape),
    ]
    out_shape = [
        jax.ShapeDtypeStruct((N_HEADS_A, N_TOK, 64), BF16),
        jax.ShapeDtypeStruct((N_KV_A, N_TOK, 64), BF16),
        jax.ShapeDtypeStruct((N_KV_A, VT_ROWS, N_TOK), BF16),
        jax.ShapeDtypeStruct((N_HEADS_B, N_TOK, KB_PAD), BF16),
        jax.ShapeDtypeStruct((N_HEADS_B, N_TOK, KB_PAD), BF16),
        jax.ShapeDtypeStruct((N_HEADS_B, VT_ROWS, N_TOK), BF16),
        jax.ShapeDtypeStruct((N_CTX_TOK, 128), F32),
        jax.ShapeDtypeStruct((N_CTX_TOK, 128), F32),
        jax.ShapeDtypeStruct((N_CTX_TOK, KV_LORA), F32),
        jax.ShapeDtypeStruct((N_CTX_TOK, QK_ROPE), F32),
    ]
    out_specs = [
        pl.BlockSpec((N_HEADS_A, TM, 64), head_tile),
        pl.BlockSpec((N_KV_A, TM, 64), head_tile),
        pl.BlockSpec((N_KV_A, VT_ROWS, TM), vt_tile),
        pl.BlockSpec((N_HEADS_B, TM, KB_PAD), head_tile),
        pl.BlockSpec((N_HEADS_B, TM, KB_PAD), head_tile),
        pl.BlockSpec((N_HEADS_B, VT_ROWS, TM), vt_tile),
        pl.BlockSpec((TM, 128), ctx_tile),
        pl.BlockSpec((TM, 128), ctx_tile),
        pl.BlockSpec((TM, KV_LORA), ctx_tile),
        pl.BlockSpec((TM, QK_ROPE), ctx_tile),
    ]
    scratch = [
        pltpu.VMEM((D_MODEL, IN_E_MAIN + KB_PAD), BF16),
        pltpu.VMEM(wuq.shape, BF16), pltpu.VMEM(wukk.shape, BF16), pltpu.VMEM(wukv.shape, BF16),
    ]
    return pl.pallas_call(
        _pre0_kernel, grid=(N_TILES,), in_specs=in_specs, out_specs=out_specs, out_shape=out_shape,
        scratch_shapes=scratch,
        compiler_params=pltpu.CompilerParams(
            dimension_semantics=("arbitrary",), vmem_limit_bytes=VMEM_LIMIT),
        name="pre0",
    )(xp, xs, mod, gmix, w_in_e, gq, gk, gcq, gckv, wuq, wukk, wukv, cosa, sina, cosb, sinb, ones)


def _pre1_kernel(xp_ref, xs_ref, mod_ref, gmix_ref, win_ref, cosa_ref, sina_ref,
                 q_ref, k_ref, v_ref, nk_ref, nv_ref, win_s):
    i = pl.program_id(0)

    @pl.when(i == 0)
    def _():
        win_s[...] = win_ref[...].astype(BF16)

    sh1 = mod_ref[0:1, :]
    sc1 = mod_ref[1:2, :]
    subs = [slice(r, r + PRE_SUB) for r in range(0, TM, PRE_SUB)]
    projs = []
    for sl in subs:
        x = jnp.where(i < N_CTX_TILES, xp_ref[sl, :], xs_ref[sl, :])
        h = _rms(x, gmix_ref[...]) * (1.0 + sc1) + sh1
        projs.append(_dot(h.astype(BF16), win_s[...]))
    for sl, proj in zip(subs, projs):
        cosa, sina = cosa_ref[sl, :], sina_ref[sl, :]
        q_s = _rope(proj[:, 0:1024], cosa, sina, 8) * (LOG2E * HEAD_DIM ** -0.5)
        k = _rope(proj[:, 1024:1152], cosa, sina, 1)
        for hh in range(N_HEADS_C):
            q_ref[hh, sl, :] = q_s[:, hh * 64:(hh + 1) * 64].astype(BF16)
        for hh in range(N_KV_C):
            k_ref[hh, sl, :] = k[:, hh * 64:(hh + 1) * 64].astype(BF16)
        _store_vt(v_ref, proj[:, 1152:1280], N_KV_C, sl)

    @pl.when(i < N_CTX_TILES)
    def _():
        for sl, proj in zip(subs, projs):
            nk_ref[sl, :] = proj[:, 1024:1152]
            nv_ref[sl, :] = proj[:, 1152:1280]


def _pre1(xp, xs, mod, gmix, w_in_o, cosa, sina):
    n_in = w_in_o.shape[-1]
    head_tile = lambda i: (0, i, 0)
    vt_tile = lambda i: (0, 0, i)
    ctx_tile = lambda i: (jnp.minimum(i, N_CTX_TILES - 1), 0)
    lat_tile = lambda i: (jnp.maximum(i - N_CTX_TILES, 0), 0)
    rope_tile = lambda i: (_rope_block(i), 0)
    in_specs = [
        pl.BlockSpec((TM, D_MODEL), ctx_tile),
        pl.BlockSpec((TM, D_MODEL), lat_tile),
        pl.BlockSpec((None, 6, D_MODEL), lambda i: (8 + _cond_row(i), 0, 0)),
        _layer_spec((1, D_MODEL), 1),
        _layer_spec((D_MODEL, n_in), 0),
        pl.BlockSpec((TM, 128), rope_tile), pl.BlockSpec((TM, 128), rope_tile),
    ]
    out_shape = [
        jax.ShapeDtypeStruct((N_HEADS_C, N_TOK, 64), BF16),
        jax.ShapeDtypeStruct((N_KV_C, N_TOK, 64), BF16),
        jax.ShapeDtypeStruct((N_KV_C, VT_ROWS, N_TOK), BF16),
        jax.ShapeDtypeStruct((N_CTX_TOK, 128), F32),
        jax.ShapeDtypeStruct((N_CTX_TOK, 128), F32),
    ]
    out_specs = [
        pl.BlockSpec((N_HEADS_C, TM, 64), head_tile),
        pl.BlockSpec((N_KV_C, TM, 64), head_tile),
        pl.BlockSpec((N_KV_C, VT_ROWS, TM), vt_tile),
        pl.BlockSpec((TM, 128), ctx_tile),
        pl.BlockSpec((TM, 128), ctx_tile),
    ]
    return pl.pallas_call(
        _pre1_kernel, grid=(N_TILES,), in_specs=in_specs, out_specs=out_specs, out_shape=out_shape,
        scratch_shapes=[pltpu.VMEM((D_MODEL, n_in), BF16)],
        compiler_params=pltpu.CompilerParams(
            dimension_semantics=("arbitrary",), vmem_limit_bytes=VMEM_LIMIT),
        name="pre1",
    )(xp, xs, mod, gmix, w_in_o, cosa, sina)


def _ctx_kernel(ak_ref, av_ref, ckv_ref, kpe_ref, ck_ref, cv_ref, wukk_ref, wukv_ref,
                ka_ref, va_ref, kb_ref, vb_ref, kc_ref, vc_ref):
    ak, av, ck, cv = ak_ref[...], av_ref[...], ck_ref[...], cv_ref[...]
    for hh in range(2):
        sl = slice(hh * 64, (hh + 1) * 64)
        ka_ref[hh] = ak[:, sl].astype(BF16)
        kc_ref[hh] = ck[:, sl].astype(BF16)
    _store_vt(va_ref, av, 2)
    _store_vt(vc_ref, cv, 2)
    ckv_b = ckv_ref[...].astype(BF16)
    kbn = _dot(ckv_b, wukk_ref[...].astype(BF16))
    vb = _dot(ckv_b, wukv_ref[...].astype(BF16))
    kpe = kpe_ref[...]
    for hh in range(N_HEADS_B):
        kb_ref[hh] = (kbn[:, hh * KB_PAD:(hh + 1) * KB_PAD] + kpe).astype(BF16)
    _store_vt(vb_ref, vb, N_HEADS_B)


def _ctx_prep(ak, av, ckv, kpe_pad, ck, cv, wukk, wukv):
    n = N_LAT_SEQ * PAST_LEN
    row = lambda b: (b, 0)
    head_row = lambda b: (0, b, 0)
    vt_row = lambda b: (0, 0, b)
    in_specs = [
        pl.BlockSpec((PAST_LEN, 128), row), pl.BlockSpec((PAST_LEN, 128), row),
        pl.BlockSpec((PAST_LEN, KV_LORA), row), pl.BlockSpec((PAST_LEN, KB_PAD), row),
        pl.BlockSpec((PAST_LEN, 128), row), pl.BlockSpec((PAST_LEN, 128), row),
        _const_spec(wukk.shape), _const_spec(wukv.shape),
    ]
    out_shape = [
        jax.ShapeDtypeStruct((2, n, 64), BF16), jax.ShapeDtypeStruct((2, VT_ROWS, n), BF16),
        jax.ShapeDtypeStruct((N_HEADS_B, n, KB_PAD), BF16), jax.ShapeDtypeStruct((N_HEADS_B, VT_ROWS, n), BF16),
        jax.ShapeDtypeStruct((2, n, 64), BF16), jax.ShapeDtypeStruct((2, VT_ROWS, n), BF16),
    ]
    out_specs = [
        pl.BlockSpec((2, PAST_LEN, 64), head_row), pl.BlockSpec((2, VT_ROWS, PAST_LEN), vt_row),
        pl.BlockSpec((N_HEADS_B, PAST_LEN, KB_PAD), head_row), pl.BlockSpec((N_HEADS_B, VT_ROWS, PAST_LEN), vt_row),
        pl.BlockSpec((2, PAST_LEN, 64), head_row), pl.BlockSpec((2, VT_ROWS, PAST_LEN), vt_row),
    ]
    return pl.pallas_call(
        _ctx_kernel, grid=(N_LAT_SEQ,), in_specs=in_specs, out_specs=out_specs, out_shape=out_shape,
        compiler_params=pltpu.CompilerParams(
            dimension_semantics=("arbitrary",), vmem_limit_bytes=VMEM_LIMIT),
        name="ctx_prep",
    )(ak, av, ckv, kpe_pad, ck, cv, wukk, wukv)


def _softmax_units(units, lookahead):
    tasks = [(u, c) for u, unit in enumerate(units) for c in range(len(unit["chunks"]))]
    scores = {}
    qts = [unit["q"].astype(F32).T.astype(BF16) if len(unit["chunks"]) >= 4 else None for unit in units]

    def emit_scores(t):
        u, c = tasks[t]
        k, _, mask = units[u]["chunks"][c]
        s = _dot_nt(k, units[u]["q"]) if qts[u] is None else _dot(k, qts[u])
        scores[t] = s if mask is None else jnp.where(mask, s, NEG_INF)

    for t in range(min(lookahead, len(tasks))):
        emit_scores(t)
    for t, (u, c) in enumerate(tasks):
        if t + lookahead < len(tasks):
            emit_scores(t + lookahead)
        unit = units[u]
        s = scores.pop(t)
        m, acc = unit["m"], unit["acc"]
        cmax = jnp.max(s, axis=0, keepdims=True)
        m_new = cmax if m is None else jnp.maximum(m, cmax)
        pv = _dot(unit["chunks"][c][1], jnp.exp2(s - m_new).astype(BF16))
        unit["acc"] = pv if acc is None else acc * jnp.exp2(m - m_new) + pv
        unit["m"] = m_new
    return [unit["acc"][0:64] * (1.0 / unit["acc"][64:65]) for unit in units]


def _attn_kernel(*refs, n_kv, group, tq, seq_len, seqs, has_ctx, has_sink, window, q_unit, key_chunk):
    refs = list(refs)
    q_ref = refs.pop(0)
    if window:
        n_blk = seqs + 2
        kb_refs, vb_refs = refs[:n_blk], refs[n_blk:2 * n_blk]
        refs = refs[2 * n_blk:]
    else:
        k_ref, vt_ref = refs[:2]
        refs = refs[2:]
    if has_ctx:
        kx_ref, vx_ref = refs[:2]
        refs = refs[2:]
    if has_sink:
        sink_ref = refs.pop(0)
    o_ref = refs.pop(0)

    j = pl.program_id(1)
    dk = q_ref.shape[-1]
    heads_per_unit = q_unit // tq
    lane = lax.broadcasted_iota(jnp.int32, (1, q_unit), 1)
    if window:
        n_band = tq + 2 * window
        krow = lax.broadcasted_iota(jnp.int32, (n_band, q_unit), 0)
        qcol = lax.broadcasted_iota(jnp.int32, (n_band, q_unit), 1) & (tq - 1)
        rel = (krow - window) - qcol
        in_band = jnp.abs(rel) <= window
        last_j = seq_len // (tq * seqs) - 1
        band_masks = []
        for sq in range(seqs):
            mask = in_band
            if sq == 0:
                mask = mask & ((krow >= window) | (j > 0))
            if sq == seqs - 1:
                mask = mask & ((krow < window + tq) | (j < last_j))
            band_masks.append(mask)
    if has_sink:
        acc0 = jnp.where(lax.broadcasted_iota(jnp.int32, (VT_ROWS, q_unit), 0) >= 64, 1.0, 0.0)

    units = []
    for sq in range(seqs):
        base = sq * (tq if window else seq_len)
        for hk in range(n_kv):
            chunks = []
            if window:
                chunks.append((jnp.concatenate([r[hk] for r in kb_refs[sq:sq + 3]], axis=0),
                               jnp.concatenate([r[hk] for r in vb_refs[sq:sq + 3]], axis=1), band_masks[sq]))
            else:
                for c in range(base, base + seq_len, key_chunk):
                    n = min(key_chunk, base + seq_len - c)
                    chunks.append((k_ref[hk, c:c + n, :], vt_ref[hk, :, c:c + n], None))
            if has_ctx:
                for c in range(0, PAST_LEN, key_chunk):
                    n = min(key_chunk, PAST_LEN - c)
                    chunks.append((kx_ref[hk, c:c + n, :], vx_ref[hk, :, c:c + n], None))
            for u in range(group // heads_per_unit):
                h0 = hk * group + u * heads_per_unit
                q = q_ref[h0:h0 + heads_per_unit, base:base + tq, :].reshape(q_unit, dk)
                unit = dict(q=q, chunks=chunks, m=None, acc=None)
                if has_sink:
                    m0 = jnp.full((1, q_unit), sink_ref[h0] * LOG2E, F32)
                    for e in range(1, heads_per_unit):
                        m0 = jnp.where(lane >= e * tq, sink_ref[h0 + e] * LOG2E, m0)
                    unit.update(m=m0, acc=acc0)
                units.append(unit)
    results = _softmax_units(units, SCORE_LOOKAHEAD)
    per_seq = len(results) // seqs
    for sq in range(seqs):
        outs = []
        for o in results[sq * per_seq:(sq + 1) * per_seq]:
            for e in range(heads_per_unit):
                outs.append(o[:, e * tq:(e + 1) * tq])
        base = sq * (tq if window else seq_len)
        o_ref[base:base + tq, :] = jnp.concatenate(outs, axis=0).T.astype(BF16)


def _attention(q, k, vt, kx, vx, sink, *, n_seq, seq_len, tok_base, tq, window, q_unit, key_chunk, name,
               seqs=1):
    n_q, _, dk = q.shape
    n_kv = k.shape[0]
    group = n_q // n_kv
    has_ctx = kx is not None
    has_sink = sink is not None
    if window:
        n_qt = seq_len // (tq * seqs)
    else:
        n_qt = seq_len // tq
        assert seqs == 1 or (n_qt == 1 and not has_ctx)
    q_base = tok_base // (seqs * tq)
    s_base = tok_base // (seqs * seq_len)
    q_blk = lambda b, j: q_base + b * n_qt + j
    in_specs = [pl.BlockSpec((n_q, seqs * tq, dk), lambda b, j: (0, q_blk(b, j), 0))]
    args = [q]
    if window:
        assert window == tq
        n_kb = seq_len // tq
        kb_base = tok_base // tq

        def key_blk(b, j, off):
            return kb_base + b * n_kb + jnp.clip(j * seqs - 1 + off, 0, n_kb - 1)

        for off in range(seqs + 2):
            in_specs.append(pl.BlockSpec((n_kv, tq, dk), lambda b, j, off=off: (0, key_blk(b, j, off), 0)))
        for off in range(seqs + 2):
            in_specs.append(pl.BlockSpec((n_kv, VT_ROWS, tq), lambda b, j, off=off: (0, 0, key_blk(b, j, off))))
        args += [k] * (seqs + 2) + [vt] * (seqs + 2)
    else:
        in_specs += [
            pl.BlockSpec((n_kv, seqs * seq_len, dk), lambda b, j: (0, s_base + b, 0)),
            pl.BlockSpec((n_kv, VT_ROWS, seqs * seq_len), lambda b, j: (0, 0, s_base + b)),
        ]
        args += [k, vt]
    if has_ctx:
        in_specs += [
            pl.BlockSpec((n_kv, PAST_LEN, dk), lambda b, j: (0, b, 0)),
            pl.BlockSpec((n_kv, VT_ROWS, PAST_LEN), lambda b, j: (0, 0, b)),
        ]
        args += [kx, vx]
    if has_sink:
        in_specs.append(pl.BlockSpec(memory_space=pltpu.SMEM))
        args.append(sink)
    kern = functools.partial(_attn_kernel, n_kv=n_kv, group=group, tq=tq, seq_len=seq_len, seqs=seqs,
                             has_ctx=has_ctx, has_sink=has_sink, window=window,
                             q_unit=q_unit, key_chunk=key_chunk)
    return pl.pallas_call(
        kern, grid=(n_seq if window else n_seq // seqs, n_qt), in_specs=in_specs,
        out_specs=pl.BlockSpec((seqs * tq, n_q * HEAD_DIM), lambda b, j: (b * n_qt + j, 0)),
        out_shape=jax.ShapeDtypeStruct((n_seq * seq_len, n_q * HEAD_DIM), BF16),
        compiler_params=pltpu.CompilerParams(
            dimension_semantics=("arbitrary", "arbitrary"), vmem_limit_bytes=VMEM_LIMIT),
        name=name,
    )(*args)


def _ffn_kernel(*refs, n_o, is_ctx, final):
    halo = 0 if is_ctx else HALO
    it = iter(refs)
    x_ref = next(it)
    xh_refs = None if is_ctx else (next(it), next(it))
    o_refs, oh_refs = [], []
    for _ in range(n_o):
        o_refs.append(next(it))
        if not is_ctx:
            oh_refs.append((next(it), next(it)))
    wo_refs = [next(it) for _ in range(n_o)]
    mod_ref, gffn_ref = next(it), next(it)
    wg_refs = [next(it) for _ in range(FF_PER_STEP)]
    wv_refs = [next(it) for _ in range(FF_PER_STEP)]
    cw_ref, cb_ref = next(it), next(it)
    wd_refs = [next(it) for _ in range(FF_PER_STEP)]
    gfin_ref, out_ref, h2e_ref, acc_ref = next(it), next(it), next(it), next(it)

    m = pl.program_id(0)
    c = pl.program_id(1)
    g1 = mod_ref[2:3, :]
    sh2 = mod_ref[3:4, :]
    sc2 = mod_ref[4:5, :]
    g2 = mod_ref[5:6, :]
    sub = FFN_SUB if is_ctx else FFN_LAT_SUB
    n_sub = FFN_TM // sub
    sub_rows = sub + 2 * halo

    @pl.when(c == 0)
    def _():
        wos = [w[...].astype(BF16) for w in wo_refs]

        def residual_and_norm(xv, ovs):
            attn = _dot(ovs[0], wos[0])
            for ov, wo in zip(ovs[1:], wos[1:]):
                attn = attn + _dot(ov, wo)
            x1 = xv + g1 * attn
            return x1, (_rms(x1, gffn_ref[...]) * (1.0 + sc2) + sh2).astype(BF16)

        for r in range(0, FFN_TM, FFN_SUB):
            x1, h2 = residual_and_norm(x_ref[r:r + FFN_SUB, :], [o[r:r + FFN_SUB, :] for o in o_refs])
            acc_ref[r:r + FFN_SUB, :] = x1
            h2e_ref[halo + r:halo + r + FFN_SUB, :] = h2
        if not is_ctx:
            _, h2h = residual_and_norm(
                jnp.concatenate([xh_refs[0][...], xh_refs[1][...]], axis=0),
                [jnp.concatenate([oh[0][...], oh[1][...]], axis=0) for oh in oh_refs])
            h2e_ref[0:HALO, :] = h2h[0:HALO]
            h2e_ref[HALO + FFN_TM:, :] = h2h[HALO:]

    row8 = lax.broadcasted_iota(jnp.int32, (8, FF_CHUNK), 0)
    if not is_ctx:
        has_prev = m % FFN_LAT_TILES != 0
        has_next = m % FFN_LAT_TILES != FFN_LAT_TILES - 1

    def ff_chunks(n_chunks):
        w_up, w_dn, cw, cb = [], [], [], []
        for j in range(n_chunks):
            w_up.append(jnp.concatenate([wg_refs[j][...].astype(BF16), wv_refs[j][...].astype(BF16)],
                                        axis=1))
            w_dn.append(wd_refs[j][...].astype(BF16))
            cols = pl.ds(pl.multiple_of((c * FF_PER_STEP + j) * FF_CHUNK, FF_CHUNK), FF_CHUNK)
            cw.append(cw_ref[:, cols])
            cb.append(cb_ref[:, cols])
        tasks = [(j, r) for j in range(n_chunks) for r in range(n_sub)]
        ups = {}

        def emit_up(t):
            j, r = tasks[t]
            ups[t] = _dot(h2e_ref[r * sub:r * sub + sub_rows, :], w_up[j])

        for t in range(min(UP_LOOKAHEAD, len(tasks))):
            emit_up(t)
        for t, (j, r) in enumerate(tasks):
            if t + UP_LOOKAHEAD < len(tasks):
                emit_up(t + UP_LOOKAHEAD)
            up = ups.pop(t)
            ge = up[:, :FF_CHUNK]
            val = up[halo:halo + sub, FF_CHUNK:]
            g_prev = pltpu.roll(ge, 1, axis=0)[halo:halo + sub]
            g_next = pltpu.roll(ge, sub_rows - 1, axis=0)[halo:halo + sub]
            prev_ok = False if is_ctx else (has_prev if r == 0 else True)
            next_ok = False if is_ctx else (has_next if r == n_sub - 1 else True)
            if prev_ok is not True:
                g_prev = jnp.concatenate(
                    [jnp.where(jnp.logical_or(row8 != 0, prev_ok), g_prev[0:8], 0.0), g_prev[8:]], axis=0)
            if next_ok is not True:
                g_next = jnp.concatenate(
                    [g_next[:-8], jnp.where(jnp.logical_or(row8 != 7, next_ok), g_next[-8:], 0.0)], axis=0)
            gate = (g_prev * cw[j][0:1, :] + ge[halo:halo + sub] * cw[j][1:2, :] + g_next * cw[j][2:3, :]
                    + cb[j])
            act = (gate * jax.nn.sigmoid(gate) * val).astype(BF16)
            acc_ref[r * sub:(r + 1) * sub, :] += g2 * _dot(act, w_dn[j])

    n_tail = N_FF_CHUNKS % FF_PER_STEP
    if n_tail:
        pl.when(c < N_FF_STEPS - 1)(functools.partial(ff_chunks, FF_PER_STEP))
    else:
        ff_chunks(FF_PER_STEP)

    @pl.when(c == N_FF_STEPS - 1)
    def _():
        if n_tail:
            ff_chunks(n_tail)
        x2 = acc_ref[...]
        out_ref[...] = _rms(x2, gfin_ref[...]) if final else x2


def _ffn(x, os, mod, layer, g_ffn, w_out, w_up, conv_w, conv_b, w_down, g_final, *, is_ctx, final):
    n_rows = x.shape[0]
    nh = FFN_TM // HALO
    nblk = n_rows // HALO
    n_o = len(os)
    halo = 0 if is_ctx else HALO
    tile = lambda m, c: (m, 0)
    prev = lambda m, c: (jnp.maximum(m * nh - 1, 0), 0)
    nxt = lambda m, c: (jnp.minimum((m + 1) * nh, nblk - 1), 0)
    chunk = lambda c, j: jnp.minimum(c * FF_PER_STEP + j, N_FF_CHUNKS - 1)
    if is_ctx:
        cond = lambda m: layer * 8
    else:
        cond = lambda m: layer * 8 + 1 + m // FFN_LAT_TILES

    def with_halo(arr):
        w = arr.shape[1]
        specs = [pl.BlockSpec((FFN_TM, w), tile)]
        if not is_ctx:
            specs += [pl.BlockSpec((HALO, w), prev), pl.BlockSpec((HALO, w), nxt)]
        return specs, [arr] * len(specs)

    in_specs, args = with_halo(x)
    for o in os:
        specs, arrs = with_halo(o)
        in_specs += specs
        args += arrs
    w_rows = D_MODEL // n_o
    for t in range(n_o):
        in_specs.append(pl.BlockSpec((None, w_rows, D_MODEL), lambda m, c, t=t: (0, t, 0)))
        args.append(w_out)
    in_specs += [
        pl.BlockSpec((None, 6, D_MODEL), lambda m, c: (cond(m), 0, 0)),
        pl.BlockSpec((None, 1, D_MODEL), lambda m, c: (layer, 0, 0)),
    ]
    args += [mod, g_ffn]
    steps = range(FF_PER_STEP)
    in_specs += [pl.BlockSpec((None, D_MODEL, FF_CHUNK), lambda m, c, j=j: (layer, 0, chunk(c, j))) for j in steps]
    in_specs += [pl.BlockSpec((None, D_MODEL, FF_CHUNK), lambda m, c, j=j: (layer, 0, N_FF_CHUNKS + chunk(c, j)))
                 for j in steps]
    in_specs += [pl.BlockSpec((None, 3, D_FF), lambda m, c: (layer, 0, 0)),
                 pl.BlockSpec((None, 1, D_FF), lambda m, c: (layer, 0, 0))]
    in_specs += [pl.BlockSpec((None, FF_CHUNK, D_MODEL), lambda m, c, j=j: (layer, chunk(c, j), 0)) for j in steps]
    args += [w_up] * (2 * FF_PER_STEP) + [conv_w, conv_b] + [w_down] * FF_PER_STEP
    in_specs.append(pl.BlockSpec((1, D_MODEL), lambda m, c: (0, 0)))
    args.append(g_final)
    return pl.pallas_call(
        functools.partial(_ffn_kernel, n_o=n_o, is_ctx=is_ctx, final=final),
        grid=(n_rows // FFN_TM, N_FF_STEPS), in_specs=in_specs,
        out_specs=pl.BlockSpec((FFN_TM, D_MODEL), tile),
        out_shape=jax.ShapeDtypeStruct((n_rows, D_MODEL), F32),
        scratch_shapes=[pltpu.VMEM((FFN_TM + 2 * halo, D_MODEL), BF16), pltpu.VMEM((FFN_TM, D_MODEL), F32)],
        compiler_params=pltpu.CompilerParams(
            dimension_semantics=("arbitrary", "arbitrary"), vmem_limit_bytes=FFN_VMEM_LIMIT),
        name=("ffn_ctx" if is_ctx else "ffn_lat") + ("_final" if final else ""),
    )(*args)


def _rope_tables(rot_dim):
    f32 = np.float32
    t = np.arange(LAT_LEN)
    row = (t // GRID_W).astype(f32)
    col = (t % GRID_W).astype(f32)
    d_axis = rot_dim // 2
    freqs = (f32(ROPE_THETA) ** (-np.arange(0, d_axis, 2, dtype=f32) / f32(d_axis))).astype(f32)
    ang = np.concatenate([row[:, None] * freqs, col[:, None] * freqs], axis=-1)
    cos = np.repeat(np.cos(ang), 2, axis=-1).astype(f32)
    sin = (np.repeat(np.sin(ang), 2, axis=-1) * np.tile(np.array([-1.0, 1.0], f32), rot_dim // 2)).astype(f32)
    if rot_dim == HEAD_DIM:
        cos = np.tile(cos, (1, 2))
        sin = np.tile(sin, (1, 2))
    else:
        cos = np.concatenate([np.ones((LAT_LEN, QK_NOPE), f32), cos,
                              np.ones((LAT_LEN, 128 - QK_NOPE - rot_dim), f32)], axis=-1)
        sin = np.concatenate([np.zeros((LAT_LEN, QK_NOPE), f32), sin,
                              np.zeros((LAT_LEN, 128 - QK_NOPE - rot_dim), f32)], axis=-1)
    cos = np.concatenate([cos, np.ones((TM, 128), f32)], axis=0)
    sin = np.concatenate([sin, np.zeros((TM, 128), f32)], axis=0)
    return jnp.asarray(cos), jnp.asarray(sin)


def kernel(x_prompt, x_sample, cache_a_k, cache_a_v, cache_b_ckv, cache_b_kpe, cache_c_k, cache_c_v, c, c_ctx, w_mod, b_mod, g_mix_norm, g_ffn_norm, w_in_e, g_qnorm_a, g_knorm_a, g_cq_b, w_uq_b, g_ckv_b, w_ukv_b, w_out_e, w_in_o, sink_c, w_out_o, w_up, conv_w, conv_b, w_down, g_final):
    depth = w_mod.shape[0]
    cond8 = jnp.concatenate([c_ctx[None, :], c, jnp.zeros((5, D_MODEL), F32)], axis=0)
    lane_pad = KB_PAD - QK_NOPE - QK_ROPE
    wuq = jnp.pad(w_uq_b[0].reshape(Q_LORA, N_HEADS_B, QK_NOPE + QK_ROPE),
                  ((0, 0), (0, 0), (0, lane_pad))).reshape(Q_LORA, N_HEADS_B * KB_PAD)
    wukv3 = w_ukv_b[0].reshape(KV_LORA, N_HEADS_B, QK_NOPE + V_DIM_B)
    wukk = jnp.pad(wukv3[:, :, :QK_NOPE], ((0, 0), (0, 0), (0, KB_PAD - QK_NOPE))
                   ).reshape(KV_LORA, N_HEADS_B * KB_PAD)
    wukv = wukv3[:, :, QK_NOPE:].reshape(KV_LORA, N_HEADS_B * V_DIM_B)
    gq = jnp.tile(g_qnorm_a[0], N_HEADS_A)[None, :]
    gk = jnp.tile(g_knorm_a[0], N_KV_A)[None, :]
    seg = np.arange(256) // HEAD_DIM
    ones = jnp.asarray(seg[:, None] == seg[None, :], dtype=BF16)
    cosa, sina = _rope_tables(HEAD_DIM)
    cosb, sinb = _rope_tables(QK_ROPE)
    g_mix3 = g_mix_norm.reshape(depth, 1, D_MODEL)
    g_ffn3 = g_ffn_norm.reshape(depth, 1, D_MODEL)
    conv_b3 = conv_b.reshape(depth, 1, D_FF)
    g_fin2 = g_final[None, :]

    mod = _modulation(cond8, w_mod, b_mod).reshape(depth * 8, 6, D_MODEL)

    n_past = N_LAT_SEQ * PAST_LEN
    kpe_pad = jnp.pad(cache_b_kpe.reshape(n_past, QK_ROPE), ((0, 0), (QK_NOPE, lane_pad)))
    ka_c, va_c, kb_c, vb_c, kc_c, vc_c = _ctx_prep(
        cache_a_k.reshape(n_past, 128), cache_a_v.reshape(n_past, 128),
        cache_b_ckv.reshape(n_past, KV_LORA), kpe_pad,
        cache_c_k.reshape(n_past, 128), cache_c_v.reshape(n_past, 128), wukk, wukv)

    xp = x_prompt.reshape(N_CTX_TOK, D_MODEL)
    xs = x_sample.reshape(N_TOK - N_CTX_TOK, D_MODEL)
    qa, ka, va, qb, kb, vb, nak, nav, nckv, nkpe = _pre0(
        xp, xs, mod, g_mix3, w_in_e, gq, gk, g_cq_b.reshape(1, 1, Q_LORA), g_ckv_b.reshape(1, 1, KV_LORA),
        wuq, wukk, wukv, cosa, sina, cosb, sinb, ones)
    ffn_w = (w_up, conv_w, conv_b3, w_down, g_fin2)
    ctx_kw = dict(n_seq=N_CTX_SEQ, seq_len=CTX_LEN, tok_base=0, tq=CTX_LEN, window=0, q_unit=256,
                  key_chunk=CTX_LEN, seqs=8)
    lat_kw = dict(n_seq=N_LAT_SEQ, seq_len=LAT_LEN, tok_base=N_CTX_TOK, q_unit=256)
    dense_kw = dict(tq=256, window=0, key_chunk=256, **lat_kw)
    oa = (_attention(qa, ka, va, None, None, None, name="attn_a_ctx", **ctx_kw),
          _attention(qa, ka, va, ka_c, va_c, None, name="attn_a_lat", **dense_kw))
    ob = (_attention(qb, kb, vb, None, None, None, name="attn_b_ctx", **ctx_kw),
          _attention(qb, kb, vb, kb_c, vb_c, None, name="attn_b_lat", **dense_kw))
    xp1 = _ffn(xp, [oa[0], ob[0]], mod, 0, g_ffn3, w_out_e, *ffn_w, is_ctx=True, final=False)
    xs1 = _ffn(xs, [oa[1], ob[1]], mod, 0, g_ffn3, w_out_e, *ffn_w, is_ctx=False, final=False)

    qc, kc, vc, nck, ncv = _pre1(xp1, xs1, mod, g_mix3, w_in_o, cosa, sina)
    sink = sink_c[0]
    oc = (_attention(qc, kc, vc, None, None, sink, name="attn_c_ctx", **ctx_kw),
          _attention(qc, kc, vc, kc_c, vc_c, sink, tq=128, window=WINDOW, key_chunk=PAST_LEN, seqs=4,
                     name="attn_c_lat", **lat_kw))
    y_prompt = _ffn(xp1, [oc[0]], mod, 1, g_ffn3, w_out_o, *ffn_w, is_ctx=True, final=True)
    y_sample = _ffn(xs1, [oc[1]], mod, 1, g_ffn3, w_out_o, *ffn_w, is_ctx=False, final=True)

    return (y_prompt.reshape(N_CTX_SEQ, CTX_LEN, D_MODEL), y_sample.reshape(N_LAT_SEQ, LAT_LEN, D_MODEL),
            nak.reshape(N_CTX_SEQ, 1, CTX_LEN, N_KV_A, HEAD_DIM),
            nav.reshape(N_CTX_SEQ, 1, CTX_LEN, N_KV_A, HEAD_DIM),
            nckv.reshape(N_CTX_SEQ, 1, CTX_LEN, KV_LORA),
            nkpe.reshape(N_CTX_SEQ, 1, CTX_LEN, QK_ROPE),
            nck.reshape(N_CTX_SEQ, 1, CTX_LEN, N_KV_C, HEAD_DIM),
            ncv.reshape(N_CTX_SEQ, 1, CTX_LEN, N_KV_C, HEAD_DIM))
```

```python
import functools

import jax
import jax.numpy as jnp
import numpy as np
from jax import lax
from jax.experimental import pallas as pl
from jax.experimental.pallas import tpu as pltpu

F32 = jnp.float32
BF16 = jnp.bfloat16

D_MODEL = 1024
N_CTX_SEQ = 16
CTX_LEN = 256
N_LAT_SEQ = 2
LAT_LEN = 2048
PAST_LEN = 512
GRID_W = 64
ROPE_THETA = 10000.0
NORM_EPS = 1e-6
WINDOW = 128
NEG_INF = -1e30
LOG2E = 1.4426950408889634
HEAD_DIM = 64
N_HEADS_A, N_KV_A = 8, 2
N_HEADS_B = 8
Q_LORA, KV_LORA = 384, 256
QK_NOPE, QK_ROPE, V_DIM_B = 64, 32, 64
N_HEADS_C, N_KV_C = 16, 2
D_FF = 2816
IN_E_MAIN = N_HEADS_A * HEAD_DIM + 2 * N_KV_A * HEAD_DIM + Q_LORA + KV_LORA

N_CTX_TOK = N_CTX_SEQ * CTX_LEN
N_TOK = N_CTX_TOK + N_LAT_SEQ * LAT_LEN
TM = 512
PRE_SUB = 256
N_TILES = N_TOK // TM
N_CTX_TILES = N_CTX_TOK // TM
LAT_TILES = LAT_LEN // TM
HALO = 16
FFN_TM = 1024
FFN_SUB = CTX_LEN
FFN_LAT_SUB = 256
UP_LOOKAHEAD = 3
FFN_LAT_TILES = LAT_LEN // FFN_TM
FF_CHUNK = 256
N_FF_CHUNKS = D_FF // FF_CHUNK
FF_PER_STEP = 3
N_FF_STEPS = -(-N_FF_CHUNKS // FF_PER_STEP)
KB_PAD = 128
VT_ROWS = 80
SCORE_LOOKAHEAD = 5
VMEM_LIMIT = 56 * 1024 * 1024
FFN_VMEM_LIMIT = 60 * 1024 * 1024


def _dot(a, b):
    return jnp.dot(a, b, preferred_element_type=F32)


def _dot_nt(a, b):
    return lax.dot_general(a, b, (((1,), (1,)), ((), ())), preferred_element_type=F32)


def _rms(x, g):
    return x * lax.rsqrt(jnp.mean(x * x, axis=-1, keepdims=True) + NORM_EPS) * g


def _split_bf16(x):
    hi = x.astype(BF16)
    return hi, (x - hi.astype(F32)).astype(BF16)


def _head_rms(x, g, ones_ref):
    w = x.shape[1]
    hi, lo = _split_bf16(x * x)
    parts = []
    for c in range(0, w, 256):
        cw = min(256, w - c)
        ones = ones_ref[0:cw, 0:cw]
        parts.append(_dot(hi[:, c:c + cw], ones) + _dot(lo[:, c:c + cw], ones))
    ssum = parts[0] if len(parts) == 1 else jnp.concatenate(parts, axis=1)
    return x * lax.rsqrt(ssum * (1.0 / HEAD_DIM) + NORM_EPS) * g


def _swap_pairs(x):
    w = x.shape[1]
    up = pltpu.roll(x, w - 1, axis=1)
    dn = pltpu.roll(x, 1, axis=1)
    lane = lax.broadcasted_iota(jnp.int32, x.shape, 1)
    return jnp.where((lane & 1) == 0, up, dn)


def _rope(x, cos, sin_signed, reps):
    if reps > 1:
        cos = jnp.concatenate([cos] * reps, axis=1)
        sin_signed = jnp.concatenate([sin_signed] * reps, axis=1)
    return x * cos + _swap_pairs(x) * sin_signed


def _store_vt(vt_ref, v, n_heads, cols=slice(None)):
    t = v.shape[0]
    vt = v.T.astype(BF16)
    ones = jnp.ones((VT_ROWS - 64, t), BF16)
    for hh in range(n_heads):
        vt_ref[hh, 0:64, cols] = vt[hh * 64:(hh + 1) * 64]
        vt_ref[hh, 64:VT_ROWS, cols] = ones


def _cond_row(i):
    return jnp.where(i < N_CTX_TILES, 0, 1 + (i - N_CTX_TILES) // LAT_TILES)


def _rope_block(i):
    return jnp.where(i < N_CTX_TILES, LAT_TILES, (i - N_CTX_TILES) % LAT_TILES)


def _const_spec(shape):
    zeros = (0,) * len(shape)
    return pl.BlockSpec(shape, lambda *_: zeros)


def _layer_spec(shape, layer):
    idx = (layer,) + (0,) * len(shape)
    return pl.BlockSpec((None,) + tuple(shape), lambda *_: idx)


def _mod_kernel(cond_ref, w_ref, b_ref, o_ref):
    c = cond_ref[...]
    s_hi, s_lo = _split_bf16(c * jax.nn.sigmoid(c))
    w_hi, w_lo = _split_bf16(w_ref[0])
    r = _dot(jnp.concatenate([s_hi, s_lo], axis=0), w_hi)
    o_ref[0] = r[0:8] + r[8:16] + _dot(s_hi, w_lo) + b_ref[0]


def _modulation(cond8, w_mod, b_mod):
    depth, _, n = w_mod.shape
    tn = 3072
    return pl.pallas_call(
        _mod_kernel,
        grid=(depth, n // tn),
        in_specs=[
            pl.BlockSpec((8, D_MODEL), lambda l, j: (0, 0)),
            pl.BlockSpec((1, D_MODEL, tn), lambda l, j: (l, 0, j)),
            pl.BlockSpec((1, 1, tn), lambda l, j: (l, 0, j)),
        ],
        out_specs=pl.BlockSpec((1, 8, tn), lambda l, j: (l, 0, j)),
        out_shape=jax.ShapeDtypeStruct((depth, 8, n), F32),
        compiler_params=pltpu.CompilerParams(
            dimension_semantics=("arbitrary", "arbitrary"), vmem_limit_bytes=VMEM_LIMIT),
        name="modulation",
    )(cond8, w_mod, b_mod.reshape(depth, 1, n))


def _pre0_kernel(xp_ref, xs_ref, mod_ref, gmix_ref, win_ref, gq_ref, gk_ref, gcq_ref, gckv_ref,
                 wuq_ref, wukk_ref, wukv_ref, cosa_ref, sina_ref, cosb_ref, sinb_ref, ones_ref,
                 qa_ref, ka_ref, va_ref, qb_ref, kb_ref, vb_ref,
                 nak_ref, nav_ref, nckv_ref, nkpe_ref,
                 win_s, wuq_s, wukk_s, wukv_s):
    i = pl.program_id(0)

    @pl.when(i == 0)
    def _():
        win_s[:, 0:IN_E_MAIN] = win_ref[:, 0:IN_E_MAIN].astype(BF16)
        win_s[:, IN_E_MAIN:] = jnp.concatenate(
            [jnp.zeros((D_MODEL, QK_NOPE), F32), win_ref[:, IN_E_MAIN:],
             jnp.zeros((D_MODEL, KB_PAD - QK_NOPE - QK_ROPE), F32)], axis=1).astype(BF16)
        wuq_s[...] = wuq_ref[...].astype(BF16)
        wukk_s[...] = wukk_ref[...].astype(BF16)
        wukv_s[...] = wukv_ref[...].astype(BF16)

    sh1 = mod_ref[0:1, :]
    sc1 = mod_ref[1:2, :]
    subs = [slice(r, r + PRE_SUB) for r in range(0, TM, PRE_SUB)]

    projs = []
    for sl in subs:
        x = jnp.where(i < N_CTX_TILES, xp_ref[sl, :], xs_ref[sl, :])
        h = _rms(x, gmix_ref[...]) * (1.0 + sc1) + sh1
        projs.append(_dot(h.astype(BF16), win_s[...]))

    parts = []
    for sl, proj in zip(subs, projs):
        qa = _head_rms(proj[:, 0:512], gq_ref[...], ones_ref)
        ka = _head_rms(proj[:, 512:640], gk_ref[...], ones_ref)
        va = proj[:, 640:768]
        cq = _rms(proj[:, 768:1152], gcq_ref[...])
        ckv = _rms(proj[:, 1152:1408], gckv_ref[...])
        kpe = proj[:, 1408:1536]
        qb = _dot(cq.astype(BF16), wuq_s[...])
        ckv_b = ckv.astype(BF16)
        kbn = _dot(ckv_b, wukk_s[...])
        vb = _dot(ckv_b, wukv_s[...])
        _store_vt(va_ref, va, N_KV_A, sl)
        _store_vt(vb_ref, vb, N_HEADS_B, sl)
        parts.append((ka, va, ckv, kpe))
        cosa, sina = cosa_ref[sl, :], sina_ref[sl, :]
        cosb, sinb = cosb_ref[sl, :], sinb_ref[sl, :]
        qa_s = _rope(qa, cosa, sina, 4) * (LOG2E * HEAD_DIM ** -0.5)
        ka_r = _rope(ka, cosa, sina, 1)
        qb_s = _rope(qb, cosb, sinb, 8) * (LOG2E * (QK_NOPE + QK_ROPE) ** -0.5)
        kpe_r = _rope(kpe, cosb, sinb, 1)
        for hh in range(N_HEADS_A):
            qa_ref[hh, sl, :] = qa_s[:, hh * 64:(hh + 1) * 64].astype(BF16)
        for hh in range(N_KV_A):
            ka_ref[hh, sl, :] = ka_r[:, hh * 64:(hh + 1) * 64].astype(BF16)
        for hh in range(N_HEADS_B):
            qb_ref[hh, sl, :] = qb_s[:, hh * KB_PAD:(hh + 1) * KB_PAD].astype(BF16)
            kb_ref[hh, sl, :] = (kbn[:, hh * KB_PAD:(hh + 1) * KB_PAD] + kpe_r).astype(BF16)

    @pl.when(i < N_CTX_TILES)
    def _():
        for sl, (ka, va, ckv, kpe) in zip(subs, parts):
            nak_ref[sl, :] = ka
            nav_ref[sl, :] = va
            nckv_ref[sl, :] = ckv
            nkpe_ref[sl, :] = kpe[:, QK_NOPE:QK_NOPE + QK_ROPE]


def _pre0(xp, xs, mod, gmix, w_in_e, gq, gk, gcq, gckv, wuq, wukk, wukv, cosa, sina, cosb, sinb, ones):
    tile = lambda i: (i, 0)
    head_tile = lambda i: (0, i, 0)
    vt_tile = lambda i: (0, 0, i)
    ctx_tile = lambda i: (jnp.minimum(i, N_CTX_TILES - 1), 0)
    lat_tile = lambda i: (jnp.maximum(i - N_CTX_TILES, 0), 0)
    rope_tile = lambda i: (_rope_block(i), 0)
    in_specs = [
        pl.BlockSpec((TM, D_MODEL), ctx_tile),
        pl.BlockSpec((TM, D_MODEL), lat_tile),
        pl.BlockSpec((None, 6, D_MODEL), lambda i: (_cond_row(i), 0, 0)),
        _layer_spec((1, D_MODEL), 0),
        _layer_spec(w_in_e.shape[1:], 0),
        _const_spec(gq.shape), _const_spec(gk.shape), _layer_spec((1, Q_LORA), 0), _layer_spec((1, KV_LORA), 0),
        _const_spec(wuq.shape), _const_spec(wukk.shape), _const_spec(wukv.shape),
        pl.BlockSpec((TM, 128), rope_tile), pl.BlockSpec((TM, 128), rope_tile),
        pl.BlockSpec((TM, 128), rope_tile), pl.BlockSpec((TM, 128), rope_tile),
        _const_spec(ones.shape),
    ]
    out_shape = [
        jax.ShapeDtypeStruct((N_HEADS_A, N_TOK, 64), BF16),
        jax.ShapeDtypeStruct((N_KV_A, N_TOK, 64), BF16),
        jax.ShapeDtypeStruct((N_KV_A, VT_ROWS, N_TOK), BF16),
        jax.ShapeDtypeStruct((N_HEADS_B, N_TOK, KB_PAD), BF16),
        jax.ShapeDtypeStruct((N_HEADS_B, N_TOK, KB_PAD), BF16),
        jax.ShapeDtypeStruct((N_HEADS_B, VT_ROWS, N_TOK), BF16),
        jax.ShapeDtypeStruct((N_CTX_TOK, 128), F32),
        jax.ShapeDtypeStruct((N_CTX_TOK, 128), F32),
        jax.ShapeDtypeStruct((N_CTX_TOK, KV_LORA), F32),
        jax.ShapeDtypeStruct((N_CTX_TOK, QK_ROPE), F32),
    ]
    out_specs = [
        pl.BlockSpec((N_HEADS_A, TM, 64), head_tile),
        pl.BlockSpec((N_KV_A, TM, 64), head_tile),
        pl.BlockSpec((N_KV_A, VT_ROWS, TM), vt_tile),
        pl.BlockSpec((N_HEADS_B, TM, KB_PAD), head_tile),
        pl.BlockSpec((N_HEADS_B, TM, KB_PAD), head_tile),
        pl.BlockSpec((N_HEADS_B, VT_ROWS, TM), vt_tile),
        pl.BlockSpec((TM, 128), ctx_tile),
        pl.BlockSpec((TM, 128), ctx_tile),
        pl.BlockSpec((TM, KV_LORA), ctx_tile),
        pl.BlockSpec((TM, QK_ROPE), ctx_tile),
    ]
    scratch = [
        pltpu.VMEM((D_MODEL, IN_E_MAIN + KB_PAD), BF16),
        pltpu.VMEM(wuq.shape, BF16), pltpu.VMEM(wukk.shape, BF16), pltpu.VMEM(wukv.shape, BF16),
    ]
    return pl.pallas_call(
        _pre0_kernel, grid=(N_TILES,), in_specs=in_specs, out_specs=out_specs, out_shape=out_shape,
        scratch_shapes=scratch,
        compiler_params=pltpu.CompilerParams(
            dimension_semantics=("arbitrary",), vmem_limit_bytes=VMEM_LIMIT),
        name="pre0",
    )(xp, xs, mod, gmix, w_in_e, gq, gk, gcq, gckv, wuq, wukk, wukv, cosa, sina, cosb, sinb, ones)


def _pre1_kernel(xp_ref, xs_ref, mod_ref, gmix_ref, win_ref, cosa_ref, sina_ref,
                 q_ref, k_ref, v_ref, nk_ref, nv_ref, win_s):
    i = pl.program_id(0)

    @pl.when(i == 0)
    def _():
        win_s[...] = win_ref[...].astype(BF16)

    sh1 = mod_ref[0:1, :]
    sc1 = mod_ref[1:2, :]
    subs = [slice(r, r + PRE_SUB) for r in range(0, TM, PRE_SUB)]
    projs = []
    for sl in subs:
        x = jnp.where(i < N_CTX_TILES, xp_ref[sl, :], xs_ref[sl, :])
        h = _rms(x, gmix_ref[...]) * (1.0 + sc1) + sh1
        projs.append(_dot(h.astype(BF16), win_s[...]))
    for sl, proj in zip(subs, projs):
        cosa, sina = cosa_ref[sl, :], sina_ref[sl, :]
        q_s = _rope(proj[:, 0:1024], cosa, sina, 8) * (LOG2E * HEAD_DIM ** -0.5)
        k = _rope(proj[:, 1024:1152], cosa, sina, 1)
        for hh in range(N_HEADS_C):
            q_ref[hh, sl, :] = q_s[:, hh * 64:(hh + 1) * 64].astype(BF16)
        for hh in range(N_KV_C):
            k_ref[hh, sl, :] = k[:, hh * 64:(hh + 1) * 64].astype(BF16)
        _store_vt(v_ref, proj[:, 1152:1280], N_KV_C, sl)

    @pl.when(i < N_CTX_TILES)
    def _():
        for sl, proj in zip(subs, projs):
            nk_ref[sl, :] = proj[:, 1024:1152]
            nv_ref[sl, :] = proj[:, 1152:1280]


def _pre1(xp, xs, mod, gmix, w_in_o, cosa, sina):
    n_in = w_in_o.shape[-1]
    head_tile = lambda i: (0, i, 0)
    vt_tile = lambda i: (0, 0, i)
    ctx_tile = lambda i: (jnp.minimum(i, N_CTX_TILES - 1), 0)
    lat_tile = lambda i: (jnp.maximum(i - N_CTX_TILES, 0), 0)
    rope_tile = lambda i: (_rope_block(i), 0)
    in_specs = [
        pl.BlockSpec((TM, D_MODEL), ctx_tile),
        pl.BlockSpec((TM, D_MODEL), lat_tile),
        pl.BlockSpec((None, 6, D_MODEL), lambda i: (8 + _cond_row(i), 0, 0)),
        _layer_spec((1, D_MODEL), 1),
        _layer_spec((D_MODEL, n_in), 0),
        pl.BlockSpec((TM, 128), rope_tile), pl.BlockSpec((TM, 128), rope_tile),
    ]
    out_shape = [
        jax.ShapeDtypeStruct((N_HEADS_C, N_TOK, 64), BF16),
        jax.ShapeDtypeStruct((N_KV_C, N_TOK, 64), BF16),
        jax.ShapeDtypeStruct((N_KV_C, VT_ROWS, N_TOK), BF16),
        jax.ShapeDtypeStruct((N_CTX_TOK, 128), F32),
        jax.ShapeDtypeStruct((N_CTX_TOK, 128), F32),
    ]
    out_specs = [
        pl.BlockSpec((N_HEADS_C, TM, 64), head_tile),
        pl.BlockSpec((N_KV_C, TM, 64), head_tile),
        pl.BlockSpec((N_KV_C, VT_ROWS, TM), vt_tile),
        pl.BlockSpec((TM, 128), ctx_tile),
        pl.BlockSpec((TM, 128), ctx_tile),
    ]
    return pl.pallas_call(
        _pre1_kernel, grid=(N_TILES,), in_specs=in_specs, out_specs=out_specs, out_shape=out_shape,
        scratch_shapes=[pltpu.VMEM((D_MODEL, n_in), BF16)],
        compiler_params=pltpu.CompilerParams(
            dimension_semantics=("arbitrary",), vmem_limit_bytes=VMEM_LIMIT),
        name="pre1",
    )(xp, xs, mod, gmix, w_in_o, cosa, sina)


def _ctx_kernel(ak_ref, av_ref, ckv_ref, kpe_ref, ck_ref, cv_ref, wukk_ref, wukv_ref,
                ka_ref, va_ref, kb_ref, vb_ref, kc_ref, vc_ref):
    ones = jnp.ones((VT_ROWS - 64, PAST_LEN), BF16)
    for hh in range(2):
        ka_ref[hh] = ak_ref[:, hh, :].astype(BF16)
        kc_ref[hh] = ck_ref[:, hh, :].astype(BF16)
        for src, dst in ((av_ref, va_ref), (cv_ref, vc_ref)):
            dst[hh, 0:64, :] = src[:, hh, :].T.astype(BF16)
            dst[hh, 64:VT_ROWS, :] = ones
    ckv_b = ckv_ref[...].astype(BF16)
    kbn = _dot(ckv_b, wukk_ref[...].astype(BF16))
    vb = _dot(ckv_b, wukv_ref[...].astype(BF16))
    kpe = kpe_ref[...]
    for hh in range(N_HEADS_B):
        kb_ref[hh] = (kbn[:, hh * KB_PAD:(hh + 1) * KB_PAD] + kpe).astype(BF16)
    _store_vt(vb_ref, vb, N_HEADS_B)


def _ctx_prep(ak, av, ckv, kpe_pad, ck, cv, wukk, wukv):
    n = N_LAT_SEQ * PAST_LEN
    row = lambda b: (b, 0)
    head_row = lambda b: (0, b, 0)
    vt_row = lambda b: (0, 0, b)
    cache_spec = pl.BlockSpec((None, None, PAST_LEN, 2, HEAD_DIM), lambda b: (b, 0, 0, 0, 0))
    in_specs = [
        cache_spec, cache_spec,
        pl.BlockSpec((PAST_LEN, KV_LORA), row), pl.BlockSpec((PAST_LEN, KB_PAD), row),
        cache_spec, cache_spec,
        _const_spec(wukk.shape), _const_spec(wukv.shape),
    ]
    out_shape = [
        jax.ShapeDtypeStruct((2, n, 64), BF16), jax.ShapeDtypeStruct((2, VT_ROWS, n), BF16),
        jax.ShapeDtypeStruct((N_HEADS_B, n, KB_PAD), BF16), jax.ShapeDtypeStruct((N_HEADS_B, VT_ROWS, n), BF16),
        jax.ShapeDtypeStruct((2, n, 64), BF16), jax.ShapeDtypeStruct((2, VT_ROWS, n), BF16),
    ]
    out_specs = [
        pl.BlockSpec((2, PAST_LEN, 64), head_row), pl.BlockSpec((2, VT_ROWS, PAST_LEN), vt_row),
        pl.BlockSpec((N_HEADS_B, PAST_LEN, KB_PAD), head_row), pl.BlockSpec((N_HEADS_B, VT_ROWS, PAST_LEN), vt_row),
        pl.BlockSpec((2, PAST_LEN, 64), head_row), pl.BlockSpec((2, VT_ROWS, PAST_LEN), vt_row),
    ]
    return pl.pallas_call(
        _ctx_kernel, grid=(N_LAT_SEQ,), in_specs=in_specs, out_specs=out_specs, out_shape=out_shape,
        compiler_params=pltpu.CompilerParams(
            dimension_semantics=("arbitrary",), vmem_limit_bytes=VMEM_LIMIT),
        name="ctx_prep",
    )(ak, av, ckv, kpe_pad, ck, cv, wukk, wukv)


def _softmax_units(units, lookahead):
    tasks = [(u, c) for u, unit in enumerate(units) for c in range(len(unit["chunks"]))]
    scores = {}
    qts = [unit["q"].astype(F32).T.astype(BF16) if len(unit["chunks"]) >= 4 else None for unit in units]

    def emit_scores(t):
        u, c = tasks[t]
        k, _, mask = units[u]["chunks"][c]
        s = _dot_nt(k, units[u]["q"]) if qts[u] is None else _dot(k, qts[u])
        scores[t] = s if mask is None else jnp.where(mask, s, NEG_INF)

    for t in range(min(lookahead, len(tasks))):
        emit_scores(t)
    for t, (u, c) in enumerate(tasks):
        if t + lookahead < len(tasks):
            emit_scores(t + lookahead)
        unit = units[u]
        s = scores.pop(t)
        m, acc = unit["m"], unit["acc"]
        cmax = jnp.max(s, axis=0, keepdims=True)
        m_new = cmax if m is None else jnp.maximum(m, cmax)
        pv = _dot(unit["chunks"][c][1], jnp.exp2(s - m_new).astype(BF16))
        unit["acc"] = pv if acc is None else acc * jnp.exp2(m - m_new) + pv
        unit["m"] = m_new
    return [unit["acc"][0:64] * (1.0 / unit["acc"][64:65]) for unit in units]


def _attn_kernel(*refs, n_kv, group, tq, seq_len, seqs, has_ctx, has_sink, window, q_unit, key_chunk):
    refs = list(refs)
    q_ref = refs.pop(0)
    if window:
        n_blk = seqs + 2
        kb_refs, vb_refs = refs[:n_blk], refs[n_blk:2 * n_blk]
        refs = refs[2 * n_blk:]
    else:
        k_ref, vt_ref = refs[:2]
        refs = refs[2:]
    if has_ctx:
        kx_ref, vx_ref = refs[:2]
        refs = refs[2:]
    if has_sink:
        sink_ref = refs.pop(0)
    o_ref = refs.pop(0)

    j = pl.program_id(1)
    dk = q_ref.shape[-1]
    heads_per_unit = q_unit // tq
    lane = lax.broadcasted_iota(jnp.int32, (1, q_unit), 1)
    if window:
        n_band = tq + 2 * window
        krow = lax.broadcasted_iota(jnp.int32, (n_band, q_unit), 0)
        qcol = lax.broadcasted_iota(jnp.int32, (n_band, q_unit), 1) & (tq - 1)
        rel = (krow - window) - qcol
        in_band = jnp.abs(rel) <= window
        last_j = seq_len // (tq * seqs) - 1
        band_masks = []
        for sq in range(seqs):
            mask = in_band
            if sq == 0:
                mask = mask & ((krow >= window) | (j > 0))
            if sq == seqs - 1:
                mask = mask & ((krow < window + tq) | (j < last_j))
            band_masks.append(mask)
    if has_sink:
        acc0 = jnp.where(lax.broadcasted_iota(jnp.int32, (VT_ROWS, q_unit), 0) >= 64, 1.0, 0.0)

    units = []
    for sq in range(seqs):
        base = sq * (tq if window else seq_len)
        for hk in range(n_kv):
            chunks = []
            if window:
                chunks.append((jnp.concatenate([r[hk] for r in kb_refs[sq:sq + 3]], axis=0),
                               jnp.concatenate([r[hk] for r in vb_refs[sq:sq + 3]], axis=1), band_masks[sq]))
            else:
                for c in range(base, base + seq_len, key_chunk):
                    n = min(key_chunk, base + seq_len - c)
                    chunks.append((k_ref[hk, c:c + n, :], vt_ref[hk, :, c:c + n], None))
            if has_ctx:
                for c in range(0, PAST_LEN, key_chunk):
                    n = min(key_chunk, PAST_LEN - c)
                    chunks.append((kx_ref[hk, c:c + n, :], vx_ref[hk, :, c:c + n], None))
            for u in range(group // heads_per_unit):
                h0 = hk * group + u * heads_per_unit
                q = q_ref[h0:h0 + heads_per_unit, base:base + tq, :].reshape(q_unit, dk)
                unit = dict(q=q, chunks=chunks, m=None, acc=None)
                if has_sink:
                    m0 = jnp.full((1, q_unit), sink_ref[h0] * LOG2E, F32)
                    for e in range(1, heads_per_unit):
                        m0 = jnp.where(lane >= e * tq, sink_ref[h0 + e] * LOG2E, m0)
                    unit.update(m=m0, acc=acc0)
                units.append(unit)
    results = _softmax_units(units, SCORE_LOOKAHEAD)
    per_seq = len(results) // seqs
    for sq in range(seqs):
        outs = []
        for o in results[sq * per_seq:(sq + 1) * per_seq]:
            for e in range(heads_per_unit):
                outs.append(o[:, e * tq:(e + 1) * tq])
        base = sq * (tq if window else seq_len)
        o_ref[base:base + tq, :] = jnp.concatenate(outs, axis=0).T.astype(BF16)


def _attention(q, k, vt, kx, vx, sink, *, n_seq, seq_len, tok_base, tq, window, q_unit, key_chunk, name,
               seqs=1):
    n_q, _, dk = q.shape
    n_kv = k.shape[0]
    group = n_q // n_kv
    has_ctx = kx is not None
    has_sink = sink is not None
    if window:
        n_qt = seq_len // (tq * seqs)
    else:
        n_qt = seq_len // tq
        assert seqs == 1 or (n_qt == 1 and not has_ctx)
    q_base = tok_base // (seqs * tq)
    s_base = tok_base // (seqs * seq_len)
    q_blk = lambda b, j: q_base + b * n_qt + j
    in_specs = [pl.BlockSpec((n_q, seqs * tq, dk), lambda b, j: (0, q_blk(b, j), 0))]
    args = [q]
    if window:
        assert window == tq
        n_kb = seq_len // tq
        kb_base = tok_base // tq

        def key_blk(b, j, off):
            return kb_base + b * n_kb + jnp.clip(j * seqs - 1 + off, 0, n_kb - 1)

        for off in range(seqs + 2):
            in_specs.append(pl.BlockSpec((n_kv, tq, dk), lambda b, j, off=off: (0, key_blk(b, j, off), 0)))
        for off in range(seqs + 2):
            in_specs.append(pl.BlockSpec((n_kv, VT_ROWS, tq), lambda b, j, off=off: (0, 0, key_blk(b, j, off))))
        args += [k] * (seqs + 2) + [vt] * (seqs + 2)
    else:
        in_specs += [
            pl.BlockSpec((n_kv, seqs * seq_len, dk), lambda b, j: (0, s_base + b, 0)),
            pl.BlockSpec((n_kv, VT_ROWS, seqs * seq_len), lambda b, j: (0, 0, s_base + b)),
        ]
        args += [k, vt]
    if has_ctx:
        in_specs += [
            pl.BlockSpec((n_kv, PAST_LEN, dk), lambda b, j: (0, b, 0)),
            pl.BlockSpec((n_kv, VT_ROWS, PAST_LEN), lambda b, j: (0, 0, b)),
        ]
        args += [kx, vx]
    if has_sink:
        in_specs.append(pl.BlockSpec(memory_space=pltpu.SMEM))
        args.append(sink)
    kern = functools.partial(_attn_kernel, n_kv=n_kv, group=group, tq=tq, seq_len=seq_len, seqs=seqs,
                             has_ctx=has_ctx, has_sink=has_sink, window=window,
                             q_unit=q_unit, key_chunk=key_chunk)
    return pl.pallas_call(
        kern, grid=(n_seq if window else n_seq // seqs, n_qt), in_specs=in_specs,
        out_specs=pl.BlockSpec((seqs * tq, n_q * HEAD_DIM), lambda b, j: (b * n_qt + j, 0)),
        out_shape=jax.ShapeDtypeStruct((n_seq * seq_len, n_q * HEAD_DIM), BF16),
        compiler_params=pltpu.CompilerParams(
            dimension_semantics=("arbitrary", "arbitrary"), vmem_limit_bytes=VMEM_LIMIT),
        name=name,
    )(*args)


def _ffn_kernel(*refs, n_o, is_ctx, final):
    halo = 0 if is_ctx else HALO
    it = iter(refs)
    x_ref = next(it)
    xh_refs = None if is_ctx else (next(it), next(it))
    o_refs, oh_refs = [], []
    for _ in range(n_o):
        o_refs.append(next(it))
        if not is_ctx:
            oh_refs.append((next(it), next(it)))
    wo_refs = [next(it) for _ in range(n_o)]
    mod_ref, gffn_ref = next(it), next(it)
    wg_refs = [next(it) for _ in range(FF_PER_STEP)]
    wv_refs = [next(it) for _ in range(FF_PER_STEP)]
    cw_ref, cb_ref = next(it), next(it)
    wd_refs = [next(it) for _ in range(FF_PER_STEP)]
    gfin_ref, out_ref, h2e_ref, acc_ref = next(it), next(it), next(it), next(it)

    m = pl.program_id(0)
    c = pl.program_id(1)
    g1 = mod_ref[2:3, :]
    sh2 = mod_ref[3:4, :]
    sc2 = mod_ref[4:5, :]
    g2 = mod_ref[5:6, :]
    sub = FFN_SUB if is_ctx else FFN_LAT_SUB
    n_sub = FFN_TM // sub
    sub_rows = sub + 2 * halo

    @pl.when(c == 0)
    def _():
        wos = [w[...].astype(BF16) for w in wo_refs]

        def residual_and_norm(xv, ovs):
            attn = _dot(ovs[0], wos[0])
            for ov, wo in zip(ovs[1:], wos[1:]):
                attn = attn + _dot(ov, wo)
            x1 = xv + g1 * attn
            return x1, (_rms(x1, gffn_ref[...]) * (1.0 + sc2) + sh2).astype(BF16)

        for r in range(0, FFN_TM, FFN_SUB):
            x1, h2 = residual_and_norm(x_ref[r:r + FFN_SUB, :], [o[r:r + FFN_SUB, :] for o in o_refs])
            acc_ref[r:r + FFN_SUB, :] = x1
            h2e_ref[halo + r:halo + r + FFN_SUB, :] = h2
        if not is_ctx:
            _, h2h = residual_and_norm(
                jnp.concatenate([xh_refs[0][...], xh_refs[1][...]], axis=0),
                [jnp.concatenate([oh[0][...], oh[1][...]], axis=0) for oh in oh_refs])
            h2e_ref[0:HALO, :] = h2h[0:HALO]
            h2e_ref[HALO + FFN_TM:, :] = h2h[HALO:]

    row8 = lax.broadcasted_iota(jnp.int32, (8, FF_CHUNK), 0)
    if not is_ctx:
        has_prev = m % FFN_LAT_TILES != 0
        has_next = m % FFN_LAT_TILES != FFN_LAT_TILES - 1

    def ff_chunks(n_chunks):
        w_up, w_dn, cw, cb = [], [], [], []
        for j in range(n_chunks):
            w_up.append(jnp.concatenate([wg_refs[j][...].astype(BF16), wv_refs[j][...].astype(BF16)],
                                        axis=1))
            w_dn.append(wd_refs[j][...].astype(BF16))
            cols = pl.ds(pl.multiple_of((c * FF_PER_STEP + j) * FF_CHUNK, FF_CHUNK), FF_CHUNK)
            cw.append(cw_ref[:, cols])
            cb.append(cb_ref[:, cols])
        tasks = [(j, r) for j in range(n_chunks) for r in range(n_sub)]
        ups = {}

        def emit_up(t):
            j, r = tasks[t]
            ups[t] = _dot(h2e_ref[r * sub:r * sub + sub_rows, :], w_up[j])

        for t in range(min(UP_LOOKAHEAD, len(tasks))):
            emit_up(t)
        for t, (j, r) in enumerate(tasks):
            if t + UP_LOOKAHEAD < len(tasks):
                emit_up(t + UP_LOOKAHEAD)
            up = ups.pop(t)
            ge = up[:, :FF_CHUNK]
            val = up[halo:halo + sub, FF_CHUNK:]
            g_prev = pltpu.roll(ge, 1, axis=0)[halo:halo + sub]
            g_next = pltpu.roll(ge, sub_rows - 1, axis=0)[halo:halo + sub]
            prev_ok = False if is_ctx else (has_prev if r == 0 else True)
            next_ok = False if is_ctx else (has_next if r == n_sub - 1 else True)
            if prev_ok is not True:
                g_prev = jnp.concatenate(
                    [jnp.where(jnp.logical_or(row8 != 0, prev_ok), g_prev[0:8], 0.0), g_prev[8:]], axis=0)
            if next_ok is not True:
                g_next = jnp.concatenate(
                    [g_next[:-8], jnp.where(jnp.logical_or(row8 != 7, next_ok), g_next[-8:], 0.0)], axis=0)
            gate = (g_prev * cw[j][0:1, :] + ge[halo:halo + sub] * cw[j][1:2, :] + g_next * cw[j][2:3, :]
                    + cb[j])
            act = (gate * jax.nn.sigmoid(gate) * val).astype(BF16)
            acc_ref[r * sub:(r + 1) * sub, :] += g2 * _dot(act, w_dn[j])

    n_tail = N_FF_CHUNKS % FF_PER_STEP
    if n_tail:
        pl.when(c < N_FF_STEPS - 1)(functools.partial(ff_chunks, FF_PER_STEP))
    else:
        ff_chunks(FF_PER_STEP)

    @pl.when(c == N_FF_STEPS - 1)
    def _():
        if n_tail:
            ff_chunks(n_tail)
        x2 = acc_ref[...]
        out_ref[...] = _rms(x2, gfin_ref[...]) if final else x2


def _ffn(x, os, mod, layer, g_ffn, w_out, w_up, conv_w, conv_b, w_down, g_final, *, is_ctx, final):
    n_rows = x.shape[0]
    nh = FFN_TM // HALO
    nblk = n_rows // HALO
    n_o = len(os)
    halo = 0 if is_ctx else HALO
    tile = lambda m, c: (m, 0)
    prev = lambda m, c: (jnp.maximum(m * nh - 1, 0), 0)
    nxt = lambda m, c: (jnp.minimum((m + 1) * nh, nblk - 1), 0)
    chunk = lambda c, j: jnp.minimum(c * FF_PER_STEP + j, N_FF_CHUNKS - 1)
    if is_ctx:
        cond = lambda m: layer * 8
    else:
        cond = lambda m: layer * 8 + 1 + m // FFN_LAT_TILES

    def with_halo(arr):
        w = arr.shape[1]
        specs = [pl.BlockSpec((FFN_TM, w), tile)]
        if not is_ctx:
            specs += [pl.BlockSpec((HALO, w), prev), pl.BlockSpec((HALO, w), nxt)]
        return specs, [arr] * len(specs)

    in_specs, args = with_halo(x)
    for o in os:
        specs, arrs = with_halo(o)
        in_specs += specs
        args += arrs
    w_rows = D_MODEL // n_o
    for t in range(n_o):
        in_specs.append(pl.BlockSpec((None, w_rows, D_MODEL), lambda m, c, t=t: (0, t, 0)))
        args.append(w_out)
    in_specs += [
        pl.BlockSpec((None, 6, D_MODEL), lambda m, c: (cond(m), 0, 0)),
        pl.BlockSpec((None, 1, D_MODEL), lambda m, c: (layer, 0, 0)),
    ]
    args += [mod, g_ffn]
    steps = range(FF_PER_STEP)
    in_specs += [pl.BlockSpec((None, D_MODEL, FF_CHUNK), lambda m, c, j=j: (layer, 0, chunk(c, j))) for j in steps]
    in_specs += [pl.BlockSpec((None, D_MODEL, FF_CHUNK), lambda m, c, j=j: (layer, 0, N_FF_CHUNKS + chunk(c, j)))
                 for j in steps]
    in_specs += [pl.BlockSpec((None, 3, D_FF), lambda m, c: (layer, 0, 0)),
                 pl.BlockSpec((None, 1, D_FF), lambda m, c: (layer, 0, 0))]
    in_specs += [pl.BlockSpec((None, FF_CHUNK, D_MODEL), lambda m, c, j=j: (layer, chunk(c, j), 0)) for j in steps]
    args += [w_up] * (2 * FF_PER_STEP) + [conv_w, conv_b] + [w_down] * FF_PER_STEP
    in_specs.append(pl.BlockSpec((1, D_MODEL), lambda m, c: (0, 0)))
    args.append(g_final)
    return pl.pallas_call(
        functools.partial(_ffn_kernel, n_o=n_o, is_ctx=is_ctx, final=final),
        grid=(n_rows // FFN_TM, N_FF_STEPS), in_specs=in_specs,
        out_specs=pl.BlockSpec((FFN_TM, D_MODEL), tile),
        out_shape=jax.ShapeDtypeStruct((n_rows, D_MODEL), F32),
        scratch_shapes=[pltpu.VMEM((FFN_TM + 2 * halo, D_MODEL), BF16), pltpu.VMEM((FFN_TM, D_MODEL), F32)],
        compiler_params=pltpu.CompilerParams(
            dimension_semantics=("arbitrary", "arbitrary"), vmem_limit_bytes=FFN_VMEM_LIMIT),
        name=("ffn_ctx" if is_ctx else "ffn_lat") + ("_final" if final else ""),
    )(*args)


def _rope_tables(rot_dim):
    f32 = np.float32
    t = np.arange(LAT_LEN)
    row = (t // GRID_W).astype(f32)
    col = (t % GRID_W).astype(f32)
    d_axis = rot_dim // 2
    freqs = (f32(ROPE_THETA) ** (-np.arange(0, d_axis, 2, dtype=f32) / f32(d_axis))).astype(f32)
    ang = np.concatenate([row[:, None] * freqs, col[:, None] * freqs], axis=-1)
    cos = np.repeat(np.cos(ang), 2, axis=-1).astype(f32)
    sin = (np.repeat(np.sin(ang), 2, axis=-1) * np.tile(np.array([-1.0, 1.0], f32), rot_dim // 2)).astype(f32)
    if rot_dim == HEAD_DIM:
        cos = np.tile(cos, (1, 2))
        sin = np.tile(sin, (1, 2))
    else:
        cos = np.concatenate([np.ones((LAT_LEN, QK_NOPE), f32), cos,
                              np.ones((LAT_LEN, 128 - QK_NOPE - rot_dim), f32)], axis=-1)
        sin = np.concatenate([np.zeros((LAT_LEN, QK_NOPE), f32), sin,
                              np.zeros((LAT_LEN, 128 - QK_NOPE - rot_dim), f32)], axis=-1)
    cos = np.concatenate([cos, np.ones((TM, 128), f32)], axis=0)
    sin = np.concatenate([sin, np.zeros((TM, 128), f32)], axis=0)
    return jnp.asarray(cos), jnp.asarray(sin)


def kernel(x_prompt, x_sample, cache_a_k, cache_a_v, cache_b_ckv, cache_b_kpe, cache_c_k, cache_c_v, c, c_ctx, w_mod, b_mod, g_mix_norm, g_ffn_norm, w_in_e, g_qnorm_a, g_knorm_a, g_cq_b, w_uq_b, g_ckv_b, w_ukv_b, w_out_e, w_in_o, sink_c, w_out_o, w_up, conv_w, conv_b, w_down, g_final):
    depth = w_mod.shape[0]
    cond8 = jnp.concatenate([c_ctx[None, :], c, jnp.zeros((5, D_MODEL), F32)], axis=0)
    lane_pad = KB_PAD - QK_NOPE - QK_ROPE
    wuq = jnp.pad(w_uq_b[0].reshape(Q_LORA, N_HEADS_B, QK_NOPE + QK_ROPE),
                  ((0, 0), (0, 0), (0, lane_pad))).reshape(Q_LORA, N_HEADS_B * KB_PAD)
    wukv3 = w_ukv_b[0].reshape(KV_LORA, N_HEADS_B, QK_NOPE + V_DIM_B)
    wukk = jnp.pad(wukv3[:, :, :QK_NOPE], ((0, 0), (0, 0), (0, KB_PAD - QK_NOPE))
                   ).reshape(KV_LORA, N_HEADS_B * KB_PAD)
    wukv = wukv3[:, :, QK_NOPE:].reshape(KV_LORA, N_HEADS_B * V_DIM_B)
    gq = jnp.tile(g_qnorm_a[0], N_HEADS_A)[None, :]
    gk = jnp.tile(g_knorm_a[0], N_KV_A)[None, :]
    seg = np.arange(256) // HEAD_DIM
    ones = jnp.asarray(seg[:, None] == seg[None, :], dtype=BF16)
    cosa, sina = _rope_tables(HEAD_DIM)
    cosb, sinb = _rope_tables(QK_ROPE)
    g_mix3 = g_mix_norm.reshape(depth, 1, D_MODEL)
    g_ffn3 = g_ffn_norm.reshape(depth, 1, D_MODEL)
    conv_b3 = conv_b.reshape(depth, 1, D_FF)
    g_fin2 = g_final[None, :]

    mod = _modulation(cond8, w_mod, b_mod).reshape(depth * 8, 6, D_MODEL)

    n_past = N_LAT_SEQ * PAST_LEN
    kpe_pad = jnp.pad(cache_b_kpe.reshape(n_past, QK_ROPE), ((0, 0), (QK_NOPE, lane_pad)))
    ka_c, va_c, kb_c, vb_c, kc_c, vc_c = _ctx_prep(
        cache_a_k, cache_a_v, cache_b_ckv.reshape(n_past, KV_LORA), kpe_pad, cache_c_k, cache_c_v, wukk, wukv)

    xp = x_prompt.reshape(N_CTX_TOK, D_MODEL)
    xs = x_sample.reshape(N_TOK - N_CTX_TOK, D_MODEL)
    qa, ka, va, qb, kb, vb, nak, nav, nckv, nkpe = _pre0(
        xp, xs, mod, g_mix3, w_in_e, gq, gk, g_cq_b.reshape(1, 1, Q_LORA), g_ckv_b.reshape(1, 1, KV_LORA),
        wuq, wukk, wukv, cosa, sina, cosb, sinb, ones)
    ffn_w = (w_up, conv_w, conv_b3, w_down, g_fin2)
    ctx_kw = dict(n_seq=N_CTX_SEQ, seq_len=CTX_LEN, tok_base=0, tq=CTX_LEN, window=0, q_unit=256,
                  key_chunk=CTX_LEN, seqs=8)
    lat_kw = dict(n_seq=N_LAT_SEQ, seq_len=LAT_LEN, tok_base=N_CTX_TOK, q_unit=256)
    dense_kw = dict(tq=256, window=0, key_chunk=256, **lat_kw)
    oa = (_attention(qa, ka, va, None, None, None, name="attn_a_ctx", **ctx_kw),
          _attention(qa, ka, va, ka_c, va_c, None, name="attn_a_lat", **dense_kw))
    ob = (_attention(qb, kb, vb, None, None, None, name="attn_b_ctx", **ctx_kw),
          _attention(qb, kb, vb, kb_c, vb_c, None, name="attn_b_lat", **dense_kw))
    xp1 = _ffn(xp, [oa[0], ob[0]], mod, 0, g_ffn3, w_out_e, *ffn_w, is_ctx=True, final=False)
    xs1 = _ffn(xs, [oa[1], ob[1]], mod, 0, g_ffn3, w_out_e, *ffn_w, is_ctx=False, final=False)

    qc, kc, vc, nck, ncv = _pre1(xp1, xs1, mod, g_mix3, w_in_o, cosa, sina)
    sink = sink_c[0]
    oc = (_attention(qc, kc, vc, None, None, sink, name="attn_c_ctx", **ctx_kw),
          _attention(qc, kc, vc, kc_c, vc_c, sink, tq=128, window=WINDOW, key_chunk=PAST_LEN, seqs=4,
                     name="attn_c_lat", **lat_kw))
    y_prompt = _ffn(xp1, [oc[0]], mod, 1, g_ffn3, w_out_o, *ffn_w, is_ctx=True, final=True)
    y_sample = _ffn(xs1, [oc[1]], mod, 1, g_ffn3, w_out_o, *ffn_w, is_ctx=False, final=True)

    return (y_prompt.reshape(N_CTX_SEQ, CTX_LEN, D_MODEL), y_sample.reshape(N_LAT_SEQ, LAT_LEN, D_MODEL),
            nak.reshape(N_CTX_SEQ, 1, CTX_LEN, N_KV_A, HEAD_DIM),
            nav.reshape(N_CTX_SEQ, 1, CTX_LEN, N_KV_A, HEAD_DIM),
            nckv.reshape(N_CTX_SEQ, 1, CTX_LEN, KV_LORA),
            nkpe.reshape(N_CTX_SEQ, 1, CTX_LEN, QK_ROPE),
            nck.reshape(N_CTX_SEQ, 1, CTX_LEN, N_KV_C, HEAD_DIM),
            ncv.reshape(N_CTX_SEQ, 1, CTX_LEN, N_KV_C, HEAD_DIM))
```

```python
import functools

import jax
import jax.numpy as jnp
import numpy as np
from jax import lax
from jax.experimental import pallas as pl
from jax.experimental.pallas import tpu as pltpu

F32 = jnp.float32
BF16 = jnp.bfloat16

D_MODEL = 1024
N_CTX_SEQ = 16
CTX_LEN = 256
N_LAT_SEQ = 2
LAT_LEN = 2048
PAST_LEN = 512
GRID_W = 64
ROPE_THETA = 10000.0
NORM_EPS = 1e-6
WINDOW = 128
NEG_INF = -1e30
LOG2E = 1.4426950408889634
HEAD_DIM = 64
N_HEADS_A, N_KV_A = 8, 2
N_HEADS_B = 8
Q_LORA, KV_LORA = 384, 256
QK_NOPE, QK_ROPE, V_DIM_B = 64, 32, 64
N_HEADS_C, N_KV_C = 16, 2
D_FF = 2816
IN_E_MAIN = N_HEADS_A * HEAD_DIM + 2 * N_KV_A * HEAD_DIM + Q_LORA + KV_LORA

N_CTX_TOK = N_CTX_SEQ * CTX_LEN
N_TOK = N_CTX_TOK + N_LAT_SEQ * LAT_LEN
TM = 512
PRE_SUB = 256
N_TILES = N_TOK // TM
N_CTX_TILES = N_CTX_TOK // TM
LAT_TILES = LAT_LEN // TM
HALO = 16
FFN_TM = 1024
FFN_SUB = CTX_LEN
FFN_LAT_SUB = 256
UP_LOOKAHEAD = 3
FFN_LAT_TILES = LAT_LEN // FFN_TM
FF_CHUNK = 256
N_FF_CHUNKS = D_FF // FF_CHUNK
FF_PER_STEP = 3
N_FF_STEPS = -(-N_FF_CHUNKS // FF_PER_STEP)
KB_PAD = 128
VT_ROWS = 80
SCORE_LOOKAHEAD = 5
VMEM_LIMIT = 56 * 1024 * 1024
FFN_VMEM_LIMIT = 60 * 1024 * 1024


def _dot(a, b):
    return jnp.dot(a, b, preferred_element_type=F32)


def _dot_nt(a, b):
    return lax.dot_general(a, b, (((1,), (1,)), ((), ())), preferred_element_type=F32)


def _rms(x, g):
    return x * lax.rsqrt(jnp.mean(x * x, axis=-1, keepdims=True) + NORM_EPS) * g


def _split_bf16(x):
    hi = x.astype(BF16)
    return hi, (x - hi.astype(F32)).astype(BF16)


def _head_rms(x, g, ones_ref):
    w = x.shape[1]
    hi, lo = _split_bf16(x * x)
    parts = []
    for c in range(0, w, 256):
        cw = min(256, w - c)
        ones = ones_ref[0:cw, 0:cw]
        parts.append(_dot(hi[:, c:c + cw], ones) + _dot(lo[:, c:c + cw], ones))
    ssum = parts[0] if len(parts) == 1 else jnp.concatenate(parts, axis=1)
    return x * lax.rsqrt(ssum * (1.0 / HEAD_DIM) + NORM_EPS) * g


def _swap_pairs(x):
    w = x.shape[1]
    up = pltpu.roll(x, w - 1, axis=1)
    dn = pltpu.roll(x, 1, axis=1)
    lane = lax.broadcasted_iota(jnp.int32, x.shape, 1)
    return jnp.where((lane & 1) == 0, up, dn)


def _rope(x, cos, sin_signed, reps):
    if reps > 1:
        cos = jnp.concatenate([cos] * reps, axis=1)
        sin_signed = jnp.concatenate([sin_signed] * reps, axis=1)
    return x * cos + _swap_pairs(x) * sin_signed


def _store_vt(vt_ref, v, n_heads, cols=slice(None)):
    t = v.shape[0]
    vt = v.T.astype(BF16)
    ones = jnp.ones((VT_ROWS - 64, t), BF16)
    for hh in range(n_heads):
        vt_ref[hh, 0:64, cols] = vt[hh * 64:(hh + 1) * 64]
        vt_ref[hh, 64:VT_ROWS, cols] = ones


def _cond_row(i):
    return jnp.where(i < N_CTX_TILES, 0, 1 + (i - N_CTX_TILES) // LAT_TILES)


def _rope_block(i):
    return jnp.where(i < N_CTX_TILES, LAT_TILES, (i - N_CTX_TILES) % LAT_TILES)


def _const_spec(shape):
    zeros = (0,) * len(shape)
    return pl.BlockSpec(shape, lambda *_: zeros)


def _layer_spec(shape, layer):
    idx = (layer,) + (0,) * len(shape)
    return pl.BlockSpec((None,) + tuple(shape), lambda *_: idx)


def _mod_kernel(cond_ref, w_ref, b_ref, o_ref):
    c = cond_ref[...]
    s_hi, s_lo = _split_bf16(c * jax.nn.sigmoid(c))
    w_hi, w_lo = _split_bf16(w_ref[0])
    r = _dot(jnp.concatenate([s_hi, s_lo], axis=0), w_hi)
    o_ref[0] = r[0:8] + r[8:16] + _dot(s_hi, w_lo) + b_ref[0]


def _modulation(cond8, w_mod, b_mod):
    depth, _, n = w_mod.shape
    tn = 3072
    return pl.pallas_call(
        _mod_kernel,
        grid=(depth, n // tn),
        in_specs=[
            pl.BlockSpec((8, D_MODEL), lambda l, j: (0, 0)),
            pl.BlockSpec((1, D_MODEL, tn), lambda l, j: (l, 0, j)),
            pl.BlockSpec((1, 1, tn), lambda l, j: (l, 0, j)),
        ],
        out_specs=pl.BlockSpec((1, 8, tn), lambda l, j: (l, 0, j)),
        out_shape=jax.ShapeDtypeStruct((depth, 8, n), F32),
        compiler_params=pltpu.CompilerParams(
            dimension_semantics=("arbitrary", "arbitrary"), vmem_limit_bytes=VMEM_LIMIT),
        name="modulation",
    )(cond8, w_mod, b_mod.reshape(depth, 1, n))


def _pre0_kernel(xp_ref, xs_ref, mod_ref, gmix_ref, win_ref, gq_ref, gk_ref, gcq_ref, gckv_ref,
                 wuq_ref, wukk_ref, wukv_ref, cosa_ref, sina_ref, cosb_ref, sinb_ref, ones_ref,
                 qa_ref, ka_ref, va_ref, qb_ref, kb_ref, vb_ref,
                 nak_ref, nav_ref, nckv_ref, nkpe_ref,
                 win_s, wuq_s, wukk_s, wukv_s):
    i = pl.program_id(0)

    @pl.when(i == 0)
    def _():
        win_s[:, 0:IN_E_MAIN] = win_ref[0:IN_E_MAIN, :].T.astype(BF16)
        win_s[:, IN_E_MAIN:] = jnp.concatenate(
            [jnp.zeros((D_MODEL, QK_NOPE), F32), win_ref[IN_E_MAIN:, :].T,
             jnp.zeros((D_MODEL, KB_PAD - QK_NOPE - QK_ROPE), F32)], axis=1).astype(BF16)
        wuq_s[...] = wuq_ref[...].astype(BF16)
        wukk_s[...] = wukk_ref[...].astype(BF16)
        wukv_s[...] = wukv_ref[...].astype(BF16)

    sh1 = mod_ref[0:1, :]
    sc1 = mod_ref[1:2, :]
    subs = [slice(r, r + PRE_SUB) for r in range(0, TM, PRE_SUB)]

    projs = []
    for sl in subs:
        x = jnp.where(i < N_CTX_TILES, xp_ref[sl, :], xs_ref[sl, :])
        h = _rms(x, gmix_ref[...]) * (1.0 + sc1) + sh1
        projs.append(_dot(h.astype(BF16), win_s[...]))

    parts = []
    for sl, proj in zip(subs, projs):
        qa = _head_rms(proj[:, 0:512], gq_ref[...], ones_ref)
        ka = _head_rms(proj[:, 512:640], gk_ref[...], ones_ref)
        va = proj[:, 640:768]
        cq = _rms(proj[:, 768:1152], gcq_ref[...])
        ckv = _rms(proj[:, 1152:1408], gckv_ref[...])
        kpe = proj[:, 1408:1536]
        qb = _dot(cq.astype(BF16), wuq_s[...])
        ckv_b = ckv.astype(BF16)
        kbn = _dot(ckv_b, wukk_s[...])
        vb = _dot(ckv_b, wukv_s[...])
        _store_vt(va_ref, va, N_KV_A, sl)
        _store_vt(vb_ref, vb, N_HEADS_B, sl)
        parts.append((ka, va, ckv, kpe))
        cosa, sina = cosa_ref[sl, :], sina_ref[sl, :]
        cosb, sinb = cosb_ref[sl, :], sinb_ref[sl, :]
        qa_s = _rope(qa, cosa, sina, 4) * (LOG2E * HEAD_DIM ** -0.5)
        ka_r = _rope(ka, cosa, sina, 1)
        qb_s = _rope(qb, cosb, sinb, 8) * (LOG2E * (QK_NOPE + QK_ROPE) ** -0.5)
        kpe_r = _rope(kpe, cosb, sinb, 1)
        for hh in range(N_HEADS_A):
            qa_ref[hh, sl, :] = qa_s[:, hh * 64:(hh + 1) * 64].astype(BF16)
        for hh in range(N_KV_A):
            ka_ref[hh, sl, :] = ka_r[:, hh * 64:(hh + 1) * 64].astype(BF16)
        for hh in range(N_HEADS_B):
            qb_ref[hh, sl, :] = qb_s[:, hh * KB_PAD:(hh + 1) * KB_PAD].astype(BF16)
            kb_ref[hh, sl, :] = (kbn[:, hh * KB_PAD:(hh + 1) * KB_PAD] + kpe_r).astype(BF16)

    @pl.when(i < N_CTX_TILES)
    def _():
        for s, (sl, (ka, va, ckv, kpe)) in enumerate(zip(subs, parts)):
            nak_ref[s] = ka.T
            nav_ref[s] = va.T
            nckv_ref[sl, :] = ckv
            nkpe_ref[s] = kpe.T[QK_NOPE:QK_NOPE + QK_ROPE]


def _pre0(xp, xs, mod, gmix, w_in_e_t, gq, gk, gcq, gckv, wuq, wukk, wukv, cosa, sina, cosb, sinb, ones):
    tile = lambda i: (i, 0)
    head_tile = lambda i: (0, i, 0)
    vt_tile = lambda i: (0, 0, i)
    ctx_tile = lambda i: (jnp.minimum(i, N_CTX_TILES - 1), 0)
    lat_tile = lambda i: (jnp.maximum(i - N_CTX_TILES, 0), 0)
    rope_tile = lambda i: (_rope_block(i), 0)
    in_specs = [
        pl.BlockSpec((TM, D_MODEL), ctx_tile),
        pl.BlockSpec((TM, D_MODEL), lat_tile),
        pl.BlockSpec((None, 6, D_MODEL), lambda i: (_cond_row(i), 0, 0)),
        _layer_spec((1, D_MODEL), 0),
        _const_spec(w_in_e_t.shape),
        _const_spec(gq.shape), _const_spec(gk.shape), _layer_spec((1, Q_LORA), 0), _layer_spec((1, KV_LORA), 0),
        _const_spec(wuq.shape), _const_spec(wukk.shape), _const_spec(wukv.shape),
        pl.BlockSpec((TM, 128), rope_tile), pl.BlockSpec((TM, 128), rope_tile),
        pl.BlockSpec((TM, 128), rope_tile), pl.BlockSpec((TM, 128), rope_tile),
        _const_spec(ones.shape),
    ]
    out_shape = [
        jax.ShapeDtypeStruct((N_HEADS_A, N_TOK, 64), BF16),
        jax.ShapeDtypeStruct((N_KV_A, N_TOK, 64), BF16),
        jax.ShapeDtypeStruct((N_KV_A, VT_ROWS, N_TOK), BF16),
        jax.ShapeDtypeStruct((N_HEADS_B, N_TOK, KB_PAD), BF16),
        jax.ShapeDtypeStruct((N_HEADS_B, N_TOK, KB_PAD), BF16),
        jax.ShapeDtypeStruct((N_HEADS_B, VT_ROWS, N_TOK), BF16),
        jax.ShapeDtypeStruct((N_CTX_SEQ, 128, CTX_LEN), F32),
        jax.ShapeDtypeStruct((N_CTX_SEQ, 128, CTX_LEN), F32),
        jax.ShapeDtypeStruct((N_CTX_TOK, KV_LORA), F32),
        jax.ShapeDtypeStruct((N_CTX_SEQ, QK_ROPE, CTX_LEN), F32),
    ]
    seq_tile = lambda i: (jnp.minimum(i, N_CTX_TILES - 1), 0, 0)
    n_sub = TM // PRE_SUB
    out_specs = [
        pl.BlockSpec((N_HEADS_A, TM, 64), head_tile),
        pl.BlockSpec((N_KV_A, TM, 64), head_tile),
        pl.BlockSpec((N_KV_A, VT_ROWS, TM), vt_tile),
        pl.BlockSpec((N_HEADS_B, TM, KB_PAD), head_tile),
        pl.BlockSpec((N_HEADS_B, TM, KB_PAD), head_tile),
        pl.BlockSpec((N_HEADS_B, VT_ROWS, TM), vt_tile),
        pl.BlockSpec((n_sub, 128, CTX_LEN), seq_tile),
        pl.BlockSpec((n_sub, 128, CTX_LEN), seq_tile),
        pl.BlockSpec((TM, KV_LORA), ctx_tile),
        pl.BlockSpec((n_sub, QK_ROPE, CTX_LEN), seq_tile),
    ]
    scratch = [
        pltpu.VMEM((D_MODEL, IN_E_MAIN + KB_PAD), BF16),
        pltpu.VMEM(wuq.shape, BF16), pltpu.VMEM(wukk.shape, BF16), pltpu.VMEM(wukv.shape, BF16),
    ]
    return pl.pallas_call(
        _pre0_kernel, grid=(N_TILES,), in_specs=in_specs, out_specs=out_specs, out_shape=out_shape,
        scratch_shapes=scratch,
        compiler_params=pltpu.CompilerParams(
            dimension_semantics=("arbitrary",), vmem_limit_bytes=VMEM_LIMIT),
        name="pre0",
    )(xp, xs, mod, gmix, w_in_e_t, gq, gk, gcq, gckv, wuq, wukk, wukv, cosa, sina, cosb, sinb, ones)


def _pre1_kernel(xp_ref, xs_ref, mod_ref, gmix_ref, win_ref, cosa_ref, sina_ref,
                 q_ref, k_ref, v_ref, nk_ref, nv_ref, win_s):
    i = pl.program_id(0)

    @pl.when(i == 0)
    def _():
        win_s[...] = win_ref[...].astype(BF16)

    sh1 = mod_ref[0:1, :]
    sc1 = mod_ref[1:2, :]
    subs = [slice(r, r + PRE_SUB) for r in range(0, TM, PRE_SUB)]
    projs = []
    for sl in subs:
        x = jnp.where(i < N_CTX_TILES, xp_ref[sl, :], xs_ref[sl, :])
        h = _rms(x, gmix_ref[...]) * (1.0 + sc1) + sh1
        projs.append(_dot(h.astype(BF16), win_s[...]))
    for sl, proj in zip(subs, projs):
        cosa, sina = cosa_ref[sl, :], sina_ref[sl, :]
        q_s = _rope(proj[:, 0:1024], cosa, sina, 8) * (LOG2E * HEAD_DIM ** -0.5)
        k = _rope(proj[:, 1024:1152], cosa, sina, 1)
        for hh in range(N_HEADS_C):
            q_ref[hh, sl, :] = q_s[:, hh * 64:(hh + 1) * 64].astype(BF16)
        for hh in range(N_KV_C):
            k_ref[hh, sl, :] = k[:, hh * 64:(hh + 1) * 64].astype(BF16)
        _store_vt(v_ref, proj[:, 1152:1280], N_KV_C, sl)

    @pl.when(i < N_CTX_TILES)
    def _():
        for s, proj in enumerate(projs):
            nk_ref[s] = proj[:, 1024:1152].T
            nv_ref[s] = proj[:, 1152:1280].T


def _pre1(xp, xs, mod, gmix, w_in_o, cosa, sina):
    n_in = w_in_o.shape[-1]
    head_tile = lambda i: (0, i, 0)
    vt_tile = lambda i: (0, 0, i)
    ctx_tile = lambda i: (jnp.minimum(i, N_CTX_TILES - 1), 0)
    lat_tile = lambda i: (jnp.maximum(i - N_CTX_TILES, 0), 0)
    rope_tile = lambda i: (_rope_block(i), 0)
    in_specs = [
        pl.BlockSpec((TM, D_MODEL), ctx_tile),
        pl.BlockSpec((TM, D_MODEL), lat_tile),
        pl.BlockSpec((None, 6, D_MODEL), lambda i: (8 + _cond_row(i), 0, 0)),
        _layer_spec((1, D_MODEL), 1),
        _layer_spec((D_MODEL, n_in), 0),
        pl.BlockSpec((TM, 128), rope_tile), pl.BlockSpec((TM, 128), rope_tile),
    ]
    out_shape = [
        jax.ShapeDtypeStruct((N_HEADS_C, N_TOK, 64), BF16),
        jax.ShapeDtypeStruct((N_KV_C, N_TOK, 64), BF16),
        jax.ShapeDtypeStruct((N_KV_C, VT_ROWS, N_TOK), BF16),
        jax.ShapeDtypeStruct((N_CTX_SEQ, 128, CTX_LEN), F32),
        jax.ShapeDtypeStruct((N_CTX_SEQ, 128, CTX_LEN), F32),
    ]
    seq_tile = lambda i: (jnp.minimum(i, N_CTX_TILES - 1), 0, 0)
    out_specs = [
        pl.BlockSpec((N_HEADS_C, TM, 64), head_tile),
        pl.BlockSpec((N_KV_C, TM, 64), head_tile),
        pl.BlockSpec((N_KV_C, VT_ROWS, TM), vt_tile),
        pl.BlockSpec((TM // PRE_SUB, 128, CTX_LEN), seq_tile),
        pl.BlockSpec((TM // PRE_SUB, 128, CTX_LEN), seq_tile),
    ]
    return pl.pallas_call(
        _pre1_kernel, grid=(N_TILES,), in_specs=in_specs, out_specs=out_specs, out_shape=out_shape,
        scratch_shapes=[pltpu.VMEM((D_MODEL, n_in), BF16)],
        compiler_params=pltpu.CompilerParams(
            dimension_semantics=("arbitrary",), vmem_limit_bytes=VMEM_LIMIT),
        name="pre1",
    )(xp, xs, mod, gmix, w_in_o, cosa, sina)


def _ctx_kernel(ak_ref, av_ref, ckv_ref, kpe_ref, ck_ref, cv_ref, wukk_ref, wukv_ref,
                ka_ref, va_ref, kb_ref, vb_ref, kc_ref, vc_ref):
    ones = jnp.ones((VT_ROWS - 64, PAST_LEN), BF16)
    for hh in range(2):
        rows = slice(hh * 64, (hh + 1) * 64)
        ka_ref[hh] = ak_ref[rows, :].T.astype(BF16)
        kc_ref[hh] = ck_ref[rows, :].T.astype(BF16)
        for src, dst in ((av_ref, va_ref), (cv_ref, vc_ref)):
            dst[hh, 0:64, :] = src[rows, :].astype(BF16)
            dst[hh, 64:VT_ROWS, :] = ones
    ckv_b = ckv_ref[...].astype(BF16)
    kbn = _dot(ckv_b, wukk_ref[...].astype(BF16))
    vb = _dot(ckv_b, wukv_ref[...].astype(BF16))
    kpe = jnp.concatenate([jnp.zeros((PAST_LEN, QK_NOPE), F32), kpe_ref[...].T,
                           jnp.zeros((PAST_LEN, KB_PAD - QK_NOPE - QK_ROPE), F32)], axis=1)
    for hh in range(N_HEADS_B):
        kb_ref[hh] = (kbn[:, hh * KB_PAD:(hh + 1) * KB_PAD] + kpe).astype(BF16)
    _store_vt(vb_ref, vb, N_HEADS_B)


def _ctx_prep(ak, av, ckv, kpe, ck, cv, wukk, wukv):
    n = N_LAT_SEQ * PAST_LEN
    row = lambda b: (b, 0)
    head_row = lambda b: (0, b, 0)
    vt_row = lambda b: (0, 0, b)
    tok_minor = lambda rows: pl.BlockSpec((None, rows, PAST_LEN), lambda b: (b, 0, 0))
    in_specs = [
        tok_minor(128), tok_minor(128),
        pl.BlockSpec((PAST_LEN, KV_LORA), row), tok_minor(QK_ROPE),
        tok_minor(128), tok_minor(128),
        _const_spec(wukk.shape), _const_spec(wukv.shape),
    ]
    out_shape = [
        jax.ShapeDtypeStruct((2, n, 64), BF16), jax.ShapeDtypeStruct((2, VT_ROWS, n), BF16),
        jax.ShapeDtypeStruct((N_HEADS_B, n, KB_PAD), BF16), jax.ShapeDtypeStruct((N_HEADS_B, VT_ROWS, n), BF16),
        jax.ShapeDtypeStruct((2, n, 64), BF16), jax.ShapeDtypeStruct((2, VT_ROWS, n), BF16),
    ]
    out_specs = [
        pl.BlockSpec((2, PAST_LEN, 64), head_row), pl.BlockSpec((2, VT_ROWS, PAST_LEN), vt_row),
        pl.BlockSpec((N_HEADS_B, PAST_LEN, KB_PAD), head_row), pl.BlockSpec((N_HEADS_B, VT_ROWS, PAST_LEN), vt_row),
        pl.BlockSpec((2, PAST_LEN, 64), head_row), pl.BlockSpec((2, VT_ROWS, PAST_LEN), vt_row),
    ]
    return pl.pallas_call(
        _ctx_kernel, grid=(N_LAT_SEQ,), in_specs=in_specs, out_specs=out_specs, out_shape=out_shape,
        compiler_params=pltpu.CompilerParams(
            dimension_semantics=("arbitrary",), vmem_limit_bytes=VMEM_LIMIT),
        name="ctx_prep",
    )(ak, av, ckv, kpe, ck, cv, wukk, wukv)


def _softmax_units(units, lookahead):
    tasks = [(u, c) for u, unit in enumerate(units) for c in range(len(unit["chunks"]))]
    scores = {}
    qts = [unit["q"].astype(F32).T.astype(BF16) if len(unit["chunks"]) >= 4 else None for unit in units]

    def emit_scores(t):
        u, c = tasks[t]
        k, _, mask = units[u]["chunks"][c]
        s = _dot_nt(k, units[u]["q"]) if qts[u] is None else _dot(k, qts[u])
        scores[t] = s if mask is None else jnp.where(mask, s, NEG_INF)

    for t in range(min(lookahead, len(tasks))):
        emit_scores(t)
    for t, (u, c) in enumerate(tasks):
        if t + lookahead < len(tasks):
            emit_scores(t + lookahead)
        unit = units[u]
        s = scores.pop(t)
        m, acc = unit["m"], unit["acc"]
        cmax = jnp.max(s, axis=0, keepdims=True)
        m_new = cmax if m is None else jnp.maximum(m, cmax)
        pv = _dot(unit["chunks"][c][1], jnp.exp2(s - m_new).astype(BF16))
        unit["acc"] = pv if acc is None else acc * jnp.exp2(m - m_new) + pv
        unit["m"] = m_new
    return [unit["acc"][0:64] * (1.0 / unit["acc"][64:65]) for unit in units]


def _attn_kernel(*refs, n_kv, group, tq, seq_len, seqs, has_ctx, has_sink, window, q_unit, key_chunk):
    refs = list(refs)
    q_ref = refs.pop(0)
    if window:
        n_blk = seqs + 2
        kb_refs, vb_refs = refs[:n_blk], refs[n_blk:2 * n_blk]
        refs = refs[2 * n_blk:]
    else:
        k_ref, vt_ref = refs[:2]
        refs = refs[2:]
    if has_ctx:
        kx_ref, vx_ref = refs[:2]
        refs = refs[2:]
    if has_sink:
        sink_ref = refs.pop(0)
    o_ref = refs.pop(0)

    j = pl.program_id(1)
    dk = q_ref.shape[-1]
    heads_per_unit = q_unit // tq
    lane = lax.broadcasted_iota(jnp.int32, (1, q_unit), 1)
    if window:
        n_band = tq + 2 * window
        krow = lax.broadcasted_iota(jnp.int32, (n_band, q_unit), 0)
        qcol = lax.broadcasted_iota(jnp.int32, (n_band, q_unit), 1) & (tq - 1)
        rel = (krow - window) - qcol
        in_band = jnp.abs(rel) <= window
        last_j = seq_len // (tq * seqs) - 1
        band_masks = []
        for sq in range(seqs):
            mask = in_band
            if sq == 0:
                mask = mask & ((krow >= window) | (j > 0))
            if sq == seqs - 1:
                mask = mask & ((krow < window + tq) | (j < last_j))
            band_masks.append(mask)
    if has_sink:
        acc0 = jnp.where(lax.broadcasted_iota(jnp.int32, (VT_ROWS, q_unit), 0) >= 64, 1.0, 0.0)

    units = []
    for sq in range(seqs):
        base = sq * (tq if window else seq_len)
        for hk in range(n_kv):
            chunks = []
            if window:
                chunks.append((jnp.concatenate([r[hk] for r in kb_refs[sq:sq + 3]], axis=0),
                               jnp.concatenate([r[hk] for r in vb_refs[sq:sq + 3]], axis=1), band_masks[sq]))
            else:
                for c in range(base, base + seq_len, key_chunk):
                    n = min(key_chunk, base + seq_len - c)
                    chunks.append((k_ref[hk, c:c + n, :], vt_ref[hk, :, c:c + n], None))
            if has_ctx:
                for c in range(0, PAST_LEN, key_chunk):
                    n = min(key_chunk, PAST_LEN - c)
                    chunks.append((kx_ref[hk, c:c + n, :], vx_ref[hk, :, c:c + n], None))
            for u in range(group // heads_per_unit):
                h0 = hk * group + u * heads_per_unit
                q = q_ref[h0:h0 + heads_per_unit, base:base + tq, :].reshape(q_unit, dk)
                unit = dict(q=q, chunks=chunks, m=None, acc=None)
                if has_sink:
                    m0 = jnp.full((1, q_unit), sink_ref[h0] * LOG2E, F32)
                    for e in range(1, heads_per_unit):
                        m0 = jnp.where(lane >= e * tq, sink_ref[h0 + e] * LOG2E, m0)
                    unit.update(m=m0, acc=acc0)
                units.append(unit)
    results = _softmax_units(units, SCORE_LOOKAHEAD)
    per_seq = len(results) // seqs
    for sq in range(seqs):
        outs = []
        for o in results[sq * per_seq:(sq + 1) * per_seq]:
            for e in range(heads_per_unit):
                outs.append(o[:, e * tq:(e + 1) * tq])
        base = sq * (tq if window else seq_len)
        o_ref[base:base + tq, :] = jnp.concatenate(outs, axis=0).T.astype(BF16)


def _attention(q, k, vt, kx, vx, sink, *, n_seq, seq_len, tok_base, tq, window, q_unit, key_chunk, name,
               seqs=1):
    n_q, _, dk = q.shape
    n_kv = k.shape[0]
    group = n_q // n_kv
    has_ctx = kx is not None
    has_sink = sink is not None
    if window:
        n_qt = seq_len // (tq * seqs)
    else:
        n_qt = seq_len // tq
        assert seqs == 1 or (n_qt == 1 and not has_ctx)
    q_base = tok_base // (seqs * tq)
    s_base = tok_base // (seqs * seq_len)
    q_blk = lambda b, j: q_base + b * n_qt + j
    in_specs = [pl.BlockSpec((n_q, seqs * tq, dk), lambda b, j: (0, q_blk(b, j), 0))]
    args = [q]
    if window:
        assert window == tq
        n_kb = seq_len // tq
        kb_base = tok_base // tq

        def key_blk(b, j, off):
            return kb_base + b * n_kb + jnp.clip(j * seqs - 1 + off, 0, n_kb - 1)

        for off in range(seqs + 2):
            in_specs.append(pl.BlockSpec((n_kv, tq, dk), lambda b, j, off=off: (0, key_blk(b, j, off), 0)))
        for off in range(seqs + 2):
            in_specs.append(pl.BlockSpec((n_kv, VT_ROWS, tq), lambda b, j, off=off: (0, 0, key_blk(b, j, off))))
        args += [k] * (seqs + 2) + [vt] * (seqs + 2)
    else:
        in_specs += [
            pl.BlockSpec((n_kv, seqs * seq_len, dk), lambda b, j: (0, s_base + b, 0)),
            pl.BlockSpec((n_kv, VT_ROWS, seqs * seq_len), lambda b, j: (0, 0, s_base + b)),
        ]
        args += [k, vt]
    if has_ctx:
        in_specs += [
            pl.BlockSpec((n_kv, PAST_LEN, dk), lambda b, j: (0, b, 0)),
            pl.BlockSpec((n_kv, VT_ROWS, PAST_LEN), lambda b, j: (0, 0, b)),
        ]
        args += [kx, vx]
    if has_sink:
        in_specs.append(pl.BlockSpec(memory_space=pltpu.SMEM))
        args.append(sink)
    kern = functools.partial(_attn_kernel, n_kv=n_kv, group=group, tq=tq, seq_len=seq_len, seqs=seqs,
                             has_ctx=has_ctx, has_sink=has_sink, window=window,
                             q_unit=q_unit, key_chunk=key_chunk)
    return pl.pallas_call(
        kern, grid=(n_seq if window else n_seq // seqs, n_qt), in_specs=in_specs,
        out_specs=pl.BlockSpec((seqs * tq, n_q * HEAD_DIM), lambda b, j: (b * n_qt + j, 0)),
        out_shape=jax.ShapeDtypeStruct((n_seq * seq_len, n_q * HEAD_DIM), BF16),
        compiler_params=pltpu.CompilerParams(
            dimension_semantics=("arbitrary", "arbitrary"), vmem_limit_bytes=VMEM_LIMIT),
        name=name,
    )(*args)


def _ffn_kernel(*refs, n_o, is_ctx, final):
    halo = 0 if is_ctx else HALO
    it = iter(refs)
    x_ref = next(it)
    xh_refs = None if is_ctx else (next(it), next(it))
    o_refs, oh_refs = [], []
    for _ in range(n_o):
        o_refs.append(next(it))
        if not is_ctx:
            oh_refs.append((next(it), next(it)))
    wo_refs = [next(it) for _ in range(n_o)]
    mod_ref, gffn_ref = next(it), next(it)
    wg_refs = [next(it) for _ in range(FF_PER_STEP)]
    wv_refs = [next(it) for _ in range(FF_PER_STEP)]
    cw_ref, cb_ref = next(it), next(it)
    wd_refs = [next(it) for _ in range(FF_PER_STEP)]
    gfin_ref, out_ref, h2e_ref, acc_ref = next(it), next(it), next(it), next(it)

    m = pl.program_id(0)
    c = pl.program_id(1)
    g1 = mod_ref[2:3, :]
    sh2 = mod_ref[3:4, :]
    sc2 = mod_ref[4:5, :]
    g2 = mod_ref[5:6, :]
    sub = FFN_SUB if is_ctx else FFN_LAT_SUB
    n_sub = FFN_TM // sub
    sub_rows = sub + 2 * halo

    @pl.when(c == 0)
    def _():
        wos = [w[...].astype(BF16) for w in wo_refs]

        def residual_and_norm(xv, ovs):
            attn = _dot(ovs[0], wos[0])
            for ov, wo in zip(ovs[1:], wos[1:]):
                attn = attn + _dot(ov, wo)
            x1 = xv + g1 * attn
            return x1, (_rms(x1, gffn_ref[...]) * (1.0 + sc2) + sh2).astype(BF16)

        for r in range(0, FFN_TM, FFN_SUB):
            x1, h2 = residual_and_norm(x_ref[r:r + FFN_SUB, :], [o[r:r + FFN_SUB, :] for o in o_refs])
            acc_ref[r:r + FFN_SUB, :] = x1
            h2e_ref[halo + r:halo + r + FFN_SUB, :] = h2
        if not is_ctx:
            _, h2h = residual_and_norm(
                jnp.concatenate([xh_refs[0][...], xh_refs[1][...]], axis=0),
                [jnp.concatenate([oh[0][...], oh[1][...]], axis=0) for oh in oh_refs])
            h2e_ref[0:HALO, :] = h2h[0:HALO]
            h2e_ref[HALO + FFN_TM:, :] = h2h[HALO:]

    row8 = lax.broadcasted_iota(jnp.int32, (8, FF_CHUNK), 0)
    if not is_ctx:
        has_prev = m % FFN_LAT_TILES != 0
        has_next = m % FFN_LAT_TILES != FFN_LAT_TILES - 1

    def ff_chunks(n_chunks):
        w_up, w_dn, cw, cb = [], [], [], []
        for j in range(n_chunks):
            w_up.append(jnp.concatenate([wg_refs[j][...].astype(BF16), wv_refs[j][...].astype(BF16)],
                                        axis=1))
            w_dn.append(wd_refs[j][...].astype(BF16))
            cols = pl.ds(pl.multiple_of((c * FF_PER_STEP + j) * FF_CHUNK, FF_CHUNK), FF_CHUNK)
            cw.append(cw_ref[:, cols])
            cb.append(cb_ref[:, cols])
        tasks = [(j, r) for j in range(n_chunks) for r in range(n_sub)]
        ups = {}

        def emit_up(t):
            j, r = tasks[t]
            ups[t] = _dot(h2e_ref[r * sub:r * sub + sub_rows, :], w_up[j])

        for t in range(min(UP_LOOKAHEAD, len(tasks))):
            emit_up(t)
        for t, (j, r) in enumerate(tasks):
            if t + UP_LOOKAHEAD < len(tasks):
                emit_up(t + UP_LOOKAHEAD)
            up = ups.pop(t)
            ge = up[:, :FF_CHUNK]
            val = up[halo:halo + sub, FF_CHUNK:]
            g_prev = pltpu.roll(ge, 1, axis=0)[halo:halo + sub]
            g_next = pltpu.roll(ge, sub_rows - 1, axis=0)[halo:halo + sub]
            prev_ok = False if is_ctx else (has_prev if r == 0 else True)
            next_ok = False if is_ctx else (has_next if r == n_sub - 1 else True)
            if prev_ok is not True:
                g_prev = jnp.concatenate(
                    [jnp.where(jnp.logical_or(row8 != 0, prev_ok), g_prev[0:8], 0.0), g_prev[8:]], axis=0)
            if next_ok is not True:
                g_next = jnp.concatenate(
                    [g_next[:-8], jnp.where(jnp.logical_or(row8 != 7, next_ok), g_next[-8:], 0.0)], axis=0)
            gate = (g_prev * cw[j][0:1, :] + ge[halo:halo + sub] * cw[j][1:2, :] + g_next * cw[j][2:3, :]
                    + cb[j])
            act = (gate * jax.nn.sigmoid(gate) * val).astype(BF16)
            acc_ref[r * sub:(r + 1) * sub, :] += g2 * _dot(act, w_dn[j])

    n_tail = N_FF_CHUNKS % FF_PER_STEP
    if n_tail:
        pl.when(c < N_FF_STEPS - 1)(functools.partial(ff_chunks, FF_PER_STEP))
    else:
        ff_chunks(FF_PER_STEP)

    @pl.when(c == N_FF_STEPS - 1)
    def _():
        if n_tail:
            ff_chunks(n_tail)
        x2 = acc_ref[...]
        out_ref[...] = _rms(x2, gfin_ref[...]) if final else x2


def _ffn(x, os, mod, layer, g_ffn, w_out, w_up, conv_w, conv_b, w_down, g_final, *, is_ctx, final):
    n_rows = x.shape[0]
    nh = FFN_TM // HALO
    nblk = n_rows // HALO
    n_o = len(os)
    halo = 0 if is_ctx else HALO
    tile = lambda m, c: (m, 0)
    prev = lambda m, c: (jnp.maximum(m * nh - 1, 0), 0)
    nxt = lambda m, c: (jnp.minimum((m + 1) * nh, nblk - 1), 0)
    chunk = lambda c, j: jnp.minimum(c * FF_PER_STEP + j, N_FF_CHUNKS - 1)
    if is_ctx:
        cond = lambda m: layer * 8
    else:
        cond = lambda m: layer * 8 + 1 + m // FFN_LAT_TILES

    def with_halo(arr):
        w = arr.shape[1]
        specs = [pl.BlockSpec((FFN_TM, w), tile)]
        if not is_ctx:
            specs += [pl.BlockSpec((HALO, w), prev), pl.BlockSpec((HALO, w), nxt)]
        return specs, [arr] * len(specs)

    in_specs, args = with_halo(x)
    for o in os:
        specs, arrs = with_halo(o)
        in_specs += specs
        args += arrs
    w_rows = D_MODEL // n_o
    for t in range(n_o):
        in_specs.append(pl.BlockSpec((None, w_rows, D_MODEL), lambda m, c, t=t: (0, t, 0)))
        args.append(w_out)
    in_specs += [
        pl.BlockSpec((None, 6, D_MODEL), lambda m, c: (cond(m), 0, 0)),
        pl.BlockSpec((None, 1, D_MODEL), lambda m, c: (layer, 0, 0)),
    ]
    args += [mod, g_ffn]
    steps = range(FF_PER_STEP)
    in_specs += [pl.BlockSpec((None, D_MODEL, FF_CHUNK), lambda m, c, j=j: (layer, 0, chunk(c, j))) for j in steps]
    in_specs += [pl.BlockSpec((None, D_MODEL, FF_CHUNK), lambda m, c, j=j: (layer, 0, N_FF_CHUNKS + chunk(c, j)))
                 for j in steps]
    in_specs += [pl.BlockSpec((None, 3, D_FF), lambda m, c: (layer, 0, 0)),
                 pl.BlockSpec((None, 1, D_FF), lambda m, c: (layer, 0, 0))]
    in_specs += [pl.BlockSpec((None, FF_CHUNK, D_MODEL), lambda m, c, j=j: (layer, chunk(c, j), 0)) for j in steps]
    args += [w_up] * (2 * FF_PER_STEP) + [conv_w, conv_b] + [w_down] * FF_PER_STEP
    in_specs.append(pl.BlockSpec((1, D_MODEL), lambda m, c: (0, 0)))
    args.append(g_final)
    return pl.pallas_call(
        functools.partial(_ffn_kernel, n_o=n_o, is_ctx=is_ctx, final=final),
        grid=(n_rows // FFN_TM, N_FF_STEPS), in_specs=in_specs,
        out_specs=pl.BlockSpec((FFN_TM, D_MODEL), tile),
        out_shape=jax.ShapeDtypeStruct((n_rows, D_MODEL), F32),
        scratch_shapes=[pltpu.VMEM((FFN_TM + 2 * halo, D_MODEL), BF16), pltpu.VMEM((FFN_TM, D_MODEL), F32)],
        compiler_params=pltpu.CompilerParams(
            dimension_semantics=("arbitrary", "arbitrary"), vmem_limit_bytes=FFN_VMEM_LIMIT),
        name=("ffn_ctx" if is_ctx else "ffn_lat") + ("_final" if final else ""),
    )(*args)


def _rope_tables(rot_dim):
    f32 = np.float32
    t = np.arange(LAT_LEN)
    row = (t // GRID_W).astype(f32)
    col = (t % GRID_W).astype(f32)
    d_axis = rot_dim // 2
    freqs = (f32(ROPE_THETA) ** (-np.arange(0, d_axis, 2, dtype=f32) / f32(d_axis))).astype(f32)
    ang = np.concatenate([row[:, None] * freqs, col[:, None] * freqs], axis=-1)
    cos = np.repeat(np.cos(ang), 2, axis=-1).astype(f32)
    sin = (np.repeat(np.sin(ang), 2, axis=-1) * np.tile(np.array([-1.0, 1.0], f32), rot_dim // 2)).astype(f32)
    if rot_dim == HEAD_DIM:
        cos = np.tile(cos, (1, 2))
        sin = np.tile(sin, (1, 2))
    else:
        cos = np.concatenate([np.ones((LAT_LEN, QK_NOPE), f32), cos,
                              np.ones((LAT_LEN, 128 - QK_NOPE - rot_dim), f32)], axis=-1)
        sin = np.concatenate([np.zeros((LAT_LEN, QK_NOPE), f32), sin,
                              np.zeros((LAT_LEN, 128 - QK_NOPE - rot_dim), f32)], axis=-1)
    cos = np.concatenate([cos, np.ones((TM, 128), f32)], axis=0)
    sin = np.concatenate([sin, np.zeros((TM, 128), f32)], axis=0)
    return jnp.asarray(cos), jnp.asarray(sin)


def _heads_last(t, n_heads):
    return t.reshape(N_CTX_SEQ, 1, n_heads, HEAD_DIM, CTX_LEN).transpose(0, 1, 4, 2, 3)


def kernel(x_prompt, x_sample, cache_a_k, cache_a_v, cache_b_ckv, cache_b_kpe, cache_c_k, cache_c_v, c, c_ctx, w_mod, b_mod, g_mix_norm, g_ffn_norm, w_in_e, g_qnorm_a, g_knorm_a, g_cq_b, w_uq_b, g_ckv_b, w_ukv_b, w_out_e, w_in_o, sink_c, w_out_o, w_up, conv_w, conv_b, w_down, g_final):
    depth = w_mod.shape[0]
    cond8 = jnp.concatenate([c_ctx[None, :], c, jnp.zeros((5, D_MODEL), F32)], axis=0)
    lane_pad = KB_PAD - QK_NOPE - QK_ROPE
    wuq = jnp.pad(w_uq_b[0].reshape(Q_LORA, N_HEADS_B, QK_NOPE + QK_ROPE),
                  ((0, 0), (0, 0), (0, lane_pad))).reshape(Q_LORA, N_HEADS_B * KB_PAD)
    wukv3 = w_ukv_b[0].reshape(KV_LORA, N_HEADS_B, QK_NOPE + V_DIM_B)
    wukk = jnp.pad(wukv3[:, :, :QK_NOPE], ((0, 0), (0, 0), (0, KB_PAD - QK_NOPE))
                   ).reshape(KV_LORA, N_HEADS_B * KB_PAD)
    wukv = wukv3[:, :, QK_NOPE:].reshape(KV_LORA, N_HEADS_B * V_DIM_B)
    gq = jnp.tile(g_qnorm_a[0], N_HEADS_A)[None, :]
    gk = jnp.tile(g_knorm_a[0], N_KV_A)[None, :]
    seg = np.arange(256) // HEAD_DIM
    ones = jnp.asarray(seg[:, None] == seg[None, :], dtype=BF16)
    cosa, sina = _rope_tables(HEAD_DIM)
    cosb, sinb = _rope_tables(QK_ROPE)
    g_mix3 = g_mix_norm.reshape(depth, 1, D_MODEL)
    g_ffn3 = g_ffn_norm.reshape(depth, 1, D_MODEL)
    conv_b3 = conv_b.reshape(depth, 1, D_FF)
    g_fin2 = g_final[None, :]

    mod = _modulation(cond8, w_mod, b_mod).reshape(depth * 8, 6, D_MODEL)

    n_past = N_LAT_SEQ * PAST_LEN
    tok_minor = lambda t: jnp.moveaxis(t[:, 0], 1, -1).reshape(N_LAT_SEQ, -1, PAST_LEN)
    ka_c, va_c, kb_c, vb_c, kc_c, vc_c = _ctx_prep(
        tok_minor(cache_a_k), tok_minor(cache_a_v), cache_b_ckv.reshape(n_past, KV_LORA),
        tok_minor(cache_b_kpe), tok_minor(cache_c_k), tok_minor(cache_c_v), wukk, wukv)

    xp = x_prompt.reshape(N_CTX_TOK, D_MODEL)
    xs = x_sample.reshape(N_TOK - N_CTX_TOK, D_MODEL)
    qa, ka, va, qb, kb, vb, nak, nav, nckv, nkpe = _pre0(
        xp, xs, mod, g_mix3, w_in_e[0].T, gq, gk, g_cq_b.reshape(1, 1, Q_LORA), g_ckv_b.reshape(1, 1, KV_LORA),
        wuq, wukk, wukv, cosa, sina, cosb, sinb, ones)
    ffn_w = (w_up, conv_w, conv_b3, w_down, g_fin2)
    ctx_kw = dict(n_seq=N_CTX_SEQ, seq_len=CTX_LEN, tok_base=0, tq=CTX_LEN, window=0, q_unit=256,
                  key_chunk=CTX_LEN, seqs=8)
    lat_kw = dict(n_seq=N_LAT_SEQ, seq_len=LAT_LEN, tok_base=N_CTX_TOK, q_unit=256)
    dense_kw = dict(tq=256, window=0, key_chunk=256, **lat_kw)
    oa = (_attention(qa, ka, va, None, None, None, name="attn_a_ctx", **ctx_kw),
          _attention(qa, ka, va, ka_c, va_c, None, name="attn_a_lat", **dense_kw))
    ob = (_attention(qb, kb, vb, None, None, None, name="attn_b_ctx", **ctx_kw),
          _attention(qb, kb, vb, kb_c, vb_c, None, name="attn_b_lat", **dense_kw))
    xp1 = _ffn(xp, [oa[0], ob[0]], mod, 0, g_ffn3, w_out_e, *ffn_w, is_ctx=True, final=False)
    xs1 = _ffn(xs, [oa[1], ob[1]], mod, 0, g_ffn3, w_out_e, *ffn_w, is_ctx=False, final=False)

    qc, kc, vc, nck, ncv = _pre1(xp1, xs1, mod, g_mix3, w_in_o, cosa, sina)
    sink = sink_c[0]
    oc = (_attention(qc, kc, vc, None, None, sink, name="attn_c_ctx", **ctx_kw),
          _attention(qc, kc, vc, kc_c, vc_c, sink, tq=128, window=WINDOW, key_chunk=PAST_LEN, seqs=4,
                     name="attn_c_lat", **lat_kw))
    y_prompt = _ffn(xp1, [oc[0]], mod, 1, g_ffn3, w_out_o, *ffn_w, is_ctx=True, final=True)
    y_sample = _ffn(xs1, [oc[1]], mod, 1, g_ffn3, w_out_o, *ffn_w, is_ctx=False, final=True)

    return (y_prompt.reshape(N_CTX_SEQ, CTX_LEN, D_MODEL), y_sample.reshape(N_LAT_SEQ, LAT_LEN, D_MODEL),
            _heads_last(nak, N_KV_A), _heads_last(nav, N_KV_A),
            nckv.reshape(N_CTX_SEQ, 1, CTX_LEN, KV_LORA),
            nkpe.reshape(N_CTX_SEQ, 1, QK_ROPE, CTX_LEN).transpose(0, 1, 3, 2),
            _heads_last(nck, N_KV_C), _heads_last(ncv, N_KV_C))
```

```python
import functools

import jax
import jax.numpy as jnp
import numpy as np
from jax import lax
from jax.experimental import pallas as pl
from jax.experimental.pallas import tpu as pltpu

F32 = jnp.float32
BF16 = jnp.bfloat16

D_MODEL = 1024
N_CTX_SEQ = 16
CTX_LEN = 256
N_LAT_SEQ = 2
LAT_LEN = 2048
PAST_LEN = 512
GRID_W = 64
ROPE_THETA = 10000.0
NORM_EPS = 1e-6
WINDOW = 128
NEG_INF = -1e30
LOG2E = 1.4426950408889634
HEAD_DIM = 64
N_HEADS_A, N_KV_A = 8, 2
N_HEADS_B = 8
Q_LORA, KV_LORA = 384, 256
QK_NOPE, QK_ROPE, V_DIM_B = 64, 32, 64
N_HEADS_C, N_KV_C = 16, 2
D_FF = 2816
IN_E_MAIN = N_HEADS_A * HEAD_DIM + 2 * N_KV_A * HEAD_DIM + Q_LORA + KV_LORA

N_CTX_TOK = N_CTX_SEQ * CTX_LEN
N_TOK = N_CTX_TOK + N_LAT_SEQ * LAT_LEN
TM = 512
PRE_SUB = 256
N_TILES = N_TOK // TM
N_CTX_TILES = N_CTX_TOK // TM
LAT_TILES = LAT_LEN // TM
HALO = 16
FFN_TM = 1024
FFN_SUB = CTX_LEN
FFN_LAT_SUB = 256
UP_LOOKAHEAD = 3
FFN_LAT_TILES = LAT_LEN // FFN_TM
FF_CHUNK = 256
N_FF_CHUNKS = D_FF // FF_CHUNK
FF_PER_STEP = 3
N_FF_STEPS = -(-N_FF_CHUNKS // FF_PER_STEP)
KB_PAD = 128
VT_ROWS = 80
SCORE_LOOKAHEAD = 5
VMEM_LIMIT = 56 * 1024 * 1024
FFN_VMEM_LIMIT = 60 * 1024 * 1024


def _dot(a, b):
    return jnp.dot(a, b, preferred_element_type=F32)


def _dot_nt(a, b):
    return lax.dot_general(a, b, (((1,), (1,)), ((), ())), preferred_element_type=F32)


def _rms(x, g):
    return x * lax.rsqrt(jnp.mean(x * x, axis=-1, keepdims=True) + NORM_EPS) * g


def _split_bf16(x):
    hi = x.astype(BF16)
    return hi, (x - hi.astype(F32)).astype(BF16)


def _head_rms(x, g, ones_ref):
    w = x.shape[1]
    hi, lo = _split_bf16(x * x)
    parts = []
    for c in range(0, w, 256):
        cw = min(256, w - c)
        ones = ones_ref[0:cw, 0:cw]
        parts.append(_dot(hi[:, c:c + cw], ones) + _dot(lo[:, c:c + cw], ones))
    ssum = parts[0] if len(parts) == 1 else jnp.concatenate(parts, axis=1)
    return x * lax.rsqrt(ssum * (1.0 / HEAD_DIM) + NORM_EPS) * g


def _swap_pairs(x):
    w = x.shape[1]
    up = pltpu.roll(x, w - 1, axis=1)
    dn = pltpu.roll(x, 1, axis=1)
    lane = lax.broadcasted_iota(jnp.int32, x.shape, 1)
    return jnp.where((lane & 1) == 0, up, dn)


def _rope(x, cos, sin_signed, reps):
    if reps > 1:
        cos = jnp.concatenate([cos] * reps, axis=1)
        sin_signed = jnp.concatenate([sin_signed] * reps, axis=1)
    return x * cos + _swap_pairs(x) * sin_signed


def _store_vt(vt_ref, v, n_heads, cols=slice(None)):
    t = v.shape[0]
    vt = v.T.astype(BF16)
    ones = jnp.ones((VT_ROWS - 64, t), BF16)
    for hh in range(n_heads):
        vt_ref[hh, 0:64, cols] = vt[hh * 64:(hh + 1) * 64]
        vt_ref[hh, 64:VT_ROWS, cols] = ones


def _cond_row(i):
    return jnp.where(i < N_CTX_TILES, 0, 1 + (i - N_CTX_TILES) // LAT_TILES)


def _rope_block(i):
    return jnp.where(i < N_CTX_TILES, LAT_TILES, (i - N_CTX_TILES) % LAT_TILES)


def _const_spec(shape):
    zeros = (0,) * len(shape)
    return pl.BlockSpec(shape, lambda *_: zeros)


def _layer_spec(shape, layer):
    idx = (layer,) + (0,) * len(shape)
    return pl.BlockSpec((None,) + tuple(shape), lambda *_: idx)


def _mod_kernel(cond_ref, w_ref, b_ref, o_ref):
    c = cond_ref[...]
    s_hi, s_lo = _split_bf16(c * jax.nn.sigmoid(c))
    w_hi, w_lo = _split_bf16(w_ref[0])
    r = _dot(jnp.concatenate([s_hi, s_lo], axis=0), w_hi)
    o_ref[0] = r[0:8] + r[8:16] + _dot(s_hi, w_lo) + b_ref[pl.ds(pl.program_id(0), 1), :]


def _modulation(cond8, w_mod, b_mod):
    depth, _, n = w_mod.shape
    tn = 3072
    return pl.pallas_call(
        _mod_kernel,
        grid=(depth, n // tn),
        in_specs=[
            pl.BlockSpec((8, D_MODEL), lambda l, j: (0, 0)),
            pl.BlockSpec((1, D_MODEL, tn), lambda l, j: (l, 0, j)),
            pl.BlockSpec((depth, tn), lambda l, j: (0, j)),
        ],
        out_specs=pl.BlockSpec((1, 8, tn), lambda l, j: (l, 0, j)),
        out_shape=jax.ShapeDtypeStruct((depth, 8, n), F32),
        compiler_params=pltpu.CompilerParams(
            dimension_semantics=("arbitrary", "arbitrary"), vmem_limit_bytes=VMEM_LIMIT),
        name="modulation",
    )(cond8, w_mod, b_mod)


def _pre0_kernel(xp_ref, xs_ref, mod_ref, gmix_ref, win_ref, gq_ref, gk_ref, gcq_ref, gckv_ref,
                 wuq_ref, wukk_ref, wukv_ref, cosa_ref, sina_ref, cosb_ref, sinb_ref, ones_ref,
                 qa_ref, ka_ref, va_ref, qb_ref, kb_ref, vb_ref,
                 nak_ref, nav_ref, nckv_ref, nkpe_ref,
                 win_s, wuq_s, wukk_s, wukv_s):
    i = pl.program_id(0)

    @pl.when(i == 0)
    def _():
        win_s[:, 0:IN_E_MAIN] = win_ref[0:IN_E_MAIN, :].T.astype(BF16)
        win_s[:, IN_E_MAIN:] = jnp.concatenate(
            [jnp.zeros((D_MODEL, QK_NOPE), F32), win_ref[IN_E_MAIN:, :].T,
             jnp.zeros((D_MODEL, KB_PAD - QK_NOPE - QK_ROPE), F32)], axis=1).astype(BF16)
        wuq_s[...] = wuq_ref[...].astype(BF16)
        wukk_s[...] = wukk_ref[...].astype(BF16)
        wukv_s[...] = wukv_ref[...].astype(BF16)

    sh1 = mod_ref[0:1, :]
    sc1 = mod_ref[1:2, :]
    subs = [slice(r, r + PRE_SUB) for r in range(0, TM, PRE_SUB)]

    projs = []
    for sl in subs:
        x = jnp.where(i < N_CTX_TILES, xp_ref[sl, :], xs_ref[sl, :])
        h = _rms(x, gmix_ref[0:1, :]) * (1.0 + sc1) + sh1
        projs.append(_dot(h.astype(BF16), win_s[...]))

    parts = []
    for sl, proj in zip(subs, projs):
        qa = _head_rms(proj[:, 0:512], gq_ref[...], ones_ref)
        ka = _head_rms(proj[:, 512:640], gk_ref[...], ones_ref)
        va = proj[:, 640:768]
        cq = _rms(proj[:, 768:1152], gcq_ref[...])
        ckv = _rms(proj[:, 1152:1408], gckv_ref[...])
        kpe = proj[:, 1408:1536]
        qb = _dot(cq.astype(BF16), wuq_s[...])
        ckv_b = ckv.astype(BF16)
        kbn = _dot(ckv_b, wukk_s[...])
        vb = _dot(ckv_b, wukv_s[...])
        _store_vt(va_ref, va, N_KV_A, sl)
        _store_vt(vb_ref, vb, N_HEADS_B, sl)
        parts.append((ka, va, ckv, kpe))
        cosa, sina = cosa_ref[sl, :], sina_ref[sl, :]
        cosb, sinb = cosb_ref[sl, :], sinb_ref[sl, :]
        qa_s = _rope(qa, cosa, sina, 4) * (LOG2E * HEAD_DIM ** -0.5)
        ka_r = _rope(ka, cosa, sina, 1)
        qb_s = _rope(qb, cosb, sinb, 8) * (LOG2E * (QK_NOPE + QK_ROPE) ** -0.5)
        kpe_r = _rope(kpe, cosb, sinb, 1)
        for hh in range(N_HEADS_A):
            qa_ref[hh, sl, :] = qa_s[:, hh * 64:(hh + 1) * 64].astype(BF16)
        for hh in range(N_KV_A):
            ka_ref[hh, sl, :] = ka_r[:, hh * 64:(hh + 1) * 64].astype(BF16)
        for hh in range(N_HEADS_B):
            qb_ref[hh, sl, :] = qb_s[:, hh * KB_PAD:(hh + 1) * KB_PAD].astype(BF16)
            kb_ref[hh, sl, :] = (kbn[:, hh * KB_PAD:(hh + 1) * KB_PAD] + kpe_r).astype(BF16)

    @pl.when(i < N_CTX_TILES)
    def _():
        for s, (sl, (ka, va, ckv, kpe)) in enumerate(zip(subs, parts)):
            nak_ref[s] = ka.T
            nav_ref[s] = va.T
            nckv_ref[sl, :] = ckv
            nkpe_ref[s] = kpe.T[QK_NOPE:QK_NOPE + QK_ROPE]


def _pre0(xp, xs, mod, gmix, w_in_e_t, gq, gk, gcq, gckv, wuq, wukk, wukv, cosa, sina, cosb, sinb, ones):
    tile = lambda i: (i, 0)
    head_tile = lambda i: (0, i, 0)
    vt_tile = lambda i: (0, 0, i)
    ctx_tile = lambda i: (jnp.minimum(i, N_CTX_TILES - 1), 0)
    lat_tile = lambda i: (jnp.maximum(i - N_CTX_TILES, 0), 0)
    rope_tile = lambda i: (_rope_block(i), 0)
    in_specs = [
        pl.BlockSpec((TM, D_MODEL), ctx_tile),
        pl.BlockSpec((TM, D_MODEL), lat_tile),
        pl.BlockSpec((None, 6, D_MODEL), lambda i: (_cond_row(i), 0, 0)),
        _const_spec(gmix.shape),
        _const_spec(w_in_e_t.shape),
        _const_spec(gq.shape), _const_spec(gk.shape), _layer_spec((1, Q_LORA), 0), _layer_spec((1, KV_LORA), 0),
        _const_spec(wuq.shape), _const_spec(wukk.shape), _const_spec(wukv.shape),
        pl.BlockSpec((TM, 128), rope_tile), pl.BlockSpec((TM, 128), rope_tile),
        pl.BlockSpec((TM, 128), rope_tile), pl.BlockSpec((TM, 128), rope_tile),
        _const_spec(ones.shape),
    ]
    out_shape = [
        jax.ShapeDtypeStruct((N_HEADS_A, N_TOK, 64), BF16),
        jax.ShapeDtypeStruct((N_KV_A, N_TOK, 64), BF16),
        jax.ShapeDtypeStruct((N_KV_A, VT_ROWS, N_TOK), BF16),
        jax.ShapeDtypeStruct((N_HEADS_B, N_TOK, KB_PAD), BF16),
        jax.ShapeDtypeStruct((N_HEADS_B, N_TOK, KB_PAD), BF16),
        jax.ShapeDtypeStruct((N_HEADS_B, VT_ROWS, N_TOK), BF16),
        jax.ShapeDtypeStruct((N_CTX_SEQ, 128, CTX_LEN), F32),
        jax.ShapeDtypeStruct((N_CTX_SEQ, 128, CTX_LEN), F32),
        jax.ShapeDtypeStruct((N_CTX_TOK, KV_LORA), F32),
        jax.ShapeDtypeStruct((N_CTX_SEQ, QK_ROPE, CTX_LEN), F32),
    ]
    seq_tile = lambda i: (jnp.minimum(i, N_CTX_TILES - 1), 0, 0)
    n_sub = TM // PRE_SUB
    out_specs = [
        pl.BlockSpec((N_HEADS_A, TM, 64), head_tile),
        pl.BlockSpec((N_KV_A, TM, 64), head_tile),
        pl.BlockSpec((N_KV_A, VT_ROWS, TM), vt_tile),
        pl.BlockSpec((N_HEADS_B, TM, KB_PAD), head_tile),
        pl.BlockSpec((N_HEADS_B, TM, KB_PAD), head_tile),
        pl.BlockSpec((N_HEADS_B, VT_ROWS, TM), vt_tile),
        pl.BlockSpec((n_sub, 128, CTX_LEN), seq_tile),
        pl.BlockSpec((n_sub, 128, CTX_LEN), seq_tile),
        pl.BlockSpec((TM, KV_LORA), ctx_tile),
        pl.BlockSpec((n_sub, QK_ROPE, CTX_LEN), seq_tile),
    ]
    scratch = [
        pltpu.VMEM((D_MODEL, IN_E_MAIN + KB_PAD), BF16),
        pltpu.VMEM(wuq.shape, BF16), pltpu.VMEM(wukk.shape, BF16), pltpu.VMEM(wukv.shape, BF16),
    ]
    return pl.pallas_call(
        _pre0_kernel, grid=(N_TILES,), in_specs=in_specs, out_specs=out_specs, out_shape=out_shape,
        scratch_shapes=scratch,
        compiler_params=pltpu.CompilerParams(
            dimension_semantics=("arbitrary",), vmem_limit_bytes=VMEM_LIMIT),
        name="pre0",
    )(xp, xs, mod, gmix, w_in_e_t, gq, gk, gcq, gckv, wuq, wukk, wukv, cosa, sina, cosb, sinb, ones)


def _pre1_kernel(xp_ref, xs_ref, mod_ref, gmix_ref, win_ref, cosa_ref, sina_ref,
                 q_ref, k_ref, v_ref, nk_ref, nv_ref, win_s):
    i = pl.program_id(0)

    @pl.when(i == 0)
    def _():
        win_s[...] = win_ref[...].astype(BF16)

    sh1 = mod_ref[0:1, :]
    sc1 = mod_ref[1:2, :]
    subs = [slice(r, r + PRE_SUB) for r in range(0, TM, PRE_SUB)]
    projs = []
    for sl in subs:
        x = jnp.where(i < N_CTX_TILES, xp_ref[sl, :], xs_ref[sl, :])
        h = _rms(x, gmix_ref[1:2, :]) * (1.0 + sc1) + sh1
        projs.append(_dot(h.astype(BF16), win_s[...]))
    for sl, proj in zip(subs, projs):
        cosa, sina = cosa_ref[sl, :], sina_ref[sl, :]
        q_s = _rope(proj[:, 0:1024], cosa, sina, 8) * (LOG2E * HEAD_DIM ** -0.5)
        k = _rope(proj[:, 1024:1152], cosa, sina, 1)
        for hh in range(N_HEADS_C):
            q_ref[hh, sl, :] = q_s[:, hh * 64:(hh + 1) * 64].astype(BF16)
        for hh in range(N_KV_C):
            k_ref[hh, sl, :] = k[:, hh * 64:(hh + 1) * 64].astype(BF16)
        _store_vt(v_ref, proj[:, 1152:1280], N_KV_C, sl)

    @pl.when(i < N_CTX_TILES)
    def _():
        for s, proj in enumerate(projs):
            nk_ref[s] = proj[:, 1024:1152].T
            nv_ref[s] = proj[:, 1152:1280].T


def _pre1(xp, xs, mod, gmix, w_in_o, cosa, sina):
    n_in = w_in_o.shape[-1]
    head_tile = lambda i: (0, i, 0)
    vt_tile = lambda i: (0, 0, i)
    ctx_tile = lambda i: (jnp.minimum(i, N_CTX_TILES - 1), 0)
    lat_tile = lambda i: (jnp.maximum(i - N_CTX_TILES, 0), 0)
    rope_tile = lambda i: (_rope_block(i), 0)
    in_specs = [
        pl.BlockSpec((TM, D_MODEL), ctx_tile),
        pl.BlockSpec((TM, D_MODEL), lat_tile),
        pl.BlockSpec((None, 6, D_MODEL), lambda i: (8 + _cond_row(i), 0, 0)),
        _const_spec(gmix.shape),
        _layer_spec((D_MODEL, n_in), 0),
        pl.BlockSpec((TM, 128), rope_tile), pl.BlockSpec((TM, 128), rope_tile),
    ]
    out_shape = [
        jax.ShapeDtypeStruct((N_HEADS_C, N_TOK, 64), BF16),
        jax.ShapeDtypeStruct((N_KV_C, N_TOK, 64), BF16),
        jax.ShapeDtypeStruct((N_KV_C, VT_ROWS, N_TOK), BF16),
        jax.ShapeDtypeStruct((N_CTX_SEQ, 128, CTX_LEN), F32),
        jax.ShapeDtypeStruct((N_CTX_SEQ, 128, CTX_LEN), F32),
    ]
    seq_tile = lambda i: (jnp.minimum(i, N_CTX_TILES - 1), 0, 0)
    out_specs = [
        pl.BlockSpec((N_HEADS_C, TM, 64), head_tile),
        pl.BlockSpec((N_KV_C, TM, 64), head_tile),
        pl.BlockSpec((N_KV_C, VT_ROWS, TM), vt_tile),
        pl.BlockSpec((TM // PRE_SUB, 128, CTX_LEN), seq_tile),
        pl.BlockSpec((TM // PRE_SUB, 128, CTX_LEN), seq_tile),
    ]
    return pl.pallas_call(
        _pre1_kernel, grid=(N_TILES,), in_specs=in_specs, out_specs=out_specs, out_shape=out_shape,
        scratch_shapes=[pltpu.VMEM((D_MODEL, n_in), BF16)],
        compiler_params=pltpu.CompilerParams(
            dimension_semantics=("arbitrary",), vmem_limit_bytes=VMEM_LIMIT),
        name="pre1",
    )(xp, xs, mod, gmix, w_in_o, cosa, sina)


def _ctx_kernel(ak_ref, av_ref, ckv_ref, kpe_ref, ck_ref, cv_ref, wukk_ref, wukv_ref,
                ka_ref, va_ref, kb_ref, vb_ref, kc_ref, vc_ref):
    ones = jnp.ones((VT_ROWS - 64, PAST_LEN), BF16)
    for hh in range(2):
        rows = slice(hh * 64, (hh + 1) * 64)
        ka_ref[hh] = ak_ref[rows, :].T.astype(BF16)
        kc_ref[hh] = ck_ref[rows, :].T.astype(BF16)
        for src, dst in ((av_ref, va_ref), (cv_ref, vc_ref)):
            dst[hh, 0:64, :] = src[rows, :].astype(BF16)
            dst[hh, 64:VT_ROWS, :] = ones
    ckv_b = ckv_ref[...].astype(BF16)
    kbn = _dot(ckv_b, wukk_ref[...].astype(BF16))
    vb = _dot(ckv_b, wukv_ref[...].astype(BF16))
    kpe = jnp.concatenate([jnp.zeros((PAST_LEN, QK_NOPE), F32), kpe_ref[...].T,
                           jnp.zeros((PAST_LEN, KB_PAD - QK_NOPE - QK_ROPE), F32)], axis=1)
    for hh in range(N_HEADS_B):
        kb_ref[hh] = (kbn[:, hh * KB_PAD:(hh + 1) * KB_PAD] + kpe).astype(BF16)
    _store_vt(vb_ref, vb, N_HEADS_B)


def _ctx_prep(ak, av, ckv, kpe, ck, cv, wukk, wukv):
    n = N_LAT_SEQ * PAST_LEN
    row = lambda b: (b, 0)
    head_row = lambda b: (0, b, 0)
    vt_row = lambda b: (0, 0, b)
    tok_minor = lambda rows: pl.BlockSpec((None, rows, PAST_LEN), lambda b: (b, 0, 0))
    in_specs = [
        tok_minor(128), tok_minor(128),
        pl.BlockSpec((PAST_LEN, KV_LORA), row), tok_minor(QK_ROPE),
        tok_minor(128), tok_minor(128),
        _const_spec(wukk.shape), _const_spec(wukv.shape),
    ]
    out_shape = [
        jax.ShapeDtypeStruct((2, n, 64), BF16), jax.ShapeDtypeStruct((2, VT_ROWS, n), BF16),
        jax.ShapeDtypeStruct((N_HEADS_B, n, KB_PAD), BF16), jax.ShapeDtypeStruct((N_HEADS_B, VT_ROWS, n), BF16),
        jax.ShapeDtypeStruct((2, n, 64), BF16), jax.ShapeDtypeStruct((2, VT_ROWS, n), BF16),
    ]
    out_specs = [
        pl.BlockSpec((2, PAST_LEN, 64), head_row), pl.BlockSpec((2, VT_ROWS, PAST_LEN), vt_row),
        pl.BlockSpec((N_HEADS_B, PAST_LEN, KB_PAD), head_row), pl.BlockSpec((N_HEADS_B, VT_ROWS, PAST_LEN), vt_row),
        pl.BlockSpec((2, PAST_LEN, 64), head_row), pl.BlockSpec((2, VT_ROWS, PAST_LEN), vt_row),
    ]
    return pl.pallas_call(
        _ctx_kernel, grid=(N_LAT_SEQ,), in_specs=in_specs, out_specs=out_specs, out_shape=out_shape,
        compiler_params=pltpu.CompilerParams(
            dimension_semantics=("arbitrary",), vmem_limit_bytes=VMEM_LIMIT),
        name="ctx_prep",
    )(ak, av, ckv, kpe, ck, cv, wukk, wukv)


def _softmax_units(units, lookahead):
    tasks = [(u, c) for u, unit in enumerate(units) for c in range(len(unit["chunks"]))]
    scores = {}
    qts = [unit["q"].astype(F32).T.astype(BF16) if len(unit["chunks"]) >= 4 else None for unit in units]

    def emit_scores(t):
        u, c = tasks[t]
        k, _, mask = units[u]["chunks"][c]
        s = _dot_nt(k, units[u]["q"]) if qts[u] is None else _dot(k, qts[u])
        scores[t] = s if mask is None else jnp.where(mask, s, NEG_INF)

    for t in range(min(lookahead, len(tasks))):
        emit_scores(t)
    for t, (u, c) in enumerate(tasks):
        if t + lookahead < len(tasks):
            emit_scores(t + lookahead)
        unit = units[u]
        s = scores.pop(t)
        m, acc = unit["m"], unit["acc"]
        cmax = jnp.max(s, axis=0, keepdims=True)
        m_new = cmax if m is None else jnp.maximum(m, cmax)
        pv = _dot(unit["chunks"][c][1], jnp.exp2(s - m_new).astype(BF16))
        unit["acc"] = pv if acc is None else acc * jnp.exp2(m - m_new) + pv
        unit["m"] = m_new
    return [unit["acc"][0:64] * (1.0 / unit["acc"][64:65]) for unit in units]


def _attn_kernel(*refs, n_kv, group, tq, seq_len, seqs, has_ctx, has_sink, window, q_unit, key_chunk):
    refs = list(refs)
    q_ref = refs.pop(0)
    if window:
        n_blk = seqs + 2
        kb_refs, vb_refs = refs[:n_blk], refs[n_blk:2 * n_blk]
        refs = refs[2 * n_blk:]
    else:
        k_ref, vt_ref = refs[:2]
        refs = refs[2:]
    if has_ctx:
        kx_ref, vx_ref = refs[:2]
        refs = refs[2:]
    if has_sink:
        sink_ref = refs.pop(0)
    o_ref = refs.pop(0)

    j = pl.program_id(1)
    dk = q_ref.shape[-1]
    heads_per_unit = q_unit // tq
    lane = lax.broadcasted_iota(jnp.int32, (1, q_unit), 1)
    if window:
        n_band = tq + 2 * window
        krow = lax.broadcasted_iota(jnp.int32, (n_band, q_unit), 0)
        qcol = lax.broadcasted_iota(jnp.int32, (n_band, q_unit), 1) & (tq - 1)
        rel = (krow - window) - qcol
        in_band = jnp.abs(rel) <= window
        last_j = seq_len // (tq * seqs) - 1
        band_masks = []
        for sq in range(seqs):
            mask = in_band
            if sq == 0:
                mask = mask & ((krow >= window) | (j > 0))
            if sq == seqs - 1:
                mask = mask & ((krow < window + tq) | (j < last_j))
            band_masks.append(mask)
    if has_sink:
        acc0 = jnp.where(lax.broadcasted_iota(jnp.int32, (VT_ROWS, q_unit), 0) >= 64, 1.0, 0.0)

    units = []
    for sq in range(seqs):
        base = sq * (tq if window else seq_len)
        for hk in range(n_kv):
            chunks = []
            if window:
                chunks.append((jnp.concatenate([r[hk] for r in kb_refs[sq:sq + 3]], axis=0),
                               jnp.concatenate([r[hk] for r in vb_refs[sq:sq + 3]], axis=1), band_masks[sq]))
            else:
                for c in range(base, base + seq_len, key_chunk):
                    n = min(key_chunk, base + seq_len - c)
                    chunks.append((k_ref[hk, c:c + n, :], vt_ref[hk, :, c:c + n], None))
            if has_ctx:
                for c in range(0, PAST_LEN, key_chunk):
                    n = min(key_chunk, PAST_LEN - c)
                    chunks.append((kx_ref[hk, c:c + n, :], vx_ref[hk, :, c:c + n], None))
            for u in range(group // heads_per_unit):
                h0 = hk * group + u * heads_per_unit
                q = q_ref[h0:h0 + heads_per_unit, base:base + tq, :].reshape(q_unit, dk)
                unit = dict(q=q, chunks=chunks, m=None, acc=None)
                if has_sink:
                    m0 = jnp.full((1, q_unit), sink_ref[h0] * LOG2E, F32)
                    for e in range(1, heads_per_unit):
                        m0 = jnp.where(lane >= e * tq, sink_ref[h0 + e] * LOG2E, m0)
                    unit.update(m=m0, acc=acc0)
                units.append(unit)
    results = _softmax_units(units, SCORE_LOOKAHEAD)
    per_seq = len(results) // seqs
    for sq in range(seqs):
        outs = []
        for o in results[sq * per_seq:(sq + 1) * per_seq]:
            for e in range(heads_per_unit):
                outs.append(o[:, e * tq:(e + 1) * tq])
        base = sq * (tq if window else seq_len)
        o_ref[base:base + tq, :] = jnp.concatenate(outs, axis=0).T.astype(BF16)


def _attention(q, k, vt, kx, vx, sink, *, n_seq, seq_len, tok_base, tq, window, q_unit, key_chunk, name,
               seqs=1):
    n_q, _, dk = q.shape
    n_kv = k.shape[0]
    group = n_q // n_kv
    has_ctx = kx is not None
    has_sink = sink is not None
    if window:
        n_qt = seq_len // (tq * seqs)
    else:
        n_qt = seq_len // tq
        assert seqs == 1 or (n_qt == 1 and not has_ctx)
    q_base = tok_base // (seqs * tq)
    s_base = tok_base // (seqs * seq_len)
    q_blk = lambda b, j: q_base + b * n_qt + j
    in_specs = [pl.BlockSpec((n_q, seqs * tq, dk), lambda b, j: (0, q_blk(b, j), 0))]
    args = [q]
    if window:
        assert window == tq
        n_kb = seq_len // tq
        kb_base = tok_base // tq

        def key_blk(b, j, off):
            return kb_base + b * n_kb + jnp.clip(j * seqs - 1 + off, 0, n_kb - 1)

        for off in range(seqs + 2):
            in_specs.append(pl.BlockSpec((n_kv, tq, dk), lambda b, j, off=off: (0, key_blk(b, j, off), 0)))
        for off in range(seqs + 2):
            in_specs.append(pl.BlockSpec((n_kv, VT_ROWS, tq), lambda b, j, off=off: (0, 0, key_blk(b, j, off))))
        args += [k] * (seqs + 2) + [vt] * (seqs + 2)
    else:
        in_specs += [
            pl.BlockSpec((n_kv, seqs * seq_len, dk), lambda b, j: (0, s_base + b, 0)),
            pl.BlockSpec((n_kv, VT_ROWS, seqs * seq_len), lambda b, j: (0, 0, s_base + b)),
        ]
        args += [k, vt]
    if has_ctx:
        in_specs += [
            pl.BlockSpec((n_kv, PAST_LEN, dk), lambda b, j: (0, b, 0)),
            pl.BlockSpec((n_kv, VT_ROWS, PAST_LEN), lambda b, j: (0, 0, b)),
        ]
        args += [kx, vx]
    if has_sink:
        in_specs.append(pl.BlockSpec(memory_space=pltpu.SMEM))
        args.append(sink)
    kern = functools.partial(_attn_kernel, n_kv=n_kv, group=group, tq=tq, seq_len=seq_len, seqs=seqs,
                             has_ctx=has_ctx, has_sink=has_sink, window=window,
                             q_unit=q_unit, key_chunk=key_chunk)
    return pl.pallas_call(
        kern, grid=(n_seq if window else n_seq // seqs, n_qt), in_specs=in_specs,
        out_specs=pl.BlockSpec((seqs * tq, n_q * HEAD_DIM), lambda b, j: (b * n_qt + j, 0)),
        out_shape=jax.ShapeDtypeStruct((n_seq * seq_len, n_q * HEAD_DIM), BF16),
        compiler_params=pltpu.CompilerParams(
            dimension_semantics=("arbitrary", "arbitrary"), vmem_limit_bytes=VMEM_LIMIT),
        name=name,
    )(*args)


def _ffn_kernel(*refs, n_o, layer, is_ctx, final):
    halo = 0 if is_ctx else HALO
    it = iter(refs)
    x_ref = next(it)
    xh_refs = None if is_ctx else (next(it), next(it))
    o_refs, oh_refs = [], []
    for _ in range(n_o):
        o_refs.append(next(it))
        if not is_ctx:
            oh_refs.append((next(it), next(it)))
    wo_refs = [next(it) for _ in range(n_o)]
    mod_ref, gffn_ref = next(it), next(it)
    wg_refs = [next(it) for _ in range(FF_PER_STEP)]
    wv_refs = [next(it) for _ in range(FF_PER_STEP)]
    cw_ref, cb_ref = next(it), next(it)
    wd_refs = [next(it) for _ in range(FF_PER_STEP)]
    gfin_ref, out_ref, h2e_ref, acc_ref = next(it), next(it), next(it), next(it)

    m = pl.program_id(0)
    c = pl.program_id(1)
    g1 = mod_ref[2:3, :]
    sh2 = mod_ref[3:4, :]
    sc2 = mod_ref[4:5, :]
    g2 = mod_ref[5:6, :]
    sub = FFN_SUB if is_ctx else FFN_LAT_SUB
    n_sub = FFN_TM // sub
    sub_rows = sub + 2 * halo

    @pl.when(c == 0)
    def _():
        wos = [w[...].astype(BF16) for w in wo_refs]

        def residual_and_norm(xv, ovs):
            attn = _dot(ovs[0], wos[0])
            for ov, wo in zip(ovs[1:], wos[1:]):
                attn = attn + _dot(ov, wo)
            x1 = xv + g1 * attn
            return x1, (_rms(x1, gffn_ref[layer:layer + 1, :]) * (1.0 + sc2) + sh2).astype(BF16)

        for r in range(0, FFN_TM, FFN_SUB):
            x1, h2 = residual_and_norm(x_ref[r:r + FFN_SUB, :], [o[r:r + FFN_SUB, :] for o in o_refs])
            acc_ref[r:r + FFN_SUB, :] = x1
            h2e_ref[halo + r:halo + r + FFN_SUB, :] = h2
        if not is_ctx:
            _, h2h = residual_and_norm(
                jnp.concatenate([xh_refs[0][...], xh_refs[1][...]], axis=0),
                [jnp.concatenate([oh[0][...], oh[1][...]], axis=0) for oh in oh_refs])
            h2e_ref[0:HALO, :] = h2h[0:HALO]
            h2e_ref[HALO + FFN_TM:, :] = h2h[HALO:]

    row8 = lax.broadcasted_iota(jnp.int32, (8, FF_CHUNK), 0)
    if not is_ctx:
        has_prev = m % FFN_LAT_TILES != 0
        has_next = m % FFN_LAT_TILES != FFN_LAT_TILES - 1

    def ff_chunks(n_chunks):
        w_up, w_dn, cw, cb = [], [], [], []
        for j in range(n_chunks):
            w_up.append(jnp.concatenate([wg_refs[j][...].astype(BF16), wv_refs[j][...].astype(BF16)],
                                        axis=1))
            w_dn.append(wd_refs[j][...].astype(BF16))
            cols = pl.ds(pl.multiple_of((c * FF_PER_STEP + j) * FF_CHUNK, FF_CHUNK), FF_CHUNK)
            cw.append([cw_ref[t, layer:layer + 1, cols] for t in range(3)])
            cb.append(cb_ref[layer:layer + 1, cols])
        tasks = [(j, r) for j in range(n_chunks) for r in range(n_sub)]
        ups = {}

        def emit_up(t):
            j, r = tasks[t]
            ups[t] = _dot(h2e_ref[r * sub:r * sub + sub_rows, :], w_up[j])

        for t in range(min(UP_LOOKAHEAD, len(tasks))):
            emit_up(t)
        for t, (j, r) in enumerate(tasks):
            if t + UP_LOOKAHEAD < len(tasks):
                emit_up(t + UP_LOOKAHEAD)
            up = ups.pop(t)
            ge = up[:, :FF_CHUNK]
            val = up[halo:halo + sub, FF_CHUNK:]
            g_prev = pltpu.roll(ge, 1, axis=0)[halo:halo + sub]
            g_next = pltpu.roll(ge, sub_rows - 1, axis=0)[halo:halo + sub]
            prev_ok = False if is_ctx else (has_prev if r == 0 else True)
            next_ok = False if is_ctx else (has_next if r == n_sub - 1 else True)
            if prev_ok is not True:
                g_prev = jnp.concatenate(
                    [jnp.where(jnp.logical_or(row8 != 0, prev_ok), g_prev[0:8], 0.0), g_prev[8:]], axis=0)
            if next_ok is not True:
                g_next = jnp.concatenate(
                    [g_next[:-8], jnp.where(jnp.logical_or(row8 != 7, next_ok), g_next[-8:], 0.0)], axis=0)
            gate = g_prev * cw[j][0] + ge[halo:halo + sub] * cw[j][1] + g_next * cw[j][2] + cb[j]
            act = (gate * jax.nn.sigmoid(gate) * val).astype(BF16)
            acc_ref[r * sub:(r + 1) * sub, :] += g2 * _dot(act, w_dn[j])

    n_tail = N_FF_CHUNKS % FF_PER_STEP
    if n_tail:
        pl.when(c < N_FF_STEPS - 1)(functools.partial(ff_chunks, FF_PER_STEP))
    else:
        ff_chunks(FF_PER_STEP)

    @pl.when(c == N_FF_STEPS - 1)
    def _():
        if n_tail:
            ff_chunks(n_tail)
        x2 = acc_ref[...]
        out_ref[...] = _rms(x2, gfin_ref[...]) if final else x2


def _ffn(x, os, mod, layer, g_ffn, w_out, w_up, conv_w, conv_b, w_down, g_final, *, is_ctx, final):
    n_rows = x.shape[0]
    nh = FFN_TM // HALO
    nblk = n_rows // HALO
    n_o = len(os)
    halo = 0 if is_ctx else HALO
    tile = lambda m, c: (m, 0)
    prev = lambda m, c: (jnp.maximum(m * nh - 1, 0), 0)
    nxt = lambda m, c: (jnp.minimum((m + 1) * nh, nblk - 1), 0)
    chunk = lambda c, j: jnp.minimum(c * FF_PER_STEP + j, N_FF_CHUNKS - 1)
    if is_ctx:
        cond = lambda m: layer * 8
    else:
        cond = lambda m: layer * 8 + 1 + m // FFN_LAT_TILES

    def with_halo(arr):
        w = arr.shape[1]
        specs = [pl.BlockSpec((FFN_TM, w), tile)]
        if not is_ctx:
            specs += [pl.BlockSpec((HALO, w), prev), pl.BlockSpec((HALO, w), nxt)]
        return specs, [arr] * len(specs)

    in_specs, args = with_halo(x)
    for o in os:
        specs, arrs = with_halo(o)
        in_specs += specs
        args += arrs
    w_rows = D_MODEL // n_o
    for t in range(n_o):
        in_specs.append(pl.BlockSpec((None, w_rows, D_MODEL), lambda m, c, t=t: (0, t, 0)))
        args.append(w_out)
    in_specs += [
        pl.BlockSpec((None, 6, D_MODEL), lambda m, c: (cond(m), 0, 0)),
        _const_spec(g_ffn.shape),
    ]
    args += [mod, g_ffn]
    steps = range(FF_PER_STEP)
    in_specs += [pl.BlockSpec((None, D_MODEL, FF_CHUNK), lambda m, c, j=j: (layer, 0, chunk(c, j))) for j in steps]
    in_specs += [pl.BlockSpec((None, D_MODEL, FF_CHUNK), lambda m, c, j=j: (layer, 0, N_FF_CHUNKS + chunk(c, j)))
                 for j in steps]
    in_specs += [_const_spec(conv_w.shape), _const_spec(conv_b.shape)]
    in_specs += [pl.BlockSpec((None, FF_CHUNK, D_MODEL), lambda m, c, j=j: (layer, chunk(c, j), 0)) for j in steps]
    args += [w_up] * (2 * FF_PER_STEP) + [conv_w, conv_b] + [w_down] * FF_PER_STEP
    in_specs.append(pl.BlockSpec((1, D_MODEL), lambda m, c: (0, 0)))
    args.append(g_final)
    return pl.pallas_call(
        functools.partial(_ffn_kernel, n_o=n_o, layer=layer, is_ctx=is_ctx, final=final),
        grid=(n_rows // FFN_TM, N_FF_STEPS), in_specs=in_specs,
        out_specs=pl.BlockSpec((FFN_TM, D_MODEL), tile),
        out_shape=jax.ShapeDtypeStruct((n_rows, D_MODEL), F32),
        scratch_shapes=[pltpu.VMEM((FFN_TM + 2 * halo, D_MODEL), BF16), pltpu.VMEM((FFN_TM, D_MODEL), F32)],
        compiler_params=pltpu.CompilerParams(
            dimension_semantics=("arbitrary", "arbitrary"), vmem_limit_bytes=FFN_VMEM_LIMIT),
        name=("ffn_ctx" if is_ctx else "ffn_lat") + ("_final" if final else ""),
    )(*args)


def _rope_tables(rot_dim):
    f32 = np.float32
    t = np.arange(LAT_LEN)
    row = (t // GRID_W).astype(f32)
    col = (t % GRID_W).astype(f32)
    d_axis = rot_dim // 2
    freqs = (f32(ROPE_THETA) ** (-np.arange(0, d_axis, 2, dtype=f32) / f32(d_axis))).astype(f32)
    ang = np.concatenate([row[:, None] * freqs, col[:, None] * freqs], axis=-1)
    cos = np.repeat(np.cos(ang), 2, axis=-1).astype(f32)
    sin = (np.repeat(np.sin(ang), 2, axis=-1) * np.tile(np.array([-1.0, 1.0], f32), rot_dim // 2)).astype(f32)
    if rot_dim == HEAD_DIM:
        cos = np.tile(cos, (1, 2))
        sin = np.tile(sin, (1, 2))
    else:
        cos = np.concatenate([np.ones((LAT_LEN, QK_NOPE), f32), cos,
                              np.ones((LAT_LEN, 128 - QK_NOPE - rot_dim), f32)], axis=-1)
        sin = np.concatenate([np.zeros((LAT_LEN, QK_NOPE), f32), sin,
                              np.zeros((LAT_LEN, 128 - QK_NOPE - rot_dim), f32)], axis=-1)
    cos = np.concatenate([cos, np.ones((TM, 128), f32)], axis=0)
    sin = np.concatenate([sin, np.zeros((TM, 128), f32)], axis=0)
    return jnp.asarray(cos), jnp.asarray(sin)


def _heads_last(t, n_heads):
    return t.reshape(N_CTX_SEQ, 1, n_heads, HEAD_DIM, CTX_LEN).transpose(0, 1, 4, 2, 3)


def kernel(x_prompt, x_sample, cache_a_k, cache_a_v, cache_b_ckv, cache_b_kpe, cache_c_k, cache_c_v, c, c_ctx, w_mod, b_mod, g_mix_norm, g_ffn_norm, w_in_e, g_qnorm_a, g_knorm_a, g_cq_b, w_uq_b, g_ckv_b, w_ukv_b, w_out_e, w_in_o, sink_c, w_out_o, w_up, conv_w, conv_b, w_down, g_final):
    depth = w_mod.shape[0]
    cond8 = jnp.concatenate([c_ctx[None, :], c, jnp.zeros((5, D_MODEL), F32)], axis=0)
    lane_pad = KB_PAD - QK_NOPE - QK_ROPE
    wuq = jnp.pad(w_uq_b[0].reshape(Q_LORA, N_HEADS_B, QK_NOPE + QK_ROPE),
                  ((0, 0), (0, 0), (0, lane_pad))).reshape(Q_LORA, N_HEADS_B * KB_PAD)
    wukv3 = w_ukv_b[0].reshape(KV_LORA, N_HEADS_B, QK_NOPE + V_DIM_B)
    wukk = jnp.pad(wukv3[:, :, :QK_NOPE], ((0, 0), (0, 0), (0, KB_PAD - QK_NOPE))
                   ).reshape(KV_LORA, N_HEADS_B * KB_PAD)
    wukv = wukv3[:, :, QK_NOPE:].reshape(KV_LORA, N_HEADS_B * V_DIM_B)
    gq = jnp.tile(g_qnorm_a[0], N_HEADS_A)[None, :]
    gk = jnp.tile(g_knorm_a[0], N_KV_A)[None, :]
    seg = np.arange(256) // HEAD_DIM
    ones = jnp.asarray(seg[:, None] == seg[None, :], dtype=BF16)
    cosa, sina = _rope_tables(HEAD_DIM)
    cosb, sinb = _rope_tables(QK_ROPE)
    g_fin2 = g_final[None, :]

    mod = _modulation(cond8, w_mod, b_mod).reshape(depth * 8, 6, D_MODEL)

    n_past = N_LAT_SEQ * PAST_LEN
    tok_minor = lambda t: jnp.moveaxis(t[:, 0], 1, -1).reshape(N_LAT_SEQ, -1, PAST_LEN)
    ka_c, va_c, kb_c, vb_c, kc_c, vc_c = _ctx_prep(
        tok_minor(cache_a_k), tok_minor(cache_a_v), cache_b_ckv.reshape(n_past, KV_LORA),
        tok_minor(cache_b_kpe), tok_minor(cache_c_k), tok_minor(cache_c_v), wukk, wukv)

    xp = x_prompt.reshape(N_CTX_TOK, D_MODEL)
    xs = x_sample.reshape(N_TOK - N_CTX_TOK, D_MODEL)
    qa, ka, va, qb, kb, vb, nak, nav, nckv, nkpe = _pre0(
        xp, xs, mod, g_mix_norm, w_in_e[0].T, gq, gk, g_cq_b.reshape(1, 1, Q_LORA), g_ckv_b.reshape(1, 1, KV_LORA),
        wuq, wukk, wukv, cosa, sina, cosb, sinb, ones)
    ffn_w = (w_up, conv_w.transpose(1, 0, 2), conv_b, w_down, g_fin2)
    ctx_kw = dict(n_seq=N_CTX_SEQ, seq_len=CTX_LEN, tok_base=0, tq=CTX_LEN, window=0, q_unit=256,
                  key_chunk=CTX_LEN, seqs=8)
    lat_kw = dict(n_seq=N_LAT_SEQ, seq_len=LAT_LEN, tok_base=N_CTX_TOK, q_unit=256)
    dense_kw = dict(tq=256, window=0, key_chunk=256, **lat_kw)
    oa = (_attention(qa, ka, va, None, None, None, name="attn_a_ctx", **ctx_kw),
          _attention(qa, ka, va, ka_c, va_c, None, name="attn_a_lat", **dense_kw))
    ob = (_attention(qb, kb, vb, None, None, None, name="attn_b_ctx", **ctx_kw),
          _attention(qb, kb, vb, kb_c, vb_c, None, name="attn_b_lat", **dense_kw))
    xp1 = _ffn(xp, [oa[0], ob[0]], mod, 0, g_ffn_norm, w_out_e, *ffn_w, is_ctx=True, final=False)
    xs1 = _ffn(xs, [oa[1], ob[1]], mod, 0, g_ffn_norm, w_out_e, *ffn_w, is_ctx=False, final=False)

    qc, kc, vc, nck, ncv = _pre1(xp1, xs1, mod, g_mix_norm, w_in_o, cosa, sina)
    sink = sink_c[0]
    oc = (_attention(qc, kc, vc, None, None, sink, name="attn_c_ctx", **ctx_kw),
          _attention(qc, kc, vc, kc_c, vc_c, sink, tq=128, window=WINDOW, key_chunk=PAST_LEN, seqs=4,
                     name="attn_c_lat", **lat_kw))
    y_prompt = _ffn(xp1, [oc[0]], mod, 1, g_ffn_norm, w_out_o, *ffn_w, is_ctx=True, final=True)
    y_sample = _ffn(xs1, [oc[1]], mod, 1, g_ffn_norm, w_out_o, *ffn_w, is_ctx=False, final=True)

    return (y_prompt.reshape(N_CTX_SEQ, CTX_LEN, D_MODEL), y_sample.reshape(N_LAT_SEQ, LAT_LEN, D_MODEL),
            _heads_last(nak, N_KV_A), _heads_last(nav, N_KV_A),
            nckv.reshape(N_CTX_SEQ, 1, CTX_LEN, KV_LORA),
            nkpe.reshape(N_CTX_SEQ, 1, QK_ROPE, CTX_LEN).transpose(0, 1, 3, 2),
            _heads_last(nck, N_KV_C), _heads_last(ncv, N_KV_C))
```

```python
import functools

import jax
import jax.numpy as jnp
import numpy as np
from jax import lax
from jax.experimental import pallas as pl
from jax.experimental.pallas import tpu as pltpu

F32 = jnp.float32
BF16 = jnp.bfloat16

D_MODEL = 1024
N_CTX_SEQ = 16
CTX_LEN = 256
N_LAT_SEQ = 2
LAT_LEN = 2048
PAST_LEN = 512
GRID_W = 64
ROPE_THETA = 10000.0
NORM_EPS = 1e-6
WINDOW = 128
NEG_INF = -1e30
LOG2E = 1.4426950408889634
HEAD_DIM = 64
N_HEADS_A, N_KV_A = 8, 2
N_HEADS_B = 8
Q_LORA, KV_LORA = 384, 256
QK_NOPE, QK_ROPE, V_DIM_B = 64, 32, 64
N_HEADS_C, N_KV_C = 16, 2
D_FF = 2816
IN_E_MAIN = N_HEADS_A * HEAD_DIM + 2 * N_KV_A * HEAD_DIM + Q_LORA + KV_LORA

N_CTX_TOK = N_CTX_SEQ * CTX_LEN
N_TOK = N_CTX_TOK + N_LAT_SEQ * LAT_LEN
TM = 512
PRE_SUB = 256
N_TILES = N_TOK // TM
N_CTX_TILES = N_CTX_TOK // TM
LAT_TILES = LAT_LEN // TM
HALO = 16
FFN_TM = 1024
FFN_SUB = CTX_LEN
FFN_LAT_SUB = 256
UP_LOOKAHEAD = 3
FFN_LAT_TILES = LAT_LEN // FFN_TM
FF_CHUNK = 256
N_FF_CHUNKS = D_FF // FF_CHUNK
FF_PER_STEP = 3
N_FF_STEPS = -(-N_FF_CHUNKS // FF_PER_STEP)
KB_PAD = 128
VT_ROWS = 80
SCORE_LOOKAHEAD = 5
VMEM_LIMIT = 56 * 1024 * 1024
FFN_VMEM_LIMIT = 60 * 1024 * 1024


def _dot(a, b):
    return jnp.dot(a, b, preferred_element_type=F32)


def _dot_nt(a, b):
    return lax.dot_general(a, b, (((1,), (1,)), ((), ())), preferred_element_type=F32)


def _rms(x, g):
    return x * lax.rsqrt(jnp.mean(x * x, axis=-1, keepdims=True) + NORM_EPS) * g


def _split_bf16(x):
    hi = x.astype(BF16)
    return hi, (x - hi.astype(F32)).astype(BF16)


def _head_rms(x, g, ones_ref):
    w = x.shape[1]
    hi, lo = _split_bf16(x * x)
    parts = []
    for c in range(0, w, 256):
        cw = min(256, w - c)
        ones = ones_ref[0:cw, 0:cw]
        parts.append(_dot(hi[:, c:c + cw], ones) + _dot(lo[:, c:c + cw], ones))
    ssum = parts[0] if len(parts) == 1 else jnp.concatenate(parts, axis=1)
    return x * lax.rsqrt(ssum * (1.0 / HEAD_DIM) + NORM_EPS) * g


def _swap_pairs(x):
    w = x.shape[1]
    up = pltpu.roll(x, w - 1, axis=1)
    dn = pltpu.roll(x, 1, axis=1)
    lane = lax.broadcasted_iota(jnp.int32, x.shape, 1)
    return jnp.where((lane & 1) == 0, up, dn)


def _rope(x, cos, sin_signed, reps):
    if reps > 1:
        cos = jnp.concatenate([cos] * reps, axis=1)
        sin_signed = jnp.concatenate([sin_signed] * reps, axis=1)
    return x * cos + _swap_pairs(x) * sin_signed


def _store_vt(vt_ref, v, n_heads, cols=slice(None)):
    t = v.shape[0]
    vt = v.T.astype(BF16)
    ones = jnp.ones((VT_ROWS - 64, t), BF16)
    for hh in range(n_heads):
        vt_ref[hh, 0:64, cols] = vt[hh * 64:(hh + 1) * 64]
        vt_ref[hh, 64:VT_ROWS, cols] = ones


def _cond_row(i):
    return jnp.where(i < N_CTX_TILES, 0, 1 + (i - N_CTX_TILES) // LAT_TILES)


def _mod_part(mod_ref, row, k):
    return mod_ref[pl.ds(row, 1), k * D_MODEL:(k + 1) * D_MODEL]


def _rope_block(i):
    return jnp.where(i < N_CTX_TILES, LAT_TILES, (i - N_CTX_TILES) % LAT_TILES)


def _const_spec(shape):
    zeros = (0,) * len(shape)
    return pl.BlockSpec(shape, lambda *_: zeros)


def _layer_spec(shape, layer):
    idx = (layer,) + (0,) * len(shape)
    return pl.BlockSpec((None,) + tuple(shape), lambda *_: idx)


def _mod_kernel(cctx_ref, c_ref, w_ref, b_ref, o_ref):
    c = jnp.concatenate([cctx_ref[...], c_ref[...], jnp.zeros((8 - 1 - N_LAT_SEQ, D_MODEL), F32)], axis=0)
    s_hi, s_lo = _split_bf16(c * jax.nn.sigmoid(c))
    w_hi, w_lo = _split_bf16(w_ref[0])
    r = _dot(jnp.concatenate([s_hi, s_lo], axis=0), w_hi)
    o_ref[0] = r[0:8] + r[8:16] + _dot(s_hi, w_lo) + b_ref[pl.ds(pl.program_id(0), 1), :]


def _modulation(c_ctx, c, w_mod, b_mod):
    depth, _, n = w_mod.shape
    tn = 3072
    return pl.pallas_call(
        _mod_kernel,
        grid=(depth, n // tn),
        in_specs=[
            pl.BlockSpec((1, D_MODEL), lambda l, j: (0, 0)),
            pl.BlockSpec((N_LAT_SEQ, D_MODEL), lambda l, j: (0, 0)),
            pl.BlockSpec((1, D_MODEL, tn), lambda l, j: (l, 0, j)),
            pl.BlockSpec((depth, tn), lambda l, j: (0, j)),
        ],
        out_specs=pl.BlockSpec((1, 8, tn), lambda l, j: (l, 0, j)),
        out_shape=jax.ShapeDtypeStruct((depth, 8, n), F32),
        compiler_params=pltpu.CompilerParams(
            dimension_semantics=("arbitrary", "arbitrary"), vmem_limit_bytes=VMEM_LIMIT),
        name="modulation",
    )(c_ctx[None, :], c, w_mod, b_mod)


def _pre0_kernel(xp_ref, xs_ref, mod_ref, gmix_ref, win_ref, gq_ref, gk_ref, gcq_ref, gckv_ref,
                 wuq_ref, wukk_ref, wukv_ref, cosa_ref, sina_ref, cosb_ref, sinb_ref, ones_ref,
                 qa_ref, ka_ref, va_ref, qb_ref, kb_ref, vb_ref,
                 nak_ref, nav_ref, nckv_ref, nkpe_ref,
                 win_s, wuq_s, wukk_s, wukv_s):
    i = pl.program_id(0)

    @pl.when(i == 0)
    def _():
        win_s[:, 0:IN_E_MAIN] = win_ref[0:IN_E_MAIN, :].T.astype(BF16)
        win_s[:, IN_E_MAIN:] = jnp.concatenate(
            [jnp.zeros((D_MODEL, QK_NOPE), F32), win_ref[IN_E_MAIN:, :].T,
             jnp.zeros((D_MODEL, KB_PAD - QK_NOPE - QK_ROPE), F32)], axis=1).astype(BF16)
        wuq_s[...] = wuq_ref[...].astype(BF16)
        wukk_s[...] = wukk_ref[...].astype(BF16)
        wukv_s[...] = wukv_ref[...].astype(BF16)

    sh1 = _mod_part(mod_ref, _cond_row(i), 0)
    sc1 = _mod_part(mod_ref, _cond_row(i), 1)
    subs = [slice(r, r + PRE_SUB) for r in range(0, TM, PRE_SUB)]

    projs = []
    for sl in subs:
        x = jnp.where(i < N_CTX_TILES, xp_ref[sl, :], xs_ref[sl, :])
        h = _rms(x, gmix_ref[0:1, :]) * (1.0 + sc1) + sh1
        projs.append(_dot(h.astype(BF16), win_s[...]))

    parts = []
    for sl, proj in zip(subs, projs):
        qa = _head_rms(proj[:, 0:512], jnp.concatenate([gq_ref[...]] * N_HEADS_A, axis=1), ones_ref)
        ka = _head_rms(proj[:, 512:640], jnp.concatenate([gk_ref[...]] * N_KV_A, axis=1), ones_ref)
        va = proj[:, 640:768]
        cq = _rms(proj[:, 768:1152], gcq_ref[...])
        ckv = _rms(proj[:, 1152:1408], gckv_ref[...])
        kpe = proj[:, 1408:1536]
        qb = _dot(cq.astype(BF16), wuq_s[...])
        ckv_b = ckv.astype(BF16)
        kbn = _dot(ckv_b, wukk_s[...])
        vb = _dot(ckv_b, wukv_s[...])
        _store_vt(va_ref, va, N_KV_A, sl)
        _store_vt(vb_ref, vb, N_HEADS_B, sl)
        parts.append((ka, va, ckv, kpe))
        cosa, sina = cosa_ref[sl, :], sina_ref[sl, :]
        cosb, sinb = cosb_ref[sl, :], sinb_ref[sl, :]
        qa_s = _rope(qa, cosa, sina, 4) * (LOG2E * HEAD_DIM ** -0.5)
        ka_r = _rope(ka, cosa, sina, 1)
        qb_s = _rope(qb, cosb, sinb, 8) * (LOG2E * (QK_NOPE + QK_ROPE) ** -0.5)
        kpe_r = _rope(kpe, cosb, sinb, 1)
        for hh in range(N_HEADS_A):
            qa_ref[hh, sl, :] = qa_s[:, hh * 64:(hh + 1) * 64].astype(BF16)
        for hh in range(N_KV_A):
            ka_ref[hh, sl, :] = ka_r[:, hh * 64:(hh + 1) * 64].astype(BF16)
        for hh in range(N_HEADS_B):
            qb_ref[hh, sl, :] = qb_s[:, hh * KB_PAD:(hh + 1) * KB_PAD].astype(BF16)
            kb_ref[hh, sl, :] = (kbn[:, hh * KB_PAD:(hh + 1) * KB_PAD] + kpe_r).astype(BF16)

    @pl.when(i < N_CTX_TILES)
    def _():
        for s, (sl, (ka, va, ckv, kpe)) in enumerate(zip(subs, parts)):
            nak_ref[s] = ka.T
            nav_ref[s] = va.T
            nckv_ref[sl, :] = ckv
            nkpe_ref[s] = kpe.T[QK_NOPE:QK_NOPE + QK_ROPE]


def _pre0(xp, xs, mod, gmix, w_in_e_t, gq, gk, gcq, gckv, wuq, wukk, wukv, cosa, sina, cosb, sinb, ones):
    tile = lambda i: (i, 0)
    head_tile = lambda i: (0, i, 0)
    vt_tile = lambda i: (0, 0, i)
    ctx_tile = lambda i: (jnp.minimum(i, N_CTX_TILES - 1), 0)
    lat_tile = lambda i: (jnp.maximum(i - N_CTX_TILES, 0), 0)
    rope_tile = lambda i: (_rope_block(i), 0)
    in_specs = [
        pl.BlockSpec((TM, D_MODEL), ctx_tile),
        pl.BlockSpec((TM, D_MODEL), lat_tile),
        _layer_spec(mod.shape[1:], 0),
        _const_spec(gmix.shape),
        _const_spec(w_in_e_t.shape),
        _const_spec(gq.shape), _const_spec(gk.shape), _layer_spec((1, Q_LORA), 0), _layer_spec((1, KV_LORA), 0),
        _const_spec(wuq.shape), _const_spec(wukk.shape), _const_spec(wukv.shape),
        pl.BlockSpec((TM, 128), rope_tile), pl.BlockSpec((TM, 128), rope_tile),
        pl.BlockSpec((TM, 128), rope_tile), pl.BlockSpec((TM, 128), rope_tile),
        _const_spec(ones.shape),
    ]
    out_shape = [
        jax.ShapeDtypeStruct((N_HEADS_A, N_TOK, 64), BF16),
        jax.ShapeDtypeStruct((N_KV_A, N_TOK, 64), BF16),
        jax.ShapeDtypeStruct((N_KV_A, VT_ROWS, N_TOK), BF16),
        jax.ShapeDtypeStruct((N_HEADS_B, N_TOK, KB_PAD), BF16),
        jax.ShapeDtypeStruct((N_HEADS_B, N_TOK, KB_PAD), BF16),
        jax.ShapeDtypeStruct((N_HEADS_B, VT_ROWS, N_TOK), BF16),
        jax.ShapeDtypeStruct((N_CTX_SEQ, 128, CTX_LEN), F32),
        jax.ShapeDtypeStruct((N_CTX_SEQ, 128, CTX_LEN), F32),
        jax.ShapeDtypeStruct((N_CTX_TOK, KV_LORA), F32),
        jax.ShapeDtypeStruct((N_CTX_SEQ, QK_ROPE, CTX_LEN), F32),
    ]
    seq_tile = lambda i: (jnp.minimum(i, N_CTX_TILES - 1), 0, 0)
    n_sub = TM // PRE_SUB
    out_specs = [
        pl.BlockSpec((N_HEADS_A, TM, 64), head_tile),
        pl.BlockSpec((N_KV_A, TM, 64), head_tile),
        pl.BlockSpec((N_KV_A, VT_ROWS, TM), vt_tile),
        pl.BlockSpec((N_HEADS_B, TM, KB_PAD), head_tile),
        pl.BlockSpec((N_HEADS_B, TM, KB_PAD), head_tile),
        pl.BlockSpec((N_HEADS_B, VT_ROWS, TM), vt_tile),
        pl.BlockSpec((n_sub, 128, CTX_LEN), seq_tile),
        pl.BlockSpec((n_sub, 128, CTX_LEN), seq_tile),
        pl.BlockSpec((TM, KV_LORA), ctx_tile),
        pl.BlockSpec((n_sub, QK_ROPE, CTX_LEN), seq_tile),
    ]
    scratch = [
        pltpu.VMEM((D_MODEL, IN_E_MAIN + KB_PAD), BF16),
        pltpu.VMEM(wuq.shape, BF16), pltpu.VMEM(wukk.shape, BF16), pltpu.VMEM(wukv.shape, BF16),
    ]
    return pl.pallas_call(
        _pre0_kernel, grid=(N_TILES,), in_specs=in_specs, out_specs=out_specs, out_shape=out_shape,
        scratch_shapes=scratch,
        compiler_params=pltpu.CompilerParams(
            dimension_semantics=("arbitrary",), vmem_limit_bytes=VMEM_LIMIT),
        name="pre0",
    )(xp, xs, mod, gmix, w_in_e_t, gq, gk, gcq, gckv, wuq, wukk, wukv, cosa, sina, cosb, sinb, ones)


def _pre1_kernel(xp_ref, xs_ref, mod_ref, gmix_ref, win_ref, cosa_ref, sina_ref,
                 q_ref, k_ref, v_ref, nk_ref, nv_ref, win_s):
    i = pl.program_id(0)

    @pl.when(i == 0)
    def _():
        win_s[...] = win_ref[...].astype(BF16)

    sh1 = _mod_part(mod_ref, _cond_row(i), 0)
    sc1 = _mod_part(mod_ref, _cond_row(i), 1)
    subs = [slice(r, r + PRE_SUB) for r in range(0, TM, PRE_SUB)]
    projs = []
    for sl in subs:
        x = jnp.where(i < N_CTX_TILES, xp_ref[sl, :], xs_ref[sl, :])
        h = _rms(x, gmix_ref[1:2, :]) * (1.0 + sc1) + sh1
        projs.append(_dot(h.astype(BF16), win_s[...]))
    for sl, proj in zip(subs, projs):
        cosa, sina = cosa_ref[sl, :], sina_ref[sl, :]
        q_s = _rope(proj[:, 0:1024], cosa, sina, 8) * (LOG2E * HEAD_DIM ** -0.5)
        k = _rope(proj[:, 1024:1152], cosa, sina, 1)
        for hh in range(N_HEADS_C):
            q_ref[hh, sl, :] = q_s[:, hh * 64:(hh + 1) * 64].astype(BF16)
        for hh in range(N_KV_C):
            k_ref[hh, sl, :] = k[:, hh * 64:(hh + 1) * 64].astype(BF16)
        _store_vt(v_ref, proj[:, 1152:1280], N_KV_C, sl)

    @pl.when(i < N_CTX_TILES)
    def _():
        for s, proj in enumerate(projs):
            nk_ref[s] = proj[:, 1024:1152].T
            nv_ref[s] = proj[:, 1152:1280].T


def _pre1(xp, xs, mod, gmix, w_in_o, cosa, sina):
    n_in = w_in_o.shape[-1]
    head_tile = lambda i: (0, i, 0)
    vt_tile = lambda i: (0, 0, i)
    ctx_tile = lambda i: (jnp.minimum(i, N_CTX_TILES - 1), 0)
    lat_tile = lambda i: (jnp.maximum(i - N_CTX_TILES, 0), 0)
    rope_tile = lambda i: (_rope_block(i), 0)
    in_specs = [
        pl.BlockSpec((TM, D_MODEL), ctx_tile),
        pl.BlockSpec((TM, D_MODEL), lat_tile),
        _layer_spec(mod.shape[1:], 1),
        _const_spec(gmix.shape),
        _layer_spec((D_MODEL, n_in), 0),
        pl.BlockSpec((TM, 128), rope_tile), pl.BlockSpec((TM, 128), rope_tile),
    ]
    out_shape = [
        jax.ShapeDtypeStruct((N_HEADS_C, N_TOK, 64), BF16),
        jax.ShapeDtypeStruct((N_KV_C, N_TOK, 64), BF16),
        jax.ShapeDtypeStruct((N_KV_C, VT_ROWS, N_TOK), BF16),
        jax.ShapeDtypeStruct((N_CTX_SEQ, 128, CTX_LEN), F32),
        jax.ShapeDtypeStruct((N_CTX_SEQ, 128, CTX_LEN), F32),
    ]
    seq_tile = lambda i: (jnp.minimum(i, N_CTX_TILES - 1), 0, 0)
    out_specs = [
        pl.BlockSpec((N_HEADS_C, TM, 64), head_tile),
        pl.BlockSpec((N_KV_C, TM, 64), head_tile),
        pl.BlockSpec((N_KV_C, VT_ROWS, TM), vt_tile),
        pl.BlockSpec((TM // PRE_SUB, 128, CTX_LEN), seq_tile),
        pl.BlockSpec((TM // PRE_SUB, 128, CTX_LEN), seq_tile),
    ]
    return pl.pallas_call(
        _pre1_kernel, grid=(N_TILES,), in_specs=in_specs, out_specs=out_specs, out_shape=out_shape,
        scratch_shapes=[pltpu.VMEM((D_MODEL, n_in), BF16)],
        compiler_params=pltpu.CompilerParams(
            dimension_semantics=("arbitrary",), vmem_limit_bytes=VMEM_LIMIT),
        name="pre1",
    )(xp, xs, mod, gmix, w_in_o, cosa, sina)


def _ctx_kernel(ak_ref, av_ref, ckv_ref, kpe_ref, ck_ref, cv_ref, wukk_ref, wukv_ref,
                ka_ref, va_ref, kb_ref, vb_ref, kc_ref, vc_ref):
    ones = jnp.ones((VT_ROWS - 64, PAST_LEN), BF16)
    for hh in range(2):
        rows = slice(hh * 64, (hh + 1) * 64)
        ka_ref[hh] = ak_ref[rows, :].T.astype(BF16)
        kc_ref[hh] = ck_ref[rows, :].T.astype(BF16)
        for src, dst in ((av_ref, va_ref), (cv_ref, vc_ref)):
            dst[hh, 0:64, :] = src[rows, :].astype(BF16)
            dst[hh, 64:VT_ROWS, :] = ones
    ckv_b = ckv_ref[...].astype(BF16)
    kbn = _dot(ckv_b, wukk_ref[...].astype(BF16))
    vb = _dot(ckv_b, wukv_ref[...].astype(BF16))
    kpe = jnp.concatenate([jnp.zeros((PAST_LEN, QK_NOPE), F32), kpe_ref[...].T,
                           jnp.zeros((PAST_LEN, KB_PAD - QK_NOPE - QK_ROPE), F32)], axis=1)
    for hh in range(N_HEADS_B):
        kb_ref[hh] = (kbn[:, hh * KB_PAD:(hh + 1) * KB_PAD] + kpe).astype(BF16)
    _store_vt(vb_ref, vb, N_HEADS_B)


def _ctx_prep(ak, av, ckv, kpe, ck, cv, wukk, wukv):
    n = N_LAT_SEQ * PAST_LEN
    row = lambda b: (b, 0)
    head_row = lambda b: (0, b, 0)
    vt_row = lambda b: (0, 0, b)
    tok_minor = lambda rows: pl.BlockSpec((None, rows, PAST_LEN), lambda b: (b, 0, 0))
    in_specs = [
        tok_minor(128), tok_minor(128),
        pl.BlockSpec((PAST_LEN, KV_LORA), row), tok_minor(QK_ROPE),
        tok_minor(128), tok_minor(128),
        _const_spec(wukk.shape), _const_spec(wukv.shape),
    ]
    out_shape = [
        jax.ShapeDtypeStruct((2, n, 64), BF16), jax.ShapeDtypeStruct((2, VT_ROWS, n), BF16),
        jax.ShapeDtypeStruct((N_HEADS_B, n, KB_PAD), BF16), jax.ShapeDtypeStruct((N_HEADS_B, VT_ROWS, n), BF16),
        jax.ShapeDtypeStruct((2, n, 64), BF16), jax.ShapeDtypeStruct((2, VT_ROWS, n), BF16),
    ]
    out_specs = [
        pl.BlockSpec((2, PAST_LEN, 64), head_row), pl.BlockSpec((2, VT_ROWS, PAST_LEN), vt_row),
        pl.BlockSpec((N_HEADS_B, PAST_LEN, KB_PAD), head_row), pl.BlockSpec((N_HEADS_B, VT_ROWS, PAST_LEN), vt_row),
        pl.BlockSpec((2, PAST_LEN, 64), head_row), pl.BlockSpec((2, VT_ROWS, PAST_LEN), vt_row),
    ]
    return pl.pallas_call(
        _ctx_kernel, grid=(N_LAT_SEQ,), in_specs=in_specs, out_specs=out_specs, out_shape=out_shape,
        compiler_params=pltpu.CompilerParams(
            dimension_semantics=("arbitrary",), vmem_limit_bytes=VMEM_LIMIT),
        name="ctx_prep",
    )(ak, av, ckv, kpe, ck, cv, wukk, wukv)


def _softmax_units(units, lookahead):
    tasks = [(u, c) for u, unit in enumerate(units) for c in range(len(unit["chunks"]))]
    scores = {}
    qts = [unit["q"].astype(F32).T.astype(BF16) if len(unit["chunks"]) >= 4 else None for unit in units]

    def emit_scores(t):
        u, c = tasks[t]
        k, _, mask = units[u]["chunks"][c]
        s = _dot_nt(k, units[u]["q"]) if qts[u] is None else _dot(k, qts[u])
        scores[t] = s if mask is None else jnp.where(mask, s, NEG_INF)

    for t in range(min(lookahead, len(tasks))):
        emit_scores(t)
    for t, (u, c) in enumerate(tasks):
        if t + lookahead < len(tasks):
            emit_scores(t + lookahead)
        unit = units[u]
        s = scores.pop(t)
        m, acc = unit["m"], unit["acc"]
        cmax = jnp.max(s, axis=0, keepdims=True)
        m_new = cmax if m is None else jnp.maximum(m, cmax)
        pv = _dot(unit["chunks"][c][1], jnp.exp2(s - m_new).astype(BF16))
        unit["acc"] = pv if acc is None else acc * jnp.exp2(m - m_new) + pv
        unit["m"] = m_new
    return [unit["acc"][0:64] * (1.0 / unit["acc"][64:65]) for unit in units]


def _attn_kernel(*refs, n_kv, group, tq, seq_len, seqs, has_ctx, has_sink, window, q_unit, key_chunk):
    refs = list(refs)
    q_ref = refs.pop(0)
    if window:
        n_blk = seqs + 2
        kb_refs, vb_refs = refs[:n_blk], refs[n_blk:2 * n_blk]
        refs = refs[2 * n_blk:]
    else:
        k_ref, vt_ref = refs[:2]
        refs = refs[2:]
    if has_ctx:
        kx_ref, vx_ref = refs[:2]
        refs = refs[2:]
    if has_sink:
        sink_ref = refs.pop(0)
    o_ref = refs.pop(0)

    j = pl.program_id(1)
    dk = q_ref.shape[-1]
    heads_per_unit = q_unit // tq
    lane = lax.broadcasted_iota(jnp.int32, (1, q_unit), 1)
    if window:
        n_band = tq + 2 * window
        krow = lax.broadcasted_iota(jnp.int32, (n_band, q_unit), 0)
        qcol = lax.broadcasted_iota(jnp.int32, (n_band, q_unit), 1) & (tq - 1)
        rel = (krow - window) - qcol
        in_band = jnp.abs(rel) <= window
        last_j = seq_len // (tq * seqs) - 1
        band_masks = []
        for sq in range(seqs):
            mask = in_band
            if sq == 0:
                mask = mask & ((krow >= window) | (j > 0))
            if sq == seqs - 1:
                mask = mask & ((krow < window + tq) | (j < last_j))
            band_masks.append(mask)
    if has_sink:
        acc0 = jnp.where(lax.broadcasted_iota(jnp.int32, (VT_ROWS, q_unit), 0) >= 64, 1.0, 0.0)

    units = []
    for sq in range(seqs):
        base = sq * (tq if window else seq_len)
        for hk in range(n_kv):
            chunks = []
            if window:
                chunks.append((jnp.concatenate([r[hk] for r in kb_refs[sq:sq + 3]], axis=0),
                               jnp.concatenate([r[hk] for r in vb_refs[sq:sq + 3]], axis=1), band_masks[sq]))
            else:
                for c in range(base, base + seq_len, key_chunk):
                    n = min(key_chunk, base + seq_len - c)
                    chunks.append((k_ref[hk, c:c + n, :], vt_ref[hk, :, c:c + n], None))
            if has_ctx:
                for c in range(0, PAST_LEN, key_chunk):
                    n = min(key_chunk, PAST_LEN - c)
                    chunks.append((kx_ref[hk, c:c + n, :], vx_ref[hk, :, c:c + n], None))
            for u in range(group // heads_per_unit):
                h0 = hk * group + u * heads_per_unit
                q = q_ref[h0:h0 + heads_per_unit, base:base + tq, :].reshape(q_unit, dk)
                unit = dict(q=q, chunks=chunks, m=None, acc=None)
                if has_sink:
                    m0 = jnp.full((1, q_unit), sink_ref[h0] * LOG2E, F32)
                    for e in range(1, heads_per_unit):
                        m0 = jnp.where(lane >= e * tq, sink_ref[h0 + e] * LOG2E, m0)
                    unit.update(m=m0, acc=acc0)
                units.append(unit)
    results = _softmax_units(units, SCORE_LOOKAHEAD)
    per_seq = len(results) // seqs
    for sq in range(seqs):
        outs = []
        for o in results[sq * per_seq:(sq + 1) * per_seq]:
            for e in range(heads_per_unit):
                outs.append(o[:, e * tq:(e + 1) * tq])
        base = sq * (tq if window else seq_len)
        o_ref[base:base + tq, :] = jnp.concatenate(outs, axis=0).T.astype(BF16)


def _attention(q, k, vt, kx, vx, sink, *, n_seq, seq_len, tok_base, tq, window, q_unit, key_chunk, name,
               seqs=1):
    n_q, _, dk = q.shape
    n_kv = k.shape[0]
    group = n_q // n_kv
    has_ctx = kx is not None
    has_sink = sink is not None
    if window:
        n_qt = seq_len // (tq * seqs)
    else:
        n_qt = seq_len // tq
        assert seqs == 1 or (n_qt == 1 and not has_ctx)
    q_base = tok_base // (seqs * tq)
    s_base = tok_base // (seqs * seq_len)
    q_blk = lambda b, j: q_base + b * n_qt + j
    in_specs = [pl.BlockSpec((n_q, seqs * tq, dk), lambda b, j: (0, q_blk(b, j), 0))]
    args = [q]
    if window:
        assert window == tq
        n_kb = seq_len // tq
        kb_base = tok_base // tq

        def key_blk(b, j, off):
            return kb_base + b * n_kb + jnp.clip(j * seqs - 1 + off, 0, n_kb - 1)

        for off in range(seqs + 2):
            in_specs.append(pl.BlockSpec((n_kv, tq, dk), lambda b, j, off=off: (0, key_blk(b, j, off), 0)))
        for off in range(seqs + 2):
            in_specs.append(pl.BlockSpec((n_kv, VT_ROWS, tq), lambda b, j, off=off: (0, 0, key_blk(b, j, off))))
        args += [k] * (seqs + 2) + [vt] * (seqs + 2)
    else:
        in_specs += [
            pl.BlockSpec((n_kv, seqs * seq_len, dk), lambda b, j: (0, s_base + b, 0)),
            pl.BlockSpec((n_kv, VT_ROWS, seqs * seq_len), lambda b, j: (0, 0, s_base + b)),
        ]
        args += [k, vt]
    if has_ctx:
        in_specs += [
            pl.BlockSpec((n_kv, PAST_LEN, dk), lambda b, j: (0, b, 0)),
            pl.BlockSpec((n_kv, VT_ROWS, PAST_LEN), lambda b, j: (0, 0, b)),
        ]
        args += [kx, vx]
    if has_sink:
        in_specs.append(pl.BlockSpec(memory_space=pltpu.SMEM))
        args.append(sink)
    kern = functools.partial(_attn_kernel, n_kv=n_kv, group=group, tq=tq, seq_len=seq_len, seqs=seqs,
                             has_ctx=has_ctx, has_sink=has_sink, window=window,
                             q_unit=q_unit, key_chunk=key_chunk)
    return pl.pallas_call(
        kern, grid=(n_seq if window else n_seq // seqs, n_qt), in_specs=in_specs,
        out_specs=pl.BlockSpec((seqs * tq, n_q * HEAD_DIM), lambda b, j: (b * n_qt + j, 0)),
        out_shape=jax.ShapeDtypeStruct((n_seq * seq_len, n_q * HEAD_DIM), BF16),
        compiler_params=pltpu.CompilerParams(
            dimension_semantics=("arbitrary", "arbitrary"), vmem_limit_bytes=VMEM_LIMIT),
        name=name,
    )(*args)


def _ffn_kernel(*refs, n_o, layer, is_ctx, final):
    halo = 0 if is_ctx else HALO
    it = iter(refs)
    x_ref = next(it)
    xh_refs = None if is_ctx else (next(it), next(it))
    o_refs, oh_refs = [], []
    for _ in range(n_o):
        o_refs.append(next(it))
        if not is_ctx:
            oh_refs.append((next(it), next(it)))
    wo_refs = [next(it) for _ in range(n_o)]
    mod_ref, gffn_ref = next(it), next(it)
    wg_refs = [next(it) for _ in range(FF_PER_STEP)]
    wv_refs = [next(it) for _ in range(FF_PER_STEP)]
    cw_ref, cb_ref = next(it), next(it)
    wd_refs = [next(it) for _ in range(FF_PER_STEP)]
    gfin_ref, out_ref, h2e_ref, acc_ref = next(it), next(it), next(it), next(it)

    m = pl.program_id(0)
    c = pl.program_id(1)
    row = 0 if is_ctx else 1 + m // FFN_LAT_TILES
    g1, sh2, sc2, g2 = (_mod_part(mod_ref, row, k) for k in (2, 3, 4, 5))
    sub = FFN_SUB if is_ctx else FFN_LAT_SUB
    n_sub = FFN_TM // sub
    sub_rows = sub + 2 * halo

    @pl.when(c == 0)
    def _():
        wos = [w[...].astype(BF16) for w in wo_refs]

        def residual_and_norm(xv, ovs):
            attn = _dot(ovs[0], wos[0])
            for ov, wo in zip(ovs[1:], wos[1:]):
                attn = attn + _dot(ov, wo)
            x1 = xv + g1 * attn
            return x1, (_rms(x1, gffn_ref[layer:layer + 1, :]) * (1.0 + sc2) + sh2).astype(BF16)

        for r in range(0, FFN_TM, FFN_SUB):
            x1, h2 = residual_and_norm(x_ref[r:r + FFN_SUB, :], [o[r:r + FFN_SUB, :] for o in o_refs])
            acc_ref[r:r + FFN_SUB, :] = x1
            h2e_ref[halo + r:halo + r + FFN_SUB, :] = h2
        if not is_ctx:
            _, h2h = residual_and_norm(
                jnp.concatenate([xh_refs[0][...], xh_refs[1][...]], axis=0),
                [jnp.concatenate([oh[0][...], oh[1][...]], axis=0) for oh in oh_refs])
            h2e_ref[0:HALO, :] = h2h[0:HALO]
            h2e_ref[HALO + FFN_TM:, :] = h2h[HALO:]

    row8 = lax.broadcasted_iota(jnp.int32, (8, FF_CHUNK), 0)
    if not is_ctx:
        has_prev = m % FFN_LAT_TILES != 0
        has_next = m % FFN_LAT_TILES != FFN_LAT_TILES - 1

    def ff_chunks(n_chunks):
        w_up, w_dn, cw, cb = [], [], [], []
        for j in range(n_chunks):
            w_up.append(jnp.concatenate([wg_refs[j][...].astype(BF16), wv_refs[j][...].astype(BF16)],
                                        axis=1))
            w_dn.append(wd_refs[j][...].astype(BF16))
            cols = pl.ds(pl.multiple_of((c * FF_PER_STEP + j) * FF_CHUNK, FF_CHUNK), FF_CHUNK)
            cw.append([cw_ref[t, layer:layer + 1, cols] for t in range(3)])
            cb.append(cb_ref[layer:layer + 1, cols])
        tasks = [(j, r) for j in range(n_chunks) for r in range(n_sub)]
        ups = {}

        def emit_up(t):
            j, r = tasks[t]
            ups[t] = _dot(h2e_ref[r * sub:r * sub + sub_rows, :], w_up[j])

        for t in range(min(UP_LOOKAHEAD, len(tasks))):
            emit_up(t)
        for t, (j, r) in enumerate(tasks):
            if t + UP_LOOKAHEAD < len(tasks):
                emit_up(t + UP_LOOKAHEAD)
            up = ups.pop(t)
            ge = up[:, :FF_CHUNK]
            val = up[halo:halo + sub, FF_CHUNK:]
            g_prev = pltpu.roll(ge, 1, axis=0)[halo:halo + sub]
            g_next = pltpu.roll(ge, sub_rows - 1, axis=0)[halo:halo + sub]
            prev_ok = False if is_ctx else (has_prev if r == 0 else True)
            next_ok = False if is_ctx else (has_next if r == n_sub - 1 else True)
            if prev_ok is not True:
                g_prev = jnp.concatenate(
                    [jnp.where(jnp.logical_or(row8 != 0, prev_ok), g_prev[0:8], 0.0), g_prev[8:]], axis=0)
            if next_ok is not True:
                g_next = jnp.concatenate(
                    [g_next[:-8], jnp.where(jnp.logical_or(row8 != 7, next_ok), g_next[-8:], 0.0)], axis=0)
            gate = g_prev * cw[j][0] + ge[halo:halo + sub] * cw[j][1] + g_next * cw[j][2] + cb[j]
            act = (gate * jax.nn.sigmoid(gate) * val).astype(BF16)
            acc_ref[r * sub:(r + 1) * sub, :] += g2 * _dot(act, w_dn[j])

    n_tail = N_FF_CHUNKS % FF_PER_STEP
    if n_tail:
        pl.when(c < N_FF_STEPS - 1)(functools.partial(ff_chunks, FF_PER_STEP))
    else:
        ff_chunks(FF_PER_STEP)

    @pl.when(c == N_FF_STEPS - 1)
    def _():
        if n_tail:
            ff_chunks(n_tail)
        x2 = acc_ref[...]
        out_ref[...] = _rms(x2, gfin_ref[...]) if final else x2


def _ffn(x, os, mod, layer, g_ffn, w_out, w_up, conv_w, conv_b, w_down, g_final, *, is_ctx, final):
    n_rows = x.shape[0]
    nh = FFN_TM // HALO
    nblk = n_rows // HALO
    n_o = len(os)
    halo = 0 if is_ctx else HALO
    tile = lambda m, c: (m, 0)
    prev = lambda m, c: (jnp.maximum(m * nh - 1, 0), 0)
    nxt = lambda m, c: (jnp.minimum((m + 1) * nh, nblk - 1), 0)
    chunk = lambda c, j: jnp.minimum(c * FF_PER_STEP + j, N_FF_CHUNKS - 1)

    def with_halo(arr):
        w = arr.shape[1]
        specs = [pl.BlockSpec((FFN_TM, w), tile)]
        if not is_ctx:
            specs += [pl.BlockSpec((HALO, w), prev), pl.BlockSpec((HALO, w), nxt)]
        return specs, [arr] * len(specs)

    in_specs, args = with_halo(x)
    for o in os:
        specs, arrs = with_halo(o)
        in_specs += specs
        args += arrs
    w_rows = D_MODEL // n_o
    for t in range(n_o):
        in_specs.append(pl.BlockSpec((None, w_rows, D_MODEL), lambda m, c, t=t: (0, t, 0)))
        args.append(w_out)
    in_specs += [
        _layer_spec(mod.shape[1:], layer),
        _const_spec(g_ffn.shape),
    ]
    args += [mod, g_ffn]
    steps = range(FF_PER_STEP)
    in_specs += [pl.BlockSpec((None, D_MODEL, FF_CHUNK), lambda m, c, j=j: (layer, 0, chunk(c, j))) for j in steps]
    in_specs += [pl.BlockSpec((None, D_MODEL, FF_CHUNK), lambda m, c, j=j: (layer, 0, N_FF_CHUNKS + chunk(c, j)))
                 for j in steps]
    in_specs += [_const_spec(conv_w.shape), _const_spec(conv_b.shape)]
    in_specs += [pl.BlockSpec((None, FF_CHUNK, D_MODEL), lambda m, c, j=j: (layer, chunk(c, j), 0)) for j in steps]
    args += [w_up] * (2 * FF_PER_STEP) + [conv_w, conv_b] + [w_down] * FF_PER_STEP
    in_specs.append(pl.BlockSpec((1, D_MODEL), lambda m, c: (0, 0)))
    args.append(g_final)
    return pl.pallas_call(
        functools.partial(_ffn_kernel, n_o=n_o, layer=layer, is_ctx=is_ctx, final=final),
        grid=(n_rows // FFN_TM, N_FF_STEPS), in_specs=in_specs,
        out_specs=pl.BlockSpec((FFN_TM, D_MODEL), tile),
        out_shape=jax.ShapeDtypeStruct((n_rows, D_MODEL), F32),
        scratch_shapes=[pltpu.VMEM((FFN_TM + 2 * halo, D_MODEL), BF16), pltpu.VMEM((FFN_TM, D_MODEL), F32)],
        compiler_params=pltpu.CompilerParams(
            dimension_semantics=("arbitrary", "arbitrary"), vmem_limit_bytes=FFN_VMEM_LIMIT),
        name=("ffn_ctx" if is_ctx else "ffn_lat") + ("_final" if final else ""),
    )(*args)


def _rope_tables(rot_dim):
    f32 = np.float32
    t = np.arange(LAT_LEN)
    row = (t // GRID_W).astype(f32)
    col = (t % GRID_W).astype(f32)
    d_axis = rot_dim // 2
    freqs = (f32(ROPE_THETA) ** (-np.arange(0, d_axis, 2, dtype=f32) / f32(d_axis))).astype(f32)
    ang = np.concatenate([row[:, None] * freqs, col[:, None] * freqs], axis=-1)
    cos = np.repeat(np.cos(ang), 2, axis=-1).astype(f32)
    sin = (np.repeat(np.sin(ang), 2, axis=-1) * np.tile(np.array([-1.0, 1.0], f32), rot_dim // 2)).astype(f32)
    if rot_dim == HEAD_DIM:
        cos = np.tile(cos, (1, 2))
        sin = np.tile(sin, (1, 2))
    else:
        cos = np.concatenate([np.ones((LAT_LEN, QK_NOPE), f32), cos,
                              np.ones((LAT_LEN, 128 - QK_NOPE - rot_dim), f32)], axis=-1)
        sin = np.concatenate([np.zeros((LAT_LEN, QK_NOPE), f32), sin,
                              np.zeros((LAT_LEN, 128 - QK_NOPE - rot_dim), f32)], axis=-1)
    cos = np.concatenate([cos, np.ones((TM, 128), f32)], axis=0)
    sin = np.concatenate([sin, np.zeros((TM, 128), f32)], axis=0)
    return jnp.asarray(cos), jnp.asarray(sin)


def _heads_last(t, n_heads):
    return t.reshape(N_CTX_SEQ, 1, n_heads, HEAD_DIM, CTX_LEN).transpose(0, 1, 4, 2, 3)


def kernel(x_prompt, x_sample, cache_a_k, cache_a_v, cache_b_ckv, cache_b_kpe, cache_c_k, cache_c_v, c, c_ctx, w_mod, b_mod, g_mix_norm, g_ffn_norm, w_in_e, g_qnorm_a, g_knorm_a, g_cq_b, w_uq_b, g_ckv_b, w_ukv_b, w_out_e, w_in_o, sink_c, w_out_o, w_up, conv_w, conv_b, w_down, g_final):
    depth = w_mod.shape[0]
    lane_pad = KB_PAD - QK_NOPE - QK_ROPE
    wuq = jnp.pad(w_uq_b[0].reshape(Q_LORA, N_HEADS_B, QK_NOPE + QK_ROPE),
                  ((0, 0), (0, 0), (0, lane_pad))).reshape(Q_LORA, N_HEADS_B * KB_PAD)
    wukv3 = w_ukv_b[0].reshape(KV_LORA, N_HEADS_B, QK_NOPE + V_DIM_B)
    wukk = jnp.pad(wukv3[:, :, :QK_NOPE], ((0, 0), (0, 0), (0, KB_PAD - QK_NOPE))
                   ).reshape(KV_LORA, N_HEADS_B * KB_PAD)
    wukv = wukv3[:, :, QK_NOPE:].reshape(KV_LORA, N_HEADS_B * V_DIM_B)
    gq, gk = g_qnorm_a, g_knorm_a
    seg = np.arange(256) // HEAD_DIM
    ones = jnp.asarray(seg[:, None] == seg[None, :], dtype=BF16)
    cosa, sina = _rope_tables(HEAD_DIM)
    cosb, sinb = _rope_tables(QK_ROPE)
    g_fin2 = g_final[None, :]

    mod = _modulation(c_ctx, c, w_mod, b_mod)

    n_past = N_LAT_SEQ * PAST_LEN
    tok_minor = lambda t: jnp.moveaxis(t[:, 0], 1, -1).reshape(N_LAT_SEQ, -1, PAST_LEN)
    ka_c, va_c, kb_c, vb_c, kc_c, vc_c = _ctx_prep(
        tok_minor(cache_a_k), tok_minor(cache_a_v), cache_b_ckv.reshape(n_past, KV_LORA),
        tok_minor(cache_b_kpe), tok_minor(cache_c_k), tok_minor(cache_c_v), wukk, wukv)

    xp = x_prompt.reshape(N_CTX_TOK, D_MODEL)
    xs = x_sample.reshape(N_TOK - N_CTX_TOK, D_MODEL)
    qa, ka, va, qb, kb, vb, nak, nav, nckv, nkpe = _pre0(
        xp, xs, mod, g_mix_norm, w_in_e[0].T, gq, gk, g_cq_b.reshape(1, 1, Q_LORA), g_ckv_b.reshape(1, 1, KV_LORA),
        wuq, wukk, wukv, cosa, sina, cosb, sinb, ones)
    ffn_w = (w_up, conv_w.transpose(1, 0, 2), conv_b, w_down, g_fin2)
    ctx_kw = dict(n_seq=N_CTX_SEQ, seq_len=CTX_LEN, tok_base=0, tq=CTX_LEN, window=0, q_unit=256,
                  key_chunk=CTX_LEN, seqs=8)
    lat_kw = dict(n_seq=N_LAT_SEQ, seq_len=LAT_LEN, tok_base=N_CTX_TOK, q_unit=256)
    dense_kw = dict(tq=256, window=0, key_chunk=256, **lat_kw)
    oa = (_attention(qa, ka, va, None, None, None, name="attn_a_ctx", **ctx_kw),
          _attention(qa, ka, va, ka_c, va_c, None, name="attn_a_lat", **dense_kw))
    ob = (_attention(qb, kb, vb, None, None, None, name="attn_b_ctx", **ctx_kw),
          _attention(qb, kb, vb, kb_c, vb_c, None, name="attn_b_lat", **dense_kw))
    xp1 = _ffn(xp, [oa[0], ob[0]], mod, 0, g_ffn_norm, w_out_e, *ffn_w, is_ctx=True, final=False)
    xs1 = _ffn(xs, [oa[1], ob[1]], mod, 0, g_ffn_norm, w_out_e, *ffn_w, is_ctx=False, final=False)

    qc, kc, vc, nck, ncv = _pre1(xp1, xs1, mod, g_mix_norm, w_in_o, cosa, sina)
    sink = sink_c[0]
    oc = (_attention(qc, kc, vc, None, None, sink, name="attn_c_ctx", **ctx_kw),
          _attention(qc, kc, vc, kc_c, vc_c, sink, tq=128, window=WINDOW, key_chunk=PAST_LEN, seqs=4,
                     name="attn_c_lat", **lat_kw))
    y_prompt = _ffn(xp1, [oc[0]], mod, 1, g_ffn_norm, w_out_o, *ffn_w, is_ctx=True, final=True)
    y_sample = _ffn(xs1, [oc[1]], mod, 1, g_ffn_norm, w_out_o, *ffn_w, is_ctx=False, final=True)

    return (y_prompt.reshape(N_CTX_SEQ, CTX_LEN, D_MODEL), y_sample.reshape(N_LAT_SEQ, LAT_LEN, D_MODEL),
            _heads_last(nak, N_KV_A), _heads_last(nav, N_KV_A),
            nckv.reshape(N_CTX_SEQ, 1, CTX_LEN, KV_LORA),
            nkpe.reshape(N_CTX_SEQ, 1, QK_ROPE, CTX_LEN).transpose(0, 1, 3, 2),
            _heads_last(nck, N_KV_C), _heads_last(ncv, N_KV_C))
```

```python
import functools

import jax
import jax.numpy as jnp
import numpy as np
from jax import lax
from jax.experimental import pallas as pl
from jax.experimental.pallas import tpu as pltpu

F32 = jnp.float32
BF16 = jnp.bfloat16

D_MODEL = 1024
N_CTX_SEQ = 16
CTX_LEN = 256
N_LAT_SEQ = 2
LAT_LEN = 2048
PAST_LEN = 512
GRID_W = 64
ROPE_THETA = 10000.0
NORM_EPS = 1e-6
WINDOW = 128
NEG_INF = -1e30
LOG2E = 1.4426950408889634
HEAD_DIM = 64
N_HEADS_A, N_KV_A = 8, 2
N_HEADS_B = 8
Q_LORA, KV_LORA = 384, 256
QK_NOPE, QK_ROPE, V_DIM_B = 64, 32, 64
N_HEADS_C, N_KV_C = 16, 2
D_FF = 2816
IN_E_MAIN = N_HEADS_A * HEAD_DIM + 2 * N_KV_A * HEAD_DIM + Q_LORA + KV_LORA

N_CTX_TOK = N_CTX_SEQ * CTX_LEN
N_TOK = N_CTX_TOK + N_LAT_SEQ * LAT_LEN
TM = 512
PRE_SUB = 256
N_TILES = N_TOK // TM
N_CTX_TILES = N_CTX_TOK // TM
LAT_TILES = LAT_LEN // TM
HALO = 16
FFN_TM = 1024
FFN_SUB = CTX_LEN
FFN_LAT_SUB = 256
UP_LOOKAHEAD = 3
FFN_LAT_TILES = LAT_LEN // FFN_TM
FF_CHUNK = 256
N_FF_CHUNKS = D_FF // FF_CHUNK
FF_PER_STEP = 4
N_FF_STEPS = -(-N_FF_CHUNKS // FF_PER_STEP)
KB_PAD = 128
VT_ROWS = 80
SCORE_LOOKAHEAD = 5
VMEM_LIMIT = 56 * 1024 * 1024
FFN_VMEM_LIMIT = 60 * 1024 * 1024


def _dot(a, b):
    return jnp.dot(a, b, preferred_element_type=F32)


def _dot_nt(a, b):
    return lax.dot_general(a, b, (((1,), (1,)), ((), ())), preferred_element_type=F32)


def _rms(x, g):
    return x * lax.rsqrt(jnp.mean(x * x, axis=-1, keepdims=True) + NORM_EPS) * g


def _split_bf16(x):
    hi = x.astype(BF16)
    return hi, (x - hi.astype(F32)).astype(BF16)


def _head_rms(x, g, ones_ref):
    w = x.shape[1]
    hi, lo = _split_bf16(x * x)
    parts = []
    for c in range(0, w, 256):
        cw = min(256, w - c)
        ones = ones_ref[0:cw, 0:cw]
        parts.append(_dot(hi[:, c:c + cw], ones) + _dot(lo[:, c:c + cw], ones))
    ssum = parts[0] if len(parts) == 1 else jnp.concatenate(parts, axis=1)
    return x * lax.rsqrt(ssum * (1.0 / HEAD_DIM) + NORM_EPS) * g


def _swap_pairs(x):
    w = x.shape[1]
    up = pltpu.roll(x, w - 1, axis=1)
    dn = pltpu.roll(x, 1, axis=1)
    lane = lax.broadcasted_iota(jnp.int32, x.shape, 1)
    return jnp.where((lane & 1) == 0, up, dn)


def _rope(x, cos, sin_signed, reps):
    if reps > 1:
        cos = jnp.concatenate([cos] * reps, axis=1)
        sin_signed = jnp.concatenate([sin_signed] * reps, axis=1)
    return x * cos + _swap_pairs(x) * sin_signed


def _store_vt(vt_ref, v, n_heads, cols=slice(None)):
    t = v.shape[0]
    vt = v.T.astype(BF16)
    ones = jnp.ones((VT_ROWS - 64, t), BF16)
    for hh in range(n_heads):
        vt_ref[hh, 0:64, cols] = vt[hh * 64:(hh + 1) * 64]
        vt_ref[hh, 64:VT_ROWS, cols] = ones


def _cond_row(i):
    return jnp.where(i < N_CTX_TILES, 0, 1 + (i - N_CTX_TILES) // LAT_TILES)


def _mod_part(mod_ref, row, k):
    return mod_ref[pl.ds(row, 1), k * D_MODEL:(k + 1) * D_MODEL]


def _rope_block(i):
    return jnp.where(i < N_CTX_TILES, LAT_TILES, (i - N_CTX_TILES) % LAT_TILES)


def _const_spec(shape):
    zeros = (0,) * len(shape)
    return pl.BlockSpec(shape, lambda *_: zeros)


def _layer_spec(shape, layer):
    idx = (layer,) + (0,) * len(shape)
    return pl.BlockSpec((None,) + tuple(shape), lambda *_: idx)


def _mod_kernel(cctx_ref, c_ref, w_ref, b_ref, o_ref):
    c = jnp.concatenate([cctx_ref[...], c_ref[...], jnp.zeros((8 - 1 - N_LAT_SEQ, D_MODEL), F32)], axis=0)
    s_hi, s_lo = _split_bf16(c * jax.nn.sigmoid(c))
    w_hi, w_lo = _split_bf16(w_ref[0])
    r = _dot(jnp.concatenate([s_hi, s_lo], axis=0), w_hi)
    o_ref[0] = r[0:8] + r[8:16] + _dot(s_hi, w_lo) + b_ref[pl.ds(pl.program_id(0), 1), :]


def _modulation(c_ctx, c, w_mod, b_mod):
    depth, _, n = w_mod.shape
    tn = 3072
    return pl.pallas_call(
        _mod_kernel,
        grid=(depth, n // tn),
        in_specs=[
            pl.BlockSpec((1, D_MODEL), lambda l, j: (0, 0)),
            pl.BlockSpec((N_LAT_SEQ, D_MODEL), lambda l, j: (0, 0)),
            pl.BlockSpec((1, D_MODEL, tn), lambda l, j: (l, 0, j)),
            pl.BlockSpec((depth, tn), lambda l, j: (0, j)),
        ],
        out_specs=pl.BlockSpec((1, 8, tn), lambda l, j: (l, 0, j)),
        out_shape=jax.ShapeDtypeStruct((depth, 8, n), F32),
        compiler_params=pltpu.CompilerParams(
            dimension_semantics=("arbitrary", "arbitrary"), vmem_limit_bytes=VMEM_LIMIT),
        name="modulation",
    )(c_ctx[None, :], c, w_mod, b_mod)


def _pre0_kernel(xp_ref, xs_ref, mod_ref, gmix_ref, win_ref, gq_ref, gk_ref, gcq_ref, gckv_ref,
                 wuq_ref, wukk_ref, wukv_ref, cosa_ref, sina_ref, cosb_ref, sinb_ref, ones_ref,
                 qa_ref, ka_ref, va_ref, qb_ref, kb_ref, vb_ref,
                 nak_ref, nav_ref, nckv_ref, nkpe_ref,
                 win_s, wuq_s, wukk_s, wukv_s):
    i = pl.program_id(0)

    @pl.when(i == 0)
    def _():
        win_s[:, 0:IN_E_MAIN] = win_ref[0:IN_E_MAIN, :].T.astype(BF16)
        win_s[:, IN_E_MAIN:] = jnp.concatenate(
            [jnp.zeros((D_MODEL, QK_NOPE), F32), win_ref[IN_E_MAIN:, :].T,
             jnp.zeros((D_MODEL, KB_PAD - QK_NOPE - QK_ROPE), F32)], axis=1).astype(BF16)
        wuq_s[...] = wuq_ref[...].astype(BF16)
        wukk_s[...] = wukk_ref[...].astype(BF16)
        wukv_s[...] = wukv_ref[...].astype(BF16)

    sh1 = _mod_part(mod_ref, _cond_row(i), 0)
    sc1 = _mod_part(mod_ref, _cond_row(i), 1)
    subs = [slice(r, r + PRE_SUB) for r in range(0, TM, PRE_SUB)]

    projs = []
    for sl in subs:
        x = jnp.where(i < N_CTX_TILES, xp_ref[sl, :], xs_ref[sl, :])
        h = _rms(x, gmix_ref[0:1, :]) * (1.0 + sc1) + sh1
        projs.append(_dot(h.astype(BF16), win_s[...]))

    parts = []
    for sl, proj in zip(subs, projs):
        qa = _head_rms(proj[:, 0:512], jnp.concatenate([gq_ref[...]] * N_HEADS_A, axis=1), ones_ref)
        ka = _head_rms(proj[:, 512:640], jnp.concatenate([gk_ref[...]] * N_KV_A, axis=1), ones_ref)
        va = proj[:, 640:768]
        cq = _rms(proj[:, 768:1152], gcq_ref[...])
        ckv = _rms(proj[:, 1152:1408], gckv_ref[...])
        kpe = proj[:, 1408:1536]
        qb = _dot(cq.astype(BF16), wuq_s[...])
        ckv_b = ckv.astype(BF16)
        kbn = _dot(ckv_b, wukk_s[...])
        vb = _dot(ckv_b, wukv_s[...])
        _store_vt(va_ref, va, N_KV_A, sl)
        _store_vt(vb_ref, vb, N_HEADS_B, sl)
        parts.append((ka, va, ckv, kpe))
        cosa, sina = cosa_ref[sl, :], sina_ref[sl, :]
        cosb, sinb = cosb_ref[sl, :], sinb_ref[sl, :]
        qa_s = _rope(qa, cosa, sina, 4) * (LOG2E * HEAD_DIM ** -0.5)
        ka_r = _rope(ka, cosa, sina, 1)
        qb_s = _rope(qb, cosb, sinb, 8) * (LOG2E * (QK_NOPE + QK_ROPE) ** -0.5)
        kpe_r = _rope(kpe, cosb, sinb, 1)
        for hh in range(N_HEADS_A):
            qa_ref[hh, sl, :] = qa_s[:, hh * 64:(hh + 1) * 64].astype(BF16)
        for hh in range(N_KV_A):
            ka_ref[hh, sl, :] = ka_r[:, hh * 64:(hh + 1) * 64].astype(BF16)
        for hh in range(N_HEADS_B):
            qb_ref[hh, sl, :] = qb_s[:, hh * KB_PAD:(hh + 1) * KB_PAD].astype(BF16)
            kb_ref[hh, sl, :] = (kbn[:, hh * KB_PAD:(hh + 1) * KB_PAD] + kpe_r).astype(BF16)

    @pl.when(i < N_CTX_TILES)
    def _():
        for s, (sl, (ka, va, ckv, kpe)) in enumerate(zip(subs, parts)):
            nak_ref[s] = ka.T
            nav_ref[s] = va.T
            nckv_ref[sl, :] = ckv
            nkpe_ref[s] = kpe.T[QK_NOPE:QK_NOPE + QK_ROPE]


def _pre0(xp, xs, mod, gmix, w_in_e_t, gq, gk, gcq, gckv, wuq, wukk, wukv, cosa, sina, cosb, sinb, ones):
    tile = lambda i: (i, 0)
    head_tile = lambda i: (0, i, 0)
    vt_tile = lambda i: (0, 0, i)
    ctx_tile = lambda i: (jnp.minimum(i, N_CTX_TILES - 1), 0)
    lat_tile = lambda i: (jnp.maximum(i - N_CTX_TILES, 0), 0)
    rope_tile = lambda i: (_rope_block(i), 0)
    in_specs = [
        pl.BlockSpec((TM, D_MODEL), ctx_tile),
        pl.BlockSpec((TM, D_MODEL), lat_tile),
        _layer_spec(mod.shape[1:], 0),
        _const_spec(gmix.shape),
        _const_spec(w_in_e_t.shape),
        _const_spec(gq.shape), _const_spec(gk.shape), _layer_spec((1, Q_LORA), 0), _layer_spec((1, KV_LORA), 0),
        _const_spec(wuq.shape), _const_spec(wukk.shape), _const_spec(wukv.shape),
        pl.BlockSpec((TM, 128), rope_tile), pl.BlockSpec((TM, 128), rope_tile),
        pl.BlockSpec((TM, 128), rope_tile), pl.BlockSpec((TM, 128), rope_tile),
        _const_spec(ones.shape),
    ]
    out_shape = [
        jax.ShapeDtypeStruct((N_HEADS_A, N_TOK, 64), BF16),
        jax.ShapeDtypeStruct((N_KV_A, N_TOK, 64), BF16),
        jax.ShapeDtypeStruct((N_KV_A, VT_ROWS, N_TOK), BF16),
        jax.ShapeDtypeStruct((N_HEADS_B, N_TOK, KB_PAD), BF16),
        jax.ShapeDtypeStruct((N_HEADS_B, N_TOK, KB_PAD), BF16),
        jax.ShapeDtypeStruct((N_HEADS_B, VT_ROWS, N_TOK), BF16),
        jax.ShapeDtypeStruct((N_CTX_SEQ, 128, CTX_LEN), F32),
        jax.ShapeDtypeStruct((N_CTX_SEQ, 128, CTX_LEN), F32),
        jax.ShapeDtypeStruct((N_CTX_TOK, KV_LORA), F32),
        jax.ShapeDtypeStruct((N_CTX_SEQ, QK_ROPE, CTX_LEN), F32),
    ]
    seq_tile = lambda i: (jnp.minimum(i, N_CTX_TILES - 1), 0, 0)
    n_sub = TM // PRE_SUB
    out_specs = [
        pl.BlockSpec((N_HEADS_A, TM, 64), head_tile),
        pl.BlockSpec((N_KV_A, TM, 64), head_tile),
        pl.BlockSpec((N_KV_A, VT_ROWS, TM), vt_tile),
        pl.BlockSpec((N_HEADS_B, TM, KB_PAD), head_tile),
        pl.BlockSpec((N_HEADS_B, TM, KB_PAD), head_tile),
        pl.BlockSpec((N_HEADS_B, VT_ROWS, TM), vt_tile),
        pl.BlockSpec((n_sub, 128, CTX_LEN), seq_tile),
        pl.BlockSpec((n_sub, 128, CTX_LEN), seq_tile),
        pl.BlockSpec((TM, KV_LORA), ctx_tile),
        pl.BlockSpec((n_sub, QK_ROPE, CTX_LEN), seq_tile),
    ]
    scratch = [
        pltpu.VMEM((D_MODEL, IN_E_MAIN + KB_PAD), BF16),
        pltpu.VMEM(wuq.shape, BF16), pltpu.VMEM(wukk.shape, BF16), pltpu.VMEM(wukv.shape, BF16),
    ]
    return pl.pallas_call(
        _pre0_kernel, grid=(N_TILES,), in_specs=in_specs, out_specs=out_specs, out_shape=out_shape,
        scratch_shapes=scratch,
        compiler_params=pltpu.CompilerParams(
            dimension_semantics=("arbitrary",), vmem_limit_bytes=VMEM_LIMIT),
        name="pre0",
    )(xp, xs, mod, gmix, w_in_e_t, gq, gk, gcq, gckv, wuq, wukk, wukv, cosa, sina, cosb, sinb, ones)


def _pre1_kernel(xp_ref, xs_ref, mod_ref, gmix_ref, win_ref, cosa_ref, sina_ref,
                 q_ref, k_ref, v_ref, nk_ref, nv_ref, win_s):
    i = pl.program_id(0)

    @pl.when(i == 0)
    def _():
        win_s[...] = win_ref[...].astype(BF16)

    sh1 = _mod_part(mod_ref, _cond_row(i), 0)
    sc1 = _mod_part(mod_ref, _cond_row(i), 1)
    subs = [slice(r, r + PRE_SUB) for r in range(0, TM, PRE_SUB)]
    projs = []
    for sl in subs:
        x = jnp.where(i < N_CTX_TILES, xp_ref[sl, :], xs_ref[sl, :])
        h = _rms(x, gmix_ref[1:2, :]) * (1.0 + sc1) + sh1
        projs.append(_dot(h.astype(BF16), win_s[...]))
    for sl, proj in zip(subs, projs):
        cosa, sina = cosa_ref[sl, :], sina_ref[sl, :]
        q_s = _rope(proj[:, 0:1024], cosa, sina, 8) * (LOG2E * HEAD_DIM ** -0.5)
        k = _rope(proj[:, 1024:1152], cosa, sina, 1)
        for hh in range(N_HEADS_C):
            q_ref[hh, sl, :] = q_s[:, hh * 64:(hh + 1) * 64].astype(BF16)
        for hh in range(N_KV_C):
            k_ref[hh, sl, :] = k[:, hh * 64:(hh + 1) * 64].astype(BF16)
        _store_vt(v_ref, proj[:, 1152:1280], N_KV_C, sl)

    @pl.when(i < N_CTX_TILES)
    def _():
        for s, proj in enumerate(projs):
            nk_ref[s] = proj[:, 1024:1152].T
            nv_ref[s] = proj[:, 1152:1280].T


def _pre1(xp, xs, mod, gmix, w_in_o, cosa, sina):
    n_in = w_in_o.shape[-1]
    head_tile = lambda i: (0, i, 0)
    vt_tile = lambda i: (0, 0, i)
    ctx_tile = lambda i: (jnp.minimum(i, N_CTX_TILES - 1), 0)
    lat_tile = lambda i: (jnp.maximum(i - N_CTX_TILES, 0), 0)
    rope_tile = lambda i: (_rope_block(i), 0)
    in_specs = [
        pl.BlockSpec((TM, D_MODEL), ctx_tile),
        pl.BlockSpec((TM, D_MODEL), lat_tile),
        _layer_spec(mod.shape[1:], 1),
        _const_spec(gmix.shape),
        _layer_spec((D_MODEL, n_in), 0),
        pl.BlockSpec((TM, 128), rope_tile), pl.BlockSpec((TM, 128), rope_tile),
    ]
    out_shape = [
        jax.ShapeDtypeStruct((N_HEADS_C, N_TOK, 64), BF16),
        jax.ShapeDtypeStruct((N_KV_C, N_TOK, 64), BF16),
        jax.ShapeDtypeStruct((N_KV_C, VT_ROWS, N_TOK), BF16),
        jax.ShapeDtypeStruct((N_CTX_SEQ, 128, CTX_LEN), F32),
        jax.ShapeDtypeStruct((N_CTX_SEQ, 128, CTX_LEN), F32),
    ]
    seq_tile = lambda i: (jnp.minimum(i, N_CTX_TILES - 1), 0, 0)
    out_specs = [
        pl.BlockSpec((N_HEADS_C, TM, 64), head_tile),
        pl.BlockSpec((N_KV_C, TM, 64), head_tile),
        pl.BlockSpec((N_KV_C, VT_ROWS, TM), vt_tile),
        pl.BlockSpec((TM // PRE_SUB, 128, CTX_LEN), seq_tile),
        pl.BlockSpec((TM // PRE_SUB, 128, CTX_LEN), seq_tile),
    ]
    return pl.pallas_call(
        _pre1_kernel, grid=(N_TILES,), in_specs=in_specs, out_specs=out_specs, out_shape=out_shape,
        scratch_shapes=[pltpu.VMEM((D_MODEL, n_in), BF16)],
        compiler_params=pltpu.CompilerParams(
            dimension_semantics=("arbitrary",), vmem_limit_bytes=VMEM_LIMIT),
        name="pre1",
    )(xp, xs, mod, gmix, w_in_o, cosa, sina)


def _ctx_kernel(ak_ref, av_ref, ckv_ref, kpe_ref, ck_ref, cv_ref, wukk_ref, wukv_ref,
                ka_ref, va_ref, kb_ref, vb_ref, kc_ref, vc_ref):
    ones = jnp.ones((VT_ROWS - 64, PAST_LEN), BF16)
    for hh in range(2):
        rows = slice(hh * 64, (hh + 1) * 64)
        ka_ref[hh] = ak_ref[rows, :].T.astype(BF16)
        kc_ref[hh] = ck_ref[rows, :].T.astype(BF16)
        for src, dst in ((av_ref, va_ref), (cv_ref, vc_ref)):
            dst[hh, 0:64, :] = src[rows, :].astype(BF16)
            dst[hh, 64:VT_ROWS, :] = ones
    ckv_b = ckv_ref[...].astype(BF16)
    kbn = _dot(ckv_b, wukk_ref[...].astype(BF16))
    vb = _dot(ckv_b, wukv_ref[...].astype(BF16))
    kpe = jnp.concatenate([jnp.zeros((PAST_LEN, QK_NOPE), F32), kpe_ref[...].T,
                           jnp.zeros((PAST_LEN, KB_PAD - QK_NOPE - QK_ROPE), F32)], axis=1)
    for hh in range(N_HEADS_B):
        kb_ref[hh] = (kbn[:, hh * KB_PAD:(hh + 1) * KB_PAD] + kpe).astype(BF16)
    _store_vt(vb_ref, vb, N_HEADS_B)


def _ctx_prep(ak, av, ckv, kpe, ck, cv, wukk, wukv):
    n = N_LAT_SEQ * PAST_LEN
    row = lambda b: (b, 0)
    head_row = lambda b: (0, b, 0)
    vt_row = lambda b: (0, 0, b)
    tok_minor = lambda rows: pl.BlockSpec((None, rows, PAST_LEN), lambda b: (b, 0, 0))
    in_specs = [
        tok_minor(128), tok_minor(128),
        pl.BlockSpec((PAST_LEN, KV_LORA), row), tok_minor(QK_ROPE),
        tok_minor(128), tok_minor(128),
        _const_spec(wukk.shape), _const_spec(wukv.shape),
    ]
    out_shape = [
        jax.ShapeDtypeStruct((2, n, 64), BF16), jax.ShapeDtypeStruct((2, VT_ROWS, n), BF16),
        jax.ShapeDtypeStruct((N_HEADS_B, n, KB_PAD), BF16), jax.ShapeDtypeStruct((N_HEADS_B, VT_ROWS, n), BF16),
        jax.ShapeDtypeStruct((2, n, 64), BF16), jax.ShapeDtypeStruct((2, VT_ROWS, n), BF16),
    ]
    out_specs = [
        pl.BlockSpec((2, PAST_LEN, 64), head_row), pl.BlockSpec((2, VT_ROWS, PAST_LEN), vt_row),
        pl.BlockSpec((N_HEADS_B, PAST_LEN, KB_PAD), head_row), pl.BlockSpec((N_HEADS_B, VT_ROWS, PAST_LEN), vt_row),
        pl.BlockSpec((2, PAST_LEN, 64), head_row), pl.BlockSpec((2, VT_ROWS, PAST_LEN), vt_row),
    ]
    return pl.pallas_call(
        _ctx_kernel, grid=(N_LAT_SEQ,), in_specs=in_specs, out_specs=out_specs, out_shape=out_shape,
        compiler_params=pltpu.CompilerParams(
            dimension_semantics=("arbitrary",), vmem_limit_bytes=VMEM_LIMIT),
        name="ctx_prep",
    )(ak, av, ckv, kpe, ck, cv, wukk, wukv)


def _softmax_units(units, lookahead):
    tasks = [(u, c) for u, unit in enumerate(units) for c in range(len(unit["chunks"]))]
    scores = {}
    qts = [unit["q"].astype(F32).T.astype(BF16) if len(unit["chunks"]) >= 4 else None for unit in units]

    def emit_scores(t):
        u, c = tasks[t]
        k, _, mask = units[u]["chunks"][c]
        s = _dot_nt(k, units[u]["q"]) if qts[u] is None else _dot(k, qts[u])
        scores[t] = s if mask is None else jnp.where(mask, s, NEG_INF)

    for t in range(min(lookahead, len(tasks))):
        emit_scores(t)
    for t, (u, c) in enumerate(tasks):
        if t + lookahead < len(tasks):
            emit_scores(t + lookahead)
        unit = units[u]
        s = scores.pop(t)
        m, acc = unit["m"], unit["acc"]
        cmax = jnp.max(s, axis=0, keepdims=True)
        m_new = cmax if m is None else jnp.maximum(m, cmax)
        pv = _dot(unit["chunks"][c][1], jnp.exp2(s - m_new).astype(BF16))
        unit["acc"] = pv if acc is None else acc * jnp.exp2(m - m_new) + pv
        unit["m"] = m_new
    return [unit["acc"][0:64] * (1.0 / unit["acc"][64:65]) for unit in units]


def _attn_kernel(*refs, n_kv, group, tq, seq_len, seqs, has_ctx, has_sink, window, q_unit, key_chunk):
    refs = list(refs)
    q_ref = refs.pop(0)
    if window:
        n_blk = seqs + 2
        kb_refs, vb_refs = refs[:n_blk], refs[n_blk:2 * n_blk]
        refs = refs[2 * n_blk:]
    else:
        k_ref, vt_ref = refs[:2]
        refs = refs[2:]
    if has_ctx:
        kx_ref, vx_ref = refs[:2]
        refs = refs[2:]
    if has_sink:
        sink_ref = refs.pop(0)
    o_ref = refs.pop(0)

    j = pl.program_id(1)
    dk = q_ref.shape[-1]
    heads_per_unit = q_unit // tq
    lane = lax.broadcasted_iota(jnp.int32, (1, q_unit), 1)
    if window:
        n_band = tq + 2 * window
        krow = lax.broadcasted_iota(jnp.int32, (n_band, q_unit), 0)
        qcol = lax.broadcasted_iota(jnp.int32, (n_band, q_unit), 1) & (tq - 1)
        rel = (krow - window) - qcol
        in_band = jnp.abs(rel) <= window
        last_j = seq_len // (tq * seqs) - 1
        band_masks = []
        for sq in range(seqs):
            mask = in_band
            if sq == 0:
                mask = mask & ((krow >= window) | (j > 0))
            if sq == seqs - 1:
                mask = mask & ((krow < window + tq) | (j < last_j))
            band_masks.append(mask)
    if has_sink:
        acc0 = jnp.where(lax.broadcasted_iota(jnp.int32, (VT_ROWS, q_unit), 0) >= 64, 1.0, 0.0)

    units = []
    for sq in range(seqs):
        base = sq * (tq if window else seq_len)
        for hk in range(n_kv):
            chunks = []
            if window:
                chunks.append((jnp.concatenate([r[hk] for r in kb_refs[sq:sq + 3]], axis=0),
                               jnp.concatenate([r[hk] for r in vb_refs[sq:sq + 3]], axis=1), band_masks[sq]))
            else:
                for c in range(base, base + seq_len, key_chunk):
                    n = min(key_chunk, base + seq_len - c)
                    chunks.append((k_ref[hk, c:c + n, :], vt_ref[hk, :, c:c + n], None))
            if has_ctx:
                for c in range(0, PAST_LEN, key_chunk):
                    n = min(key_chunk, PAST_LEN - c)
                    chunks.append((kx_ref[hk, c:c + n, :], vx_ref[hk, :, c:c + n], None))
            for u in range(group // heads_per_unit):
                h0 = hk * group + u * heads_per_unit
                q = q_ref[h0:h0 + heads_per_unit, base:base + tq, :].reshape(q_unit, dk)
                unit = dict(q=q, chunks=chunks, m=None, acc=None)
                if has_sink:
                    m0 = jnp.full((1, q_unit), sink_ref[h0] * LOG2E, F32)
                    for e in range(1, heads_per_unit):
                        m0 = jnp.where(lane >= e * tq, sink_ref[h0 + e] * LOG2E, m0)
                    unit.update(m=m0, acc=acc0)
                units.append(unit)
    results = _softmax_units(units, SCORE_LOOKAHEAD)
    per_seq = len(results) // seqs
    for sq in range(seqs):
        outs = []
        for o in results[sq * per_seq:(sq + 1) * per_seq]:
            for e in range(heads_per_unit):
                outs.append(o[:, e * tq:(e + 1) * tq])
        base = sq * (tq if window else seq_len)
        o_ref[base:base + tq, :] = jnp.concatenate(outs, axis=0).T.astype(BF16)


def _attention(q, k, vt, kx, vx, sink, *, n_seq, seq_len, tok_base, tq, window, q_unit, key_chunk, name,
               seqs=1):
    n_q, _, dk = q.shape
    n_kv = k.shape[0]
    group = n_q // n_kv
    has_ctx = kx is not None
    has_sink = sink is not None
    if window:
        n_qt = seq_len // (tq * seqs)
    else:
        n_qt = seq_len // tq
        assert seqs == 1 or (n_qt == 1 and not has_ctx)
    q_base = tok_base // (seqs * tq)
    s_base = tok_base // (seqs * seq_len)
    q_blk = lambda b, j: q_base + b * n_qt + j
    in_specs = [pl.BlockSpec((n_q, seqs * tq, dk), lambda b, j: (0, q_blk(b, j), 0))]
    args = [q]
    if window:
        assert window == tq
        n_kb = seq_len // tq
        kb_base = tok_base // tq

        def key_blk(b, j, off):
            return kb_base + b * n_kb + jnp.clip(j * seqs - 1 + off, 0, n_kb - 1)

        for off in range(seqs + 2):
            in_specs.append(pl.BlockSpec((n_kv, tq, dk), lambda b, j, off=off: (0, key_blk(b, j, off), 0)))
        for off in range(seqs + 2):
            in_specs.append(pl.BlockSpec((n_kv, VT_ROWS, tq), lambda b, j, off=off: (0, 0, key_blk(b, j, off))))
        args += [k] * (seqs + 2) + [vt] * (seqs + 2)
    else:
        in_specs += [
            pl.BlockSpec((n_kv, seqs * seq_len, dk), lambda b, j: (0, s_base + b, 0)),
            pl.BlockSpec((n_kv, VT_ROWS, seqs * seq_len), lambda b, j: (0, 0, s_base + b)),
        ]
        args += [k, vt]
    if has_ctx:
        in_specs += [
            pl.BlockSpec((n_kv, PAST_LEN, dk), lambda b, j: (0, b, 0)),
            pl.BlockSpec((n_kv, VT_ROWS, PAST_LEN), lambda b, j: (0, 0, b)),
        ]
        args += [kx, vx]
    if has_sink:
        in_specs.append(pl.BlockSpec(memory_space=pltpu.SMEM))
        args.append(sink)
    kern = functools.partial(_attn_kernel, n_kv=n_kv, group=group, tq=tq, seq_len=seq_len, seqs=seqs,
                             has_ctx=has_ctx, has_sink=has_sink, window=window,
                             q_unit=q_unit, key_chunk=key_chunk)
    return pl.pallas_call(
        kern, grid=(n_seq if window else n_seq // seqs, n_qt), in_specs=in_specs,
        out_specs=pl.BlockSpec((seqs * tq, n_q * HEAD_DIM), lambda b, j: (b * n_qt + j, 0)),
        out_shape=jax.ShapeDtypeStruct((n_seq * seq_len, n_q * HEAD_DIM), BF16),
        compiler_params=pltpu.CompilerParams(
            dimension_semantics=("arbitrary", "arbitrary"), vmem_limit_bytes=VMEM_LIMIT),
        name=name,
    )(*args)


def _ffn_kernel(*refs, n_o, layer, is_ctx, final):
    halo = 0 if is_ctx else HALO
    it = iter(refs)
    x_ref = next(it)
    xh_refs = None if is_ctx else (next(it), next(it))
    o_refs, oh_refs = [], []
    for _ in range(n_o):
        o_refs.append(next(it))
        if not is_ctx:
            oh_refs.append((next(it), next(it)))
    wo_refs = [next(it) for _ in range(n_o)]
    mod_ref, gffn_ref = next(it), next(it)
    wg_refs = [next(it) for _ in range(FF_PER_STEP)]
    wv_refs = [next(it) for _ in range(FF_PER_STEP)]
    cw_ref, cb_ref = next(it), next(it)
    wd_refs = [next(it) for _ in range(FF_PER_STEP)]
    gfin_ref, out_ref, h2e_ref, acc_ref = next(it), next(it), next(it), next(it)

    m = pl.program_id(0)
    c = pl.program_id(1)
    row = 0 if is_ctx else 1 + m // FFN_LAT_TILES
    g1, sh2, sc2, g2 = (_mod_part(mod_ref, row, k) for k in (2, 3, 4, 5))
    sub = FFN_SUB if is_ctx else FFN_LAT_SUB
    n_sub = FFN_TM // sub
    sub_rows = sub + 2 * halo

    @pl.when(c == 0)
    def _():
        wos = [w[...].astype(BF16) for w in wo_refs]

        def residual_and_norm(xv, ovs):
            attn = _dot(ovs[0], wos[0])
            for ov, wo in zip(ovs[1:], wos[1:]):
                attn = attn + _dot(ov, wo)
            x1 = xv + g1 * attn
            return x1, (_rms(x1, gffn_ref[layer:layer + 1, :]) * (1.0 + sc2) + sh2).astype(BF16)

        for r in range(0, FFN_TM, FFN_SUB):
            x1, h2 = residual_and_norm(x_ref[r:r + FFN_SUB, :], [o[r:r + FFN_SUB, :] for o in o_refs])
            acc_ref[r:r + FFN_SUB, :] = x1
            h2e_ref[halo + r:halo + r + FFN_SUB, :] = h2
        if not is_ctx:
            _, h2h = residual_and_norm(
                jnp.concatenate([xh_refs[0][...], xh_refs[1][...]], axis=0),
                [jnp.concatenate([oh[0][...], oh[1][...]], axis=0) for oh in oh_refs])
            h2e_ref[0:HALO, :] = h2h[0:HALO]
            h2e_ref[HALO + FFN_TM:, :] = h2h[HALO:]

    row8 = lax.broadcasted_iota(jnp.int32, (8, FF_CHUNK), 0)
    if not is_ctx:
        has_prev = m % FFN_LAT_TILES != 0
        has_next = m % FFN_LAT_TILES != FFN_LAT_TILES - 1

    def ff_chunks(n_chunks):
        w_up, w_dn, cw, cb = [], [], [], []
        for j in range(n_chunks):
            w_up.append(jnp.concatenate([wg_refs[j][...].astype(BF16), wv_refs[j][...].astype(BF16)],
                                        axis=1))
            w_dn.append(wd_refs[j][...].astype(BF16))
            cols = pl.ds(pl.multiple_of((c * FF_PER_STEP + j) * FF_CHUNK, FF_CHUNK), FF_CHUNK)
            cw.append([cw_ref[t, layer:layer + 1, cols] for t in range(3)])
            cb.append(cb_ref[layer:layer + 1, cols])
        tasks = [(j, r) for j in range(n_chunks) for r in range(n_sub)]
        ups = {}

        def emit_up(t):
            j, r = tasks[t]
            ups[t] = _dot(h2e_ref[r * sub:r * sub + sub_rows, :], w_up[j])

        for t in range(min(UP_LOOKAHEAD, len(tasks))):
            emit_up(t)
        for t, (j, r) in enumerate(tasks):
            if t + UP_LOOKAHEAD < len(tasks):
                emit_up(t + UP_LOOKAHEAD)
            up = ups.pop(t)
            ge = up[:, :FF_CHUNK]
            val = up[halo:halo + sub, FF_CHUNK:]
            g_prev = pltpu.roll(ge, 1, axis=0)[halo:halo + sub]
            g_next = pltpu.roll(ge, sub_rows - 1, axis=0)[halo:halo + sub]
            prev_ok = False if is_ctx else (has_prev if r == 0 else True)
            next_ok = False if is_ctx else (has_next if r == n_sub - 1 else True)
            if prev_ok is not True:
                g_prev = jnp.concatenate(
                    [jnp.where(jnp.logical_or(row8 != 0, prev_ok), g_prev[0:8], 0.0), g_prev[8:]], axis=0)
            if next_ok is not True:
                g_next = jnp.concatenate(
                    [g_next[:-8], jnp.where(jnp.logical_or(row8 != 7, next_ok), g_next[-8:], 0.0)], axis=0)
            gate = g_prev * cw[j][0] + ge[halo:halo + sub] * cw[j][1] + g_next * cw[j][2] + cb[j]
            act = (gate * jax.nn.sigmoid(gate) * val).astype(BF16)
            acc_ref[r * sub:(r + 1) * sub, :] += g2 * _dot(act, w_dn[j])

    n_tail = N_FF_CHUNKS % FF_PER_STEP
    if n_tail:
        pl.when(c < N_FF_STEPS - 1)(functools.partial(ff_chunks, FF_PER_STEP))
    else:
        ff_chunks(FF_PER_STEP)

    @pl.when(c == N_FF_STEPS - 1)
    def _():
        if n_tail:
            ff_chunks(n_tail)
        x2 = acc_ref[...]
        out_ref[...] = _rms(x2, gfin_ref[...]) if final else x2


def _ffn(x, os, mod, layer, g_ffn, w_out, w_up, conv_w, conv_b, w_down, g_final, *, is_ctx, final):
    n_rows = x.shape[0]
    nh = FFN_TM // HALO
    nblk = n_rows // HALO
    n_o = len(os)
    halo = 0 if is_ctx else HALO
    tile = lambda m, c: (m, 0)
    prev = lambda m, c: (jnp.maximum(m * nh - 1, 0), 0)
    nxt = lambda m, c: (jnp.minimum((m + 1) * nh, nblk - 1), 0)
    chunk = lambda c, j: jnp.minimum(c * FF_PER_STEP + j, N_FF_CHUNKS - 1)

    def with_halo(arr):
        w = arr.shape[1]
        specs = [pl.BlockSpec((FFN_TM, w), tile)]
        if not is_ctx:
            specs += [pl.BlockSpec((HALO, w), prev), pl.BlockSpec((HALO, w), nxt)]
        return specs, [arr] * len(specs)

    in_specs, args = with_halo(x)
    for o in os:
        specs, arrs = with_halo(o)
        in_specs += specs
        args += arrs
    w_rows = D_MODEL // n_o
    for t in range(n_o):
        in_specs.append(pl.BlockSpec((None, w_rows, D_MODEL), lambda m, c, t=t: (0, t, 0)))
        args.append(w_out)
    in_specs += [
        _layer_spec(mod.shape[1:], layer),
        _const_spec(g_ffn.shape),
    ]
    args += [mod, g_ffn]
    steps = range(FF_PER_STEP)
    in_specs += [pl.BlockSpec((None, D_MODEL, FF_CHUNK), lambda m, c, j=j: (layer, 0, chunk(c, j))) for j in steps]
    in_specs += [pl.BlockSpec((None, D_MODEL, FF_CHUNK), lambda m, c, j=j: (layer, 0, N_FF_CHUNKS + chunk(c, j)))
                 for j in steps]
    in_specs += [_const_spec(conv_w.shape), _const_spec(conv_b.shape)]
    in_specs += [pl.BlockSpec((None, FF_CHUNK, D_MODEL), lambda m, c, j=j: (layer, chunk(c, j), 0)) for j in steps]
    args += [w_up] * (2 * FF_PER_STEP) + [conv_w, conv_b] + [w_down] * FF_PER_STEP
    in_specs.append(pl.BlockSpec((1, D_MODEL), lambda m, c: (0, 0)))
    args.append(g_final)
    return pl.pallas_call(
        functools.partial(_ffn_kernel, n_o=n_o, layer=layer, is_ctx=is_ctx, final=final),
        grid=(n_rows // FFN_TM, N_FF_STEPS), in_specs=in_specs,
        out_specs=pl.BlockSpec((FFN_TM, D_MODEL), tile),
        out_shape=jax.ShapeDtypeStruct((n_rows, D_MODEL), F32),
        scratch_shapes=[pltpu.VMEM((FFN_TM + 2 * halo, D_MODEL), BF16), pltpu.VMEM((FFN_TM, D_MODEL), F32)],
        compiler_params=pltpu.CompilerParams(
            dimension_semantics=("arbitrary", "arbitrary"), vmem_limit_bytes=FFN_VMEM_LIMIT),
        name=("ffn_ctx" if is_ctx else "ffn_lat") + ("_final" if final else ""),
    )(*args)


def _rope_tables(rot_dim):
    f32 = np.float32
    t = np.arange(LAT_LEN)
    row = (t // GRID_W).astype(f32)
    col = (t % GRID_W).astype(f32)
    d_axis = rot_dim // 2
    freqs = (f32(ROPE_THETA) ** (-np.arange(0, d_axis, 2, dtype=f32) / f32(d_axis))).astype(f32)
    ang = np.concatenate([row[:, None] * freqs, col[:, None] * freqs], axis=-1)
    cos = np.repeat(np.cos(ang), 2, axis=-1).astype(f32)
    sin = (np.repeat(np.sin(ang), 2, axis=-1) * np.tile(np.array([-1.0, 1.0], f32), rot_dim // 2)).astype(f32)
    if rot_dim == HEAD_DIM:
        cos = np.tile(cos, (1, 2))
        sin = np.tile(sin, (1, 2))
    else:
        cos = np.concatenate([np.ones((LAT_LEN, QK_NOPE), f32), cos,
                              np.ones((LAT_LEN, 128 - QK_NOPE - rot_dim), f32)], axis=-1)
        sin = np.concatenate([np.zeros((LAT_LEN, QK_NOPE), f32), sin,
                              np.zeros((LAT_LEN, 128 - QK_NOPE - rot_dim), f32)], axis=-1)
    cos = np.concatenate([cos, np.ones((TM, 128), f32)], axis=0)
    sin = np.concatenate([sin, np.zeros((TM, 128), f32)], axis=0)
    return jnp.asarray(cos), jnp.asarray(sin)


def _heads_last(t, n_heads):
    return t.reshape(N_CTX_SEQ, 1, n_heads, HEAD_DIM, CTX_LEN).transpose(0, 1, 4, 2, 3)


def kernel(x_prompt, x_sample, cache_a_k, cache_a_v, cache_b_ckv, cache_b_kpe, cache_c_k, cache_c_v, c, c_ctx, w_mod, b_mod, g_mix_norm, g_ffn_norm, w_in_e, g_qnorm_a, g_knorm_a, g_cq_b, w_uq_b, g_ckv_b, w_ukv_b, w_out_e, w_in_o, sink_c, w_out_o, w_up, conv_w, conv_b, w_down, g_final):
    depth = w_mod.shape[0]
    lane_pad = KB_PAD - QK_NOPE - QK_ROPE
    wuq = jnp.pad(w_uq_b[0].reshape(Q_LORA, N_HEADS_B, QK_NOPE + QK_ROPE),
                  ((0, 0), (0, 0), (0, lane_pad))).reshape(Q_LORA, N_HEADS_B * KB_PAD)
    wukv3 = w_ukv_b[0].reshape(KV_LORA, N_HEADS_B, QK_NOPE + V_DIM_B)
    wukk = jnp.pad(wukv3[:, :, :QK_NOPE], ((0, 0), (0, 0), (0, KB_PAD - QK_NOPE))
                   ).reshape(KV_LORA, N_HEADS_B * KB_PAD)
    wukv = wukv3[:, :, QK_NOPE:].reshape(KV_LORA, N_HEADS_B * V_DIM_B)
    gq, gk = g_qnorm_a, g_knorm_a
    seg = np.arange(256) // HEAD_DIM
    ones = jnp.asarray(seg[:, None] == seg[None, :], dtype=BF16)
    cosa, sina = _rope_tables(HEAD_DIM)
    cosb, sinb = _rope_tables(QK_ROPE)
    g_fin2 = g_final[None, :]

    mod = _modulation(c_ctx, c, w_mod, b_mod)

    n_past = N_LAT_SEQ * PAST_LEN
    tok_minor = lambda t: jnp.moveaxis(t[:, 0], 1, -1).reshape(N_LAT_SEQ, -1, PAST_LEN)
    ka_c, va_c, kb_c, vb_c, kc_c, vc_c = _ctx_prep(
        tok_minor(cache_a_k), tok_minor(cache_a_v), cache_b_ckv.reshape(n_past, KV_LORA),
        tok_minor(cache_b_kpe), tok_minor(cache_c_k), tok_minor(cache_c_v), wukk, wukv)

    xp = x_prompt.reshape(N_CTX_TOK, D_MODEL)
    xs = x_sample.reshape(N_TOK - N_CTX_TOK, D_MODEL)
    qa, ka, va, qb, kb, vb, nak, nav, nckv, nkpe = _pre0(
        xp, xs, mod, g_mix_norm, w_in_e[0].T, gq, gk, g_cq_b.reshape(1, 1, Q_LORA), g_ckv_b.reshape(1, 1, KV_LORA),
        wuq, wukk, wukv, cosa, sina, cosb, sinb, ones)
    ffn_w = (w_up, conv_w.transpose(1, 0, 2), conv_b, w_down, g_fin2)
    ctx_kw = dict(n_seq=N_CTX_SEQ, seq_len=CTX_LEN, tok_base=0, tq=CTX_LEN, window=0, q_unit=256,
                  key_chunk=CTX_LEN, seqs=8)
    lat_kw = dict(n_seq=N_LAT_SEQ, seq_len=LAT_LEN, tok_base=N_CTX_TOK, q_unit=256)
    dense_kw = dict(tq=256, window=0, key_chunk=256, **lat_kw)
    oa = (_attention(qa, ka, va, None, None, None, name="attn_a_ctx", **ctx_kw),
          _attention(qa, ka, va, ka_c, va_c, None, name="attn_a_lat", **dense_kw))
    ob = (_attention(qb, kb, vb, None, None, None, name="attn_b_ctx", **ctx_kw),
          _attention(qb, kb, vb, kb_c, vb_c, None, name="attn_b_lat", **dense_kw))
    xp1 = _ffn(xp, [oa[0], ob[0]], mod, 0, g_ffn_norm, w_out_e, *ffn_w, is_ctx=True, final=False)
    xs1 = _ffn(xs, [oa[1], ob[1]], mod, 0, g_ffn_norm, w_out_e, *ffn_w, is_ctx=False, final=False)

    qc, kc, vc, nck, ncv = _pre1(xp1, xs1, mod, g_mix_norm, w_in_o, cosa, sina)
    sink = sink_c[0]
    oc = (_attention(qc, kc, vc, None, None, sink, name="attn_c_ctx", **ctx_kw),
          _attention(qc, kc, vc, kc_c, vc_c, sink, tq=128, window=WINDOW, key_chunk=PAST_LEN, seqs=4,
                     name="attn_c_lat", **lat_kw))
    y_prompt = _ffn(xp1, [oc[0]], mod, 1, g_ffn_norm, w_out_o, *ffn_w, is_ctx=True, final=True)
    y_sample = _ffn(xs1, [oc[1]], mod, 1, g_ffn_norm, w_out_o, *ffn_w, is_ctx=False, final=True)

    return (y_prompt.reshape(N_CTX_SEQ, CTX_LEN, D_MODEL), y_sample.reshape(N_LAT_SEQ, LAT_LEN, D_MODEL),
            _heads_last(nak, N_KV_A), _heads_last(nav, N_KV_A),
            nckv.reshape(N_CTX_SEQ, 1, CTX_LEN, KV_LORA),
            nkpe.reshape(N_CTX_SEQ, 1, QK_ROPE, CTX_LEN).transpose(0, 1, 3, 2),
            _heads_last(nck, N_KV_C), _heads_last(ncv, N_KV_C))
```
